```python
import jax
import jax.numpy as jnp
from jax import lax
import numpy as np

D_MODEL = 1024
BATCH = 8
SEQ = 2048
DEPTH = 2

GRID_W = 64
CTX_LEN = 256
HEAD_DIM = 64
FOURIER_WIDTH = D_MODEL // 4
FOURIER_GROUPS = 4
FOURIER_GROUP_DIM = FOURIER_WIDTH // FOURIER_GROUPS
SWA_HEADS = (3 * D_MODEL // 8) // HEAD_DIM
SWA_KV_HEADS = 2
SWA_WINDOW = 128
SWA_BLOCK = 128
NA_HEADS = (3 * D_MODEL // 8) // HEAD_DIM
NA_WIN_R = 8
NA_WIN_C = 16
NA_QCOLS = 16
NA_KSPAN = NA_QCOLS + NA_WIN_C
ROPE_BASE = 10000.0
N_GROUPS = 4
EXPERTS_PER_GROUP = 8
N_EXPERTS = N_GROUPS * EXPERTS_PER_GROUP
TOP_K = 2
D_EXPERT = D_MODEL // 2
EPS = 1e-6
NEG_INF = -1e30

SWA_Q_W = SWA_HEADS * HEAD_DIM
SWA_KV_W = SWA_KV_HEADS * HEAD_DIM
NA_W = NA_HEADS * HEAD_DIM
MIX_WIDTH = FOURIER_WIDTH + SWA_Q_W + NA_W
Q_COLS = MIX_WIDTH
KV_COLS = 2 * SWA_KV_W + 2 * NA_W
IN_COLS = Q_COLS + KV_COLS

kernel_name = 'hybrid_fourier_swa_natten_hmoe_dit'


def rmsnorm(x, gain):
    xf = x.astype(jnp.float32)
    y = xf * lax.rsqrt(jnp.mean(xf * xf, axis=-1, keepdims=True) + EPS)
    return (y * gain.astype(jnp.float32)).astype(x.dtype)


def modulate(h, shift, scale):
    return h * (1 + scale) + shift


def rope_1d(t, pos):
    half = t.shape[-1] // 2
    freqs = ROPE_BASE ** (-jnp.arange(half, dtype=jnp.float32) / half)
    ang = pos.astype(jnp.float32)[:, None] * freqs[None, :]
    cos = jnp.cos(ang)[None, :, None, :].astype(t.dtype)
    sin = jnp.sin(ang)[None, :, None, :].astype(t.dtype)
    t1, t2 = t[..., :half], t[..., half:]
    return jnp.concatenate([t1 * cos - t2 * sin, t1 * sin + t2 * cos], axis=-1)


def axial_rope(t, rows, cols):
    a = t.shape[-1] // 2
    return jnp.concatenate([rope_1d(t[..., :a], rows), rope_1d(t[..., a:], cols)], axis=-1)


def softmax_sink(logits, sink):
    m = jnp.maximum(jnp.max(logits, axis=-1, keepdims=True), sink)
    e = jnp.exp(logits - m)
    return e / (jnp.sum(e, axis=-1, keepdims=True) + jnp.exp(sink - m))


def fourier_mix(f, w_four):
    b, n, _ = f.shape
    fg = f.astype(jnp.float32).reshape(b, n, FOURIER_GROUPS, FOURIER_GROUP_DIM)
    mixed = jnp.fft.fft2(fg, axes=(1, 3), norm='ortho').real
    return mixed.reshape(b, n, FOURIER_WIDTH).astype(f.dtype) @ w_four


def swa_latent(q, k, v, kc, vc, sink):
    b, s, _, hd = q.shape
    nb = s // SWA_BLOCK
    grp = SWA_HEADS // SWA_KV_HEADS
    scale = hd ** -0.5
    qb = q.reshape(b, nb, SWA_BLOCK, SWA_KV_HEADS, grp, hd)
    pad = ((0, 0), (SWA_BLOCK, SWA_BLOCK), (0, 0), (0, 0))

    def band(t):
        tp = jnp.pad(t, pad).reshape(b, nb + 2, SWA_BLOCK, SWA_KV_HEADS, hd)
        return jnp.concatenate([tp[:, :-2], tp[:, 1:-1], tp[:, 2:]], axis=2)

    kb, vb = band(k), band(v)
    s_lat = jnp.einsum('bnqkgd,bnmkd->bnkgqm', qb, kb).astype(jnp.float32) * scale
    qpos = jnp.arange(nb)[:, None] * SWA_BLOCK + jnp.arange(SWA_BLOCK)[None, :]
    kpos = (jnp.arange(nb)[:, None] - 1) * SWA_BLOCK + jnp.arange(3 * SWA_BLOCK)[None, :]
    valid = (jnp.abs(kpos[:, None, :] - qpos[:, :, None]) <= SWA_WINDOW) & ((kpos >= 0) & (kpos < s))[:, None, :]
    s_lat = jnp.where(valid[None, :, None, None], s_lat, NEG_INF)
    s_ctx = jnp.einsum('bnqkgd,bckd->bnkgqc', qb, kc).astype(jnp.float32) * scale
    sk = sink.astype(jnp.float32).reshape(SWA_KV_HEADS, grp)[:, :, None, None]
    probs = softmax_sink(jnp.concatenate([s_lat, s_ctx], axis=-1), sk).astype(v.dtype)
    n_lat = 3 * SWA_BLOCK
    out = (jnp.einsum('bnkgqm,bnmkd->bnqkgd', probs[..., :n_lat], vb)
           + jnp.einsum('bnkgqc,bckd->bnqkgd', probs[..., n_lat:], vc))
    return out.reshape(b, s, SWA_HEADS * hd)


def swa_context(qc, kc, vc, sink):
    b, l, _, hd = qc.shape
    grp = SWA_HEADS // SWA_KV_HEADS
    qg = qc.reshape(b, l, SWA_KV_HEADS, grp, hd)
    logits = jnp.einsum('blkgd,bmkd->bkglm', qg, kc).astype(jnp.float32) * hd ** -0.5
    sk = sink.astype(jnp.float32).reshape(SWA_KV_HEADS, grp)[:, :, None, None]
    probs = softmax_sink(logits, sk).astype(vc.dtype)
    return jnp.einsum('bkglm,bmkd->blkgd', probs, vc).reshape(b, l, SWA_HEADS * hd)


def dense_attention(q, k, v):
    b, l, h, hd = q.shape
    logits = jnp.einsum('blhd,bmhd->bhlm', q, k).astype(jnp.float32) * hd ** -0.5
    probs = jax.nn.softmax(logits, axis=-1).astype(v.dtype)
    return jnp.einsum('bhlm,bmhd->blhd', probs, v).reshape(b, l, h * hd)


def na_latent(q, k, v, kc, vc, rpb):
    b, s, h, hd = q.shape
    n_rows = s // GRID_W
    kr = min(NA_WIN_R, n_rows)
    ncb = GRID_W // NA_QCOLS
    scale = hd ** -0.5
    r = jnp.arange(n_rows)
    r0 = jnp.clip(r - kr // 2, 0, n_rows - kr)
    row_idx = r0[:, None] + jnp.arange(kr)[None, :]
    qcol = jnp.arange(GRID_W).reshape(ncb, NA_QCOLS)
    c0 = jnp.clip(qcol - NA_WIN_C // 2, 0, GRID_W - NA_WIN_C)
    cstart = jnp.clip(jnp.arange(ncb) * NA_QCOLS - NA_WIN_C // 2, 0, GRID_W - NA_KSPAN)
    col_idx = cstart[:, None] + jnp.arange(NA_KSPAN)[None, :]

    def gather(t):
        tg = t.reshape(b, n_rows, GRID_W, h, hd)
        tg = jnp.take(tg, row_idx, axis=1)
        return jnp.take(tg, col_idx, axis=3)

    kg, vg = gather(k), gather(v)
    qg = q.reshape(b, n_rows, ncb, NA_QCOLS, h, hd)
    s_lat = jnp.einsum('brjqhd,brkjmhd->brjhqkm', qg, kg).astype(jnp.float32) * scale
    dr = row_idx - r[:, None] + (NA_WIN_R - 1)
    dc = jnp.clip(col_idx[:, None, :] - qcol[:, :, None], -(NA_WIN_C - 1), NA_WIN_C - 1) + (NA_WIN_C - 1)
    bias = rpb[:, dr[:, None, None, :, None], dc[None, :, :, None, :]]
    bias = jnp.moveaxis(bias, 0, 2).astype(jnp.float32)
    valid = (col_idx[:, None, :] >= c0[:, :, None]) & (col_idx[:, None, :] < c0[:, :, None] + NA_WIN_C)
    s_lat = jnp.where(valid[None, None, :, None, :, None, :], s_lat + bias[None], NEG_INF)
    n_lat = kr * NA_KSPAN
    s_lat = s_lat.reshape(b, n_rows, ncb, h, NA_QCOLS, n_lat)
    s_ctx = jnp.einsum('brjqhd,bchd->brjhqc', qg, kc).astype(jnp.float32) * scale
    probs = jax.nn.softmax(jnp.concatenate([s_lat, s_ctx], axis=-1), axis=-1).astype(v.dtype)
    p_lat = probs[..., :n_lat].reshape(b, n_rows, ncb, h, NA_QCOLS, kr, NA_KSPAN)
    out = (jnp.einsum('brjhqkm,brkjmhd->brjqhd', p_lat, vg)
           + jnp.einsum('brjhqc,bchd->brjqhd', probs[..., n_lat:], vc))
    return out.reshape(b, s, h * hd)


def split_q(p):
    b, n, _ = p.shape
    f = p[..., :FOURIER_WIDTH]
    q_swa = p[..., FOURIER_WIDTH:FOURIER_WIDTH + SWA_Q_W].reshape(b, n, SWA_HEADS, HEAD_DIM)
    q_na = p[..., FOURIER_WIDTH + SWA_Q_W:Q_COLS].reshape(b, n, NA_HEADS, HEAD_DIM)
    return f, q_swa, q_na


def split_kv(p):
    b, n, _ = p.shape
    k_swa = p[..., :SWA_KV_W].reshape(b, n, SWA_KV_HEADS, HEAD_DIM)
    v_swa = p[..., SWA_KV_W:2 * SWA_KV_W].reshape(b, n, SWA_KV_HEADS, HEAD_DIM)
    k_na = p[..., 2 * SWA_KV_W:2 * SWA_KV_W + NA_W].reshape(b, n, NA_HEADS, HEAD_DIM)
    v_na = p[..., 2 * SWA_KV_W + NA_W:].reshape(b, n, NA_HEADS, HEAD_DIM)
    return k_swa, v_swa, k_na, v_na


def token_mixer(h, hc, w_in, w_four, w_out, sink, rpb, rows, cols, with_ctx_out):
    p = h @ w_in
    pc_kv = hc @ w_in[:, Q_COLS:]
    f, q_swa, q_na = split_q(p[..., :Q_COLS])
    k_swa, v_swa, k_na, v_na = split_kv(p[..., Q_COLS:])
    kc_swa, vc_swa, kc_na, vc_na = split_kv(pc_kv)
    y_four = fourier_mix(f, w_four)
    y_swa = swa_latent(axial_rope(q_swa, rows, cols), axial_rope(k_swa, rows, cols), v_swa, kc_swa, vc_swa, sink)
    y_na = na_latent(q_na, k_na, v_na, kc_na, vc_na, rpb)
    y = jnp.concatenate([y_four, y_swa, y_na], axis=-1) @ w_out
    if not with_ctx_out:
        return y, None
    fc, qc_swa, qc_na = split_q(hc @ w_in[:, :Q_COLS])
    yc = jnp.concatenate([fourier_mix(fc, w_four),
                          swa_context(qc_swa, kc_swa, vc_swa, sink),
                          dense_attention(qc_na, kc_na, vc_na)], axis=-1) @ w_out
    return y, yc


def hier_moe(t, w_rg, b_rg, w_re, b_re, w_gate, w_up, w_down):
    n = t.shape[0]
    g_logits = (t @ w_rg + b_rg).astype(jnp.float32)
    g_sel = jnp.argmax(g_logits, axis=-1)
    p_g = jnp.take_along_axis(jax.nn.softmax(g_logits, axis=-1), g_sel[:, None], axis=-1)
    e_logits = (t @ w_re + b_re).astype(jnp.float32).reshape(n, N_GROUPS, EXPERTS_PER_GROUP)
    e_in = jnp.take_along_axis(e_logits, g_sel[:, None, None], axis=1)[:, 0]
    top_v, top_i = lax.top_k(e_in, TOP_K)
    w_top = jax.nn.softmax(top_v, axis=-1) * p_g
    ids = g_sel[:, None] * EXPERTS_PER_GROUP + top_i
    comb = jnp.sum(jax.nn.one_hot(ids, N_EXPERTS, dtype=jnp.float32) * w_top[..., None], axis=1).astype(t.dtype)
    out = jnp.zeros_like(t)
    for g in range(N_GROUPS):
        sl = slice(g * EXPERTS_PER_GROUP, (g + 1) * EXPERTS_PER_GROUP)
        hid = jax.nn.silu(jnp.einsum('nd,edf->nef', t, w_gate[sl])) * jnp.einsum('nd,edf->nef', t, w_up[sl])
        out = out + jnp.einsum('nef,efd->nd', hid * comb[:, sl, None], w_down[sl])
    return out


def setup_inputs(seed: int = 0) -> dict:
    key = jax.random.key(seed)
    ks = jax.random.split(key, 21)
    d = D_MODEL

    def nrm(k, shape, s):
        return jax.random.normal(k, shape, jnp.float32) * s

    return {
        'x': nrm(ks[0], (BATCH, SEQ, d), 1.0),
        'c': nrm(ks[1], (BATCH, d), 1.0),
        'ctx': nrm(ks[2], (BATCH, CTX_LEN, d), 1.0),
        'c_ctx': nrm(ks[3], (d,), 1.0),
        'w_mod': nrm(ks[4], (DEPTH, d, 6 * d), 0.5 * d ** -0.5),
        'b_mod': nrm(ks[5], (DEPTH, 6 * d), 0.01),
        'g_norm1': 1.0 + nrm(ks[6], (DEPTH, d), 0.05),
        'g_norm2': 1.0 + nrm(ks[7], (DEPTH, d), 0.05),
        'w_in': nrm(ks[8], (DEPTH, d, IN_COLS), d ** -0.5),
        'w_four': nrm(ks[9], (DEPTH, FOURIER_WIDTH, FOURIER_WIDTH), FOURIER_WIDTH ** -0.5),
        'w_out': nrm(ks[10], (DEPTH, MIX_WIDTH, d), MIX_WIDTH ** -0.5),
        'swa_sink': nrm(ks[11], (DEPTH, SWA_HEADS), 0.5),
        'na_rpb': nrm(ks[12], (DEPTH, NA_HEADS, 2 * NA_WIN_R - 1, 2 * NA_WIN_C - 1), 0.1),
        'w_route_group': nrm(ks[13], (DEPTH, d, N_GROUPS), d ** -0.5),
        'b_route_group': nrm(ks[14], (DEPTH, N_GROUPS), 0.01),
        'w_route_expert': nrm(ks[15], (DEPTH, d, N_EXPERTS), d ** -0.5),
        'b_route_expert': nrm(ks[16], (DEPTH, N_EXPERTS), 0.01),
        'w_exp_gate': nrm(ks[17], (DEPTH, N_EXPERTS, d, D_EXPERT), d ** -0.5),
        'w_exp_up': nrm(ks[18], (DEPTH, N_EXPERTS, d, D_EXPERT), d ** -0.5),
        'w_exp_down': nrm(ks[19], (DEPTH, N_EXPERTS, D_EXPERT, d), D_EXPERT ** -0.5),
        'g_final': 1.0 + nrm(ks[20], (d,), 0.05),
    }


def reference(x, c, ctx, c_ctx, w_mod, b_mod, g_norm1, g_norm2, w_in, w_four, w_out, swa_sink, na_rpb,
              w_route_group, b_route_group, w_route_expert, b_route_expert, w_exp_gate, w_exp_up,
              w_exp_down, g_final):
    b, s, d = x.shape
    t = jnp.arange(s)
    rows, cols = t // GRID_W, t % GRID_W
    xc = ctx
    for layer in range(DEPTH):
        with_ctx_out = layer < DEPTH - 1
        mod = jax.nn.silu(c) @ w_mod[layer] + b_mod[layer]
        sh1, sc1, g1, sh2, sc2, g2 = [m[:, None, :] for m in jnp.split(mod, 6, axis=-1)]
        modc = jax.nn.silu(c_ctx) @ w_mod[layer] + b_mod[layer]
        shc1, scc1, gc1, shc2, scc2, gc2 = jnp.split(modc, 6)
        h = modulate(rmsnorm(x, g_norm1[layer]), sh1, sc1)
        hc = modulate(rmsnorm(xc, g_norm1[layer]), shc1, scc1)
        y, yc = token_mixer(h, hc, w_in[layer], w_four[layer], w_out[layer], swa_sink[layer], na_rpb[layer],
                            rows, cols, with_ctx_out)
        x = x + g1 * y
        h2 = modulate(rmsnorm(x, g_norm2[layer]), sh2, sc2)
        moe_args = (w_route_group[layer], b_route_group[layer], w_route_expert[layer], b_route_expert[layer],
                    w_exp_gate[layer], w_exp_up[layer], w_exp_down[layer])
        if with_ctx_out:
            xc = xc + gc1 * yc
            hc2 = modulate(rmsnorm(xc, g_norm2[layer]), shc2, scc2)
            n_lat = b * s
            f = hier_moe(jnp.concatenate([h2.reshape(-1, d), hc2.reshape(-1, d)], axis=0), *moe_args)
            x = x + g2 * f[:n_lat].reshape(x.shape)
            xc = xc + gc2 * f[n_lat:].reshape(xc.shape)
        else:
            f = hier_moe(h2.reshape(-1, d), *moe_args)
            x = x + g2 * f.reshape(x.shape)
    return rmsnorm(x, g_final)
```

```python
import functools

import jax
import jax.numpy as jnp
from jax import lax
from jax.experimental import pallas as pl
from jax.experimental.pallas import tpu as pltpu

F32 = jnp.float32
BF16 = jnp.bfloat16

D_MODEL = 1024
GRID_W = 64
HEAD_DIM = 64
FOURIER_WIDTH = D_MODEL // 4
FOURIER_GROUPS = 4
FOURIER_GROUP_DIM = FOURIER_WIDTH // FOURIER_GROUPS
SWA_HEADS = (3 * D_MODEL // 8) // HEAD_DIM
SWA_KV_HEADS = 2
SWA_GROUP = SWA_HEADS // SWA_KV_HEADS
SWA_WINDOW = 128
SWA_BLOCK = 128
NA_HEADS = (3 * D_MODEL // 8) // HEAD_DIM
NA_WIN_R = 8
NA_WIN_C = 16
ROPE_BASE = 10000.0
N_GROUPS = 4
EXPERTS_PER_GROUP = 8
N_EXPERTS = N_GROUPS * EXPERTS_PER_GROUP
D_EXPERT = D_MODEL // 2
EPS = 1e-6
NEG_INF = -1e30

SWA_Q_W = SWA_HEADS * HEAD_DIM
SWA_KV_W = SWA_KV_HEADS * HEAD_DIM
NA_W = NA_HEADS * HEAD_DIM
MIX_WIDTH = FOURIER_WIDTH + SWA_Q_W + NA_W
Q_COLS = MIX_WIDTH
IN_COLS = 2 * MIX_WIDTH

F_LO, F_HI = 0, FOURIER_WIDTH
QS_LO, QS_HI = F_HI, F_HI + SWA_Q_W
QN_LO, QN_HI = QS_HI, QS_HI + NA_W
KS_LO, KS_HI = QN_HI, QN_HI + SWA_KV_W
VS_LO, VS_HI = KS_HI, KS_HI + SWA_KV_W
KN_LO, KN_HI = VS_HI, VS_HI + NA_W
VN_LO, VN_HI = KN_HI, KN_HI + NA_W

LANE = 128
ROUTE_W = LANE
NA_QROWS = 4
NA_KROWS = NA_QROWS + NA_WIN_R
MOE_TILE = 256
VMEM_LIMIT = 48 * 1024 * 1024


def _cparams(*sem):
    return pltpu.CompilerParams(dimension_semantics=sem, vmem_limit_bytes=VMEM_LIMIT)


def _dot(a, b):
    return jnp.dot(a, b, preferred_element_type=F32)


def _dot_tn(a, b):
    return lax.dot_general(a, b, (((0,), (0,)), ((), ())), preferred_element_type=F32)


def _dot_nt(a, b):
    return lax.dot_general(a, b, (((1,), (1,)), ((), ())), preferred_element_type=F32)


def _split_dot(a, w):
    a_hi = a.astype(BF16)
    a_lo = (a - a_hi.astype(F32)).astype(BF16)
    w_hi = w.astype(BF16)
    w_lo = (w - w_hi.astype(F32)).astype(BF16)
    return _dot(a_hi, w_hi) + (_dot(a_hi, w_lo) + _dot(a_lo, w_hi))


def _mod_kernel(c_ref, w_ref, b_ref, o_ref):
    c = c_ref[...]
    a = c * (1.0 / (1.0 + jnp.exp(-c)))
    o_ref[0] = _split_dot(a, w_ref[0]) + b_ref[0]


def _modulation(c_rows, w_mod, b_mod):
    depth, d, n6 = w_mod.shape
    r = c_rows.shape[0]
    tn = 1536
    return pl.pallas_call(
        _mod_kernel,
        out_shape=jax.ShapeDtypeStruct((depth, r, n6), F32),
        grid=(depth, n6 // tn),
        in_specs=[
            pl.BlockSpec((r, d), lambda l, j: (0, 0)),
            pl.BlockSpec((1, d, tn), lambda l, j: (l, 0, j)),
            pl.BlockSpec((1, 1, tn), lambda l, j: (l, 0, j)),
        ],
        out_specs=pl.BlockSpec((1, r, tn), lambda l, j: (l, 0, j)),
        compiler_params=_cparams("parallel", "parallel"),
        name="modulation",
    )(c_rows, w_mod, b_mod.reshape(depth, 1, n6))


def _rope_rows(t, cos_t, sin_t, n_heads):
    outs = []
    for h in range(n_heads):
        th = t[HEAD_DIM * h:HEAD_DIM * (h + 1)]
        sw = jnp.concatenate([th[16:32], th[0:16], th[48:64], th[32:48]], axis=0)
        outs.append(th * cos_t + sw * sin_t)
    return jnp.concatenate(outs, axis=0)


def _in_proj_kernel(x_ref, sh_ref, sc_ref, g_ref, wt_ref, bdt_ref, cos_ref, sin_ref, *outs, with_q, rope):
    xf = x_ref[0]
    ms = jnp.mean(xf * xf, axis=-1, keepdims=True)
    y = xf * lax.rsqrt(ms + EPS) * g_ref[...]
    h = y * (1.0 + sc_ref[0]) + sh_ref[0]
    pt = _dot_nt(wt_ref[...], h.astype(BF16))
    q_scale = HEAD_DIM ** -0.5
    if with_q:
        fz_ref, qs_ref, qn_ref, ks_ref, vs_ref, kn_ref, vn_ref = outs
        fz_ref[0] = _dot(bdt_ref[...], pt[F_LO:F_HI].astype(BF16)).astype(BF16)
        qs = pt[QS_LO:QS_HI]
        if rope:
            qs = _rope_rows(qs, cos_ref[...], sin_ref[...], SWA_HEADS)
        qs_ref[0] = (qs * q_scale).astype(BF16)
        qn_ref[0] = (pt[QN_LO:QN_HI] * q_scale).astype(BF16)
        off = 0
    else:
        ks_ref, vs_ref, kn_ref, vn_ref = outs
        off = KS_LO
    ks = pt[KS_LO - off:KS_HI - off]
    if rope:
        ks = _rope_rows(ks, cos_ref[...], sin_ref[...], SWA_KV_HEADS)
    ks_ref[0] = ks.astype(BF16)
    vs_ref[0] = pt[VS_LO - off:VS_HI - off].astype(BF16)
    kn_ref[0] = pt[KN_LO - off:KN_HI - off].astype(BF16)
    vn_ref[0] = pt[VN_LO - off:VN_HI - off].astype(BF16)


def _in_proj(x, shift, scale, gain, w_t, bd_t, cos_t, sin_t, *, with_q, rope, tm):
    b, t, d = x.shape
    nf = w_t.shape[0]
    rows = ([2 * FOURIER_WIDTH, SWA_Q_W, NA_W] if with_q else []) + [SWA_KV_W, SWA_KV_W, NA_W, NA_W]
    return pl.pallas_call(
        functools.partial(_in_proj_kernel, with_q=with_q, rope=rope),
        out_shape=[jax.ShapeDtypeStruct((b, r, t), BF16) for r in rows],
        grid=(b, t // tm),
        in_specs=[
            pl.BlockSpec((1, tm, d), lambda i, j: (i, j, 0)),
            pl.BlockSpec((1, 1, d), lambda i, j: (i, 0, 0)),
            pl.BlockSpec((1, 1, d), lambda i, j: (i, 0, 0)),
            pl.BlockSpec((1, d), lambda i, j: (0, 0)),
            pl.BlockSpec((nf, d), lambda i, j: (0, 0)),
            pl.BlockSpec(bd_t.shape, lambda i, j: (0, 0)),
            pl.BlockSpec((HEAD_DIM, tm), lambda i, j: (0, j)),
            pl.BlockSpec((HEAD_DIM, tm), lambda i, j: (0, j)),
        ],
        out_specs=[pl.BlockSpec((1, r, tm), lambda i, j: (i, 0, j)) for r in rows],
        compiler_params=_cparams("parallel", "parallel"),
        name="in_proj_q" if with_q else "in_proj_kv",
    )(x, shift, scale, gain, w_t, bd_t, cos_t, sin_t)


def _fourier_kernel(fz_ref, cn_ref, sn_ref, wft_ref, o_ref):
    zc = fz_ref[0, 0:FOURIER_WIDTH, :]
    zs = fz_ref[0, FOURIER_WIDTH:2 * FOURIER_WIDTH, :]
    y = _dot(zc, cn_ref[...]) - _dot(zs, sn_ref[...])
    o_ref[0] = _dot(wft_ref[...], y.astype(BF16)).astype(BF16)


def _fourier(fz, cn, sn, wf_t, *, tk):
    b, _, t = fz.shape
    return pl.pallas_call(
        _fourier_kernel,
        out_shape=jax.ShapeDtypeStruct((b, FOURIER_WIDTH, t), BF16),
        grid=(t // tk, b),
        in_specs=[
            pl.BlockSpec((1, 2 * FOURIER_WIDTH, t), lambda k, i: (i, 0, 0)),
            pl.BlockSpec((t, tk), lambda k, i: (0, k)),
            pl.BlockSpec((t, tk), lambda k, i: (0, k)),
            pl.BlockSpec((FOURIER_WIDTH, FOURIER_WIDTH), lambda k, i: (0, 0)),
        ],
        out_specs=pl.BlockSpec((1, FOURIER_WIDTH, tk), lambda k, i: (i, 0, k)),
        compiler_params=_cparams("parallel", "parallel"),
        name="fourier",
    )(fz, cn, sn, wf_t)


def _attend(q_t, k_parts, v_parts, biases, sink_row):
    logits = []
    for k_t, bias in zip(k_parts, biases):
        s = _dot_tn(k_t, q_t)
        logits.append(s if bias is None else s + bias)
    m = functools.reduce(jnp.maximum, [jnp.max(s, axis=0, keepdims=True) for s in logits])
    if sink_row is not None:
        m = jnp.maximum(m, sink_row)
    den = None
    acc = None
    for s, v_t in zip(logits, v_parts):
        p = jnp.exp(s - m)
        ps = jnp.sum(p, axis=0, keepdims=True)
        den = ps if den is None else den + ps
        o = _dot(v_t, p.astype(BF16))
        acc = o if acc is None else acc + o
    if sink_row is not None:
        den = den + jnp.exp(sink_row - m)
    return acc / den


def _swa_kernel(q_ref, k_ref, v_ref, kc_ref, vc_ref, mask_ref, sink_ref, o_ref, *, seq):
    nb = seq // SWA_BLOCK
    span = 3 * SWA_BLOCK
    kc = kc_ref[0]
    vc = vc_ref[0]
    sink_row = sink_ref[0]
    for n in range(nb):
        ks = min(max((n - 1) * SWA_BLOCK, 0), seq - span)
        mtype = 0 if n == 0 else (2 if n == nb - 1 else 1)
        q0 = n * SWA_BLOCK
        q_t = jnp.concatenate(
            [q_ref[0, HEAD_DIM * h:HEAD_DIM * (h + 1), q0:q0 + SWA_BLOCK] for h in range(SWA_GROUP)], axis=1)
        o = _attend(q_t, [k_ref[0, :, ks:ks + span], kc], [v_ref[0, :, ks:ks + span], vc],
                    [mask_ref[mtype], None], sink_row)
        for h in range(SWA_GROUP):
            o_ref[0, HEAD_DIM * h:HEAD_DIM * (h + 1), q0:q0 + SWA_BLOCK] = (
                o[:, SWA_BLOCK * h:SWA_BLOCK * (h + 1)].astype(BF16))


def _swa(qs, ks, vs, kc, vc, mask, sink_rows):
    b, _, t = qs.shape
    lc = kc.shape[2]
    gw = SWA_GROUP * HEAD_DIM
    span = 3 * SWA_BLOCK
    return pl.pallas_call(
        functools.partial(_swa_kernel, seq=t),
        out_shape=jax.ShapeDtypeStruct((b, SWA_Q_W, t), BF16),
        grid=(b, SWA_KV_HEADS),
        in_specs=[
            pl.BlockSpec((1, gw, t), lambda i, g: (i, g, 0)),
            pl.BlockSpec((1, HEAD_DIM, t), lambda i, g: (i, g, 0)),
            pl.BlockSpec((1, HEAD_DIM, t), lambda i, g: (i, g, 0)),
            pl.BlockSpec((1, HEAD_DIM, lc), lambda i, g: (i, g, 0)),
            pl.BlockSpec((1, HEAD_DIM, lc), lambda i, g: (i, g, 0)),
            pl.BlockSpec((3, span, SWA_GROUP * SWA_BLOCK), lambda i, g: (0, 0, 0)),
            pl.BlockSpec((1, 1, SWA_GROUP * SWA_BLOCK), lambda i, g: (g, 0, 0)),
        ],
        out_specs=pl.BlockSpec((1, gw, t), lambda i, g: (i, g, 0)),
        compiler_params=_cparams("parallel", "parallel"),
        name="swa",
    )(qs, ks, vs, kc, vc, mask, sink_rows)


def _na_bias_kernel(rpb_ref, o_ref, u_ref, *, total_rows):
    hd = pl.program_id(0)
    kc = lax.broadcasted_iota(jnp.int32, (GRID_W, LANE), 0)
    lane = lax.broadcasted_iota(jnp.int32, (GRID_W, LANE), 1)
    qc = lane % GRID_W
    dc = jnp.clip(kc - qc, -(NA_WIN_C - 1), NA_WIN_C - 1) + (NA_WIN_C - 1)
    c0 = jnp.clip(qc - NA_WIN_C // 2, 0, GRID_W - NA_WIN_C)
    valid_c = (kc >= c0) & (kc < c0 + NA_WIN_C)
    n_dr = 2 * NA_WIN_R - 1
    for dr in range(n_dr):
        u = jnp.full((GRID_W, LANE), NEG_INF, F32)
        for d in range(2 * NA_WIN_C - 1):
            u = jnp.where(valid_c & (dc == d), rpb_ref[hd, dr, d], u)
        u_ref[dr] = u
    n_rows = o_ref.shape[2] // GRID_W
    block_types = [(0, 0), (NA_QROWS, 0), (total_rows - NA_QROWS, total_rows - NA_KROWS)]
    neg = jnp.full((GRID_W, LANE), NEG_INF, F32)
    for t, (r_base, k_base) in enumerate(block_types):
        for kl in range(n_rows):
            kr = k_base + kl
            for lg in range(NA_QROWS // 2):
                halves = []
                for rq in (2 * lg, 2 * lg + 1):
                    r = r_base + rq
                    r0 = min(max(r - NA_WIN_R // 2, 0), total_rows - NA_WIN_R)
                    ok = r0 <= kr < r0 + NA_WIN_R
                    halves.append(u_ref[kr - r + NA_WIN_R - 1] if ok else neg)
                o_ref[0, t, GRID_W * kl:GRID_W * (kl + 1), LANE * lg:LANE * (lg + 1)] = jnp.where(
                    lane < GRID_W, halves[0], halves[1])


def _na_bias(rpb, seq):
    nh = rpb.shape[0]
    return pl.pallas_call(
        functools.partial(_na_bias_kernel, total_rows=seq // GRID_W),
        out_shape=jax.ShapeDtypeStruct((nh, 3, NA_KROWS * GRID_W, NA_QROWS * GRID_W), F32),
        grid=(nh,),
        in_specs=[pl.BlockSpec(memory_space=pltpu.SMEM)],
        out_specs=pl.BlockSpec((1, 3, NA_KROWS * GRID_W, NA_QROWS * GRID_W), lambda h: (h, 0, 0, 0)),
        scratch_shapes=[pltpu.VMEM((2 * NA_WIN_R - 1, GRID_W, LANE), F32)],
        compiler_params=_cparams("parallel"),
        name="na_bias",
    )(rpb)


def _na_kernel(q_ref, k_ref, v_ref, kc_ref, vc_ref, bias_ref, o_ref, *, seq):
    n_rows = seq // GRID_W
    qw = NA_QROWS * GRID_W
    kw = NA_KROWS * GRID_W
    kc = kc_ref[0]
    vc = vc_ref[0]
    nblk = n_rows // NA_QROWS
    for j in range(nblk):
        ks = min(max(j * NA_QROWS - NA_WIN_R // 2, 0), n_rows - NA_KROWS) * GRID_W
        btype = 0 if j == 0 else (2 if j == nblk - 1 else 1)
        q0 = j * qw
        o = _attend(q_ref[0, :, q0:q0 + qw], [k_ref[0, :, ks:ks + kw], kc], [v_ref[0, :, ks:ks + kw], vc],
                    [bias_ref[0, btype], None], None)
        o_ref[0, :, q0:q0 + qw] = o.astype(BF16)


def _na(qn, kn, vn, kc, vc, bias):
    b, _, t = qn.shape
    lc = kc.shape[2]
    head = lambda i, h: (i, h, 0)
    return pl.pallas_call(
        functools.partial(_na_kernel, seq=t),
        out_shape=jax.ShapeDtypeStruct((b, NA_W, t), BF16),
        grid=(b, NA_HEADS),
        in_specs=[
            pl.BlockSpec((1, HEAD_DIM, t), head),
            pl.BlockSpec((1, HEAD_DIM, t), head),
            pl.BlockSpec((1, HEAD_DIM, t), head),
            pl.BlockSpec((1, HEAD_DIM, lc), head),
            pl.BlockSpec((1, HEAD_DIM, lc), head),
            pl.BlockSpec((1,) + bias.shape[1:], lambda i, h: (h, 0, 0, 0)),
        ],
        out_specs=pl.BlockSpec((1, HEAD_DIM, t), head),
        compiler_params=_cparams("parallel", "parallel"),
        name="na",
    )(qn, kn, vn, kc, vc, bias)


def _ctx_attn_kernel(qs_ref, ks_ref, vs_ref, qn_ref, kn_ref, vn_ref, sink_ref, ys_ref, yn_ref):
    lc = qs_ref.shape[2]
    for g in range(SWA_KV_HEADS):
        q_t = jnp.concatenate(
            [qs_ref[0, HEAD_DIM * (SWA_GROUP * g + h):HEAD_DIM * (SWA_GROUP * g + h + 1), :] for h in range(SWA_GROUP)],
            axis=1)
        kv = slice(HEAD_DIM * g, HEAD_DIM * (g + 1))
        o = _attend(q_t, [ks_ref[0, kv, :]], [vs_ref[0, kv, :]], [None], sink_ref[g])
        for h in range(SWA_GROUP):
            hh = SWA_GROUP * g + h
            ys_ref[0, HEAD_DIM * hh:HEAD_DIM * (hh + 1), :] = o[:, lc * h:lc * (h + 1)].astype(BF16)
    for h in range(NA_HEADS):
        sl = slice(HEAD_DIM * h, HEAD_DIM * (h + 1))
        o = _attend(qn_ref[0, sl, :], [kn_ref[0, sl, :]], [vn_ref[0, sl, :]], [None], None)
        yn_ref[0, sl, :] = o.astype(BF16)


def _ctx_attn(qs, ks, vs, qn, kn, vn, sink_rows):
    b, _, lc = qs.shape
    full = lambda a: pl.BlockSpec((1,) + a.shape[1:], lambda i: (i, 0, 0))
    return pl.pallas_call(
        _ctx_attn_kernel,
        out_shape=[jax.ShapeDtypeStruct((b, SWA_Q_W, lc), BF16), jax.ShapeDtypeStruct((b, NA_W, lc), BF16)],
        grid=(b,),
        in_specs=[full(qs), full(ks), full(vs), full(qn), full(kn), full(vn),
                  pl.BlockSpec(sink_rows.shape, lambda i: (0, 0, 0))],
        out_specs=[pl.BlockSpec((1, SWA_Q_W, lc), lambda i: (i, 0, 0)),
                   pl.BlockSpec((1, NA_W, lc), lambda i: (i, 0, 0))],
        compiler_params=_cparams("parallel"),
        name="ctx_attn",
    )(qs, ks, vs, qn, kn, vn, sink_rows)


def _route(logits):
    lane = lax.broadcasted_iota(jnp.int32, logits.shape, 1)
    big = jnp.int32(ROUTE_W)
    gmask = lane < N_GROUPS
    gl = jnp.where(gmask, logits, NEG_INF)
    gmax = jnp.max(gl, axis=-1, keepdims=True)
    g_sel = jnp.min(jnp.where(gl == gmax, lane, big), axis=-1, keepdims=True)
    p_g = 1.0 / jnp.sum(jnp.where(gmask, jnp.exp(logits - gmax), 0.0), axis=-1, keepdims=True)
    lo = N_GROUPS + EXPERTS_PER_GROUP * g_sel
    emask = (lane >= lo) & (lane < lo + EXPERTS_PER_GROUP)
    el = jnp.where(emask, logits, NEG_INF)
    v1 = jnp.max(el, axis=-1, keepdims=True)
    i1 = jnp.min(jnp.where(el == v1, lane, big), axis=-1, keepdims=True)
    el2 = jnp.where(lane == i1, NEG_INF, el)
    v2 = jnp.max(el2, axis=-1, keepdims=True)
    i2 = jnp.min(jnp.where(el2 == v2, lane, big), axis=-1, keepdims=True)
    e21 = jnp.exp(v2 - v1)
    w1 = p_g / (1.0 + e21)
    w2 = p_g * e21 / (1.0 + e21)
    out = jnp.where(lane == 0, (i1 - N_GROUPS).astype(F32), 0.0)
    out = jnp.where(lane == 1, (i2 - N_GROUPS).astype(F32), out)
    out = jnp.where(lane == 2, w1, out)
    out = jnp.where(lane == 3, w2, out)
    return out


def _out_proj_kernel(x_ref, yf_ref, ys_ref, yn_ref, wo_ref, g1_ref, sh_ref, sc_ref, gn_ref, wr_ref, br_ref,
                     x1_ref, h2_ref, rt_ref):
    y = (_dot_tn(yf_ref[0], wo_ref[F_LO:F_HI, :]) + _dot_tn(ys_ref[0], wo_ref[QS_LO:QS_HI, :])
         + _dot_tn(yn_ref[0], wo_ref[QN_LO:QN_HI, :]))
    x1 = x_ref[0] + g1_ref[0] * y
    x1_ref[0] = x1
    ms = jnp.mean(x1 * x1, axis=-1, keepdims=True)
    h2 = (x1 * lax.rsqrt(ms + EPS) * gn_ref[...]) * (1.0 + sc_ref[0]) + sh_ref[0]
    h2_ref[0] = h2.astype(BF16)
    rt_ref[0] = _route(_split_dot(h2, wr_ref[...]) + br_ref[...])


def _out_proj(x, yf, ys, yn, w_out, gate, shift, scale, gain, w_route, b_route, *, tm):
    b, t, d = x.shape
    vec = pl.BlockSpec((1, 1, d), lambda i, j: (i, 0, 0))
    feat = lambda r: pl.BlockSpec((1, r, tm), lambda i, j: (i, 0, j))
    tok = lambda w: pl.BlockSpec((1, tm, w), lambda i, j: (i, j, 0))
    return pl.pallas_call(
        _out_proj_kernel,
        out_shape=[jax.ShapeDtypeStruct((b, t, d), F32), jax.ShapeDtypeStruct((b, t, d), BF16),
                   jax.ShapeDtypeStruct((b, t, ROUTE_W), F32)],
        grid=(b, t // tm),
        in_specs=[tok(d), feat(FOURIER_WIDTH), feat(SWA_Q_W), feat(NA_W),
                  pl.BlockSpec((MIX_WIDTH, d), lambda i, j: (0, 0)),
                  vec, vec, vec,
                  pl.BlockSpec((1, d), lambda i, j: (0, 0)),
                  pl.BlockSpec((d, ROUTE_W), lambda i, j: (0, 0)),
                  pl.BlockSpec((1, ROUTE_W), lambda i, j: (0, 0))],
        out_specs=[tok(d), tok(d), tok(ROUTE_W)],
        compiler_params=_cparams("parallel", "parallel"),
        name="out_proj",
    )(x, yf, ys, yn, w_out, gate, shift, scale, gain, w_route, b_route)


def _experts_kernel(te_ref, nt_ref, x_ref, wg_ref, wu_ref, wd_ref, y_ref, wg_s, wu_s, wd_s):
    i = pl.program_id(0)
    fresh = jnp.logical_or(i == 0, te_ref[i] != te_ref[jnp.maximum(i - 1, 0)])

    @pl.when(jnp.logical_and(fresh, i < nt_ref[0]))
    def _():
        wg_s[...] = wg_ref[0].astype(BF16)
        wu_s[...] = wu_ref[0].astype(BF16)
        wd_s[...] = wd_ref[0].astype(BF16)

    @pl.when(i < nt_ref[0])
    def _():
        x = x_ref[...]
        g = _dot(x, wg_s[...])
        u = _dot(x, wu_s[...])
        hid = (g * (1.0 / (1.0 + jnp.exp(-g)))) * u
        y_ref[...] = _dot(hid.astype(BF16), wd_s[...]).astype(BF16)


def _experts(xs, tile_expert, n_tiles, w_gate, w_up, w_down):
    r, d = xs.shape
    de = w_gate.shape[2]
    last = lambda i, te, nt: jnp.minimum(i, nt[0] - 1)
    return pl.pallas_call(
        _experts_kernel,
        out_shape=jax.ShapeDtypeStruct((r, d), BF16),
        grid_spec=pltpu.PrefetchScalarGridSpec(
            num_scalar_prefetch=2,
            grid=(r // MOE_TILE,),
            in_specs=[
                pl.BlockSpec((MOE_TILE, d), lambda i, te, nt: (last(i, te, nt), 0)),
                pl.BlockSpec((1, d, de), lambda i, te, nt: (te[i], 0, 0)),
                pl.BlockSpec((1, d, de), lambda i, te, nt: (te[i], 0, 0)),
                pl.BlockSpec((1, de, d), lambda i, te, nt: (te[i], 0, 0)),
            ],
            out_specs=pl.BlockSpec((MOE_TILE, d), lambda i, te, nt: (last(i, te, nt), 0)),
            scratch_shapes=[pltpu.VMEM((d, de), BF16), pltpu.VMEM((d, de), BF16), pltpu.VMEM((de, d), BF16)],
        ),
        compiler_params=_cparams("arbitrary"),
        name="experts",
    )(tile_expert, n_tiles, xs, w_gate, w_up, w_down)


def _combine_kernel(x1_ref, y1_ref, y2_ref, rt_ref, g2_ref, gf_ref, o_ref, *, final_norm):
    rt = rt_ref[0]
    f = rt[:, 2:3] * y1_ref[0].astype(F32) + rt[:, 3:4] * y2_ref[0].astype(F32)
    x2 = x1_ref[0] + g2_ref[0] * f
    if final_norm:
        ms = jnp.mean(x2 * x2, axis=-1, keepdims=True)
        x2 = x2 * lax.rsqrt(ms + EPS) * gf_ref[...]
    o_ref[0] = x2


def _combine(x1, y1, y2, route, gate, g_final, *, final_norm, tm):
    b, t, d = x1.shape
    tok = lambda w: pl.BlockSpec((1, tm, w), lambda i, j: (i, j, 0))
    return pl.pallas_call(
        functools.partial(_combine_kernel, final_norm=final_norm),
        out_shape=jax.ShapeDtypeStruct((b, t, d), F32),
        grid=(b, t // tm),
        in_specs=[tok(d), tok(d), tok(d), tok(ROUTE_W),
                  pl.BlockSpec((1, 1, d), lambda i, j: (i, 0, 0)),
                  pl.BlockSpec((1, d), lambda i, j: (0, 0))],
        out_specs=tok(d),
        compiler_params=_cparams("parallel", "parallel"),
        name="combine",
    )(x1, y1, y2, route, gate, g_final)


def _rope_tables(seq):
    half = HEAD_DIM // 4
    t = jnp.arange(seq)
    rows, cols = t // GRID_W, t % GRID_W
    freqs = ROPE_BASE ** (-jnp.arange(half, dtype=F32) / half)

    def cs(pos):
        ang = pos.astype(F32)[None, :] * freqs[:, None]
        return jnp.cos(ang), jnp.sin(ang)

    cr, sr = cs(rows)
    cc, sc = cs(cols)
    return jnp.concatenate([cr, cr, cc, cc], axis=0), jnp.concatenate([-sr, sr, -sc, sc], axis=0)


def _dft_tables(n):
    idx = jnp.arange(n)
    ang = ((idx[:, None] * idx[None, :]) % n).astype(F32) * (2.0 * jnp.pi / n)
    return jnp.cos(ang), jnp.sin(ang)


def _channel_dft_t():
    c, s = _dft_tables(FOURIER_GROUP_DIM)
    eye = jnp.eye(FOURIER_GROUPS, dtype=F32)
    scale = FOURIER_GROUP_DIM ** -0.5
    return jnp.concatenate([jnp.kron(eye, c), jnp.kron(eye, s)], axis=0) * scale


def _position_dft(n):
    c, s = _dft_tables(n)
    scale = n ** -0.5
    return (c * scale).astype(BF16), (s * scale).astype(BF16)


def _swa_mask(seq):
    span = 3 * SWA_BLOCK
    nb = seq // SWA_BLOCK
    kk = jnp.arange(span)[:, None]
    q = jnp.arange(SWA_BLOCK)[None, :]
    tiles = []
    for n, ks in ((0, 0), (1, 0), (nb - 1, seq - span)):
        ok = jnp.abs(ks + kk - (n * SWA_BLOCK + q)) <= SWA_WINDOW
        tiles.append(jnp.tile(jnp.where(ok, 0.0, NEG_INF).astype(F32), (1, SWA_GROUP)))
    return jnp.stack(tiles)


def _sink_rows(sink, width):
    return jnp.repeat(sink.astype(F32).reshape(SWA_KV_HEADS, SWA_GROUP), width, axis=1)[:, None, :]


def _dispatch_plan(route2d, n_rows_padded):
    n = route2d.shape[0]
    ids = route2d[:, 0:2].astype(jnp.int32).reshape(-1)
    order = jnp.argsort(ids, stable=True)
    counts = jnp.bincount(ids, length=N_EXPERTS)
    padded = ((counts + MOE_TILE - 1) // MOE_TILE) * MOE_TILE
    starts = jnp.cumsum(padded) - padded
    raw_starts = jnp.cumsum(counts) - counts
    sorted_ids = ids[order]
    dest_sorted = starts[sorted_ids] + (jnp.arange(2 * n) - raw_starts[sorted_ids])
    src_tok = jnp.zeros((n_rows_padded,), jnp.int32).at[dest_sorted].set((order // 2).astype(jnp.int32))
    slot = jnp.zeros((2 * n,), jnp.int32).at[order].set(dest_sorted.astype(jnp.int32)).reshape(n, 2)
    ends = starts + padded
    n_tiles = (ends[-1] // MOE_TILE).astype(jnp.int32)
    tile_start = jnp.arange(n_rows_padded // MOE_TILE) * MOE_TILE
    te = jnp.searchsorted(ends, tile_start, side="right").astype(jnp.int32)
    te_last = te[jnp.maximum(n_tiles - 1, 0)]
    te = jnp.where(tile_start < ends[-1], jnp.minimum(te, N_EXPERTS - 1), te_last)
    return src_tok, slot, te, n_tiles.reshape(1)


def _moe(h2_2d, route2d, w_gate, w_up, w_down):
    n, d = h2_2d.shape
    r = -(-(2 * n + N_EXPERTS * (MOE_TILE - 1)) // MOE_TILE) * MOE_TILE
    src_tok, slot, te, n_tiles = _dispatch_plan(route2d, r)
    xs = jnp.take(h2_2d, src_tok, axis=0)
    ys = _experts(xs, te, n_tiles, w_gate, w_up, w_down)
    return jnp.take(ys, slot[:, 0], axis=0), jnp.take(ys, slot[:, 1], axis=0)


def kernel(x, c, ctx, c_ctx, w_mod, b_mod, g_norm1, g_norm2, w_in, w_four, w_out, swa_sink, na_rpb,
           w_route_group, b_route_group, w_route_expert, b_route_expert, w_exp_gate, w_exp_up,
           w_exp_down, g_final):
    b, s, d = x.shape
    lc = ctx.shape[1]
    depth = w_mod.shape[0]
    tm = 512

    c_rows = jnp.concatenate([c, c_ctx[None, :], jnp.zeros((7, d), F32)], axis=0)
    mod = _modulation(c_rows, w_mod, b_mod)

    cos_t, sin_t = _rope_tables(s)
    cos_c, sin_c = cos_t[:, :lc], sin_t[:, :lc]
    bd_t = _channel_dft_t().astype(BF16)
    cn, sn = _position_dft(s)
    cn_c, sn_c = _position_dft(lc)
    mask = _swa_mask(s)
    pad = jnp.zeros((d, ROUTE_W - N_GROUPS - N_EXPERTS), F32)

    xc = ctx
    for layer in range(depth):
        with_ctx_out = layer < depth - 1
        lat = [mod[layer, :b, i * d:(i + 1) * d][:, None, :] for i in range(6)]
        cx = [jnp.broadcast_to(mod[layer, b, i * d:(i + 1) * d][None, None, :], (b, 1, d)) for i in range(6)]
        sh1, sc1, g1, sh2, sc2, g2 = lat
        shc1, scc1, gc1, shc2, scc2, gc2 = cx
        gn1 = g_norm1[layer][None, :]
        gn2 = g_norm2[layer][None, :]
        w_t = w_in[layer].T.astype(BF16)
        wf_t = w_four[layer].T.astype(BF16)
        wo = w_out[layer].astype(BF16)
        w_r = jnp.concatenate([w_route_group[layer], w_route_expert[layer], pad], axis=1)
        b_r = jnp.concatenate([b_route_group[layer], b_route_expert[layer],
                               jnp.zeros((ROUTE_W - N_GROUPS - N_EXPERTS,), F32)])[None, :]
        sink_lat = _sink_rows(swa_sink[layer], SWA_BLOCK)
        sink_ctx = _sink_rows(swa_sink[layer], lc)

        fz, qs, qn, ks, vs, kn, vn = _in_proj(x, sh1, sc1, gn1, w_t, bd_t, cos_t, sin_t,
                                              with_q=True, rope=True, tm=tm)
        if with_ctx_out:
            fz_c, qs_c, qn_c, ks_c, vs_c, kn_c, vn_c = _in_proj(xc, shc1, scc1, gn1, w_t, bd_t, cos_c, sin_c,
                                                                with_q=True, rope=False, tm=lc)
        else:
            ks_c, vs_c, kn_c, vn_c = _in_proj(xc, shc1, scc1, gn1, w_t[KS_LO:], bd_t, cos_c, sin_c,
                                              with_q=False, rope=False, tm=lc)

        yf = _fourier(fz, cn, sn, wf_t, tk=512)
        ys = _swa(qs, ks, vs, ks_c, vs_c, mask, sink_lat)
        yn = _na(qn, kn, vn, kn_c, vn_c, _na_bias(na_rpb[layer], s))
        x1, h2, route = _out_proj(x, yf, ys, yn, wo, g1, sh2, sc2, gn2, w_r, b_r, tm=tm)

        moe_w = (w_exp_gate[layer], w_exp_up[layer], w_exp_down[layer])
        if with_ctx_out:
            yf_c = _fourier(fz_c, cn_c, sn_c, wf_t, tk=lc)
            ys_c, yn_c = _ctx_attn(qs_c, ks_c, vs_c, qn_c, kn_c, vn_c, sink_ctx)
            xc1, hc2, route_c = _out_proj(xc, yf_c, ys_c, yn_c, wo, gc1, shc2, scc2, gn2, w_r, b_r, tm=lc)
            n_lat = b * s
            h_all = jnp.concatenate([h2.reshape(n_lat, d), hc2.reshape(b * lc, d)], axis=0)
            r_all = jnp.concatenate([route.reshape(n_lat, ROUTE_W), route_c.reshape(b * lc, ROUTE_W)], axis=0)
            y1, y2 = _moe(h_all, r_all, *moe_w)
            x = _combine(x1, y1[:n_lat].reshape(b, s, d), y2[:n_lat].reshape(b, s, d), route, g2,
                         g_final[None, :], final_norm=False, tm=tm)
            xc = _combine(xc1, y1[n_lat:].reshape(b, lc, d), y2[n_lat:].reshape(b, lc, d), route_c, gc2,
                          g_final[None, :], final_norm=False, tm=lc)
        else:
            y1, y2 = _moe(h2.reshape(b * s, d), route.reshape(b * s, ROUTE_W), *moe_w)
            x = _combine(x1, y1.reshape(b, s, d), y2.reshape(b, s, d), route, g2, g_final[None, :],
                         final_norm=True, tm=tm)
    return x
```

```python
import functools

import jax
import jax.numpy as jnp
from jax import lax
from jax.experimental import pallas as pl
from jax.experimental.pallas import tpu as pltpu

F32 = jnp.float32
BF16 = jnp.bfloat16

D_MODEL = 1024
GRID_W = 64
HEAD_DIM = 64
FOURIER_WIDTH = D_MODEL // 4
FOURIER_GROUPS = 4
FOURIER_GROUP_DIM = FOURIER_WIDTH // FOURIER_GROUPS
SWA_HEADS = (3 * D_MODEL // 8) // HEAD_DIM
SWA_KV_HEADS = 2
SWA_GROUP = SWA_HEADS // SWA_KV_HEADS
SWA_WINDOW = 128
SWA_BLOCK = 128
NA_HEADS = (3 * D_MODEL // 8) // HEAD_DIM
NA_WIN_R = 8
NA_WIN_C = 16
ROPE_BASE = 10000.0
N_GROUPS = 4
EXPERTS_PER_GROUP = 8
N_EXPERTS = N_GROUPS * EXPERTS_PER_GROUP
D_EXPERT = D_MODEL // 2
EPS = 1e-6
NEG_INF = -1e30

SWA_Q_W = SWA_HEADS * HEAD_DIM
SWA_KV_W = SWA_KV_HEADS * HEAD_DIM
NA_W = NA_HEADS * HEAD_DIM
MIX_WIDTH = FOURIER_WIDTH + SWA_Q_W + NA_W
Q_COLS = MIX_WIDTH
IN_COLS = 2 * MIX_WIDTH

F_LO, F_HI = 0, FOURIER_WIDTH
QS_LO, QS_HI = F_HI, F_HI + SWA_Q_W
QN_LO, QN_HI = QS_HI, QS_HI + NA_W
KS_LO, KS_HI = QN_HI, QN_HI + SWA_KV_W
VS_LO, VS_HI = KS_HI, KS_HI + SWA_KV_W
KN_LO, KN_HI = VS_HI, VS_HI + NA_W
VN_LO, VN_HI = KN_HI, KN_HI + NA_W

LANE = 128
ROUTE_W = LANE
NA_QROWS = 4
NA_KROWS = NA_QROWS + NA_WIN_R
MOE_TILE = 256
ROW_BLOCK = 8
MOE_CHUNK = 256
MOE_LROWS = 2 * MOE_CHUNK + (ROW_BLOCK - 1) * N_EXPERTS
MOE_NBLK = MOE_LROWS // ROW_BLOCK
VMEM_LIMIT = 48 * 1024 * 1024


def _cparams(*sem):
    return pltpu.CompilerParams(dimension_semantics=sem, vmem_limit_bytes=VMEM_LIMIT)


def _dot(a, b):
    return jnp.dot(a, b, preferred_element_type=F32)


def _dot_tn(a, b):
    return lax.dot_general(a, b, (((0,), (0,)), ((), ())), preferred_element_type=F32)


def _dot_nt(a, b):
    return lax.dot_general(a, b, (((1,), (1,)), ((), ())), preferred_element_type=F32)


def _split_dot(a, w):
    a_hi = a.astype(BF16)
    a_lo = (a - a_hi.astype(F32)).astype(BF16)
    w_hi = w.astype(BF16)
    w_lo = (w - w_hi.astype(F32)).astype(BF16)
    return _dot(a_hi, w_hi) + (_dot(a_hi, w_lo) + _dot(a_lo, w_hi))


def _mod_kernel(c_ref, w_ref, b_ref, o_ref):
    c = c_ref[...]
    a = c * (1.0 / (1.0 + jnp.exp(-c)))
    o_ref[0] = _split_dot(a, w_ref[0]) + b_ref[0]


def _modulation(c_rows, w_mod, b_mod):
    depth, d, n6 = w_mod.shape
    r = c_rows.shape[0]
    tn = 1536
    return pl.pallas_call(
        _mod_kernel,
        out_shape=jax.ShapeDtypeStruct((depth, r, n6), F32),
        grid=(depth, n6 // tn),
        in_specs=[
            pl.BlockSpec((r, d), lambda l, j: (0, 0)),
            pl.BlockSpec((1, d, tn), lambda l, j: (l, 0, j)),
            pl.BlockSpec((1, 1, tn), lambda l, j: (l, 0, j)),
        ],
        out_specs=pl.BlockSpec((1, r, tn), lambda l, j: (l, 0, j)),
        compiler_params=_cparams("parallel", "parallel"),
        name="modulation",
    )(c_rows, w_mod, b_mod.reshape(depth, 1, n6))


def _rope_rows(t, cos_t, sin_t, n_heads):
    outs = []
    for h in range(n_heads):
        th = t[HEAD_DIM * h:HEAD_DIM * (h + 1)]
        sw = jnp.concatenate([th[16:32], th[0:16], th[48:64], th[32:48]], axis=0)
        outs.append(th * cos_t + sw * sin_t)
    return jnp.concatenate(outs, axis=0)


def _in_proj_kernel(x_ref, sh_ref, sc_ref, g_ref, wt_ref, bdt_ref, cos_ref, sin_ref, *outs, with_q, rope):
    xf = x_ref[0]
    ms = jnp.mean(xf * xf, axis=-1, keepdims=True)
    y = xf * lax.rsqrt(ms + EPS) * g_ref[...]
    h = y * (1.0 + sc_ref[0]) + sh_ref[0]
    pt = _dot_nt(wt_ref[...], h.astype(BF16))
    q_scale = HEAD_DIM ** -0.5
    if with_q:
        fz_ref, qs_ref, qn_ref, ks_ref, vs_ref, kn_ref, vn_ref = outs
        fz_ref[0] = _dot(bdt_ref[...], pt[F_LO:F_HI].astype(BF16)).astype(BF16)
        qs = pt[QS_LO:QS_HI]
        if rope:
            qs = _rope_rows(qs, cos_ref[...], sin_ref[...], SWA_HEADS)
        qs_ref[0] = (qs * q_scale).astype(BF16)
        qn_ref[0] = (pt[QN_LO:QN_HI] * q_scale).astype(BF16)
        off = 0
    else:
        ks_ref, vs_ref, kn_ref, vn_ref = outs
        off = KS_LO
    ks = pt[KS_LO - off:KS_HI - off]
    if rope:
        ks = _rope_rows(ks, cos_ref[...], sin_ref[...], SWA_KV_HEADS)
    ks_ref[0] = ks.astype(BF16)
    vs_ref[0] = pt[VS_LO - off:VS_HI - off].astype(BF16)
    kn_ref[0] = pt[KN_LO - off:KN_HI - off].astype(BF16)
    vn_ref[0] = pt[VN_LO - off:VN_HI - off].astype(BF16)


def _in_proj(x, shift, scale, gain, w_t, bd_t, cos_t, sin_t, *, with_q, rope, tm):
    b, t, d = x.shape
    nf = w_t.shape[0]
    rows = ([2 * FOURIER_WIDTH, SWA_Q_W, NA_W] if with_q else []) + [SWA_KV_W, SWA_KV_W, NA_W, NA_W]
    return pl.pallas_call(
        functools.partial(_in_proj_kernel, with_q=with_q, rope=rope),
        out_shape=[jax.ShapeDtypeStruct((b, r, t), BF16) for r in rows],
        grid=(b, t // tm),
        in_specs=[
            pl.BlockSpec((1, tm, d), lambda i, j: (i, j, 0)),
            pl.BlockSpec((1, 1, d), lambda i, j: (i, 0, 0)),
            pl.BlockSpec((1, 1, d), lambda i, j: (i, 0, 0)),
            pl.BlockSpec((1, d), lambda i, j: (0, 0)),
            pl.BlockSpec((nf, d), lambda i, j: (0, 0)),
            pl.BlockSpec(bd_t.shape, lambda i, j: (0, 0)),
            pl.BlockSpec((HEAD_DIM, tm), lambda i, j: (0, j)),
            pl.BlockSpec((HEAD_DIM, tm), lambda i, j: (0, j)),
        ],
        out_specs=[pl.BlockSpec((1, r, tm), lambda i, j: (i, 0, j)) for r in rows],
        compiler_params=_cparams("parallel", "parallel"),
        name="in_proj_q" if with_q else "in_proj_kv",
    )(x, shift, scale, gain, w_t, bd_t, cos_t, sin_t)


def _fourier_kernel(fz_ref, cn_ref, sn_ref, wft_ref, o_ref):
    zc = fz_ref[0, 0:FOURIER_WIDTH, :]
    zs = fz_ref[0, FOURIER_WIDTH:2 * FOURIER_WIDTH, :]
    y = _dot(zc, cn_ref[...]) - _dot(zs, sn_ref[...])
    o_ref[0] = _dot(wft_ref[...], y.astype(BF16)).astype(BF16)


def _fourier(fz, cn, sn, wf_t, *, tk):
    b, _, t = fz.shape
    return pl.pallas_call(
        _fourier_kernel,
        out_shape=jax.ShapeDtypeStruct((b, FOURIER_WIDTH, t), BF16),
        grid=(t // tk, b),
        in_specs=[
            pl.BlockSpec((1, 2 * FOURIER_WIDTH, t), lambda k, i: (i, 0, 0)),
            pl.BlockSpec((t, tk), lambda k, i: (0, k)),
            pl.BlockSpec((t, tk), lambda k, i: (0, k)),
            pl.BlockSpec((FOURIER_WIDTH, FOURIER_WIDTH), lambda k, i: (0, 0)),
        ],
        out_specs=pl.BlockSpec((1, FOURIER_WIDTH, tk), lambda k, i: (i, 0, k)),
        compiler_params=_cparams("parallel", "parallel"),
        name="fourier",
    )(fz, cn, sn, wf_t)


def _attend(q_t, k_parts, v_parts, biases, sink_row):
    logits = []
    for k_t, bias in zip(k_parts, biases):
        s = _dot_tn(k_t, q_t)
        logits.append(s if bias is None else s + bias)
    m = functools.reduce(jnp.maximum, [jnp.max(s, axis=0, keepdims=True) for s in logits])
    if sink_row is not None:
        m = jnp.maximum(m, sink_row)
    den = None
    acc = None
    for s, v_t in zip(logits, v_parts):
        p = jnp.exp(s - m)
        ps = jnp.sum(p, axis=0, keepdims=True)
        den = ps if den is None else den + ps
        o = _dot(v_t, p.astype(BF16))
        acc = o if acc is None else acc + o
    if sink_row is not None:
        den = den + jnp.exp(sink_row - m)
    return acc / den


def _swa_kernel(q_ref, k_ref, v_ref, kc_ref, vc_ref, mask_ref, sink_ref, o_ref, *, seq):
    nb = seq // SWA_BLOCK
    span = 3 * SWA_BLOCK
    kc = kc_ref[0]
    vc = vc_ref[0]
    sink_row = sink_ref[0]
    for n in range(nb):
        ks = min(max((n - 1) * SWA_BLOCK, 0), seq - span)
        mtype = 0 if n == 0 else (2 if n == nb - 1 else 1)
        q0 = n * SWA_BLOCK
        q_t = jnp.concatenate(
            [q_ref[0, HEAD_DIM * h:HEAD_DIM * (h + 1), q0:q0 + SWA_BLOCK] for h in range(SWA_GROUP)], axis=1)
        o = _attend(q_t, [k_ref[0, :, ks:ks + span], kc], [v_ref[0, :, ks:ks + span], vc],
                    [mask_ref[mtype], None], sink_row)
        for h in range(SWA_GROUP):
            o_ref[0, HEAD_DIM * h:HEAD_DIM * (h + 1), q0:q0 + SWA_BLOCK] = (
                o[:, SWA_BLOCK * h:SWA_BLOCK * (h + 1)].astype(BF16))


def _swa(qs, ks, vs, kc, vc, mask, sink_rows):
    b, _, t = qs.shape
    lc = kc.shape[2]
    gw = SWA_GROUP * HEAD_DIM
    span = 3 * SWA_BLOCK
    return pl.pallas_call(
        functools.partial(_swa_kernel, seq=t),
        out_shape=jax.ShapeDtypeStruct((b, SWA_Q_W, t), BF16),
        grid=(b, SWA_KV_HEADS),
        in_specs=[
            pl.BlockSpec((1, gw, t), lambda i, g: (i, g, 0)),
            pl.BlockSpec((1, HEAD_DIM, t), lambda i, g: (i, g, 0)),
            pl.BlockSpec((1, HEAD_DIM, t), lambda i, g: (i, g, 0)),
            pl.BlockSpec((1, HEAD_DIM, lc), lambda i, g: (i, g, 0)),
            pl.BlockSpec((1, HEAD_DIM, lc), lambda i, g: (i, g, 0)),
            pl.BlockSpec((3, span, SWA_GROUP * SWA_BLOCK), lambda i, g: (0, 0, 0)),
            pl.BlockSpec((1, 1, SWA_GROUP * SWA_BLOCK), lambda i, g: (g, 0, 0)),
        ],
        out_specs=pl.BlockSpec((1, gw, t), lambda i, g: (i, g, 0)),
        compiler_params=_cparams("parallel", "parallel"),
        name="swa",
    )(qs, ks, vs, kc, vc, mask, sink_rows)


def _na_bias_kernel(rpb_ref, o_ref, u_ref, *, total_rows):
    hd = pl.program_id(0)
    kc = lax.broadcasted_iota(jnp.int32, (GRID_W, LANE), 0)
    lane = lax.broadcasted_iota(jnp.int32, (GRID_W, LANE), 1)
    qc = lane % GRID_W
    dc = jnp.clip(kc - qc, -(NA_WIN_C - 1), NA_WIN_C - 1) + (NA_WIN_C - 1)
    c0 = jnp.clip(qc - NA_WIN_C // 2, 0, GRID_W - NA_WIN_C)
    valid_c = (kc >= c0) & (kc < c0 + NA_WIN_C)
    n_dr = 2 * NA_WIN_R - 1
    for dr in range(n_dr):
        u = jnp.full((GRID_W, LANE), NEG_INF, F32)
        for d in range(2 * NA_WIN_C - 1):
            u = jnp.where(valid_c & (dc == d), rpb_ref[hd, dr, d], u)
        u_ref[dr] = u
    n_rows = o_ref.shape[2] // GRID_W
    block_types = [(0, 0), (NA_QROWS, 0), (total_rows - NA_QROWS, total_rows - NA_KROWS)]
    neg = jnp.full((GRID_W, LANE), NEG_INF, F32)
    for t, (r_base, k_base) in enumerate(block_types):
        for kl in range(n_rows):
            kr = k_base + kl
            for lg in range(NA_QROWS // 2):
                halves = []
                for rq in (2 * lg, 2 * lg + 1):
                    r = r_base + rq
                    r0 = min(max(r - NA_WIN_R // 2, 0), total_rows - NA_WIN_R)
                    ok = r0 <= kr < r0 + NA_WIN_R
                    halves.append(u_ref[kr - r + NA_WIN_R - 1] if ok else neg)
                o_ref[0, t, GRID_W * kl:GRID_W * (kl + 1), LANE * lg:LANE * (lg + 1)] = jnp.where(
                    lane < GRID_W, halves[0], halves[1])


def _na_bias(rpb, seq):
    nh = rpb.shape[0]
    return pl.pallas_call(
        functools.partial(_na_bias_kernel, total_rows=seq // GRID_W),
        out_shape=jax.ShapeDtypeStruct((nh, 3, NA_KROWS * GRID_W, NA_QROWS * GRID_W), F32),
        grid=(nh,),
        in_specs=[pl.BlockSpec(memory_space=pltpu.SMEM)],
        out_specs=pl.BlockSpec((1, 3, NA_KROWS * GRID_W, NA_QROWS * GRID_W), lambda h: (h, 0, 0, 0)),
        scratch_shapes=[pltpu.VMEM((2 * NA_WIN_R - 1, GRID_W, LANE), F32)],
        compiler_params=_cparams("parallel"),
        name="na_bias",
    )(rpb)


def _na_kernel(q_ref, k_ref, v_ref, kc_ref, vc_ref, bias_ref, o_ref, *, seq):
    n_rows = seq // GRID_W
    qw = NA_QROWS * GRID_W
    kw = NA_KROWS * GRID_W
    kc = kc_ref[0]
    vc = vc_ref[0]
    nblk = n_rows // NA_QROWS
    for j in range(nblk):
        ks = min(max(j * NA_QROWS - NA_WIN_R // 2, 0), n_rows - NA_KROWS) * GRID_W
        btype = 0 if j == 0 else (2 if j == nblk - 1 else 1)
        q0 = j * qw
        o = _attend(q_ref[0, :, q0:q0 + qw], [k_ref[0, :, ks:ks + kw], kc], [v_ref[0, :, ks:ks + kw], vc],
                    [bias_ref[0, btype], None], None)
        o_ref[0, :, q0:q0 + qw] = o.astype(BF16)


def _na(qn, kn, vn, kc, vc, bias):
    b, _, t = qn.shape
    lc = kc.shape[2]
    head = lambda i, h: (i, h, 0)
    return pl.pallas_call(
        functools.partial(_na_kernel, seq=t),
        out_shape=jax.ShapeDtypeStruct((b, NA_W, t), BF16),
        grid=(b, NA_HEADS),
        in_specs=[
            pl.BlockSpec((1, HEAD_DIM, t), head),
            pl.BlockSpec((1, HEAD_DIM, t), head),
            pl.BlockSpec((1, HEAD_DIM, t), head),
            pl.BlockSpec((1, HEAD_DIM, lc), head),
            pl.BlockSpec((1, HEAD_DIM, lc), head),
            pl.BlockSpec((1,) + bias.shape[1:], lambda i, h: (h, 0, 0, 0)),
        ],
        out_specs=pl.BlockSpec((1, HEAD_DIM, t), head),
        compiler_params=_cparams("parallel", "parallel"),
        name="na",
    )(qn, kn, vn, kc, vc, bias)


def _ctx_attn_kernel(qs_ref, ks_ref, vs_ref, qn_ref, kn_ref, vn_ref, sink_ref, ys_ref, yn_ref):
    lc = qs_ref.shape[2]
    for g in range(SWA_KV_HEADS):
        q_t = jnp.concatenate(
            [qs_ref[0, HEAD_DIM * (SWA_GROUP * g + h):HEAD_DIM * (SWA_GROUP * g + h + 1), :] for h in range(SWA_GROUP)],
            axis=1)
        kv = slice(HEAD_DIM * g, HEAD_DIM * (g + 1))
        o = _attend(q_t, [ks_ref[0, kv, :]], [vs_ref[0, kv, :]], [None], sink_ref[g])
        for h in range(SWA_GROUP):
            hh = SWA_GROUP * g + h
            ys_ref[0, HEAD_DIM * hh:HEAD_DIM * (hh + 1), :] = o[:, lc * h:lc * (h + 1)].astype(BF16)
    for h in range(NA_HEADS):
        sl = slice(HEAD_DIM * h, HEAD_DIM * (h + 1))
        o = _attend(qn_ref[0, sl, :], [kn_ref[0, sl, :]], [vn_ref[0, sl, :]], [None], None)
        yn_ref[0, sl, :] = o.astype(BF16)


def _ctx_attn(qs, ks, vs, qn, kn, vn, sink_rows):
    b, _, lc = qs.shape
    full = lambda a: pl.BlockSpec((1,) + a.shape[1:], lambda i: (i, 0, 0))
    return pl.pallas_call(
        _ctx_attn_kernel,
        out_shape=[jax.ShapeDtypeStruct((b, SWA_Q_W, lc), BF16), jax.ShapeDtypeStruct((b, NA_W, lc), BF16)],
        grid=(b,),
        in_specs=[full(qs), full(ks), full(vs), full(qn), full(kn), full(vn),
                  pl.BlockSpec(sink_rows.shape, lambda i: (0, 0, 0))],
        out_specs=[pl.BlockSpec((1, SWA_Q_W, lc), lambda i: (i, 0, 0)),
                   pl.BlockSpec((1, NA_W, lc), lambda i: (i, 0, 0))],
        compiler_params=_cparams("parallel"),
        name="ctx_attn",
    )(qs, ks, vs, qn, kn, vn, sink_rows)


def _route(logits):
    lane = lax.broadcasted_iota(jnp.int32, logits.shape, 1)
    big = jnp.int32(ROUTE_W)
    gmask = lane < N_GROUPS
    gl = jnp.where(gmask, logits, NEG_INF)
    gmax = jnp.max(gl, axis=-1, keepdims=True)
    g_sel = jnp.min(jnp.where(gl == gmax, lane, big), axis=-1, keepdims=True)
    p_g = 1.0 / jnp.sum(jnp.where(gmask, jnp.exp(logits - gmax), 0.0), axis=-1, keepdims=True)
    lo = N_GROUPS + EXPERTS_PER_GROUP * g_sel
    emask = (lane >= lo) & (lane < lo + EXPERTS_PER_GROUP)
    el = jnp.where(emask, logits, NEG_INF)
    v1 = jnp.max(el, axis=-1, keepdims=True)
    i1 = jnp.min(jnp.where(el == v1, lane, big), axis=-1, keepdims=True)
    el2 = jnp.where(lane == i1, NEG_INF, el)
    v2 = jnp.max(el2, axis=-1, keepdims=True)
    i2 = jnp.min(jnp.where(el2 == v2, lane, big), axis=-1, keepdims=True)
    e21 = jnp.exp(v2 - v1)
    w1 = p_g / (1.0 + e21)
    w2 = p_g * e21 / (1.0 + e21)
    out = jnp.where(lane == 0, (i1 - N_GROUPS).astype(F32), 0.0)
    out = jnp.where(lane == 1, (i2 - N_GROUPS).astype(F32), out)
    out = jnp.where(lane == 2, w1, out)
    out = jnp.where(lane == 3, w2, out)
    return out


def _out_proj_kernel(x_ref, yf_ref, ys_ref, yn_ref, wo_ref, g1_ref, sh_ref, sc_ref, gn_ref, wr_ref, br_ref,
                     x1_ref, h2_ref, rt_ref):
    y = (_dot_tn(yf_ref[0], wo_ref[F_LO:F_HI, :]) + _dot_tn(ys_ref[0], wo_ref[QS_LO:QS_HI, :])
         + _dot_tn(yn_ref[0], wo_ref[QN_LO:QN_HI, :]))
    x1 = x_ref[0] + g1_ref[0] * y
    x1_ref[0] = x1
    ms = jnp.mean(x1 * x1, axis=-1, keepdims=True)
    h2 = (x1 * lax.rsqrt(ms + EPS) * gn_ref[...]) * (1.0 + sc_ref[0]) + sh_ref[0]
    h2_ref[0] = h2.astype(BF16)
    rt_ref[0] = _route(_split_dot(h2, wr_ref[...]) + br_ref[...])


def _out_proj(x, yf, ys, yn, w_out, gate, shift, scale, gain, w_route, b_route, *, tm):
    b, t, d = x.shape
    vec = pl.BlockSpec((1, 1, d), lambda i, j: (i, 0, 0))
    feat = lambda r: pl.BlockSpec((1, r, tm), lambda i, j: (i, 0, j))
    tok = lambda w: pl.BlockSpec((1, tm, w), lambda i, j: (i, j, 0))
    return pl.pallas_call(
        _out_proj_kernel,
        out_shape=[jax.ShapeDtypeStruct((b, t, d), F32), jax.ShapeDtypeStruct((b, t, d), BF16),
                   jax.ShapeDtypeStruct((b, t, ROUTE_W), F32)],
        grid=(b, t // tm),
        in_specs=[tok(d), feat(FOURIER_WIDTH), feat(SWA_Q_W), feat(NA_W),
                  pl.BlockSpec((MIX_WIDTH, d), lambda i, j: (0, 0)),
                  vec, vec, vec,
                  pl.BlockSpec((1, d), lambda i, j: (0, 0)),
                  pl.BlockSpec((d, ROUTE_W), lambda i, j: (0, 0)),
                  pl.BlockSpec((1, ROUTE_W), lambda i, j: (0, 0))],
        out_specs=[tok(d), tok(d), tok(ROUTE_W)],
        compiler_params=_cparams("parallel", "parallel"),
        name="out_proj",
    )(x, yf, ys, yn, w_out, gate, shift, scale, gain, w_route, b_route)


def _experts_kernel(te_ref, nt_ref, x_ref, wg_ref, wu_ref, wd_ref, y_ref, wg_s, wu_s, wd_s):
    i = pl.program_id(0)
    fresh = jnp.logical_or(i == 0, te_ref[i] != te_ref[jnp.maximum(i - 1, 0)])

    @pl.when(jnp.logical_and(fresh, i < nt_ref[0]))
    def _():
        wg_s[...] = wg_ref[0].astype(BF16)
        wu_s[...] = wu_ref[0].astype(BF16)
        wd_s[...] = wd_ref[0].astype(BF16)

    @pl.when(i < nt_ref[0])
    def _():
        x = x_ref[...].astype(BF16)
        g = _dot(x, wg_s[...])
        u = _dot(x, wu_s[...])
        hid = (g * (1.0 / (1.0 + jnp.exp(-g)))) * u
        y_ref[...] = _dot(hid.astype(BF16), wd_s[...])

    @pl.when(i >= nt_ref[0])
    def _():
        y_ref[...] = jnp.zeros(y_ref.shape, F32)


def _experts(xs, tile_expert, n_tiles, w_gate, w_up, w_down):
    r, d = xs.shape
    de = w_gate.shape[2]
    last = lambda i, te, nt: jnp.minimum(i, nt[0] - 1)
    return pl.pallas_call(
        _experts_kernel,
        out_shape=jax.ShapeDtypeStruct((r, d), F32),
        grid_spec=pltpu.PrefetchScalarGridSpec(
            num_scalar_prefetch=2,
            grid=(r // MOE_TILE,),
            in_specs=[
                pl.BlockSpec((MOE_TILE, d), lambda i, te, nt: (last(i, te, nt), 0)),
                pl.BlockSpec((1, d, de), lambda i, te, nt: (te[i], 0, 0)),
                pl.BlockSpec((1, d, de), lambda i, te, nt: (te[i], 0, 0)),
                pl.BlockSpec((1, de, d), lambda i, te, nt: (te[i], 0, 0)),
            ],
            out_specs=pl.BlockSpec((MOE_TILE, d), lambda i, te, nt: (i, 0)),
            scratch_shapes=[pltpu.VMEM((d, de), BF16), pltpu.VMEM((d, de), BF16), pltpu.VMEM((de, d), BF16)],
        ),
        compiler_params=_cparams("arbitrary"),
        name="experts",
    )(tile_expert, n_tiles, xs, w_gate, w_up, w_down)


def _chunk_copy(local, remote, sem, dst_ref, c, q, to_remote):
    loc = local.at[pl.ds(pl.multiple_of(q * ROW_BLOCK, ROW_BLOCK), ROW_BLOCK)]
    rem = remote.at[pl.ds(pl.multiple_of(dst_ref[c, q], ROW_BLOCK), ROW_BLOCK)]
    return pltpu.make_async_copy(loc, rem, sem) if to_remote else pltpu.make_async_copy(rem, loc, sem)


def _chunk_copies(local, remote, sem, dst_ref, nblk_ref, c, *, to_remote, wait):
    def body(q, carry):
        cp = _chunk_copy(local, remote, sem, dst_ref, c, q, to_remote)
        if wait:
            cp.wait()
        else:
            cp.start()
        return carry

    lax.fori_loop(0, nblk_ref[c], body, 0)


def _tail_copies(zero, remote, sem, tstart_ref, tn_ref, *, wait):
    def per_expert(e, carry):
        def body(q, inner):
            row = pl.multiple_of(tstart_ref[e] + q * ROW_BLOCK, ROW_BLOCK)
            cp = pltpu.make_async_copy(zero, remote.at[pl.ds(row, ROW_BLOCK)], sem)
            if wait:
                cp.wait()
            else:
                cp.start()
            return inner

        return lax.fori_loop(0, tn_ref[e], body, carry)

    lax.fori_loop(0, N_EXPERTS, per_expert, 0)


def _unused_tile_copies(zero, remote, sem, nt_ref, total_tiles, *, wait):
    def body(t, carry):
        row = pl.multiple_of(t * MOE_TILE, MOE_TILE)
        cp = pltpu.make_async_copy(zero, remote.at[pl.ds(row, MOE_TILE)], sem)
        if wait:
            cp.wait()
        else:
            cp.start()
        return carry

    lax.fori_loop(nt_ref[0], total_tiles, body, 0)


def _dispatch_kernel(dst_ref, nblk_ref, tstart_ref, tn_ref, nt_ref, *rest, part_chunks):
    n_parts = len(part_chunks)
    h_refs = rest[:n_parts]
    pos_ref, xs_ref, buf, zero, sem, zsem = rest[n_parts:]
    n_chunks = sum(part_chunks)
    c = pl.program_id(0)
    slot = c % 2
    row = lax.broadcasted_iota(jnp.int32, (MOE_LROWS, MOE_CHUNK), 0)
    hit = (row == pos_ref[0, 0:1, :]) | (row == pos_ref[0, 1:2, :])
    sel = jnp.where(hit, 1.0, 0.0).astype(BF16)
    first = 0
    for h_ref, n in zip(h_refs, part_chunks):
        @pl.when((c >= first) & (c < first + n))
        def _(h_ref=h_ref):
            buf[slot] = _dot(sel, h_ref[...])
        first += n
    _chunk_copies(buf.at[slot], xs_ref, sem.at[slot], dst_ref, nblk_ref, c, to_remote=True, wait=False)

    @pl.when(c > 0)
    def _():
        _chunk_copies(buf.at[1 - slot], xs_ref, sem.at[1 - slot], dst_ref, nblk_ref, c - 1,
                      to_remote=True, wait=True)

    @pl.when(c == n_chunks - 1)
    def _():
        total_tiles = xs_ref.shape[0] // MOE_TILE
        zero[...] = jnp.zeros(zero.shape, F32)
        zero8 = zero.at[pl.ds(0, ROW_BLOCK)]
        _tail_copies(zero8, xs_ref, zsem, tstart_ref, tn_ref, wait=False)
        _unused_tile_copies(zero, xs_ref, zsem, nt_ref, total_tiles, wait=False)
        _chunk_copies(buf.at[slot], xs_ref, sem.at[slot], dst_ref, nblk_ref, c, to_remote=True, wait=True)
        _tail_copies(zero8, xs_ref, zsem, tstart_ref, tn_ref, wait=True)
        _unused_tile_copies(zero, xs_ref, zsem, nt_ref, total_tiles, wait=True)


def _dispatch(h_parts, plan):
    d = h_parts[0].shape[1]
    part_chunks = tuple(h.shape[0] // MOE_CHUNK for h in h_parts)
    in_specs = []
    first = 0
    for n in part_chunks:
        in_specs.append(pl.BlockSpec(
            (MOE_CHUNK, d), lambda i, *_, first=first, n=n: (jnp.clip(i - first, 0, n - 1), 0)))
        first += n
    in_specs.append(pl.BlockSpec((1, 8, MOE_CHUNK), lambda i, *_: (i, 0, 0)))
    return pl.pallas_call(
        functools.partial(_dispatch_kernel, part_chunks=part_chunks),
        out_shape=jax.ShapeDtypeStruct((plan["rows"], d), F32),
        grid_spec=pltpu.PrefetchScalarGridSpec(
            num_scalar_prefetch=5,
            grid=(sum(part_chunks),),
            in_specs=in_specs,
            out_specs=pl.BlockSpec(memory_space=pl.ANY),
            scratch_shapes=[pltpu.VMEM((2, MOE_LROWS, d), F32), pltpu.VMEM((MOE_TILE, d), F32),
                            pltpu.SemaphoreType.DMA((2,)), pltpu.SemaphoreType.DMA(())],
        ),
        compiler_params=_cparams("arbitrary"),
        name="dispatch",
    )(plan["dst"], plan["nblk"], plan["tstart"], plan["tn"], plan["n_tiles"], *h_parts, plan["pos"])


def _combine_kernel(dst_ref, nblk_ref, x1_ref, rt_ref, pos_ref, g2_ref, gf_ref, ys_ref, o_ref, buf, sem,
                    *, c0, n_chunks, final_norm):
    i = pl.program_id(0)
    c = i + c0
    slot = i % 2
    gather = functools.partial(_chunk_copies, remote=ys_ref, dst_ref=dst_ref, nblk_ref=nblk_ref, to_remote=False)

    @pl.when(i == 0)
    def _():
        buf[...] = jnp.zeros(buf.shape, F32)
        gather(buf.at[slot], sem=sem.at[slot], c=c, wait=False)

    @pl.when(i + 1 < n_chunks)
    def _():
        gather(buf.at[1 - slot], sem=sem.at[1 - slot], c=c + 1, wait=False)

    gather(buf.at[slot], sem=sem.at[slot], c=c, wait=True)
    yl = buf[slot].astype(BF16)
    row = lax.broadcasted_iota(jnp.int32, (MOE_LROWS, MOE_CHUNK), 0)
    y1 = _dot_tn(jnp.where(row == pos_ref[0, 0:1, :], 1.0, 0.0).astype(BF16), yl)
    y2 = _dot_tn(jnp.where(row == pos_ref[0, 1:2, :], 1.0, 0.0).astype(BF16), yl)
    rt = rt_ref[0]
    x2 = x1_ref[0] + g2_ref[0] * (rt[:, 2:3] * y1 + rt[:, 3:4] * y2)
    if final_norm:
        ms = jnp.mean(x2 * x2, axis=-1, keepdims=True)
        x2 = x2 * lax.rsqrt(ms + EPS) * gf_ref[...]
    o_ref[0] = x2


def _combine(x1, route, plan, gate, g_final, ys, *, c0, final_norm):
    b, t, d = x1.shape
    per_batch = t // MOE_CHUNK
    n_chunks = b * per_batch
    tok = lambda w: pl.BlockSpec((1, MOE_CHUNK, w), lambda i, *_: (i // per_batch, i % per_batch, 0))
    return pl.pallas_call(
        functools.partial(_combine_kernel, c0=c0, n_chunks=n_chunks, final_norm=final_norm),
        out_shape=jax.ShapeDtypeStruct((b, t, d), F32),
        grid_spec=pltpu.PrefetchScalarGridSpec(
            num_scalar_prefetch=2,
            grid=(n_chunks,),
            in_specs=[tok(d), tok(ROUTE_W),
                      pl.BlockSpec((1, 8, MOE_CHUNK), lambda i, *_: (i + c0, 0, 0)),
                      pl.BlockSpec((1, 1, d), lambda i, *_: (i // per_batch, 0, 0)),
                      pl.BlockSpec((1, d), lambda i, *_: (0, 0)),
                      pl.BlockSpec(memory_space=pl.ANY)],
            out_specs=tok(d),
            scratch_shapes=[pltpu.VMEM((2, MOE_LROWS, d), F32), pltpu.SemaphoreType.DMA((2,))],
        ),
        compiler_params=_cparams("arbitrary"),
        name="combine",
    )(plan["dst"], plan["nblk"], x1, route, plan["pos"], gate, g_final, ys)


def _rope_tables(seq):
    half = HEAD_DIM // 4
    t = jnp.arange(seq)
    rows, cols = t // GRID_W, t % GRID_W
    freqs = ROPE_BASE ** (-jnp.arange(half, dtype=F32) / half)

    def cs(pos):
        ang = pos.astype(F32)[None, :] * freqs[:, None]
        return jnp.cos(ang), jnp.sin(ang)

    cr, sr = cs(rows)
    cc, sc = cs(cols)
    return jnp.concatenate([cr, cr, cc, cc], axis=0), jnp.concatenate([-sr, sr, -sc, sc], axis=0)


def _dft_tables(n):
    idx = jnp.arange(n)
    ang = ((idx[:, None] * idx[None, :]) % n).astype(F32) * (2.0 * jnp.pi / n)
    return jnp.cos(ang), jnp.sin(ang)


def _channel_dft_t():
    c, s = _dft_tables(FOURIER_GROUP_DIM)
    eye = jnp.eye(FOURIER_GROUPS, dtype=F32)
    scale = FOURIER_GROUP_DIM ** -0.5
    return jnp.concatenate([jnp.kron(eye, c), jnp.kron(eye, s)], axis=0) * scale


def _position_dft(n):
    c, s = _dft_tables(n)
    scale = n ** -0.5
    return (c * scale).astype(BF16), (s * scale).astype(BF16)


def _swa_mask(seq):
    span = 3 * SWA_BLOCK
    nb = seq // SWA_BLOCK
    kk = jnp.arange(span)[:, None]
    q = jnp.arange(SWA_BLOCK)[None, :]
    tiles = []
    for n, ks in ((0, 0), (1, 0), (nb - 1, seq - span)):
        ok = jnp.abs(ks + kk - (n * SWA_BLOCK + q)) <= SWA_WINDOW
        tiles.append(jnp.tile(jnp.where(ok, 0.0, NEG_INF).astype(F32), (1, SWA_GROUP)))
    return jnp.stack(tiles)


def _sink_rows(sink, width):
    return jnp.repeat(sink.astype(F32).reshape(SWA_KV_HEADS, SWA_GROUP), width, axis=1)[:, None, :]


def _moe_plan(route2d):
    i32 = jnp.int32
    n = route2d.shape[0]
    nc = n // MOE_CHUNK
    ids = route2d[:, 0:2].astype(i32)
    onehot = (ids[:, :, None] == jnp.arange(N_EXPERTS, dtype=i32)).astype(i32)
    onehot = onehot.reshape(nc, 2 * MOE_CHUNK, N_EXPERTS)
    rank = jnp.sum((jnp.cumsum(onehot, axis=1) - onehot) * onehot, axis=-1)
    cnt = jnp.sum(onehot, axis=1)
    run = (cnt + ROW_BLOCK - 1) // ROW_BLOCK * ROW_BLOCK
    lo = jnp.cumsum(run, axis=1) - run
    pos = (jnp.sum(onehot * lo[:, None, :], axis=-1) + rank).reshape(nc, MOE_CHUNK, 2)
    pos = jnp.concatenate([jnp.swapaxes(pos, 1, 2), jnp.full((nc, 6, MOE_CHUNK), -1, i32)], axis=1)
    seg = jnp.sum(run, axis=0)
    padded = (seg + MOE_TILE - 1) // MOE_TILE * MOE_TILE
    start = jnp.cumsum(padded) - padded
    off = start[None, :] + jnp.cumsum(run, axis=0) - run
    q = jnp.arange(MOE_NBLK, dtype=i32)[None, :, None] * ROW_BLOCK
    inside = (lo[:, None, :] <= q) & (q < (lo + run)[:, None, :])
    dst = jnp.sum(jnp.where(inside, off[:, None, :] + q - lo[:, None, :], 0), axis=-1)
    nblk = jnp.sum(run, axis=1) // ROW_BLOCK
    ends = start + padded
    rows = -(-(2 * n + (ROW_BLOCK - 1) * N_EXPERTS * nc + (MOE_TILE - 1) * N_EXPERTS) // MOE_TILE) * MOE_TILE
    tile_row = jnp.arange(rows // MOE_TILE, dtype=i32) * MOE_TILE
    te = jnp.minimum(jnp.sum((ends[None, :] <= tile_row[:, None]).astype(i32), axis=1), N_EXPERTS - 1)
    n_tiles = ends[-1] // MOE_TILE
    te_last = jnp.sum(jnp.where(tile_row == (n_tiles - 1) * MOE_TILE, te, 0))
    te = jnp.where(tile_row < ends[-1], te, te_last)
    return {"pos": pos, "dst": dst.astype(i32), "nblk": nblk.astype(i32), "tstart": (start + seg).astype(i32),
            "tn": ((padded - seg) // ROW_BLOCK).astype(i32), "te": te.astype(i32),
            "n_tiles": n_tiles.astype(i32).reshape(1), "rows": rows}


def kernel(x, c, ctx, c_ctx, w_mod, b_mod, g_norm1, g_norm2, w_in, w_four, w_out, swa_sink, na_rpb,
           w_route_group, b_route_group, w_route_expert, b_route_expert, w_exp_gate, w_exp_up,
           w_exp_down, g_final):
    b, s, d = x.shape
    lc = ctx.shape[1]
    depth = w_mod.shape[0]
    tm = 512

    c_rows = jnp.concatenate([c, c_ctx[None, :], jnp.zeros((7, d), F32)], axis=0)
    mod = _modulation(c_rows, w_mod, b_mod)

    cos_t, sin_t = _rope_tables(s)
    cos_c, sin_c = cos_t[:, :lc], sin_t[:, :lc]
    bd_t = _channel_dft_t().astype(BF16)
    cn, sn = _position_dft(s)
    cn_c, sn_c = _position_dft(lc)
    mask = _swa_mask(s)
    pad = jnp.zeros((d, ROUTE_W - N_GROUPS - N_EXPERTS), F32)

    xc = ctx
    for layer in range(depth):
        with_ctx_out = layer < depth - 1
        lat = [mod[layer, :b, i * d:(i + 1) * d][:, None, :] for i in range(6)]
        cx = [jnp.broadcast_to(mod[layer, b, i * d:(i + 1) * d][None, None, :], (b, 1, d)) for i in range(6)]
        sh1, sc1, g1, sh2, sc2, g2 = lat
        shc1, scc1, gc1, shc2, scc2, gc2 = cx
        gn1 = g_norm1[layer][None, :]
        gn2 = g_norm2[layer][None, :]
        w_t = w_in[layer].T.astype(BF16)
        wf_t = w_four[layer].T.astype(BF16)
        wo = w_out[layer].astype(BF16)
        w_r = jnp.concatenate([w_route_group[layer], w_route_expert[layer], pad], axis=1)
        b_r = jnp.concatenate([b_route_group[layer], b_route_expert[layer],
                               jnp.zeros((ROUTE_W - N_GROUPS - N_EXPERTS,), F32)])[None, :]
        sink_lat = _sink_rows(swa_sink[layer], SWA_BLOCK)
        sink_ctx = _sink_rows(swa_sink[layer], lc)

        fz, qs, qn, ks, vs, kn, vn = _in_proj(x, sh1, sc1, gn1, w_t, bd_t, cos_t, sin_t,
                                              with_q=True, rope=True, tm=tm)
        if with_ctx_out:
            fz_c, qs_c, qn_c, ks_c, vs_c, kn_c, vn_c = _in_proj(xc, shc1, scc1, gn1, w_t, bd_t, cos_c, sin_c,
                                                                with_q=True, rope=False, tm=lc)
        else:
            ks_c, vs_c, kn_c, vn_c = _in_proj(xc, shc1, scc1, gn1, w_t[KS_LO:], bd_t, cos_c, sin_c,
                                              with_q=False, rope=False, tm=lc)

        yf = _fourier(fz, cn, sn, wf_t, tk=512)
        ys = _swa(qs, ks, vs, ks_c, vs_c, mask, sink_lat)
        yn = _na(qn, kn, vn, kn_c, vn_c, _na_bias(na_rpb[layer], s))
        x1, h2, route = _out_proj(x, yf, ys, yn, wo, g1, sh2, sc2, gn2, w_r, b_r, tm=tm)

        moe_w = (w_exp_gate[layer], w_exp_up[layer], w_exp_down[layer])
        if with_ctx_out:
            yf_c = _fourier(fz_c, cn_c, sn_c, wf_t, tk=lc)
            ys_c, yn_c = _ctx_attn(qs_c, ks_c, vs_c, qn_c, kn_c, vn_c, sink_ctx)
            xc1, hc2, route_c = _out_proj(xc, yf_c, ys_c, yn_c, wo, gc1, shc2, scc2, gn2, w_r, b_r, tm=lc)
            n_lat = b * s
            lat_chunks = n_lat // MOE_CHUNK
            plan = _moe_plan(jnp.concatenate([route.reshape(n_lat, ROUTE_W),
                                              route_c.reshape(b * lc, ROUTE_W)], axis=0))
            xs = _dispatch([h2.reshape(n_lat, d), hc2.reshape(b * lc, d)], plan)
            ye = _experts(xs, plan["te"], plan["n_tiles"], *moe_w)
            x = _combine(x1, route, plan, g2, g_final[None, :], ye, c0=0, final_norm=False)
            xc = _combine(xc1, route_c, plan, gc2, g_final[None, :], ye, c0=lat_chunks, final_norm=False)
        else:
            plan = _moe_plan(route.reshape(b * s, ROUTE_W))
            xs = _dispatch([h2.reshape(b * s, d)], plan)
            ye = _experts(xs, plan["te"], plan["n_tiles"], *moe_w)
            x = _combine(x1, route, plan, g2, g_final[None, :], ye, c0=0, final_norm=True)
    return x
```

```python
import functools

import jax
import jax.numpy as jnp
from jax import lax
from jax.experimental import pallas as pl
from jax.experimental.pallas import tpu as pltpu

F32 = jnp.float32
BF16 = jnp.bfloat16

D_MODEL = 1024
GRID_W = 64
HEAD_DIM = 64
FOURIER_WIDTH = D_MODEL // 4
FOURIER_GROUPS = 4
FOURIER_GROUP_DIM = FOURIER_WIDTH // FOURIER_GROUPS
SWA_HEADS = (3 * D_MODEL // 8) // HEAD_DIM
SWA_KV_HEADS = 2
SWA_GROUP = SWA_HEADS // SWA_KV_HEADS
SWA_WINDOW = 128
SWA_BLOCK = 128
NA_HEADS = (3 * D_MODEL // 8) // HEAD_DIM
NA_WIN_R = 8
NA_WIN_C = 16
ROPE_BASE = 10000.0
N_GROUPS = 4
EXPERTS_PER_GROUP = 8
N_EXPERTS = N_GROUPS * EXPERTS_PER_GROUP
D_EXPERT = D_MODEL // 2
EPS = 1e-6
NEG_INF = -1e30

SWA_Q_W = SWA_HEADS * HEAD_DIM
SWA_KV_W = SWA_KV_HEADS * HEAD_DIM
NA_W = NA_HEADS * HEAD_DIM
MIX_WIDTH = FOURIER_WIDTH + SWA_Q_W + NA_W
Q_COLS = MIX_WIDTH
IN_COLS = 2 * MIX_WIDTH

F_LO, F_HI = 0, FOURIER_WIDTH
QS_LO, QS_HI = F_HI, F_HI + SWA_Q_W
QN_LO, QN_HI = QS_HI, QS_HI + NA_W
KS_LO, KS_HI = QN_HI, QN_HI + SWA_KV_W
VS_LO, VS_HI = KS_HI, KS_HI + SWA_KV_W
KN_LO, KN_HI = VS_HI, VS_HI + NA_W
VN_LO, VN_HI = KN_HI, KN_HI + NA_W

LANE = 128
ROUTE_W = LANE
NA_QROWS = 4
NA_KROWS = NA_QROWS + NA_WIN_R
MOE_TILE = 256
ROW_BLOCK = 8
MOE_CHUNK = 256
MOE_LROWS = 2 * MOE_CHUNK + (ROW_BLOCK - 1) * N_EXPERTS
MOE_NBLK = MOE_LROWS // ROW_BLOCK
VMEM_LIMIT = 48 * 1024 * 1024


def _cparams(*sem):
    return pltpu.CompilerParams(dimension_semantics=sem, vmem_limit_bytes=VMEM_LIMIT)


def _dot(a, b):
    return jnp.dot(a, b, preferred_element_type=F32)


def _dot_tn(a, b):
    return lax.dot_general(a, b, (((0,), (0,)), ((), ())), preferred_element_type=F32)


def _dot_nt(a, b):
    return lax.dot_general(a, b, (((1,), (1,)), ((), ())), preferred_element_type=F32)


def _split_dot(a, w):
    a_hi = a.astype(BF16)
    a_lo = (a - a_hi.astype(F32)).astype(BF16)
    w_hi = w.astype(BF16)
    w_lo = (w - w_hi.astype(F32)).astype(BF16)
    return _dot(a_hi, w_hi) + (_dot(a_hi, w_lo) + _dot(a_lo, w_hi))


def _mod_kernel(c_ref, w_ref, b_ref, o_ref):
    c = c_ref[...]
    a = c * (1.0 / (1.0 + jnp.exp(-c)))
    o_ref[0] = _split_dot(a, w_ref[0]) + b_ref[0]


def _modulation(c_rows, w_mod, b_mod):
    depth, d, n6 = w_mod.shape
    r = c_rows.shape[0]
    tn = 1536
    return pl.pallas_call(
        _mod_kernel,
        out_shape=jax.ShapeDtypeStruct((depth, r, n6), F32),
        grid=(depth, n6 // tn),
        in_specs=[
            pl.BlockSpec((r, d), lambda l, j: (0, 0)),
            pl.BlockSpec((1, d, tn), lambda l, j: (l, 0, j)),
            pl.BlockSpec((1, 1, tn), lambda l, j: (l, 0, j)),
        ],
        out_specs=pl.BlockSpec((1, r, tn), lambda l, j: (l, 0, j)),
        compiler_params=_cparams("parallel", "parallel"),
        name="modulation",
    )(c_rows, w_mod, b_mod.reshape(depth, 1, n6))


def _rope_rows(t, cos_t, sin_t, n_heads):
    outs = []
    for h in range(n_heads):
        th = t[HEAD_DIM * h:HEAD_DIM * (h + 1)]
        sw = jnp.concatenate([th[16:32], th[0:16], th[48:64], th[32:48]], axis=0)
        outs.append(th * cos_t + sw * sin_t)
    return jnp.concatenate(outs, axis=0)


def _in_proj_kernel(x_ref, sh_ref, sc_ref, g_ref, wt_ref, bdt_ref, cos_ref, sin_ref, *outs, with_q, rope):
    xf = x_ref[0]
    ms = jnp.mean(xf * xf, axis=-1, keepdims=True)
    y = xf * lax.rsqrt(ms + EPS) * g_ref[...]
    h = y * (1.0 + sc_ref[0]) + sh_ref[0]
    pt = _dot_nt(wt_ref[...], h.astype(BF16))
    q_scale = HEAD_DIM ** -0.5
    if with_q:
        fz_ref, qs_ref, qn_ref, ks_ref, vs_ref, kn_ref, vn_ref = outs
        fz_ref[0] = _dot(bdt_ref[...], pt[F_LO:F_HI].astype(BF16)).astype(BF16)
        qs = pt[QS_LO:QS_HI]
        if rope:
            qs = _rope_rows(qs, cos_ref[...], sin_ref[...], SWA_HEADS)
        qs_ref[0] = (qs * q_scale).astype(BF16)
        qn_ref[0] = (pt[QN_LO:QN_HI] * q_scale).astype(BF16)
        off = 0
    else:
        ks_ref, vs_ref, kn_ref, vn_ref = outs
        off = KS_LO
    ks = pt[KS_LO - off:KS_HI - off]
    if rope:
        ks = _rope_rows(ks, cos_ref[...], sin_ref[...], SWA_KV_HEADS)
    ks_ref[0] = ks.astype(BF16)
    vs_ref[0] = pt[VS_LO - off:VS_HI - off].astype(BF16)
    kn_ref[0] = pt[KN_LO - off:KN_HI - off].astype(BF16)
    vn_ref[0] = pt[VN_LO - off:VN_HI - off].astype(BF16)


def _in_proj(x, shift, scale, gain, w_t, bd_t, cos_t, sin_t, *, with_q, rope, tm):
    b, t, d = x.shape
    nf = w_t.shape[0]
    rows = ([2 * FOURIER_WIDTH, SWA_Q_W, NA_W] if with_q else []) + [SWA_KV_W, SWA_KV_W, NA_W, NA_W]
    return pl.pallas_call(
        functools.partial(_in_proj_kernel, with_q=with_q, rope=rope),
        out_shape=[jax.ShapeDtypeStruct((b, r, t), BF16) for r in rows],
        grid=(b, t // tm),
        in_specs=[
            pl.BlockSpec((1, tm, d), lambda i, j: (i, j, 0)),
            pl.BlockSpec((1, 1, d), lambda i, j: (i, 0, 0)),
            pl.BlockSpec((1, 1, d), lambda i, j: (i, 0, 0)),
            pl.BlockSpec((1, d), lambda i, j: (0, 0)),
            pl.BlockSpec((nf, d), lambda i, j: (0, 0)),
            pl.BlockSpec(bd_t.shape, lambda i, j: (0, 0)),
            pl.BlockSpec((HEAD_DIM, tm), lambda i, j: (0, j)),
            pl.BlockSpec((HEAD_DIM, tm), lambda i, j: (0, j)),
        ],
        out_specs=[pl.BlockSpec((1, r, tm), lambda i, j: (i, 0, j)) for r in rows],
        compiler_params=_cparams("parallel", "parallel"),
        name="in_proj_q" if with_q else "in_proj_kv",
    )(x, shift, scale, gain, w_t, bd_t, cos_t, sin_t)


def _fourier_kernel(fz_ref, cn_ref, sn_ref, wft_ref, o_ref):
    zc = fz_ref[0, 0:FOURIER_WIDTH, :]
    zs = fz_ref[0, FOURIER_WIDTH:2 * FOURIER_WIDTH, :]
    y = _dot(zc, cn_ref[...]) - _dot(zs, sn_ref[...])
    o_ref[0] = _dot(wft_ref[...], y.astype(BF16)).astype(BF16)


def _fourier(fz, cn, sn, wf_t, *, tk):
    b, _, t = fz.shape
    return pl.pallas_call(
        _fourier_kernel,
        out_shape=jax.ShapeDtypeStruct((b, FOURIER_WIDTH, t), BF16),
        grid=(t // tk, b),
        in_specs=[
            pl.BlockSpec((1, 2 * FOURIER_WIDTH, t), lambda k, i: (i, 0, 0)),
            pl.BlockSpec((t, tk), lambda k, i: (0, k)),
            pl.BlockSpec((t, tk), lambda k, i: (0, k)),
            pl.BlockSpec((FOURIER_WIDTH, FOURIER_WIDTH), lambda k, i: (0, 0)),
        ],
        out_specs=pl.BlockSpec((1, FOURIER_WIDTH, tk), lambda k, i: (i, 0, k)),
        compiler_params=_cparams("parallel", "parallel"),
        name="fourier",
    )(fz, cn, sn, wf_t)


def _attend(q_t, k_parts, v_parts, biases, sink_row):
    logits = []
    for k_t, bias in zip(k_parts, biases):
        s = _dot_tn(k_t, q_t)
        logits.append(s if bias is None else s + bias)
    m = functools.reduce(jnp.maximum, [jnp.max(s, axis=0, keepdims=True) for s in logits])
    if sink_row is not None:
        m = jnp.maximum(m, sink_row)
    den = None
    acc = None
    for s, v_t in zip(logits, v_parts):
        p = jnp.exp(s - m)
        ps = jnp.sum(p, axis=0, keepdims=True)
        den = ps if den is None else den + ps
        o = _dot(v_t, p.astype(BF16))
        acc = o if acc is None else acc + o
    if sink_row is not None:
        den = den + jnp.exp(sink_row - m)
    return acc / den


def _swa_kernel(q_ref, k_ref, v_ref, kc_ref, vc_ref, mask_ref, sink_ref, o_ref, *, seq):
    nb = seq // SWA_BLOCK
    span = 3 * SWA_BLOCK
    kc = kc_ref[0]
    vc = vc_ref[0]
    sink_row = sink_ref[0]
    for n in range(nb):
        ks = min(max((n - 1) * SWA_BLOCK, 0), seq - span)
        mtype = 0 if n == 0 else (2 if n == nb - 1 else 1)
        q0 = n * SWA_BLOCK
        q_t = jnp.concatenate(
            [q_ref[0, HEAD_DIM * h:HEAD_DIM * (h + 1), q0:q0 + SWA_BLOCK] for h in range(SWA_GROUP)], axis=1)
        o = _attend(q_t, [k_ref[0, :, ks:ks + span], kc], [v_ref[0, :, ks:ks + span], vc],
                    [mask_ref[mtype], None], sink_row)
        for h in range(SWA_GROUP):
            o_ref[0, HEAD_DIM * h:HEAD_DIM * (h + 1), q0:q0 + SWA_BLOCK] = (
                o[:, SWA_BLOCK * h:SWA_BLOCK * (h + 1)].astype(BF16))


def _swa(qs, ks, vs, kc, vc, mask, sink_rows):
    b, _, t = qs.shape
    lc = kc.shape[2]
    gw = SWA_GROUP * HEAD_DIM
    span = 3 * SWA_BLOCK
    return pl.pallas_call(
        functools.partial(_swa_kernel, seq=t),
        out_shape=jax.ShapeDtypeStruct((b, SWA_Q_W, t), BF16),
        grid=(b, SWA_KV_HEADS),
        in_specs=[
            pl.BlockSpec((1, gw, t), lambda i, g: (i, g, 0)),
            pl.BlockSpec((1, HEAD_DIM, t), lambda i, g: (i, g, 0)),
            pl.BlockSpec((1, HEAD_DIM, t), lambda i, g: (i, g, 0)),
            pl.BlockSpec((1, HEAD_DIM, lc), lambda i, g: (i, g, 0)),
            pl.BlockSpec((1, HEAD_DIM, lc), lambda i, g: (i, g, 0)),
            pl.BlockSpec((3, span, SWA_GROUP * SWA_BLOCK), lambda i, g: (0, 0, 0)),
            pl.BlockSpec((1, 1, SWA_GROUP * SWA_BLOCK), lambda i, g: (g, 0, 0)),
        ],
        out_specs=pl.BlockSpec((1, gw, t), lambda i, g: (i, g, 0)),
        compiler_params=_cparams("parallel", "parallel"),
        name="swa",
    )(qs, ks, vs, kc, vc, mask, sink_rows)


def _na_bias_kernel(rpb_ref, o_ref, u_ref, *, total_rows):
    hd = pl.program_id(0)
    kc = lax.broadcasted_iota(jnp.int32, (GRID_W, LANE), 0)
    lane = lax.broadcasted_iota(jnp.int32, (GRID_W, LANE), 1)
    qc = lane % GRID_W
    dc = jnp.clip(kc - qc, -(NA_WIN_C - 1), NA_WIN_C - 1) + (NA_WIN_C - 1)
    c0 = jnp.clip(qc - NA_WIN_C // 2, 0, GRID_W - NA_WIN_C)
    valid_c = (kc >= c0) & (kc < c0 + NA_WIN_C)
    n_dr = 2 * NA_WIN_R - 1
    for dr in range(n_dr):
        u = jnp.full((GRID_W, LANE), NEG_INF, F32)
        for d in range(2 * NA_WIN_C - 1):
            u = jnp.where(valid_c & (dc == d), rpb_ref[hd, dr, d], u)
        u_ref[dr] = u
    n_rows = o_ref.shape[2] // GRID_W
    block_types = [(0, 0), (NA_QROWS, 0), (total_rows - NA_QROWS, total_rows - NA_KROWS)]
    neg = jnp.full((GRID_W, LANE), NEG_INF, F32)
    for t, (r_base, k_base) in enumerate(block_types):
        for kl in range(n_rows):
            kr = k_base + kl
            for lg in range(NA_QROWS // 2):
                halves = []
                for rq in (2 * lg, 2 * lg + 1):
                    r = r_base + rq
                    r0 = min(max(r - NA_WIN_R // 2, 0), total_rows - NA_WIN_R)
                    ok = r0 <= kr < r0 + NA_WIN_R
                    halves.append(u_ref[kr - r + NA_WIN_R - 1] if ok else neg)
                o_ref[0, t, GRID_W * kl:GRID_W * (kl + 1), LANE * lg:LANE * (lg + 1)] = jnp.where(
                    lane < GRID_W, halves[0], halves[1])


def _na_bias(rpb, seq):
    nh = rpb.shape[0]
    return pl.pallas_call(
        functools.partial(_na_bias_kernel, total_rows=seq // GRID_W),
        out_shape=jax.ShapeDtypeStruct((nh, 3, NA_KROWS * GRID_W, NA_QROWS * GRID_W), F32),
        grid=(nh,),
        in_specs=[pl.BlockSpec(memory_space=pltpu.SMEM)],
        out_specs=pl.BlockSpec((1, 3, NA_KROWS * GRID_W, NA_QROWS * GRID_W), lambda h: (h, 0, 0, 0)),
        scratch_shapes=[pltpu.VMEM((2 * NA_WIN_R - 1, GRID_W, LANE), F32)],
        compiler_params=_cparams("parallel"),
        name="na_bias",
    )(rpb)


def _na_kernel(q_ref, k_ref, v_ref, kc_ref, vc_ref, bias_ref, o_ref, *, seq):
    n_rows = seq // GRID_W
    qw = NA_QROWS * GRID_W
    kw = NA_KROWS * GRID_W
    kc = kc_ref[0]
    vc = vc_ref[0]
    nblk = n_rows // NA_QROWS
    for j in range(nblk):
        ks = min(max(j * NA_QROWS - NA_WIN_R // 2, 0), n_rows - NA_KROWS) * GRID_W
        btype = 0 if j == 0 else (2 if j == nblk - 1 else 1)
        q0 = j * qw
        o = _attend(q_ref[0, :, q0:q0 + qw], [k_ref[0, :, ks:ks + kw], kc], [v_ref[0, :, ks:ks + kw], vc],
                    [bias_ref[0, btype], None], None)
        o_ref[0, :, q0:q0 + qw] = o.astype(BF16)


def _na(qn, kn, vn, kc, vc, bias):
    b, _, t = qn.shape
    lc = kc.shape[2]
    head = lambda i, h: (i, h, 0)
    return pl.pallas_call(
        functools.partial(_na_kernel, seq=t),
        out_shape=jax.ShapeDtypeStruct((b, NA_W, t), BF16),
        grid=(b, NA_HEADS),
        in_specs=[
            pl.BlockSpec((1, HEAD_DIM, t), head),
            pl.BlockSpec((1, HEAD_DIM, t), head),
            pl.BlockSpec((1, HEAD_DIM, t), head),
            pl.BlockSpec((1, HEAD_DIM, lc), head),
            pl.BlockSpec((1, HEAD_DIM, lc), head),
            pl.BlockSpec((1,) + bias.shape[1:], lambda i, h: (h, 0, 0, 0)),
        ],
        out_specs=pl.BlockSpec((1, HEAD_DIM, t), head),
        compiler_params=_cparams("parallel", "parallel"),
        name="na",
    )(qn, kn, vn, kc, vc, bias)


def _ctx_attn_kernel(qs_ref, ks_ref, vs_ref, qn_ref, kn_ref, vn_ref, sink_ref, ys_ref, yn_ref):
    lc = qs_ref.shape[2]
    for g in range(SWA_KV_HEADS):
        q_t = jnp.concatenate(
            [qs_ref[0, HEAD_DIM * (SWA_GROUP * g + h):HEAD_DIM * (SWA_GROUP * g + h + 1), :] for h in range(SWA_GROUP)],
            axis=1)
        kv = slice(HEAD_DIM * g, HEAD_DIM * (g + 1))
        o = _attend(q_t, [ks_ref[0, kv, :]], [vs_ref[0, kv, :]], [None], sink_ref[g])
        for h in range(SWA_GROUP):
            hh = SWA_GROUP * g + h
            ys_ref[0, HEAD_DIM * hh:HEAD_DIM * (hh + 1), :] = o[:, lc * h:lc * (h + 1)].astype(BF16)
    for h in range(NA_HEADS):
        sl = slice(HEAD_DIM * h, HEAD_DIM * (h + 1))
        o = _attend(qn_ref[0, sl, :], [kn_ref[0, sl, :]], [vn_ref[0, sl, :]], [None], None)
        yn_ref[0, sl, :] = o.astype(BF16)


def _ctx_attn(qs, ks, vs, qn, kn, vn, sink_rows):
    b, _, lc = qs.shape
    full = lambda a: pl.BlockSpec((1,) + a.shape[1:], lambda i: (i, 0, 0))
    return pl.pallas_call(
        _ctx_attn_kernel,
        out_shape=[jax.ShapeDtypeStruct((b, SWA_Q_W, lc), BF16), jax.ShapeDtypeStruct((b, NA_W, lc), BF16)],
        grid=(b,),
        in_specs=[full(qs), full(ks), full(vs), full(qn), full(kn), full(vn),
                  pl.BlockSpec(sink_rows.shape, lambda i: (0, 0, 0))],
        out_specs=[pl.BlockSpec((1, SWA_Q_W, lc), lambda i: (i, 0, 0)),
                   pl.BlockSpec((1, NA_W, lc), lambda i: (i, 0, 0))],
        compiler_params=_cparams("parallel"),
        name="ctx_attn",
    )(qs, ks, vs, qn, kn, vn, sink_rows)


def _route(logits):
    lane = lax.broadcasted_iota(jnp.int32, logits.shape, 1)
    big = jnp.int32(ROUTE_W)
    gmask = lane < N_GROUPS
    gl = jnp.where(gmask, logits, NEG_INF)
    gmax = jnp.max(gl, axis=-1, keepdims=True)
    g_sel = jnp.min(jnp.where(gl == gmax, lane, big), axis=-1, keepdims=True)
    p_g = 1.0 / jnp.sum(jnp.where(gmask, jnp.exp(logits - gmax), 0.0), axis=-1, keepdims=True)
    lo = N_GROUPS + EXPERTS_PER_GROUP * g_sel
    emask = (lane >= lo) & (lane < lo + EXPERTS_PER_GROUP)
    el = jnp.where(emask, logits, NEG_INF)
    v1 = jnp.max(el, axis=-1, keepdims=True)
    i1 = jnp.min(jnp.where(el == v1, lane, big), axis=-1, keepdims=True)
    el2 = jnp.where(lane == i1, NEG_INF, el)
    v2 = jnp.max(el2, axis=-1, keepdims=True)
    i2 = jnp.min(jnp.where(el2 == v2, lane, big), axis=-1, keepdims=True)
    e21 = jnp.exp(v2 - v1)
    w1 = p_g / (1.0 + e21)
    w2 = p_g * e21 / (1.0 + e21)
    out = jnp.where(lane == 0, (i1 - N_GROUPS).astype(F32), 0.0)
    out = jnp.where(lane == 1, (i2 - N_GROUPS).astype(F32), out)
    out = jnp.where(lane == 2, w1, out)
    out = jnp.where(lane == 3, w2, out)
    return out


def _out_proj_kernel(x_ref, yf_ref, ys_ref, yn_ref, wo_ref, g1_ref, sh_ref, sc_ref, gn_ref, wr_ref, br_ref,
                     x1_ref, h2_ref, rt_ref):
    y = (_dot_tn(yf_ref[0], wo_ref[F_LO:F_HI, :]) + _dot_tn(ys_ref[0], wo_ref[QS_LO:QS_HI, :])
         + _dot_tn(yn_ref[0], wo_ref[QN_LO:QN_HI, :]))
    x1 = x_ref[0] + g1_ref[0] * y
    x1_ref[0] = x1
    ms = jnp.mean(x1 * x1, axis=-1, keepdims=True)
    h2 = (x1 * lax.rsqrt(ms + EPS) * gn_ref[...]) * (1.0 + sc_ref[0]) + sh_ref[0]
    h2_ref[0] = h2.astype(BF16)
    rt_ref[0] = _route(_split_dot(h2, wr_ref[...]) + br_ref[...])


def _out_proj(x, yf, ys, yn, w_out, gate, shift, scale, gain, w_route, b_route, *, tm):
    b, t, d = x.shape
    vec = pl.BlockSpec((1, 1, d), lambda i, j: (i, 0, 0))
    feat = lambda r: pl.BlockSpec((1, r, tm), lambda i, j: (i, 0, j))
    tok = lambda w: pl.BlockSpec((1, tm, w), lambda i, j: (i, j, 0))
    return pl.pallas_call(
        _out_proj_kernel,
        out_shape=[jax.ShapeDtypeStruct((b, t, d), F32), jax.ShapeDtypeStruct((b, t, d), BF16),
                   jax.ShapeDtypeStruct((b, t, ROUTE_W), F32)],
        grid=(b, t // tm),
        in_specs=[tok(d), feat(FOURIER_WIDTH), feat(SWA_Q_W), feat(NA_W),
                  pl.BlockSpec((MIX_WIDTH, d), lambda i, j: (0, 0)),
                  vec, vec, vec,
                  pl.BlockSpec((1, d), lambda i, j: (0, 0)),
                  pl.BlockSpec((d, ROUTE_W), lambda i, j: (0, 0)),
                  pl.BlockSpec((1, ROUTE_W), lambda i, j: (0, 0))],
        out_specs=[tok(d), tok(d), tok(ROUTE_W)],
        compiler_params=_cparams("parallel", "parallel"),
        name="out_proj",
    )(x, yf, ys, yn, w_out, gate, shift, scale, gain, w_route, b_route)


def _experts_kernel(te_ref, nt_ref, x_ref, wg_ref, wu_ref, wd_ref, y_ref, wg_s, wu_s, wd_s):
    i = pl.program_id(0)
    fresh = jnp.logical_or(i == 0, te_ref[i] != te_ref[jnp.maximum(i - 1, 0)])

    @pl.when(jnp.logical_and(fresh, i < nt_ref[0]))
    def _():
        wg_s[...] = wg_ref[0, 0].astype(BF16)
        wu_s[...] = wu_ref[0, 0].astype(BF16)
        wd_s[...] = wd_ref[0, 0].astype(BF16)

    @pl.when(i < nt_ref[0])
    def _():
        x = x_ref[...].astype(BF16)
        g = _dot(x, wg_s[...])
        u = _dot(x, wu_s[...])
        hid = (g * (1.0 / (1.0 + jnp.exp(-g)))) * u
        y_ref[...] = _dot(hid.astype(BF16), wd_s[...])

    @pl.when(i >= nt_ref[0])
    def _():
        y_ref[...] = jnp.zeros(y_ref.shape, F32)


def _experts(xs, tile_expert, n_tiles, w_gate, w_up, w_down, layer):
    r, d = xs.shape
    de = w_gate.shape[3]
    last = lambda i, te, nt: jnp.minimum(i, nt[0] - 1)
    wspec = lambda a, b: pl.BlockSpec((1, 1, a, b), lambda i, te, nt: (layer, te[i], 0, 0))
    return pl.pallas_call(
        _experts_kernel,
        out_shape=jax.ShapeDtypeStruct((r, d), F32),
        grid_spec=pltpu.PrefetchScalarGridSpec(
            num_scalar_prefetch=2,
            grid=(r // MOE_TILE,),
            in_specs=[
                pl.BlockSpec((MOE_TILE, d), lambda i, te, nt: (last(i, te, nt), 0)),
                wspec(d, de), wspec(d, de), wspec(de, d),
            ],
            out_specs=pl.BlockSpec((MOE_TILE, d), lambda i, te, nt: (i, 0)),
            scratch_shapes=[pltpu.VMEM((d, de), BF16), pltpu.VMEM((d, de), BF16), pltpu.VMEM((de, d), BF16)],
        ),
        compiler_params=_cparams("arbitrary"),
        name="experts",
    )(tile_expert, n_tiles, xs, w_gate, w_up, w_down)


def _chunk_copy(local, remote, sem, dst_ref, c, q, to_remote):
    loc = local.at[pl.ds(pl.multiple_of(q * ROW_BLOCK, ROW_BLOCK), ROW_BLOCK)]
    rem = remote.at[pl.ds(pl.multiple_of(dst_ref[c, q], ROW_BLOCK), ROW_BLOCK)]
    return pltpu.make_async_copy(loc, rem, sem) if to_remote else pltpu.make_async_copy(rem, loc, sem)


def _chunk_copies(local, remote, sem, dst_ref, nblk_ref, c, *, to_remote, wait):
    def body(q, carry):
        cp = _chunk_copy(local, remote, sem, dst_ref, c, q, to_remote)
        if wait:
            cp.wait()
        else:
            cp.start()
        return carry

    lax.fori_loop(0, nblk_ref[c], body, 0)


def _tail_copies(zero, remote, sem, tstart_ref, tn_ref, *, wait):
    def per_expert(e, carry):
        def body(q, inner):
            row = pl.multiple_of(tstart_ref[e] + q * ROW_BLOCK, ROW_BLOCK)
            cp = pltpu.make_async_copy(zero, remote.at[pl.ds(row, ROW_BLOCK)], sem)
            if wait:
                cp.wait()
            else:
                cp.start()
            return inner

        return lax.fori_loop(0, tn_ref[e], body, carry)

    lax.fori_loop(0, N_EXPERTS, per_expert, 0)


def _unused_tile_copies(zero, remote, sem, nt_ref, total_tiles, *, wait):
    def body(t, carry):
        row = pl.multiple_of(t * MOE_TILE, MOE_TILE)
        cp = pltpu.make_async_copy(zero, remote.at[pl.ds(row, MOE_TILE)], sem)
        if wait:
            cp.wait()
        else:
            cp.start()
        return carry

    lax.fori_loop(nt_ref[0], total_tiles, body, 0)


def _dispatch_kernel(dst_ref, nblk_ref, tstart_ref, tn_ref, nt_ref, *rest, part_chunks):
    n_parts = len(part_chunks)
    h_refs = rest[:n_parts]
    pos_ref, xs_ref, buf, zero, sem, zsem = rest[n_parts:]
    n_chunks = sum(part_chunks)
    c = pl.program_id(0)
    slot = c % 2
    row = lax.broadcasted_iota(jnp.int32, (MOE_LROWS, MOE_CHUNK), 0)
    hit = (row == pos_ref[0, 0:1, :]) | (row == pos_ref[0, 1:2, :])
    sel = jnp.where(hit, 1.0, 0.0).astype(BF16)
    first = 0
    for h_ref, n in zip(h_refs, part_chunks):
        @pl.when((c >= first) & (c < first + n))
        def _(h_ref=h_ref):
            buf[slot] = _dot(sel, h_ref[...])
        first += n
    _chunk_copies(buf.at[slot], xs_ref, sem.at[slot], dst_ref, nblk_ref, c, to_remote=True, wait=False)

    @pl.when(c > 0)
    def _():
        _chunk_copies(buf.at[1 - slot], xs_ref, sem.at[1 - slot], dst_ref, nblk_ref, c - 1,
                      to_remote=True, wait=True)

    @pl.when(c == n_chunks - 1)
    def _():
        total_tiles = xs_ref.shape[0] // MOE_TILE
        zero[...] = jnp.zeros(zero.shape, F32)
        zero8 = zero.at[pl.ds(0, ROW_BLOCK)]
        _tail_copies(zero8, xs_ref, zsem, tstart_ref, tn_ref, wait=False)
        _unused_tile_copies(zero, xs_ref, zsem, nt_ref, total_tiles, wait=False)
        _chunk_copies(buf.at[slot], xs_ref, sem.at[slot], dst_ref, nblk_ref, c, to_remote=True, wait=True)
        _tail_copies(zero8, xs_ref, zsem, tstart_ref, tn_ref, wait=True)
        _unused_tile_copies(zero, xs_ref, zsem, nt_ref, total_tiles, wait=True)


def _dispatch(h_parts, plan):
    d = h_parts[0].shape[1]
    part_chunks = tuple(h.shape[0] // MOE_CHUNK for h in h_parts)
    in_specs = []
    first = 0
    for n in part_chunks:
        in_specs.append(pl.BlockSpec(
            (MOE_CHUNK, d), lambda i, *_, first=first, n=n: (jnp.clip(i - first, 0, n - 1), 0)))
        first += n
    in_specs.append(pl.BlockSpec((1, 8, MOE_CHUNK), lambda i, *_: (i, 0, 0)))
    return pl.pallas_call(
        functools.partial(_dispatch_kernel, part_chunks=part_chunks),
        out_shape=jax.ShapeDtypeStruct((plan["rows"], d), F32),
        grid_spec=pltpu.PrefetchScalarGridSpec(
            num_scalar_prefetch=5,
            grid=(sum(part_chunks),),
            in_specs=in_specs,
            out_specs=pl.BlockSpec(memory_space=pl.ANY),
            scratch_shapes=[pltpu.VMEM((2, MOE_LROWS, d), F32), pltpu.VMEM((MOE_TILE, d), F32),
                            pltpu.SemaphoreType.DMA((2,)), pltpu.SemaphoreType.DMA(())],
        ),
        compiler_params=_cparams("arbitrary"),
        name="dispatch",
    )(plan["dst"], plan["nblk"], plan["tstart"], plan["tn"], plan["n_tiles"], *h_parts, plan["pos"])


def _combine_kernel(dst_ref, nblk_ref, x1_ref, rt_ref, pos_ref, g2_ref, gf_ref, ys_ref, o_ref, buf, sem,
                    *, c0, n_chunks, final_norm):
    i = pl.program_id(0)
    c = i + c0
    slot = i % 2
    gather = functools.partial(_chunk_copies, remote=ys_ref, dst_ref=dst_ref, nblk_ref=nblk_ref, to_remote=False)

    @pl.when(i == 0)
    def _():
        buf[...] = jnp.zeros(buf.shape, F32)
        gather(buf.at[slot], sem=sem.at[slot], c=c, wait=False)

    @pl.when(i + 1 < n_chunks)
    def _():
        gather(buf.at[1 - slot], sem=sem.at[1 - slot], c=c + 1, wait=False)

    gather(buf.at[slot], sem=sem.at[slot], c=c, wait=True)
    yl = buf[slot].astype(BF16)
    row = lax.broadcasted_iota(jnp.int32, (MOE_LROWS, MOE_CHUNK), 0)
    y1 = _dot_tn(jnp.where(row == pos_ref[0, 0:1, :], 1.0, 0.0).astype(BF16), yl)
    y2 = _dot_tn(jnp.where(row == pos_ref[0, 1:2, :], 1.0, 0.0).astype(BF16), yl)
    rt = rt_ref[0]
    x2 = x1_ref[0] + g2_ref[0] * (rt[:, 2:3] * y1 + rt[:, 3:4] * y2)
    if final_norm:
        ms = jnp.mean(x2 * x2, axis=-1, keepdims=True)
        x2 = x2 * lax.rsqrt(ms + EPS) * gf_ref[...]
    o_ref[0] = x2


def _combine(x1, route, plan, gate, g_final, ys, *, c0, final_norm):
    b, t, d = x1.shape
    per_batch = t // MOE_CHUNK
    n_chunks = b * per_batch
    tok = lambda w: pl.BlockSpec((1, MOE_CHUNK, w), lambda i, *_: (i // per_batch, i % per_batch, 0))
    return pl.pallas_call(
        functools.partial(_combine_kernel, c0=c0, n_chunks=n_chunks, final_norm=final_norm),
        out_shape=jax.ShapeDtypeStruct((b, t, d), F32),
        grid_spec=pltpu.PrefetchScalarGridSpec(
            num_scalar_prefetch=2,
            grid=(n_chunks,),
            in_specs=[tok(d), tok(ROUTE_W),
                      pl.BlockSpec((1, 8, MOE_CHUNK), lambda i, *_: (i + c0, 0, 0)),
                      pl.BlockSpec((1, 1, d), lambda i, *_: (i // per_batch, 0, 0)),
                      pl.BlockSpec((1, d), lambda i, *_: (0, 0)),
                      pl.BlockSpec(memory_space=pl.ANY)],
            out_specs=tok(d),
            scratch_shapes=[pltpu.VMEM((2, MOE_LROWS, d), F32), pltpu.SemaphoreType.DMA((2,))],
        ),
        compiler_params=_cparams("arbitrary"),
        name="combine",
    )(plan["dst"], plan["nblk"], x1, route, plan["pos"], gate, g_final, ys)


def _rope_tables(seq):
    half = HEAD_DIM // 4
    t = jnp.arange(seq)
    rows, cols = t // GRID_W, t % GRID_W
    freqs = ROPE_BASE ** (-jnp.arange(half, dtype=F32) / half)

    def cs(pos):
        ang = pos.astype(F32)[None, :] * freqs[:, None]
        return jnp.cos(ang), jnp.sin(ang)

    cr, sr = cs(rows)
    cc, sc = cs(cols)
    return jnp.concatenate([cr, cr, cc, cc], axis=0), jnp.concatenate([-sr, sr, -sc, sc], axis=0)


def _dft_tables(n):
    k = jnp.arange(n)

    def cs(m):
        ang = ((m[:, None] * k[None, :]) % n).astype(F32) * (2.0 * jnp.pi / n)
        return jnp.cos(ang), jnp.sin(ang)

    if n <= GRID_W:
        return cs(k)
    ca, sa = cs(jnp.arange(n // GRID_W) * GRID_W)
    cb, sb = cs(jnp.arange(GRID_W))
    c = ca[:, None, :] * cb[None, :, :] - sa[:, None, :] * sb[None, :, :]
    s = sa[:, None, :] * cb[None, :, :] + ca[:, None, :] * sb[None, :, :]
    return c.reshape(n, n), s.reshape(n, n)


def _channel_dft_t():
    c, s = _dft_tables(FOURIER_GROUP_DIM)
    eye = jnp.eye(FOURIER_GROUPS, dtype=F32)
    scale = FOURIER_GROUP_DIM ** -0.5
    return jnp.concatenate([jnp.kron(eye, c), jnp.kron(eye, s)], axis=0) * scale


def _position_dft(n):
    c, s = _dft_tables(n)
    scale = n ** -0.5
    return (c * scale).astype(BF16), (s * scale).astype(BF16)


def _swa_mask(seq):
    span = 3 * SWA_BLOCK
    nb = seq // SWA_BLOCK
    kk = jnp.arange(span)[:, None]
    q = jnp.arange(SWA_BLOCK)[None, :]
    tiles = []
    for n, ks in ((0, 0), (1, 0), (nb - 1, seq - span)):
        ok = jnp.abs(ks + kk - (n * SWA_BLOCK + q)) <= SWA_WINDOW
        tiles.append(jnp.tile(jnp.where(ok, 0.0, NEG_INF).astype(F32), (1, SWA_GROUP)))
    return jnp.stack(tiles)


def _sink_rows(sink, width):
    return jnp.repeat(sink.astype(F32).reshape(SWA_KV_HEADS, SWA_GROUP), width, axis=1)[:, None, :]


def _moe_plan(route2d):
    i32 = jnp.int32
    n = route2d.shape[0]
    nc = n // MOE_CHUNK
    ids = route2d[:, 0:2].astype(i32)
    onehot = (ids[:, :, None] == jnp.arange(N_EXPERTS, dtype=i32)).astype(i32)
    onehot = onehot.reshape(nc, 2 * MOE_CHUNK, N_EXPERTS)
    pair = jnp.arange(2 * MOE_CHUNK, dtype=i32)
    earlier = (pair[None, :] < pair[:, None]).astype(BF16)
    before = jnp.einsum("pq,cqe->cpe", earlier, onehot.astype(BF16), preferred_element_type=F32).astype(i32)
    rank = jnp.sum(before * onehot, axis=-1)
    cnt = jnp.sum(onehot, axis=1)
    run = (cnt + ROW_BLOCK - 1) // ROW_BLOCK * ROW_BLOCK
    lo = jnp.cumsum(run, axis=1) - run
    pos = (jnp.sum(onehot * lo[:, None, :], axis=-1) + rank).reshape(nc, MOE_CHUNK, 2)
    pos = jnp.concatenate([jnp.swapaxes(pos, 1, 2), jnp.full((nc, 6, MOE_CHUNK), -1, i32)], axis=1)
    seg = jnp.sum(run, axis=0)
    padded = (seg + MOE_TILE - 1) // MOE_TILE * MOE_TILE
    start = jnp.cumsum(padded) - padded
    off = start[None, :] + jnp.cumsum(run, axis=0) - run
    q = jnp.arange(MOE_NBLK, dtype=i32)[None, :, None] * ROW_BLOCK
    inside = (lo[:, None, :] <= q) & (q < (lo + run)[:, None, :])
    dst = jnp.sum(jnp.where(inside, off[:, None, :] + q - lo[:, None, :], 0), axis=-1)
    nblk = jnp.sum(run, axis=1) // ROW_BLOCK
    ends = start + padded
    rows = -(-(2 * n + (ROW_BLOCK - 1) * N_EXPERTS * nc + (MOE_TILE - 1) * N_EXPERTS) // MOE_TILE) * MOE_TILE
    tile_row = jnp.arange(rows // MOE_TILE, dtype=i32) * MOE_TILE
    te = jnp.minimum(jnp.sum((ends[None, :] <= tile_row[:, None]).astype(i32), axis=1), N_EXPERTS - 1)
    n_tiles = ends[-1] // MOE_TILE
    te_last = jnp.sum(jnp.where(tile_row == (n_tiles - 1) * MOE_TILE, te, 0))
    te = jnp.where(tile_row < ends[-1], te, te_last)
    return {"pos": pos, "dst": dst.astype(i32), "nblk": nblk.astype(i32), "tstart": (start + seg).astype(i32),
            "tn": ((padded - seg) // ROW_BLOCK).astype(i32), "te": te.astype(i32),
            "n_tiles": n_tiles.astype(i32).reshape(1), "rows": rows}


def kernel(x, c, ctx, c_ctx, w_mod, b_mod, g_norm1, g_norm2, w_in, w_four, w_out, swa_sink, na_rpb,
           w_route_group, b_route_group, w_route_expert, b_route_expert, w_exp_gate, w_exp_up,
           w_exp_down, g_final):
    b, s, d = x.shape
    lc = ctx.shape[1]
    depth = w_mod.shape[0]
    tm = 512

    c_rows = jnp.concatenate([c, c_ctx[None, :], jnp.zeros((7, d), F32)], axis=0)
    mod = _modulation(c_rows, w_mod, b_mod)

    cos_t, sin_t = _rope_tables(s)
    cos_c, sin_c = cos_t[:, :lc], sin_t[:, :lc]
    bd_t = _channel_dft_t().astype(BF16)
    cn, sn = _position_dft(s)
    cn_c, sn_c = _position_dft(lc)
    mask = _swa_mask(s)
    pad = jnp.zeros((d, ROUTE_W - N_GROUPS - N_EXPERTS), F32)

    xc = ctx
    for layer in range(depth):
        with_ctx_out = layer < depth - 1
        lat = [mod[layer, :b, i * d:(i + 1) * d][:, None, :] for i in range(6)]
        cx = [jnp.broadcast_to(mod[layer, b, i * d:(i + 1) * d][None, None, :], (b, 1, d)) for i in range(6)]
        sh1, sc1, g1, sh2, sc2, g2 = lat
        shc1, scc1, gc1, shc2, scc2, gc2 = cx
        gn1 = g_norm1[layer][None, :]
        gn2 = g_norm2[layer][None, :]
        w_t = w_in[layer].T.astype(BF16)
        wf_t = w_four[layer].T.astype(BF16)
        wo = w_out[layer].astype(BF16)
        w_r = jnp.concatenate([w_route_group[layer], w_route_expert[layer], pad], axis=1)
        b_r = jnp.concatenate([b_route_group[layer], b_route_expert[layer],
                               jnp.zeros((ROUTE_W - N_GROUPS - N_EXPERTS,), F32)])[None, :]
        sink_lat = _sink_rows(swa_sink[layer], SWA_BLOCK)
        sink_ctx = _sink_rows(swa_sink[layer], lc)

        fz, qs, qn, ks, vs, kn, vn = _in_proj(x, sh1, sc1, gn1, w_t, bd_t, cos_t, sin_t,
                                              with_q=True, rope=True, tm=tm)
        if with_ctx_out:
            fz_c, qs_c, qn_c, ks_c, vs_c, kn_c, vn_c = _in_proj(xc, shc1, scc1, gn1, w_t, bd_t, cos_c, sin_c,
                                                                with_q=True, rope=False, tm=lc)
        else:
            ks_c, vs_c, kn_c, vn_c = _in_proj(xc, shc1, scc1, gn1, w_t[KS_LO:], bd_t, cos_c, sin_c,
                                              with_q=False, rope=False, tm=lc)

        yf = _fourier(fz, cn, sn, wf_t, tk=512)
        ys = _swa(qs, ks, vs, ks_c, vs_c, mask, sink_lat)
        yn = _na(qn, kn, vn, kn_c, vn_c, _na_bias(na_rpb[layer], s))
        x1, h2, route = _out_proj(x, yf, ys, yn, wo, g1, sh2, sc2, gn2, w_r, b_r, tm=tm)

        moe_w = (w_exp_gate, w_exp_up, w_exp_down, layer)
        if with_ctx_out:
            yf_c = _fourier(fz_c, cn_c, sn_c, wf_t, tk=lc)
            ys_c, yn_c = _ctx_attn(qs_c, ks_c, vs_c, qn_c, kn_c, vn_c, sink_ctx)
            xc1, hc2, route_c = _out_proj(xc, yf_c, ys_c, yn_c, wo, gc1, shc2, scc2, gn2, w_r, b_r, tm=lc)
            n_lat = b * s
            lat_chunks = n_lat // MOE_CHUNK
            plan = _moe_plan(jnp.concatenate([route.reshape(n_lat, ROUTE_W),
                                              route_c.reshape(b * lc, ROUTE_W)], axis=0))
            xs = _dispatch([h2.reshape(n_lat, d), hc2.reshape(b * lc, d)], plan)
            ye = _experts(xs, plan["te"], plan["n_tiles"], *moe_w)
            x = _combine(x1, route, plan, g2, g_final[None, :], ye, c0=0, final_norm=False)
            xc = _combine(xc1, route_c, plan, gc2, g_final[None, :], ye, c0=lat_chunks, final_norm=False)
        else:
            plan = _moe_plan(route.reshape(b * s, ROUTE_W))
            xs = _dispatch([h2.reshape(b * s, d)], plan)
            ye = _experts(xs, plan["te"], plan["n_tiles"], *moe_w)
            x = _combine(x1, route, plan, g2, g_final[None, :], ye, c0=0, final_norm=True)
    return x
```

```python
import functools

import jax
import jax.numpy as jnp
from jax import lax
from jax.experimental import pallas as pl
from jax.experimental.pallas import tpu as pltpu

F32 = jnp.float32
BF16 = jnp.bfloat16

D_MODEL = 1024
GRID_W = 64
HEAD_DIM = 64
FOURIER_WIDTH = D_MODEL // 4
FOURIER_GROUPS = 4
FOURIER_GROUP_DIM = FOURIER_WIDTH // FOURIER_GROUPS
SWA_HEADS = (3 * D_MODEL // 8) // HEAD_DIM
SWA_KV_HEADS = 2
SWA_GROUP = SWA_HEADS // SWA_KV_HEADS
SWA_WINDOW = 128
SWA_BLOCK = 128
NA_HEADS = (3 * D_MODEL // 8) // HEAD_DIM
NA_WIN_R = 8
NA_WIN_C = 16
ROPE_BASE = 10000.0
N_GROUPS = 4
EXPERTS_PER_GROUP = 8
N_EXPERTS = N_GROUPS * EXPERTS_PER_GROUP
D_EXPERT = D_MODEL // 2
EPS = 1e-6
NEG_INF = -1e30

SWA_Q_W = SWA_HEADS * HEAD_DIM
SWA_KV_W = SWA_KV_HEADS * HEAD_DIM
NA_W = NA_HEADS * HEAD_DIM
MIX_WIDTH = FOURIER_WIDTH + SWA_Q_W + NA_W
Q_COLS = MIX_WIDTH
IN_COLS = 2 * MIX_WIDTH

F_LO, F_HI = 0, FOURIER_WIDTH
QS_LO, QS_HI = F_HI, F_HI + SWA_Q_W
QN_LO, QN_HI = QS_HI, QS_HI + NA_W
KS_LO, KS_HI = QN_HI, QN_HI + SWA_KV_W
VS_LO, VS_HI = KS_HI, KS_HI + SWA_KV_W
KN_LO, KN_HI = VS_HI, VS_HI + NA_W
VN_LO, VN_HI = KN_HI, KN_HI + NA_W

LANE = 128
ROUTE_W = LANE
NA_QROWS = 4
NA_KROWS = NA_QROWS + NA_WIN_R
MOE_TILE = 256
ROW_BLOCK = 8
MOE_CHUNK = 512
MOE_LROWS = 2 * MOE_CHUNK + (ROW_BLOCK - 1) * N_EXPERTS
MOE_NBLK = MOE_LROWS // ROW_BLOCK
VMEM_LIMIT = 48 * 1024 * 1024


def _cparams(*sem):
    return pltpu.CompilerParams(dimension_semantics=sem, vmem_limit_bytes=VMEM_LIMIT)


def _dot(a, b):
    return jnp.dot(a, b, preferred_element_type=F32)


def _dot_tn(a, b):
    return lax.dot_general(a, b, (((0,), (0,)), ((), ())), preferred_element_type=F32)


def _dot_nt(a, b):
    return lax.dot_general(a, b, (((1,), (1,)), ((), ())), preferred_element_type=F32)


def _split_dot(a, w):
    a_hi = a.astype(BF16)
    a_lo = (a - a_hi.astype(F32)).astype(BF16)
    w_hi = w.astype(BF16)
    w_lo = (w - w_hi.astype(F32)).astype(BF16)
    return _dot(a_hi, w_hi) + (_dot(a_hi, w_lo) + _dot(a_lo, w_hi))


def _mod_kernel(c_ref, w_ref, b_ref, o_ref):
    c = c_ref[...]
    a = c * (1.0 / (1.0 + jnp.exp(-c)))
    o_ref[0] = _split_dot(a, w_ref[0]) + b_ref[0]


def _modulation(c_rows, w_mod, b_mod):
    depth, d, n6 = w_mod.shape
    r = c_rows.shape[0]
    tn = 1536
    return pl.pallas_call(
        _mod_kernel,
        out_shape=jax.ShapeDtypeStruct((depth, r, n6), F32),
        grid=(depth, n6 // tn),
        in_specs=[
            pl.BlockSpec((r, d), lambda l, j: (0, 0)),
            pl.BlockSpec((1, d, tn), lambda l, j: (l, 0, j)),
            pl.BlockSpec((1, 1, tn), lambda l, j: (l, 0, j)),
        ],
        out_specs=pl.BlockSpec((1, r, tn), lambda l, j: (l, 0, j)),
        compiler_params=_cparams("parallel", "parallel"),
        name="modulation",
    )(c_rows, w_mod, b_mod.reshape(depth, 1, n6))


def _rope_rows(t, cos_t, sin_t, n_heads):
    outs = []
    for h in range(n_heads):
        th = t[HEAD_DIM * h:HEAD_DIM * (h + 1)]
        sw = jnp.concatenate([th[16:32], th[0:16], th[48:64], th[32:48]], axis=0)
        outs.append(th * cos_t + sw * sin_t)
    return jnp.concatenate(outs, axis=0)


def _in_proj_kernel(x_ref, sh_ref, sc_ref, g_ref, wt_ref, bdt_ref, cos_ref, sin_ref, *outs, with_q, rope):
    xf = x_ref[0]
    ms = jnp.mean(xf * xf, axis=-1, keepdims=True)
    y = xf * lax.rsqrt(ms + EPS) * g_ref[...]
    h = y * (1.0 + sc_ref[0]) + sh_ref[0]
    pt = _dot_nt(wt_ref[...], h.astype(BF16))
    q_scale = HEAD_DIM ** -0.5
    if with_q:
        fz_ref, qs_ref, qn_ref, ks_ref, vs_ref, kn_ref, vn_ref = outs
        fz_ref[0] = _dot(bdt_ref[...], pt[F_LO:F_HI].astype(BF16)).astype(BF16)
        qs = pt[QS_LO:QS_HI]
        if rope:
            qs = _rope_rows(qs, cos_ref[...], sin_ref[...], SWA_HEADS)
        qs_ref[0] = (qs * q_scale).astype(BF16)
        qn_ref[0] = (pt[QN_LO:QN_HI] * q_scale).astype(BF16)
        off = 0
    else:
        ks_ref, vs_ref, kn_ref, vn_ref = outs
        off = KS_LO
    ks = pt[KS_LO - off:KS_HI - off]
    if rope:
        ks = _rope_rows(ks, cos_ref[...], sin_ref[...], SWA_KV_HEADS)
    ks_ref[0] = ks.astype(BF16)
    vs_ref[0] = pt[VS_LO - off:VS_HI - off].astype(BF16)
    kn_ref[0] = pt[KN_LO - off:KN_HI - off].astype(BF16)
    vn_ref[0] = pt[VN_LO - off:VN_HI - off].astype(BF16)


def _in_proj(x, shift, scale, gain, w_t, bd_t, cos_t, sin_t, *, with_q, rope, tm):
    b, t, d = x.shape
    nf = w_t.shape[0]
    rows = ([2 * FOURIER_WIDTH, SWA_Q_W, NA_W] if with_q else []) + [SWA_KV_W, SWA_KV_W, NA_W, NA_W]
    return pl.pallas_call(
        functools.partial(_in_proj_kernel, with_q=with_q, rope=rope),
        out_shape=[jax.ShapeDtypeStruct((b, r, t), BF16) for r in rows],
        grid=(b, t // tm),
        in_specs=[
            pl.BlockSpec((1, tm, d), lambda i, j: (i, j, 0)),
            pl.BlockSpec((1, 1, d), lambda i, j: (i, 0, 0)),
            pl.BlockSpec((1, 1, d), lambda i, j: (i, 0, 0)),
            pl.BlockSpec((1, d), lambda i, j: (0, 0)),
            pl.BlockSpec((nf, d), lambda i, j: (0, 0)),
            pl.BlockSpec(bd_t.shape, lambda i, j: (0, 0)),
            pl.BlockSpec((HEAD_DIM, tm), lambda i, j: (0, j)),
            pl.BlockSpec((HEAD_DIM, tm), lambda i, j: (0, j)),
        ],
        out_specs=[pl.BlockSpec((1, r, tm), lambda i, j: (i, 0, j)) for r in rows],
        compiler_params=_cparams("parallel", "parallel"),
        name="in_proj_q" if with_q else "in_proj_kv",
    )(x, shift, scale, gain, w_t, bd_t, cos_t, sin_t)


def _fourier_kernel(fz_ref, cn_ref, sn_ref, wft_ref, o_ref):
    zc = fz_ref[0, 0:FOURIER_WIDTH, :]
    zs = fz_ref[0, FOURIER_WIDTH:2 * FOURIER_WIDTH, :]
    y = _dot(zc, cn_ref[...]) - _dot(zs, sn_ref[...])
    o_ref[0] = _dot(wft_ref[...], y.astype(BF16)).astype(BF16)


def _fourier(fz, cn, sn, wf_t, *, tk):
    b, _, t = fz.shape
    return pl.pallas_call(
        _fourier_kernel,
        out_shape=jax.ShapeDtypeStruct((b, FOURIER_WIDTH, t), BF16),
        grid=(t // tk, b),
        in_specs=[
            pl.BlockSpec((1, 2 * FOURIER_WIDTH, t), lambda k, i: (i, 0, 0)),
            pl.BlockSpec((t, tk), lambda k, i: (0, k)),
            pl.BlockSpec((t, tk), lambda k, i: (0, k)),
            pl.BlockSpec((FOURIER_WIDTH, FOURIER_WIDTH), lambda k, i: (0, 0)),
        ],
        out_specs=pl.BlockSpec((1, FOURIER_WIDTH, tk), lambda k, i: (i, 0, k)),
        compiler_params=_cparams("parallel", "parallel"),
        name="fourier",
    )(fz, cn, sn, wf_t)


def _attend(q_t, k_parts, v_parts, biases, sink_row):
    logits = []
    for k_t, bias in zip(k_parts, biases):
        s = _dot_tn(k_t, q_t)
        logits.append(s if bias is None else s + bias)
    m = functools.reduce(jnp.maximum, [jnp.max(s, axis=0, keepdims=True) for s in logits])
    if sink_row is not None:
        m = jnp.maximum(m, sink_row)
    den = None
    acc = None
    for s, v_t in zip(logits, v_parts):
        p = jnp.exp(s - m)
        ps = jnp.sum(p, axis=0, keepdims=True)
        den = ps if den is None else den + ps
        o = _dot(v_t, p.astype(BF16))
        acc = o if acc is None else acc + o
    if sink_row is not None:
        den = den + jnp.exp(sink_row - m)
    return acc / den


def _swa_kernel(q_ref, k_ref, v_ref, kc_ref, vc_ref, mask_ref, sink_ref, o_ref, *, seq):
    nb = seq // SWA_BLOCK
    span = 3 * SWA_BLOCK
    kc = kc_ref[0]
    vc = vc_ref[0]
    sink_row = sink_ref[0]
    for n in range(nb):
        ks = min(max((n - 1) * SWA_BLOCK, 0), seq - span)
        mtype = 0 if n == 0 else (2 if n == nb - 1 else 1)
        q0 = n * SWA_BLOCK
        q_t = jnp.concatenate(
            [q_ref[0, HEAD_DIM * h:HEAD_DIM * (h + 1), q0:q0 + SWA_BLOCK] for h in range(SWA_GROUP)], axis=1)
        o = _attend(q_t, [k_ref[0, :, ks:ks + span], kc], [v_ref[0, :, ks:ks + span], vc],
                    [mask_ref[mtype], None], sink_row)
        for h in range(SWA_GROUP):
            o_ref[0, HEAD_DIM * h:HEAD_DIM * (h + 1), q0:q0 + SWA_BLOCK] = (
                o[:, SWA_BLOCK * h:SWA_BLOCK * (h + 1)].astype(BF16))


def _swa(qs, ks, vs, kc, vc, mask, sink_rows):
    b, _, t = qs.shape
    lc = kc.shape[2]
    gw = SWA_GROUP * HEAD_DIM
    span = 3 * SWA_BLOCK
    return pl.pallas_call(
        functools.partial(_swa_kernel, seq=t),
        out_shape=jax.ShapeDtypeStruct((b, SWA_Q_W, t), BF16),
        grid=(b, SWA_KV_HEADS),
        in_specs=[
            pl.BlockSpec((1, gw, t), lambda i, g: (i, g, 0)),
            pl.BlockSpec((1, HEAD_DIM, t), lambda i, g: (i, g, 0)),
            pl.BlockSpec((1, HEAD_DIM, t), lambda i, g: (i, g, 0)),
            pl.BlockSpec((1, HEAD_DIM, lc), lambda i, g: (i, g, 0)),
            pl.BlockSpec((1, HEAD_DIM, lc), lambda i, g: (i, g, 0)),
            pl.BlockSpec((3, span, SWA_GROUP * SWA_BLOCK), lambda i, g: (0, 0, 0)),
            pl.BlockSpec((1, 1, SWA_GROUP * SWA_BLOCK), lambda i, g: (g, 0, 0)),
        ],
        out_specs=pl.BlockSpec((1, gw, t), lambda i, g: (i, g, 0)),
        compiler_params=_cparams("parallel", "parallel"),
        name="swa",
    )(qs, ks, vs, kc, vc, mask, sink_rows)


def _na_bias_kernel(rpb_ref, o_ref, u_ref, *, total_rows):
    hd = pl.program_id(0)
    kc = lax.broadcasted_iota(jnp.int32, (GRID_W, LANE), 0)
    lane = lax.broadcasted_iota(jnp.int32, (GRID_W, LANE), 1)
    qc = lane % GRID_W
    dc = jnp.clip(kc - qc, -(NA_WIN_C - 1), NA_WIN_C - 1) + (NA_WIN_C - 1)
    c0 = jnp.clip(qc - NA_WIN_C // 2, 0, GRID_W - NA_WIN_C)
    valid_c = (kc >= c0) & (kc < c0 + NA_WIN_C)
    n_dr = 2 * NA_WIN_R - 1
    for dr in range(n_dr):
        u = jnp.full((GRID_W, LANE), NEG_INF, F32)
        for d in range(2 * NA_WIN_C - 1):
            u = jnp.where(valid_c & (dc == d), rpb_ref[hd, dr, d], u)
        u_ref[dr] = u
    n_rows = o_ref.shape[2] // GRID_W
    block_types = [(0, 0), (NA_QROWS, 0), (total_rows - NA_QROWS, total_rows - NA_KROWS)]
    neg = jnp.full((GRID_W, LANE), NEG_INF, F32)
    for t, (r_base, k_base) in enumerate(block_types):
        for kl in range(n_rows):
            kr = k_base + kl
            for lg in range(NA_QROWS // 2):
                halves = []
                for rq in (2 * lg, 2 * lg + 1):
                    r = r_base + rq
                    r0 = min(max(r - NA_WIN_R // 2, 0), total_rows - NA_WIN_R)
                    ok = r0 <= kr < r0 + NA_WIN_R
                    halves.append(u_ref[kr - r + NA_WIN_R - 1] if ok else neg)
                o_ref[0, t, GRID_W * kl:GRID_W * (kl + 1), LANE * lg:LANE * (lg + 1)] = jnp.where(
                    lane < GRID_W, halves[0], halves[1])


def _na_bias(rpb, seq):
    nh = rpb.shape[0]
    return pl.pallas_call(
        functools.partial(_na_bias_kernel, total_rows=seq // GRID_W),
        out_shape=jax.ShapeDtypeStruct((nh, 3, NA_KROWS * GRID_W, NA_QROWS * GRID_W), F32),
        grid=(nh,),
        in_specs=[pl.BlockSpec(memory_space=pltpu.SMEM)],
        out_specs=pl.BlockSpec((1, 3, NA_KROWS * GRID_W, NA_QROWS * GRID_W), lambda h: (h, 0, 0, 0)),
        scratch_shapes=[pltpu.VMEM((2 * NA_WIN_R - 1, GRID_W, LANE), F32)],
        compiler_params=_cparams("parallel"),
        name="na_bias",
    )(rpb)


def _na_kernel(q_ref, k_ref, v_ref, kc_ref, vc_ref, bias_ref, o_ref, *, seq):
    n_rows = seq // GRID_W
    qw = NA_QROWS * GRID_W
    kw = NA_KROWS * GRID_W
    kc = kc_ref[0]
    vc = vc_ref[0]
    nblk = n_rows // NA_QROWS
    for j in range(nblk):
        ks = min(max(j * NA_QROWS - NA_WIN_R // 2, 0), n_rows - NA_KROWS) * GRID_W
        btype = 0 if j == 0 else (2 if j == nblk - 1 else 1)
        q0 = j * qw
        o = _attend(q_ref[0, :, q0:q0 + qw], [k_ref[0, :, ks:ks + kw], kc], [v_ref[0, :, ks:ks + kw], vc],
                    [bias_ref[0, btype], None], None)
        o_ref[0, :, q0:q0 + qw] = o.astype(BF16)


def _na(qn, kn, vn, kc, vc, bias):
    b, _, t = qn.shape
    lc = kc.shape[2]
    head = lambda i, h: (i, h, 0)
    return pl.pallas_call(
        functools.partial(_na_kernel, seq=t),
        out_shape=jax.ShapeDtypeStruct((b, NA_W, t), BF16),
        grid=(b, NA_HEADS),
        in_specs=[
            pl.BlockSpec((1, HEAD_DIM, t), head),
            pl.BlockSpec((1, HEAD_DIM, t), head),
            pl.BlockSpec((1, HEAD_DIM, t), head),
            pl.BlockSpec((1, HEAD_DIM, lc), head),
            pl.BlockSpec((1, HEAD_DIM, lc), head),
            pl.BlockSpec((1,) + bias.shape[1:], lambda i, h: (h, 0, 0, 0)),
        ],
        out_specs=pl.BlockSpec((1, HEAD_DIM, t), head),
        compiler_params=_cparams("parallel", "parallel"),
        name="na",
    )(qn, kn, vn, kc, vc, bias)


def _ctx_attn_kernel(qs_ref, ks_ref, vs_ref, qn_ref, kn_ref, vn_ref, sink_ref, ys_ref, yn_ref):
    lc = qs_ref.shape[2]
    for g in range(SWA_KV_HEADS):
        q_t = jnp.concatenate(
            [qs_ref[0, HEAD_DIM * (SWA_GROUP * g + h):HEAD_DIM * (SWA_GROUP * g + h + 1), :] for h in range(SWA_GROUP)],
            axis=1)
        kv = slice(HEAD_DIM * g, HEAD_DIM * (g + 1))
        o = _attend(q_t, [ks_ref[0, kv, :]], [vs_ref[0, kv, :]], [None], sink_ref[g])
        for h in range(SWA_GROUP):
            hh = SWA_GROUP * g + h
            ys_ref[0, HEAD_DIM * hh:HEAD_DIM * (hh + 1), :] = o[:, lc * h:lc * (h + 1)].astype(BF16)
    for h in range(NA_HEADS):
        sl = slice(HEAD_DIM * h, HEAD_DIM * (h + 1))
        o = _attend(qn_ref[0, sl, :], [kn_ref[0, sl, :]], [vn_ref[0, sl, :]], [None], None)
        yn_ref[0, sl, :] = o.astype(BF16)


def _ctx_attn(qs, ks, vs, qn, kn, vn, sink_rows):
    b, _, lc = qs.shape
    full = lambda a: pl.BlockSpec((1,) + a.shape[1:], lambda i: (i, 0, 0))
    return pl.pallas_call(
        _ctx_attn_kernel,
        out_shape=[jax.ShapeDtypeStruct((b, SWA_Q_W, lc), BF16), jax.ShapeDtypeStruct((b, NA_W, lc), BF16)],
        grid=(b,),
        in_specs=[full(qs), full(ks), full(vs), full(qn), full(kn), full(vn),
                  pl.BlockSpec(sink_rows.shape, lambda i: (0, 0, 0))],
        out_specs=[pl.BlockSpec((1, SWA_Q_W, lc), lambda i: (i, 0, 0)),
                   pl.BlockSpec((1, NA_W, lc), lambda i: (i, 0, 0))],
        compiler_params=_cparams("parallel"),
        name="ctx_attn",
    )(qs, ks, vs, qn, kn, vn, sink_rows)


def _route(logits):
    lane = lax.broadcasted_iota(jnp.int32, logits.shape, 1)
    big = jnp.int32(ROUTE_W)
    gmask = lane < N_GROUPS
    gl = jnp.where(gmask, logits, NEG_INF)
    gmax = jnp.max(gl, axis=-1, keepdims=True)
    g_sel = jnp.min(jnp.where(gl == gmax, lane, big), axis=-1, keepdims=True)
    p_g = 1.0 / jnp.sum(jnp.where(gmask, jnp.exp(logits - gmax), 0.0), axis=-1, keepdims=True)
    lo = N_GROUPS + EXPERTS_PER_GROUP * g_sel
    emask = (lane >= lo) & (lane < lo + EXPERTS_PER_GROUP)
    el = jnp.where(emask, logits, NEG_INF)
    v1 = jnp.max(el, axis=-1, keepdims=True)
    i1 = jnp.min(jnp.where(el == v1, lane, big), axis=-1, keepdims=True)
    el2 = jnp.where(lane == i1, NEG_INF, el)
    v2 = jnp.max(el2, axis=-1, keepdims=True)
    i2 = jnp.min(jnp.where(el2 == v2, lane, big), axis=-1, keepdims=True)
    e21 = jnp.exp(v2 - v1)
    w1 = p_g / (1.0 + e21)
    w2 = p_g * e21 / (1.0 + e21)
    out = jnp.where(lane == 0, (i1 - N_GROUPS).astype(F32), 0.0)
    out = jnp.where(lane == 1, (i2 - N_GROUPS).astype(F32), out)
    out = jnp.where(lane == 2, w1, out)
    out = jnp.where(lane == 3, w2, out)
    return out


def _out_proj_kernel(x_ref, yf_ref, ys_ref, yn_ref, wo_ref, g1_ref, sh_ref, sc_ref, gn_ref, wr_ref, br_ref,
                     x1_ref, h2_ref, rt_ref):
    y = (_dot_tn(yf_ref[0], wo_ref[F_LO:F_HI, :]) + _dot_tn(ys_ref[0], wo_ref[QS_LO:QS_HI, :])
         + _dot_tn(yn_ref[0], wo_ref[QN_LO:QN_HI, :]))
    x1 = x_ref[0] + g1_ref[0] * y
    x1_ref[0] = x1
    ms = jnp.mean(x1 * x1, axis=-1, keepdims=True)
    h2 = (x1 * lax.rsqrt(ms + EPS) * gn_ref[...]) * (1.0 + sc_ref[0]) + sh_ref[0]
    h2_ref[0] = h2.astype(BF16)
    rt_ref[0] = _route(_split_dot(h2, wr_ref[...]) + br_ref[...])


def _out_proj(x, yf, ys, yn, w_out, gate, shift, scale, gain, w_route, b_route, *, tm):
    b, t, d = x.shape
    vec = pl.BlockSpec((1, 1, d), lambda i, j: (i, 0, 0))
    feat = lambda r: pl.BlockSpec((1, r, tm), lambda i, j: (i, 0, j))
    tok = lambda w: pl.BlockSpec((1, tm, w), lambda i, j: (i, j, 0))
    return pl.pallas_call(
        _out_proj_kernel,
        out_shape=[jax.ShapeDtypeStruct((b, t, d), F32), jax.ShapeDtypeStruct((b, t, d), BF16),
                   jax.ShapeDtypeStruct((b, t, ROUTE_W), F32)],
        grid=(b, t // tm),
        in_specs=[tok(d), feat(FOURIER_WIDTH), feat(SWA_Q_W), feat(NA_W),
                  pl.BlockSpec((MIX_WIDTH, d), lambda i, j: (0, 0)),
                  vec, vec, vec,
                  pl.BlockSpec((1, d), lambda i, j: (0, 0)),
                  pl.BlockSpec((d, ROUTE_W), lambda i, j: (0, 0)),
                  pl.BlockSpec((1, ROUTE_W), lambda i, j: (0, 0))],
        out_specs=[tok(d), tok(d), tok(ROUTE_W)],
        compiler_params=_cparams("parallel", "parallel"),
        name="out_proj",
    )(x, yf, ys, yn, w_out, gate, shift, scale, gain, w_route, b_route)


def _experts_kernel(te_ref, nt_ref, x_ref, wg_ref, wu_ref, wd_ref, y_ref, wg_s, wu_s, wd_s):
    i = pl.program_id(0)
    fresh = jnp.logical_or(i == 0, te_ref[i] != te_ref[jnp.maximum(i - 1, 0)])

    @pl.when(jnp.logical_and(fresh, i < nt_ref[0]))
    def _():
        wg_s[...] = wg_ref[0, 0].astype(BF16)
        wu_s[...] = wu_ref[0, 0].astype(BF16)
        wd_s[...] = wd_ref[0, 0].astype(BF16)

    @pl.when(i < nt_ref[0])
    def _():
        x = x_ref[...].astype(BF16)
        g = _dot(x, wg_s[...])
        u = _dot(x, wu_s[...])
        hid = (g * (1.0 / (1.0 + jnp.exp(-g)))) * u
        y_ref[...] = _dot(hid.astype(BF16), wd_s[...])

    @pl.when(i >= nt_ref[0])
    def _():
        y_ref[...] = jnp.zeros(y_ref.shape, F32)


def _experts(xs, tile_expert, n_tiles, w_gate, w_up, w_down, layer):
    r, d = xs.shape
    de = w_gate.shape[3]
    last = lambda i, te, nt: jnp.minimum(i, nt[0] - 1)
    wspec = lambda a, b: pl.BlockSpec((1, 1, a, b), lambda i, te, nt: (layer, te[i], 0, 0))
    return pl.pallas_call(
        _experts_kernel,
        out_shape=jax.ShapeDtypeStruct((r, d), F32),
        grid_spec=pltpu.PrefetchScalarGridSpec(
            num_scalar_prefetch=2,
            grid=(r // MOE_TILE,),
            in_specs=[
                pl.BlockSpec((MOE_TILE, d), lambda i, te, nt: (last(i, te, nt), 0)),
                wspec(d, de), wspec(d, de), wspec(de, d),
            ],
            out_specs=pl.BlockSpec((MOE_TILE, d), lambda i, te, nt: (i, 0)),
            scratch_shapes=[pltpu.VMEM((d, de), BF16), pltpu.VMEM((d, de), BF16), pltpu.VMEM((de, d), BF16)],
        ),
        compiler_params=_cparams("arbitrary"),
        name="experts",
    )(tile_expert, n_tiles, xs, w_gate, w_up, w_down)


def _chunk_copy(local, remote, sem, dst_ref, c, q, to_remote):
    loc = local.at[pl.ds(pl.multiple_of(q * ROW_BLOCK, ROW_BLOCK), ROW_BLOCK)]
    rem = remote.at[pl.ds(pl.multiple_of(dst_ref[c, q], ROW_BLOCK), ROW_BLOCK)]
    return pltpu.make_async_copy(loc, rem, sem) if to_remote else pltpu.make_async_copy(rem, loc, sem)


def _chunk_copies(local, remote, sem, dst_ref, nblk_ref, c, *, to_remote, wait):
    n = nblk_ref[c]
    if wait:
        def body(q, carry):
            _chunk_copy(local, remote, sem, dst_ref, c, q, to_remote).wait()
            return carry

        lax.fori_loop(0, n, body, 0)
        return

    def pair(i, carry):
        q = 2 * i
        _chunk_copy(local, remote, sem, dst_ref, c, q, to_remote).start(priority=0)

        @pl.when(q + 1 < n)
        def _():
            _chunk_copy(local, remote, sem, dst_ref, c, q + 1, to_remote).start(priority=1)

        return carry

    lax.fori_loop(0, (n + 1) // 2, pair, 0)


def _tail_copies(zero, remote, sem, tstart_ref, tn_ref, *, wait):
    def per_expert(e, carry):
        def body(q, inner):
            row = pl.multiple_of(tstart_ref[e] + q * ROW_BLOCK, ROW_BLOCK)
            cp = pltpu.make_async_copy(zero, remote.at[pl.ds(row, ROW_BLOCK)], sem)
            if wait:
                cp.wait()
            else:
                cp.start()
            return inner

        return lax.fori_loop(0, tn_ref[e], body, carry)

    lax.fori_loop(0, N_EXPERTS, per_expert, 0)


def _unused_tile_copies(zero, remote, sem, nt_ref, total_tiles, *, wait):
    def body(t, carry):
        row = pl.multiple_of(t * MOE_TILE, MOE_TILE)
        cp = pltpu.make_async_copy(zero, remote.at[pl.ds(row, MOE_TILE)], sem)
        if wait:
            cp.wait()
        else:
            cp.start()
        return carry

    lax.fori_loop(nt_ref[0], total_tiles, body, 0)


def _dispatch_kernel(dst_ref, nblk_ref, tstart_ref, tn_ref, nt_ref, *rest, part_chunks):
    n_parts = len(part_chunks)
    h_refs = rest[:n_parts]
    pos_ref, xs_ref, buf, zero, sem, zsem = rest[n_parts:]
    n_chunks = sum(part_chunks)
    c = pl.program_id(0)
    slot = c % 2
    row = lax.broadcasted_iota(jnp.int32, (MOE_LROWS, MOE_CHUNK), 0)
    hit = (row == pos_ref[0, 0:1, :]) | (row == pos_ref[0, 1:2, :])
    sel = jnp.where(hit, 1.0, 0.0).astype(BF16)
    first = 0
    for h_ref, n in zip(h_refs, part_chunks):
        @pl.when((c >= first) & (c < first + n))
        def _(h_ref=h_ref):
            buf[slot] = _dot(sel, h_ref[...])
        first += n
    _chunk_copies(buf.at[slot], xs_ref, sem.at[slot], dst_ref, nblk_ref, c, to_remote=True, wait=False)

    @pl.when(c > 0)
    def _():
        _chunk_copies(buf.at[1 - slot], xs_ref, sem.at[1 - slot], dst_ref, nblk_ref, c - 1,
                      to_remote=True, wait=True)

    @pl.when(c == n_chunks - 1)
    def _():
        total_tiles = xs_ref.shape[0] // MOE_TILE
        zero[...] = jnp.zeros(zero.shape, F32)
        zero8 = zero.at[pl.ds(0, ROW_BLOCK)]
        _tail_copies(zero8, xs_ref, zsem, tstart_ref, tn_ref, wait=False)
        _unused_tile_copies(zero, xs_ref, zsem, nt_ref, total_tiles, wait=False)
        _chunk_copies(buf.at[slot], xs_ref, sem.at[slot], dst_ref, nblk_ref, c, to_remote=True, wait=True)
        _tail_copies(zero8, xs_ref, zsem, tstart_ref, tn_ref, wait=True)
        _unused_tile_copies(zero, xs_ref, zsem, nt_ref, total_tiles, wait=True)


def _dispatch(h_parts, plan):
    d = h_parts[0].shape[1]
    part_chunks = tuple(h.shape[0] // MOE_CHUNK for h in h_parts)
    in_specs = []
    first = 0
    for n in part_chunks:
        in_specs.append(pl.BlockSpec(
            (MOE_CHUNK, d), lambda i, *_, first=first, n=n: (jnp.clip(i - first, 0, n - 1), 0)))
        first += n
    in_specs.append(pl.BlockSpec((1, 8, MOE_CHUNK), lambda i, *_: (i, 0, 0)))
    return pl.pallas_call(
        functools.partial(_dispatch_kernel, part_chunks=part_chunks),
        out_shape=jax.ShapeDtypeStruct((plan["rows"], d), F32),
        grid_spec=pltpu.PrefetchScalarGridSpec(
            num_scalar_prefetch=5,
            grid=(sum(part_chunks),),
            in_specs=in_specs,
            out_specs=pl.BlockSpec(memory_space=pl.ANY),
            scratch_shapes=[pltpu.VMEM((2, MOE_LROWS, d), F32), pltpu.VMEM((MOE_TILE, d), F32),
                            pltpu.SemaphoreType.DMA((2,)), pltpu.SemaphoreType.DMA(())],
        ),
        compiler_params=_cparams("arbitrary"),
        name="dispatch",
    )(plan["dst"], plan["nblk"], plan["tstart"], plan["tn"], plan["n_tiles"], *h_parts, plan["pos"])


def _combine_kernel(dst_ref, nblk_ref, x1_ref, rt_ref, pos_ref, g2_ref, gf_ref, ys_ref, o_ref, buf, sem,
                    *, c0, n_chunks, final_norm):
    i = pl.program_id(0)
    c = i + c0
    slot = i % 2
    gather = functools.partial(_chunk_copies, remote=ys_ref, dst_ref=dst_ref, nblk_ref=nblk_ref, to_remote=False)

    @pl.when(i == 0)
    def _():
        buf[...] = jnp.zeros(buf.shape, F32)
        gather(buf.at[slot], sem=sem.at[slot], c=c, wait=False)

    @pl.when(i + 1 < n_chunks)
    def _():
        gather(buf.at[1 - slot], sem=sem.at[1 - slot], c=c + 1, wait=False)

    gather(buf.at[slot], sem=sem.at[slot], c=c, wait=True)
    yl = buf[slot].astype(BF16)
    row = lax.broadcasted_iota(jnp.int32, (MOE_LROWS, MOE_CHUNK), 0)
    y1 = _dot_tn(jnp.where(row == pos_ref[0, 0:1, :], 1.0, 0.0).astype(BF16), yl)
    y2 = _dot_tn(jnp.where(row == pos_ref[0, 1:2, :], 1.0, 0.0).astype(BF16), yl)
    rt = rt_ref[0]
    x2 = x1_ref[0] + g2_ref[0] * (rt[:, 2:3] * y1 + rt[:, 3:4] * y2)
    if final_norm:
        ms = jnp.mean(x2 * x2, axis=-1, keepdims=True)
        x2 = x2 * lax.rsqrt(ms + EPS) * gf_ref[...]
    o_ref[0] = x2


def _combine(x1, route, plan, gate, g_final, ys, *, c0, final_norm):
    b, t, d = x1.shape
    per_batch = t // MOE_CHUNK
    n_chunks = b * per_batch
    tok = lambda w: pl.BlockSpec((1, MOE_CHUNK, w), lambda i, *_: (i // per_batch, i % per_batch, 0))
    return pl.pallas_call(
        functools.partial(_combine_kernel, c0=c0, n_chunks=n_chunks, final_norm=final_norm),
        out_shape=jax.ShapeDtypeStruct((b, t, d), F32),
        grid_spec=pltpu.PrefetchScalarGridSpec(
            num_scalar_prefetch=2,
            grid=(n_chunks,),
            in_specs=[tok(d), tok(ROUTE_W),
                      pl.BlockSpec((1, 8, MOE_CHUNK), lambda i, *_: (i + c0, 0, 0)),
                      pl.BlockSpec((1, 1, d), lambda i, *_: (i // per_batch, 0, 0)),
                      pl.BlockSpec((1, d), lambda i, *_: (0, 0)),
                      pl.BlockSpec(memory_space=pl.ANY)],
            out_specs=tok(d),
            scratch_shapes=[pltpu.VMEM((2, MOE_LROWS, d), F32), pltpu.SemaphoreType.DMA((2,))],
        ),
        compiler_params=_cparams("arbitrary"),
        name="combine",
    )(plan["dst"], plan["nblk"], x1, route, plan["pos"], gate, g_final, ys)


def _rope_tables(seq):
    half = HEAD_DIM // 4
    t = jnp.arange(seq)
    rows, cols = t // GRID_W, t % GRID_W
    freqs = ROPE_BASE ** (-jnp.arange(half, dtype=F32) / half)

    def cs(pos):
        ang = pos.astype(F32)[None, :] * freqs[:, None]
        return jnp.cos(ang), jnp.sin(ang)

    cr, sr = cs(rows)
    cc, sc = cs(cols)
    return jnp.concatenate([cr, cr, cc, cc], axis=0), jnp.concatenate([-sr, sr, -sc, sc], axis=0)


def _dft_tables(n):
    k = jnp.arange(n)

    def cs(m):
        ang = ((m[:, None] * k[None, :]) % n).astype(F32) * (2.0 * jnp.pi / n)
        return jnp.cos(ang), jnp.sin(ang)

    if n <= GRID_W:
        return cs(k)
    ca, sa = cs(jnp.arange(n // GRID_W) * GRID_W)
    cb, sb = cs(jnp.arange(GRID_W))
    c = ca[:, None, :] * cb[None, :, :] - sa[:, None, :] * sb[None, :, :]
    s = sa[:, None, :] * cb[None, :, :] + ca[:, None, :] * sb[None, :, :]
    return c.reshape(n, n), s.reshape(n, n)


def _channel_dft_t():
    c, s = _dft_tables(FOURIER_GROUP_DIM)
    eye = jnp.eye(FOURIER_GROUPS, dtype=F32)
    scale = FOURIER_GROUP_DIM ** -0.5
    return jnp.concatenate([jnp.kron(eye, c), jnp.kron(eye, s)], axis=0) * scale


def _position_dft(n):
    c, s = _dft_tables(n)
    scale = n ** -0.5
    return (c * scale).astype(BF16), (s * scale).astype(BF16)


def _swa_mask(seq):
    span = 3 * SWA_BLOCK
    nb = seq // SWA_BLOCK
    kk = jnp.arange(span)[:, None]
    q = jnp.arange(SWA_BLOCK)[None, :]
    tiles = []
    for n, ks in ((0, 0), (1, 0), (nb - 1, seq - span)):
        ok = jnp.abs(ks + kk - (n * SWA_BLOCK + q)) <= SWA_WINDOW
        tiles.append(jnp.tile(jnp.where(ok, 0.0, NEG_INF).astype(F32), (1, SWA_GROUP)))
    return jnp.stack(tiles)


def _sink_rows(sink, width):
    return jnp.repeat(sink.astype(F32).reshape(SWA_KV_HEADS, SWA_GROUP), width, axis=1)[:, None, :]


def _moe_plan(route2d):
    i32 = jnp.int32
    n = route2d.shape[0]
    nc = n // MOE_CHUNK
    ids = route2d[:, 0:2].astype(i32)
    onehot = (ids[:, :, None] == jnp.arange(N_EXPERTS, dtype=i32)).astype(i32)
    onehot = onehot.reshape(nc, 2 * MOE_CHUNK, N_EXPERTS)
    pair = jnp.arange(2 * MOE_CHUNK, dtype=i32)
    earlier = (pair[None, :] < pair[:, None]).astype(BF16)
    before = jnp.einsum("pq,cqe->cpe", earlier, onehot.astype(BF16), preferred_element_type=F32).astype(i32)
    rank = jnp.sum(before * onehot, axis=-1)
    cnt = jnp.sum(onehot, axis=1)
    run = (cnt + ROW_BLOCK - 1) // ROW_BLOCK * ROW_BLOCK
    lo = jnp.cumsum(run, axis=1) - run
    pos = (jnp.sum(onehot * lo[:, None, :], axis=-1) + rank).reshape(nc, MOE_CHUNK, 2)
    pos = jnp.concatenate([jnp.swapaxes(pos, 1, 2), jnp.full((nc, 6, MOE_CHUNK), -1, i32)], axis=1)
    seg = jnp.sum(run, axis=0)
    padded = (seg + MOE_TILE - 1) // MOE_TILE * MOE_TILE
    start = jnp.cumsum(padded) - padded
    off = start[None, :] + jnp.cumsum(run, axis=0) - run
    q = jnp.arange(MOE_NBLK, dtype=i32)[None, :, None] * ROW_BLOCK
    inside = (lo[:, None, :] <= q) & (q < (lo + run)[:, None, :])
    dst = jnp.sum(jnp.where(inside, off[:, None, :] + q - lo[:, None, :], 0), axis=-1)
    nblk = jnp.sum(run, axis=1) // ROW_BLOCK
    ends = start + padded
    rows = -(-(2 * n + (ROW_BLOCK - 1) * N_EXPERTS * nc + (MOE_TILE - 1) * N_EXPERTS) // MOE_TILE) * MOE_TILE
    tile_row = jnp.arange(rows // MOE_TILE, dtype=i32) * MOE_TILE
    te = jnp.minimum(jnp.sum((ends[None, :] <= tile_row[:, None]).astype(i32), axis=1), N_EXPERTS - 1)
    n_tiles = ends[-1] // MOE_TILE
    te_last = jnp.sum(jnp.where(tile_row == (n_tiles - 1) * MOE_TILE, te, 0))
    te = jnp.where(tile_row < ends[-1], te, te_last)
    return {"pos": pos, "dst": dst.astype(i32), "nblk": nblk.astype(i32), "tstart": (start + seg).astype(i32),
            "tn": ((padded - seg) // ROW_BLOCK).astype(i32), "te": te.astype(i32),
            "n_tiles": n_tiles.astype(i32).reshape(1), "rows": rows}


def kernel(x, c, ctx, c_ctx, w_mod, b_mod, g_norm1, g_norm2, w_in, w_four, w_out, swa_sink, na_rpb,
           w_route_group, b_route_group, w_route_expert, b_route_expert, w_exp_gate, w_exp_up,
           w_exp_down, g_final):
    b, s, d = x.shape
    lc = ctx.shape[1]
    depth = w_mod.shape[0]
    tm = 512

    c_rows = jnp.concatenate([c, c_ctx[None, :], jnp.zeros((7, d), F32)], axis=0)
    mod = _modulation(c_rows, w_mod, b_mod)

    cos_t, sin_t = _rope_tables(s)
    cos_c, sin_c = cos_t[:, :lc], sin_t[:, :lc]
    bd_t = _channel_dft_t().astype(BF16)
    cn, sn = _position_dft(s)
    cn_c, sn_c = _position_dft(lc)
    mask = _swa_mask(s)
    pad = jnp.zeros((d, ROUTE_W - N_GROUPS - N_EXPERTS), F32)

    xc = ctx
    for layer in range(depth):
        with_ctx_out = layer < depth - 1
        lat = [mod[layer, :b, i * d:(i + 1) * d][:, None, :] for i in range(6)]
        cx = [jnp.broadcast_to(mod[layer, b, i * d:(i + 1) * d][None, None, :], (b, 1, d)) for i in range(6)]
        sh1, sc1, g1, sh2, sc2, g2 = lat
        shc1, scc1, gc1, shc2, scc2, gc2 = cx
        gn1 = g_norm1[layer][None, :]
        gn2 = g_norm2[layer][None, :]
        w_t = w_in[layer].T.astype(BF16)
        wf_t = w_four[layer].T.astype(BF16)
        wo = w_out[layer].astype(BF16)
        w_r = jnp.concatenate([w_route_group[layer], w_route_expert[layer], pad], axis=1)
        b_r = jnp.concatenate([b_route_group[layer], b_route_expert[layer],
                               jnp.zeros((ROUTE_W - N_GROUPS - N_EXPERTS,), F32)])[None, :]
        sink_lat = _sink_rows(swa_sink[layer], SWA_BLOCK)
        sink_ctx = _sink_rows(swa_sink[layer], lc)

        fz, qs, qn, ks, vs, kn, vn = _in_proj(x, sh1, sc1, gn1, w_t, bd_t, cos_t, sin_t,
                                              with_q=True, rope=True, tm=tm)
        if with_ctx_out:
            fz_c, qs_c, qn_c, ks_c, vs_c, kn_c, vn_c = _in_proj(xc, shc1, scc1, gn1, w_t, bd_t, cos_c, sin_c,
                                                                with_q=True, rope=False, tm=lc)
        else:
            ks_c, vs_c, kn_c, vn_c = _in_proj(xc, shc1, scc1, gn1, w_t[KS_LO:], bd_t, cos_c, sin_c,
                                              with_q=False, rope=False, tm=lc)

        yf = _fourier(fz, cn, sn, wf_t, tk=512)
        ys = _swa(qs, ks, vs, ks_c, vs_c, mask, sink_lat)
        yn = _na(qn, kn, vn, kn_c, vn_c, _na_bias(na_rpb[layer], s))
        x1, h2, route = _out_proj(x, yf, ys, yn, wo, g1, sh2, sc2, gn2, w_r, b_r, tm=tm)

        moe_w = (w_exp_gate, w_exp_up, w_exp_down, layer)
        if with_ctx_out:
            yf_c = _fourier(fz_c, cn_c, sn_c, wf_t, tk=lc)
            ys_c, yn_c = _ctx_attn(qs_c, ks_c, vs_c, qn_c, kn_c, vn_c, sink_ctx)
            xc1, hc2, route_c = _out_proj(xc, yf_c, ys_c, yn_c, wo, gc1, shc2, scc2, gn2, w_r, b_r, tm=lc)
            n_lat = b * s
            lat_chunks = n_lat // MOE_CHUNK
            plan = _moe_plan(jnp.concatenate([route.reshape(n_lat, ROUTE_W),
                                              route_c.reshape(b * lc, ROUTE_W)], axis=0))
            xs = _dispatch([h2.reshape(n_lat, d), hc2.reshape(b * lc, d)], plan)
            ye = _experts(xs, plan["te"], plan["n_tiles"], *moe_w)
            x = _combine(x1, route, plan, g2, g_final[None, :], ye, c0=0, final_norm=False)
            ctx_chunks = b * lc // MOE_CHUNK
            xc = _combine(xc1.reshape(ctx_chunks, MOE_CHUNK, d), route_c.reshape(ctx_chunks, MOE_CHUNK, ROUTE_W),
                          plan, gc2[:ctx_chunks], g_final[None, :], ye, c0=lat_chunks,
                          final_norm=False).reshape(b, lc, d)
        else:
            plan = _moe_plan(route.reshape(b * s, ROUTE_W))
            xs = _dispatch([h2.reshape(b * s, d)], plan)
            ye = _experts(xs, plan["te"], plan["n_tiles"], *moe_w)
            x = _combine(x1, route, plan, g2, g_final[None, :], ye, c0=0, final_norm=True)
    return x
```

```python
import functools

import jax
import jax.numpy as jnp
from jax import lax
from jax.experimental import pallas as pl
from jax.experimental.pallas import tpu as pltpu

F32 = jnp.float32
BF16 = jnp.bfloat16

D_MODEL = 1024
GRID_W = 64
HEAD_DIM = 64
FOURIER_WIDTH = D_MODEL // 4
FOURIER_GROUPS = 4
FOURIER_GROUP_DIM = FOURIER_WIDTH // FOURIER_GROUPS
SWA_HEADS = (3 * D_MODEL // 8) // HEAD_DIM
SWA_KV_HEADS = 2
SWA_GROUP = SWA_HEADS // SWA_KV_HEADS
SWA_WINDOW = 128
SWA_BLOCK = 128
NA_HEADS = (3 * D_MODEL // 8) // HEAD_DIM
NA_WIN_R = 8
NA_WIN_C = 16
ROPE_BASE = 10000.0
N_GROUPS = 4
EXPERTS_PER_GROUP = 8
N_EXPERTS = N_GROUPS * EXPERTS_PER_GROUP
D_EXPERT = D_MODEL // 2
EPS = 1e-6
NEG_INF = -1e30

SWA_Q_W = SWA_HEADS * HEAD_DIM
SWA_KV_W = SWA_KV_HEADS * HEAD_DIM
NA_W = NA_HEADS * HEAD_DIM
MIX_WIDTH = FOURIER_WIDTH + SWA_Q_W + NA_W
Q_COLS = MIX_WIDTH
IN_COLS = 2 * MIX_WIDTH

F_LO, F_HI = 0, FOURIER_WIDTH
QS_LO, QS_HI = F_HI, F_HI + SWA_Q_W
QN_LO, QN_HI = QS_HI, QS_HI + NA_W
KS_LO, KS_HI = QN_HI, QN_HI + SWA_KV_W
VS_LO, VS_HI = KS_HI, KS_HI + SWA_KV_W
KN_LO, KN_HI = VS_HI, VS_HI + NA_W
VN_LO, VN_HI = KN_HI, KN_HI + NA_W

LANE = 128
ROUTE_W = LANE
NA_QROWS = 4
NA_KROWS = NA_QROWS + NA_WIN_R
ATTN_KCHUNK = LANE
MOE_TILE = 256
ROW_BLOCK = 8
MOE_CHUNK = 512
MOE_LROWS = 2 * MOE_CHUNK + (ROW_BLOCK - 1) * N_EXPERTS
MOE_NBLK = MOE_LROWS // ROW_BLOCK
VMEM_LIMIT = 48 * 1024 * 1024


def _cparams(*sem):
    return pltpu.CompilerParams(dimension_semantics=sem, vmem_limit_bytes=VMEM_LIMIT)


def _dot(a, b):
    return jnp.dot(a, b, preferred_element_type=F32)


def _dot_tn(a, b):
    return lax.dot_general(a, b, (((0,), (0,)), ((), ())), preferred_element_type=F32)


def _dot_nt(a, b):
    return lax.dot_general(a, b, (((1,), (1,)), ((), ())), preferred_element_type=F32)


def _split_dot(a, w):
    a_hi = a.astype(BF16)
    a_lo = (a - a_hi.astype(F32)).astype(BF16)
    w_hi = w.astype(BF16)
    w_lo = (w - w_hi.astype(F32)).astype(BF16)
    return _dot(a_hi, w_hi) + (_dot(a_hi, w_lo) + _dot(a_lo, w_hi))


def _mod_kernel(c_ref, w_ref, b_ref, o_ref):
    c = c_ref[...]
    a = c * (1.0 / (1.0 + jnp.exp(-c)))
    o_ref[0] = _split_dot(a, w_ref[0]) + b_ref[0]


def _modulation(c_rows, w_mod, b_mod):
    depth, d, n6 = w_mod.shape
    r = c_rows.shape[0]
    tn = 1536
    return pl.pallas_call(
        _mod_kernel,
        out_shape=jax.ShapeDtypeStruct((depth, r, n6), F32),
        grid=(depth, n6 // tn),
        in_specs=[
            pl.BlockSpec((r, d), lambda l, j: (0, 0)),
            pl.BlockSpec((1, d, tn), lambda l, j: (l, 0, j)),
            pl.BlockSpec((1, 1, tn), lambda l, j: (l, 0, j)),
        ],
        out_specs=pl.BlockSpec((1, r, tn), lambda l, j: (l, 0, j)),
        compiler_params=_cparams("parallel", "parallel"),
        name="modulation",
    )(c_rows, w_mod, b_mod.reshape(depth, 1, n6))


def _rope_rows(t, cos_t, sin_t, n_heads):
    outs = []
    for h in range(n_heads):
        th = t[HEAD_DIM * h:HEAD_DIM * (h + 1)]
        sw = jnp.concatenate([th[16:32], th[0:16], th[48:64], th[32:48]], axis=0)
        outs.append(th * cos_t + sw * sin_t)
    return jnp.concatenate(outs, axis=0)


def _in_proj_kernel(x_ref, sh_ref, sc_ref, g_ref, wt_ref, bdt_ref, cos_ref, sin_ref, *outs, with_q, rope):
    xf = x_ref[0]
    ms = jnp.mean(xf * xf, axis=-1, keepdims=True)
    y = xf * lax.rsqrt(ms + EPS) * g_ref[...]
    h = y * (1.0 + sc_ref[0]) + sh_ref[0]
    pt = _dot_nt(wt_ref[...], h.astype(BF16))
    q_scale = HEAD_DIM ** -0.5
    if with_q:
        fz_ref, qs_ref, qn_ref, ks_ref, vs_ref, kn_ref, vn_ref = outs
        fz_ref[0] = _dot(bdt_ref[...], pt[F_LO:F_HI].astype(BF16)).astype(BF16)
        qs = pt[QS_LO:QS_HI]
        if rope:
            qs = _rope_rows(qs, cos_ref[...], sin_ref[...], SWA_HEADS)
        qs_ref[0] = (qs * q_scale).astype(BF16)
        qn_ref[0] = (pt[QN_LO:QN_HI] * q_scale).astype(BF16)
        off = 0
    else:
        ks_ref, vs_ref, kn_ref, vn_ref = outs
        off = KS_LO
    ks = pt[KS_LO - off:KS_HI - off]
    if rope:
        ks = _rope_rows(ks, cos_ref[...], sin_ref[...], SWA_KV_HEADS)
    ks_ref[0] = ks.astype(BF16)
    vs_ref[0] = pt[VS_LO - off:VS_HI - off].astype(BF16)
    kn_ref[0] = pt[KN_LO - off:KN_HI - off].astype(BF16)
    vn_ref[0] = pt[VN_LO - off:VN_HI - off].astype(BF16)


def _in_proj(x, shift, scale, gain, w_t, bd_t, cos_t, sin_t, *, with_q, rope, tm):
    b, t, d = x.shape
    nf = w_t.shape[0]
    rows = ([2 * FOURIER_WIDTH, SWA_Q_W, NA_W] if with_q else []) + [SWA_KV_W, SWA_KV_W, NA_W, NA_W]
    return pl.pallas_call(
        functools.partial(_in_proj_kernel, with_q=with_q, rope=rope),
        out_shape=[jax.ShapeDtypeStruct((b, r, t), BF16) for r in rows],
        grid=(b, t // tm),
        in_specs=[
            pl.BlockSpec((1, tm, d), lambda i, j: (i, j, 0)),
            pl.BlockSpec((1, 1, d), lambda i, j: (i, 0, 0)),
            pl.BlockSpec((1, 1, d), lambda i, j: (i, 0, 0)),
            pl.BlockSpec((1, d), lambda i, j: (0, 0)),
            pl.BlockSpec((nf, d), lambda i, j: (0, 0)),
            pl.BlockSpec(bd_t.shape, lambda i, j: (0, 0)),
            pl.BlockSpec((HEAD_DIM, tm), lambda i, j: (0, j)),
            pl.BlockSpec((HEAD_DIM, tm), lambda i, j: (0, j)),
        ],
        out_specs=[pl.BlockSpec((1, r, tm), lambda i, j: (i, 0, j)) for r in rows],
        compiler_params=_cparams("parallel", "parallel"),
        name="in_proj_q" if with_q else "in_proj_kv",
    )(x, shift, scale, gain, w_t, bd_t, cos_t, sin_t)


def _fourier_kernel(fz_ref, cn_ref, sn_ref, wft_ref, o_ref):
    zc = fz_ref[0, 0:FOURIER_WIDTH, :]
    zs = fz_ref[0, FOURIER_WIDTH:2 * FOURIER_WIDTH, :]
    y = _dot(zc, cn_ref[...]) - _dot(zs, sn_ref[...])
    o_ref[0] = _dot(wft_ref[...], y.astype(BF16)).astype(BF16)


def _fourier(fz, cn, sn, wf_t, *, tk):
    b, _, t = fz.shape
    return pl.pallas_call(
        _fourier_kernel,
        out_shape=jax.ShapeDtypeStruct((b, FOURIER_WIDTH, t), BF16),
        grid=(t // tk, b),
        in_specs=[
            pl.BlockSpec((1, 2 * FOURIER_WIDTH, t), lambda k, i: (i, 0, 0)),
            pl.BlockSpec((t, tk), lambda k, i: (0, k)),
            pl.BlockSpec((t, tk), lambda k, i: (0, k)),
            pl.BlockSpec((FOURIER_WIDTH, FOURIER_WIDTH), lambda k, i: (0, 0)),
        ],
        out_specs=pl.BlockSpec((1, FOURIER_WIDTH, tk), lambda k, i: (i, 0, k)),
        compiler_params=_cparams("parallel", "parallel"),
        name="fourier",
    )(fz, cn, sn, wf_t)


def _key_chunks(k, v, bias=None):
    n = k.shape[1] // ATTN_KCHUNK
    cut = lambda a, j, axis: lax.slice_in_dim(a, j * ATTN_KCHUNK, (j + 1) * ATTN_KCHUNK, axis=axis)
    return [(cut(k, j, 1), cut(v, j, 1), None if bias is None else cut(bias, j, 0)) for j in range(n)]


def _logits(q_t, chunks):
    return _dot_tn(jnp.concatenate([k_t for k_t, _, _ in chunks], axis=1), q_t)


def _softmax_pv(s, chunks, sink_row):
    pieces = []
    off = 0
    for k_t, _, bias in chunks:
        piece = s[off:off + k_t.shape[1]]
        pieces.append(piece if bias is None else piece + bias)
        off += k_t.shape[1]
    m = functools.reduce(jnp.maximum, [jnp.max(p, axis=0, keepdims=True) for p in pieces])
    if sink_row is not None:
        m = jnp.maximum(m, sink_row)
    probs = [jnp.exp(p - m) for p in pieces]
    den = functools.reduce(jnp.add, [jnp.sum(p, axis=0, keepdims=True) for p in probs])
    if sink_row is not None:
        den = den + jnp.exp(sink_row - m)
    v_all = jnp.concatenate([v_t for _, v_t, _ in chunks], axis=1)
    p_all = jnp.concatenate([p.astype(BF16) for p in probs], axis=0)
    return _dot(v_all, p_all) / den


def _attend(q_t, chunks, sink_row):
    return _softmax_pv(_logits(q_t, chunks), chunks, sink_row)


def _attend_blocks(n_blocks, make_block, sink_row, store):
    cur = make_block(0)
    s_cur = _logits(*cur)
    for j in range(n_blocks):
        nxt = s_nxt = None
        if j + 1 < n_blocks:
            nxt = make_block(j + 1)
            s_nxt = _logits(*nxt)
        store(j, _softmax_pv(s_cur, cur[1], sink_row))
        cur, s_cur = nxt, s_nxt


def _swa_kernel(q_ref, k_ref, v_ref, kc_ref, vc_ref, mask_ref, sink_ref, o_ref, *, seq):
    nb = seq // SWA_BLOCK
    ctx_chunks = _key_chunks(kc_ref[0], vc_ref[0])

    def make_block(n):
        q0 = n * SWA_BLOCK
        q_t = jnp.concatenate(
            [q_ref[0, HEAD_DIM * h:HEAD_DIM * (h + 1), q0:q0 + SWA_BLOCK] for h in range(SWA_GROUP)], axis=1)
        chunks = [(k_ref[0, :, q0:q0 + SWA_BLOCK], v_ref[0, :, q0:q0 + SWA_BLOCK], None)] + ctx_chunks
        for side, kb in ((0, n - 1), (1, n + 1)):
            if 0 <= kb < nb:
                k0 = kb * SWA_BLOCK
                chunks.append((k_ref[0, :, k0:k0 + SWA_BLOCK], v_ref[0, :, k0:k0 + SWA_BLOCK], mask_ref[side]))
        return q_t, chunks

    def store(n, o):
        q0 = n * SWA_BLOCK
        for h in range(SWA_GROUP):
            o_ref[0, HEAD_DIM * h:HEAD_DIM * (h + 1), q0:q0 + SWA_BLOCK] = (
                o[:, SWA_BLOCK * h:SWA_BLOCK * (h + 1)].astype(BF16))

    _attend_blocks(nb, make_block, sink_ref[0], store)


def _swa(qs, ks, vs, kc, vc, mask, sink_rows):
    b, _, t = qs.shape
    lc = kc.shape[2]
    gw = SWA_GROUP * HEAD_DIM
    return pl.pallas_call(
        functools.partial(_swa_kernel, seq=t),
        out_shape=jax.ShapeDtypeStruct((b, SWA_Q_W, t), BF16),
        grid=(b, SWA_KV_HEADS),
        in_specs=[
            pl.BlockSpec((1, gw, t), lambda i, g: (i, g, 0)),
            pl.BlockSpec((1, HEAD_DIM, t), lambda i, g: (i, g, 0)),
            pl.BlockSpec((1, HEAD_DIM, t), lambda i, g: (i, g, 0)),
            pl.BlockSpec((1, HEAD_DIM, lc), lambda i, g: (i, g, 0)),
            pl.BlockSpec((1, HEAD_DIM, lc), lambda i, g: (i, g, 0)),
            pl.BlockSpec((2, SWA_BLOCK, SWA_GROUP * SWA_BLOCK), lambda i, g: (0, 0, 0)),
            pl.BlockSpec((1, 1, SWA_GROUP * SWA_BLOCK), lambda i, g: (g, 0, 0)),
        ],
        out_specs=pl.BlockSpec((1, gw, t), lambda i, g: (i, g, 0)),
        compiler_params=_cparams("parallel", "parallel"),
        name="swa",
    )(qs, ks, vs, kc, vc, mask, sink_rows)


def _na_bias_kernel(rpb_ref, o_ref, u_ref, *, total_rows):
    hd = pl.program_id(0)
    kc = lax.broadcasted_iota(jnp.int32, (GRID_W, LANE), 0)
    lane = lax.broadcasted_iota(jnp.int32, (GRID_W, LANE), 1)
    qc = lane % GRID_W
    dc = jnp.clip(kc - qc, -(NA_WIN_C - 1), NA_WIN_C - 1) + (NA_WIN_C - 1)
    c0 = jnp.clip(qc - NA_WIN_C // 2, 0, GRID_W - NA_WIN_C)
    valid_c = (kc >= c0) & (kc < c0 + NA_WIN_C)
    n_dr = 2 * NA_WIN_R - 1
    for dr in range(n_dr):
        u = jnp.full((GRID_W, LANE), NEG_INF, F32)
        for d in range(2 * NA_WIN_C - 1):
            u = jnp.where(valid_c & (dc == d), rpb_ref[hd, dr, d], u)
        u_ref[dr] = u
    n_rows = o_ref.shape[2] // GRID_W
    block_types = [(0, 0), (NA_QROWS, 0), (total_rows - NA_QROWS, total_rows - NA_KROWS)]
    neg = jnp.full((GRID_W, LANE), NEG_INF, F32)
    for t, (r_base, k_base) in enumerate(block_types):
        for kl in range(n_rows):
            kr = k_base + kl
            for lg in range(NA_QROWS // 2):
                halves = []
                for rq in (2 * lg, 2 * lg + 1):
                    r = r_base + rq
                    r0 = min(max(r - NA_WIN_R // 2, 0), total_rows - NA_WIN_R)
                    ok = r0 <= kr < r0 + NA_WIN_R
                    halves.append(u_ref[kr - r + NA_WIN_R - 1] if ok else neg)
                o_ref[0, t, GRID_W * kl:GRID_W * (kl + 1), LANE * lg:LANE * (lg + 1)] = jnp.where(
                    lane < GRID_W, halves[0], halves[1])


def _na_bias(rpb, seq):
    nh = rpb.shape[0]
    return pl.pallas_call(
        functools.partial(_na_bias_kernel, total_rows=seq // GRID_W),
        out_shape=jax.ShapeDtypeStruct((nh, 3, NA_KROWS * GRID_W, NA_QROWS * GRID_W), F32),
        grid=(nh,),
        in_specs=[pl.BlockSpec(memory_space=pltpu.SMEM)],
        out_specs=pl.BlockSpec((1, 3, NA_KROWS * GRID_W, NA_QROWS * GRID_W), lambda h: (h, 0, 0, 0)),
        scratch_shapes=[pltpu.VMEM((2 * NA_WIN_R - 1, GRID_W, LANE), F32)],
        compiler_params=_cparams("parallel"),
        name="na_bias",
    )(rpb)


def _na_kernel(q_ref, k_ref, v_ref, kc_ref, vc_ref, bias_ref, o_ref, *, seq):
    n_rows = seq // GRID_W
    qw = NA_QROWS * GRID_W
    kw = NA_KROWS * GRID_W
    ctx_chunks = _key_chunks(kc_ref[0], vc_ref[0])
    nblk = n_rows // NA_QROWS
    rows_per_chunk = ATTN_KCHUNK // GRID_W

    def make_block(j):
        k_row = min(max(j * NA_QROWS - NA_WIN_R // 2, 0), n_rows - NA_KROWS)
        btype = 0 if j == 0 else (2 if j == nblk - 1 else 1)
        q0 = j * qw
        chunks = list(ctx_chunks)
        for cj in range(kw // ATTN_KCHUNK):
            first = k_row + cj * rows_per_chunk
            in_window = False
            for r in range(j * NA_QROWS, (j + 1) * NA_QROWS):
                r0 = min(max(r - NA_WIN_R // 2, 0), n_rows - NA_WIN_R)
                in_window = in_window or (first < r0 + NA_WIN_R and first + rows_per_chunk > r0)
            if in_window:
                k0 = first * GRID_W
                chunks.append((k_ref[0, :, k0:k0 + ATTN_KCHUNK], v_ref[0, :, k0:k0 + ATTN_KCHUNK],
                               bias_ref[0, btype, cj * ATTN_KCHUNK:(cj + 1) * ATTN_KCHUNK, :]))
        return q_ref[0, :, q0:q0 + qw], chunks

    def store(j, o):
        o_ref[0, :, j * qw:(j + 1) * qw] = o.astype(BF16)

    _attend_blocks(nblk, make_block, None, store)


def _na(qn, kn, vn, kc, vc, bias):
    b, _, t = qn.shape
    lc = kc.shape[2]
    head = lambda i, h: (i, h, 0)
    return pl.pallas_call(
        functools.partial(_na_kernel, seq=t),
        out_shape=jax.ShapeDtypeStruct((b, NA_W, t), BF16),
        grid=(b, NA_HEADS),
        in_specs=[
            pl.BlockSpec((1, HEAD_DIM, t), head),
            pl.BlockSpec((1, HEAD_DIM, t), head),
            pl.BlockSpec((1, HEAD_DIM, t), head),
            pl.BlockSpec((1, HEAD_DIM, lc), head),
            pl.BlockSpec((1, HEAD_DIM, lc), head),
            pl.BlockSpec((1,) + bias.shape[1:], lambda i, h: (h, 0, 0, 0)),
        ],
        out_specs=pl.BlockSpec((1, HEAD_DIM, t), head),
        compiler_params=_cparams("parallel", "parallel"),
        name="na",
    )(qn, kn, vn, kc, vc, bias)


def _ctx_attn_kernel(qs_ref, ks_ref, vs_ref, qn_ref, kn_ref, vn_ref, sink_ref, ys_ref, yn_ref):
    lc = qs_ref.shape[2]
    for g in range(SWA_KV_HEADS):
        q_t = jnp.concatenate(
            [qs_ref[0, HEAD_DIM * (SWA_GROUP * g + h):HEAD_DIM * (SWA_GROUP * g + h + 1), :] for h in range(SWA_GROUP)],
            axis=1)
        kv = slice(HEAD_DIM * g, HEAD_DIM * (g + 1))
        o = _attend(q_t, _key_chunks(ks_ref[0, kv, :], vs_ref[0, kv, :]), sink_ref[g])
        for h in range(SWA_GROUP):
            hh = SWA_GROUP * g + h
            ys_ref[0, HEAD_DIM * hh:HEAD_DIM * (hh + 1), :] = o[:, lc * h:lc * (h + 1)].astype(BF16)
    for h in range(NA_HEADS):
        sl = slice(HEAD_DIM * h, HEAD_DIM * (h + 1))
        o = _attend(qn_ref[0, sl, :], _key_chunks(kn_ref[0, sl, :], vn_ref[0, sl, :]), None)
        yn_ref[0, sl, :] = o.astype(BF16)


def _ctx_attn(qs, ks, vs, qn, kn, vn, sink_rows):
    b, _, lc = qs.shape
    full = lambda a: pl.BlockSpec((1,) + a.shape[1:], lambda i: (i, 0, 0))
    return pl.pallas_call(
        _ctx_attn_kernel,
        out_shape=[jax.ShapeDtypeStruct((b, SWA_Q_W, lc), BF16), jax.ShapeDtypeStruct((b, NA_W, lc), BF16)],
        grid=(b,),
        in_specs=[full(qs), full(ks), full(vs), full(qn), full(kn), full(vn),
                  pl.BlockSpec(sink_rows.shape, lambda i: (0, 0, 0))],
        out_specs=[pl.BlockSpec((1, SWA_Q_W, lc), lambda i: (i, 0, 0)),
                   pl.BlockSpec((1, NA_W, lc), lambda i: (i, 0, 0))],
        compiler_params=_cparams("parallel"),
        name="ctx_attn",
    )(qs, ks, vs, qn, kn, vn, sink_rows)


def _route(logits):
    lane = lax.broadcasted_iota(jnp.int32, logits.shape, 1)
    big = jnp.int32(ROUTE_W)
    gmask = lane < N_GROUPS
    gl = jnp.where(gmask, logits, NEG_INF)
    gmax = jnp.max(gl, axis=-1, keepdims=True)
    g_sel = jnp.min(jnp.where(gl == gmax, lane, big), axis=-1, keepdims=True)
    p_g = 1.0 / jnp.sum(jnp.where(gmask, jnp.exp(logits - gmax), 0.0), axis=-1, keepdims=True)
    lo = N_GROUPS + EXPERTS_PER_GROUP * g_sel
    emask = (lane >= lo) & (lane < lo + EXPERTS_PER_GROUP)
    el = jnp.where(emask, logits, NEG_INF)
    v1 = jnp.max(el, axis=-1, keepdims=True)
    i1 = jnp.min(jnp.where(el == v1, lane, big), axis=-1, keepdims=True)
    el2 = jnp.where(lane == i1, NEG_INF, el)
    v2 = jnp.max(el2, axis=-1, keepdims=True)
    i2 = jnp.min(jnp.where(el2 == v2, lane, big), axis=-1, keepdims=True)
    e21 = jnp.exp(v2 - v1)
    w1 = p_g / (1.0 + e21)
    w2 = p_g * e21 / (1.0 + e21)
    out = jnp.where(lane == 0, (i1 - N_GROUPS).astype(F32), 0.0)
    out = jnp.where(lane == 1, (i2 - N_GROUPS).astype(F32), out)
    out = jnp.where(lane == 2, w1, out)
    out = jnp.where(lane == 3, w2, out)
    return out


def _out_proj_kernel(x_ref, yf_ref, ys_ref, yn_ref, wo_ref, g1_ref, sh_ref, sc_ref, gn_ref, wr_ref, br_ref,
                     x1_ref, h2_ref, rt_ref):
    y = (_dot_tn(yf_ref[0], wo_ref[F_LO:F_HI, :]) + _dot_tn(ys_ref[0], wo_ref[QS_LO:QS_HI, :])
         + _dot_tn(yn_ref[0], wo_ref[QN_LO:QN_HI, :]))
    x1 = x_ref[0] + g1_ref[0] * y
    x1_ref[0] = x1
    ms = jnp.mean(x1 * x1, axis=-1, keepdims=True)
    h2 = (x1 * lax.rsqrt(ms + EPS) * gn_ref[...]) * (1.0 + sc_ref[0]) + sh_ref[0]
    h2_ref[0] = h2.astype(BF16)
    rt_ref[0] = _route(_split_dot(h2, wr_ref[...]) + br_ref[...])


def _out_proj(x, yf, ys, yn, w_out, gate, shift, scale, gain, w_route, b_route, *, tm):
    b, t, d = x.shape
    vec = pl.BlockSpec((1, 1, d), lambda i, j: (i, 0, 0))
    feat = lambda r: pl.BlockSpec((1, r, tm), lambda i, j: (i, 0, j))
    tok = lambda w: pl.BlockSpec((1, tm, w), lambda i, j: (i, j, 0))
    return pl.pallas_call(
        _out_proj_kernel,
        out_shape=[jax.ShapeDtypeStruct((b, t, d), F32), jax.ShapeDtypeStruct((b, t, d), BF16),
                   jax.ShapeDtypeStruct((b, t, ROUTE_W), F32)],
        grid=(b, t // tm),
        in_specs=[tok(d), feat(FOURIER_WIDTH), feat(SWA_Q_W), feat(NA_W),
                  pl.BlockSpec((MIX_WIDTH, d), lambda i, j: (0, 0)),
                  vec, vec, vec,
                  pl.BlockSpec((1, d), lambda i, j: (0, 0)),
                  pl.BlockSpec((d, ROUTE_W), lambda i, j: (0, 0)),
                  pl.BlockSpec((1, ROUTE_W), lambda i, j: (0, 0))],
        out_specs=[tok(d), tok(d), tok(ROUTE_W)],
        compiler_params=_cparams("parallel", "parallel"),
        name="out_proj",
    )(x, yf, ys, yn, w_out, gate, shift, scale, gain, w_route, b_route)


def _experts_kernel(te_ref, nt_ref, x_ref, wg_ref, wu_ref, wd_ref, y_ref, wg_s, wu_s, wd_s):
    i = pl.program_id(0)
    fresh = jnp.logical_or(i == 0, te_ref[i] != te_ref[jnp.maximum(i - 1, 0)])

    @pl.when(jnp.logical_and(fresh, i < nt_ref[0]))
    def _():
        wg_s[...] = wg_ref[0, 0].astype(BF16)
        wu_s[...] = wu_ref[0, 0].astype(BF16)
        wd_s[...] = wd_ref[0, 0].astype(BF16)

    @pl.when(i < nt_ref[0])
    def _():
        x = x_ref[...].astype(BF16)
        g = _dot(x, wg_s[...])
        u = _dot(x, wu_s[...])
        hid = (g * (1.0 / (1.0 + jnp.exp(-g)))) * u
        y_ref[...] = _dot(hid.astype(BF16), wd_s[...])

    @pl.when(i >= nt_ref[0])
    def _():
        y_ref[...] = jnp.zeros(y_ref.shape, F32)


def _experts(xs, tile_expert, n_tiles, w_gate, w_up, w_down, layer):
    r, d = xs.shape
    de = w_gate.shape[3]
    last = lambda i, te, nt: jnp.minimum(i, nt[0] - 1)
    wspec = lambda a, b: pl.BlockSpec((1, 1, a, b), lambda i, te, nt: (layer, te[i], 0, 0))
    return pl.pallas_call(
        _experts_kernel,
        out_shape=jax.ShapeDtypeStruct((r, d), F32),
        grid_spec=pltpu.PrefetchScalarGridSpec(
            num_scalar_prefetch=2,
            grid=(r // MOE_TILE,),
            in_specs=[
                pl.BlockSpec((MOE_TILE, d), lambda i, te, nt: (last(i, te, nt), 0)),
                wspec(d, de), wspec(d, de), wspec(de, d),
            ],
            out_specs=pl.BlockSpec((MOE_TILE, d), lambda i, te, nt: (i, 0)),
            scratch_shapes=[pltpu.VMEM((d, de), BF16), pltpu.VMEM((d, de), BF16), pltpu.VMEM((de, d), BF16)],
        ),
        compiler_params=_cparams("arbitrary"),
        name="experts",
    )(tile_expert, n_tiles, xs, w_gate, w_up, w_down)


def _chunk_copy(local, remote, sem, dst_ref, c, q, to_remote):
    loc = local.at[pl.ds(pl.multiple_of(q * ROW_BLOCK, ROW_BLOCK), ROW_BLOCK)]
    rem = remote.at[pl.ds(pl.multiple_of(dst_ref[c, q], ROW_BLOCK), ROW_BLOCK)]
    return pltpu.make_async_copy(loc, rem, sem) if to_remote else pltpu.make_async_copy(rem, loc, sem)


def _chunk_copies(local, remote, sem, dst_ref, nblk_ref, c, *, to_remote, wait):
    n = nblk_ref[c]
    if wait:
        def body(q, carry):
            _chunk_copy(local, remote, sem, dst_ref, c, q, to_remote).wait()
            return carry

        lax.fori_loop(0, n, body, 0)
        return

    def pair(i, carry):
        q = 2 * i
        _chunk_copy(local, remote, sem, dst_ref, c, q, to_remote).start(priority=0)

        @pl.when(q + 1 < n)
        def _():
            _chunk_copy(local, remote, sem, dst_ref, c, q + 1, to_remote).start(priority=1)

        return carry

    lax.fori_loop(0, (n + 1) // 2, pair, 0)


def _tail_copies(zero, remote, sem, tstart_ref, tn_ref, *, wait):
    def per_expert(e, carry):
        def body(q, inner):
            row = pl.multiple_of(tstart_ref[e] + q * ROW_BLOCK, ROW_BLOCK)
            cp = pltpu.make_async_copy(zero, remote.at[pl.ds(row, ROW_BLOCK)], sem)
            if wait:
                cp.wait()
            else:
                cp.start()
            return inner

        return lax.fori_loop(0, tn_ref[e], body, carry)

    lax.fori_loop(0, N_EXPERTS, per_expert, 0)


def _unused_tile_copies(zero, remote, sem, nt_ref, total_tiles, *, wait):
    def body(t, carry):
        row = pl.multiple_of(t * MOE_TILE, MOE_TILE)
        cp = pltpu.make_async_copy(zero, remote.at[pl.ds(row, MOE_TILE)], sem)
        if wait:
            cp.wait()
        else:
            cp.start()
        return carry

    lax.fori_loop(nt_ref[0], total_tiles, body, 0)


def _dispatch_kernel(dst_ref, nblk_ref, tstart_ref, tn_ref, nt_ref, *rest, part_chunks):
    n_parts = len(part_chunks)
    h_refs = rest[:n_parts]
    pos_ref, xs_ref, buf, zero, sem, zsem = rest[n_parts:]
    n_chunks = sum(part_chunks)
    c = pl.program_id(0)
    slot = c % 2
    row = lax.broadcasted_iota(jnp.int32, (MOE_LROWS, MOE_CHUNK), 0)
    hit = (row == pos_ref[0, 0:1, :]) | (row == pos_ref[0, 1:2, :])
    sel = jnp.where(hit, 1.0, 0.0).astype(BF16)
    first = 0
    for h_ref, n in zip(h_refs, part_chunks):
        @pl.when((c >= first) & (c < first + n))
        def _(h_ref=h_ref):
            buf[slot] = _dot(sel, h_ref[...])
        first += n
    _chunk_copies(buf.at[slot], xs_ref, sem.at[slot], dst_ref, nblk_ref, c, to_remote=True, wait=False)

    @pl.when(c > 0)
    def _():
        _chunk_copies(buf.at[1 - slot], xs_ref, sem.at[1 - slot], dst_ref, nblk_ref, c - 1,
                      to_remote=True, wait=True)

    @pl.when(c == n_chunks - 1)
    def _():
        total_tiles = xs_ref.shape[0] // MOE_TILE
        zero[...] = jnp.zeros(zero.shape, F32)
        zero8 = zero.at[pl.ds(0, ROW_BLOCK)]
        _tail_copies(zero8, xs_ref, zsem, tstart_ref, tn_ref, wait=False)
        _unused_tile_copies(zero, xs_ref, zsem, nt_ref, total_tiles, wait=False)
        _chunk_copies(buf.at[slot], xs_ref, sem.at[slot], dst_ref, nblk_ref, c, to_remote=True, wait=True)
        _tail_copies(zero8, xs_ref, zsem, tstart_ref, tn_ref, wait=True)
        _unused_tile_copies(zero, xs_ref, zsem, nt_ref, total_tiles, wait=True)


def _dispatch(h_parts, plan):
    d = h_parts[0].shape[1]
    part_chunks = tuple(h.shape[0] // MOE_CHUNK for h in h_parts)
    in_specs = []
    first = 0
    for n in part_chunks:
        in_specs.append(pl.BlockSpec(
            (MOE_CHUNK, d), lambda i, *_, first=first, n=n: (jnp.clip(i - first, 0, n - 1), 0)))
        first += n
    in_specs.append(pl.BlockSpec((1, 8, MOE_CHUNK), lambda i, *_: (i, 0, 0)))
    return pl.pallas_call(
        functools.partial(_dispatch_kernel, part_chunks=part_chunks),
        out_shape=jax.ShapeDtypeStruct((plan["rows"], d), F32),
        grid_spec=pltpu.PrefetchScalarGridSpec(
            num_scalar_prefetch=5,
            grid=(sum(part_chunks),),
            in_specs=in_specs,
            out_specs=pl.BlockSpec(memory_space=pl.ANY),
            scratch_shapes=[pltpu.VMEM((2, MOE_LROWS, d), F32), pltpu.VMEM((MOE_TILE, d), F32),
                            pltpu.SemaphoreType.DMA((2,)), pltpu.SemaphoreType.DMA(())],
        ),
        compiler_params=_cparams("arbitrary"),
        name="dispatch",
    )(plan["dst"], plan["nblk"], plan["tstart"], plan["tn"], plan["n_tiles"], *h_parts, plan["pos"])


def _combine_kernel(dst_ref, nblk_ref, x1_ref, rt_ref, pos_ref, g2_ref, gf_ref, ys_ref, o_ref, buf, sem,
                    *, c0, n_chunks, final_norm):
    i = pl.program_id(0)
    c = i + c0
    slot = i % 2
    gather = functools.partial(_chunk_copies, remote=ys_ref, dst_ref=dst_ref, nblk_ref=nblk_ref, to_remote=False)

    @pl.when(i == 0)
    def _():
        buf[...] = jnp.zeros(buf.shape, F32)
        gather(buf.at[slot], sem=sem.at[slot], c=c, wait=False)

    @pl.when(i + 1 < n_chunks)
    def _():
        gather(buf.at[1 - slot], sem=sem.at[1 - slot], c=c + 1, wait=False)

    gather(buf.at[slot], sem=sem.at[slot], c=c, wait=True)
    yl = buf[slot].astype(BF16)
    row = lax.broadcasted_iota(jnp.int32, (MOE_LROWS, MOE_CHUNK), 0)
    y1 = _dot_tn(jnp.where(row == pos_ref[0, 0:1, :], 1.0, 0.0).astype(BF16), yl)
    y2 = _dot_tn(jnp.where(row == pos_ref[0, 1:2, :], 1.0, 0.0).astype(BF16), yl)
    rt = rt_ref[0]
    x2 = x1_ref[0] + g2_ref[0] * (rt[:, 2:3] * y1 + rt[:, 3:4] * y2)
    if final_norm:
        ms = jnp.mean(x2 * x2, axis=-1, keepdims=True)
        x2 = x2 * lax.rsqrt(ms + EPS) * gf_ref[...]
    o_ref[0] = x2


def _combine(x1, route, plan, gate, g_final, ys, *, c0, final_norm):
    b, t, d = x1.shape
    per_batch = t // MOE_CHUNK
    n_chunks = b * per_batch
    tok = lambda w: pl.BlockSpec((1, MOE_CHUNK, w), lambda i, *_: (i // per_batch, i % per_batch, 0))
    return pl.pallas_call(
        functools.partial(_combine_kernel, c0=c0, n_chunks=n_chunks, final_norm=final_norm),
        out_shape=jax.ShapeDtypeStruct((b, t, d), F32),
        grid_spec=pltpu.PrefetchScalarGridSpec(
            num_scalar_prefetch=2,
            grid=(n_chunks,),
            in_specs=[tok(d), tok(ROUTE_W),
                      pl.BlockSpec((1, 8, MOE_CHUNK), lambda i, *_: (i + c0, 0, 0)),
                      pl.BlockSpec((1, 1, d), lambda i, *_: (i // per_batch, 0, 0)),
                      pl.BlockSpec((1, d), lambda i, *_: (0, 0)),
                      pl.BlockSpec(memory_space=pl.ANY)],
            out_specs=tok(d),
            scratch_shapes=[pltpu.VMEM((2, MOE_LROWS, d), F32), pltpu.SemaphoreType.DMA((2,))],
        ),
        compiler_params=_cparams("arbitrary"),
        name="combine",
    )(plan["dst"], plan["nblk"], x1, route, plan["pos"], gate, g_final, ys)


def _rope_tables(seq):
    half = HEAD_DIM // 4
    t = jnp.arange(seq)
    rows, cols = t // GRID_W, t % GRID_W
    freqs = ROPE_BASE ** (-jnp.arange(half, dtype=F32) / half)

    def cs(pos):
        ang = pos.astype(F32)[None, :] * freqs[:, None]
        return jnp.cos(ang), jnp.sin(ang)

    cr, sr = cs(rows)
    cc, sc = cs(cols)
    return jnp.concatenate([cr, cr, cc, cc], axis=0), jnp.concatenate([-sr, sr, -sc, sc], axis=0)


def _dft_tables(n):
    k = jnp.arange(n)

    def cs(m):
        ang = ((m[:, None] * k[None, :]) % n).astype(F32) * (2.0 * jnp.pi / n)
        return jnp.cos(ang), jnp.sin(ang)

    if n <= GRID_W:
        return cs(k)
    ca, sa = cs(jnp.arange(n // GRID_W) * GRID_W)
    cb, sb = cs(jnp.arange(GRID_W))
    c = ca[:, None, :] * cb[None, :, :] - sa[:, None, :] * sb[None, :, :]
    s = sa[:, None, :] * cb[None, :, :] + ca[:, None, :] * sb[None, :, :]
    return c.reshape(n, n), s.reshape(n, n)


def _channel_dft_t():
    c, s = _dft_tables(FOURIER_GROUP_DIM)
    eye = jnp.eye(FOURIER_GROUPS, dtype=F32)
    scale = FOURIER_GROUP_DIM ** -0.5
    return jnp.concatenate([jnp.kron(eye, c), jnp.kron(eye, s)], axis=0) * scale


def _position_dft(n):
    c, s = _dft_tables(n)
    scale = n ** -0.5
    return (c * scale).astype(BF16), (s * scale).astype(BF16)


def _swa_mask():
    kk = jnp.arange(SWA_BLOCK)[:, None]
    q = jnp.arange(SWA_BLOCK)[None, :]
    tiles = []
    for block_offset in (-SWA_BLOCK, SWA_BLOCK):
        ok = jnp.abs(block_offset + kk - q) <= SWA_WINDOW
        tiles.append(jnp.tile(jnp.where(ok, 0.0, NEG_INF).astype(F32), (1, SWA_GROUP)))
    return jnp.stack(tiles)


def _sink_rows(sink, width):
    return jnp.repeat(sink.astype(F32).reshape(SWA_KV_HEADS, SWA_GROUP), width, axis=1)[:, None, :]


def _moe_plan(route2d):
    i32 = jnp.int32
    n = route2d.shape[0]
    nc = n // MOE_CHUNK
    ids = route2d[:, 0:2].astype(i32)
    onehot = (ids[:, :, None] == jnp.arange(N_EXPERTS, dtype=i32)).astype(i32)
    onehot = onehot.reshape(nc, 2 * MOE_CHUNK, N_EXPERTS)
    pair = jnp.arange(2 * MOE_CHUNK, dtype=i32)
    earlier = (pair[None, :] < pair[:, None]).astype(BF16)
    before = jnp.einsum("pq,cqe->cpe", earlier, onehot.astype(BF16), preferred_element_type=F32).astype(i32)
    rank = jnp.sum(before * onehot, axis=-1)
    cnt = jnp.sum(onehot, axis=1)
    run = (cnt + ROW_BLOCK - 1) // ROW_BLOCK * ROW_BLOCK
    lo = jnp.cumsum(run, axis=1) - run
    pos = (jnp.sum(onehot * lo[:, None, :], axis=-1) + rank).reshape(nc, MOE_CHUNK, 2)
    pos = jnp.concatenate([jnp.swapaxes(pos, 1, 2), jnp.full((nc, 6, MOE_CHUNK), -1, i32)], axis=1)
    seg = jnp.sum(run, axis=0)
    padded = (seg + MOE_TILE - 1) // MOE_TILE * MOE_TILE
    start = jnp.cumsum(padded) - padded
    off = start[None, :] + jnp.cumsum(run, axis=0) - run
    q = jnp.arange(MOE_NBLK, dtype=i32)[None, :, None] * ROW_BLOCK
    inside = (lo[:, None, :] <= q) & (q < (lo + run)[:, None, :])
    dst = jnp.sum(jnp.where(inside, off[:, None, :] + q - lo[:, None, :], 0), axis=-1)
    nblk = jnp.sum(run, axis=1) // ROW_BLOCK
    ends = start + padded
    rows = -(-(2 * n + (ROW_BLOCK - 1) * N_EXPERTS * nc + (MOE_TILE - 1) * N_EXPERTS) // MOE_TILE) * MOE_TILE
    tile_row = jnp.arange(rows // MOE_TILE, dtype=i32) * MOE_TILE
    te = jnp.minimum(jnp.sum((ends[None, :] <= tile_row[:, None]).astype(i32), axis=1), N_EXPERTS - 1)
    n_tiles = ends[-1] // MOE_TILE
    te_last = jnp.sum(jnp.where(tile_row == (n_tiles - 1) * MOE_TILE, te, 0))
    te = jnp.where(tile_row < ends[-1], te, te_last)
    return {"pos": pos, "dst": dst.astype(i32), "nblk": nblk.astype(i32), "tstart": (start + seg).astype(i32),
            "tn": ((padded - seg) // ROW_BLOCK).astype(i32), "te": te.astype(i32),
            "n_tiles": n_tiles.astype(i32).reshape(1), "rows": rows}


def kernel(x, c, ctx, c_ctx, w_mod, b_mod, g_norm1, g_norm2, w_in, w_four, w_out, swa_sink, na_rpb,
           w_route_group, b_route_group, w_route_expert, b_route_expert, w_exp_gate, w_exp_up,
           w_exp_down, g_final):
    b, s, d = x.shape
    lc = ctx.shape[1]
    depth = w_mod.shape[0]
    tm = 512

    c_rows = jnp.concatenate([c, c_ctx[None, :], jnp.zeros((7, d), F32)], axis=0)
    mod = _modulation(c_rows, w_mod, b_mod)

    cos_t, sin_t = _rope_tables(s)
    cos_c, sin_c = cos_t[:, :lc], sin_t[:, :lc]
    bd_t = _channel_dft_t().astype(BF16)
    cn, sn = _position_dft(s)
    cn_c, sn_c = _position_dft(lc)
    mask = _swa_mask()
    pad = jnp.zeros((d, ROUTE_W - N_GROUPS - N_EXPERTS), F32)

    xc = ctx
    for layer in range(depth):
        with_ctx_out = layer < depth - 1
        lat = [mod[layer, :b, i * d:(i + 1) * d][:, None, :] for i in range(6)]
        cx = [jnp.broadcast_to(mod[layer, b, i * d:(i + 1) * d][None, None, :], (b, 1, d)) for i in range(6)]
        sh1, sc1, g1, sh2, sc2, g2 = lat
        shc1, scc1, gc1, shc2, scc2, gc2 = cx
        gn1 = g_norm1[layer][None, :]
        gn2 = g_norm2[layer][None, :]
        w_t = w_in[layer].T.astype(BF16)
        wf_t = w_four[layer].T.astype(BF16)
        wo = w_out[layer].astype(BF16)
        w_r = jnp.concatenate([w_route_group[layer], w_route_expert[layer], pad], axis=1)
        b_r = jnp.concatenate([b_route_group[layer], b_route_expert[layer],
                               jnp.zeros((ROUTE_W - N_GROUPS - N_EXPERTS,), F32)])[None, :]
        sink_lat = _sink_rows(swa_sink[layer], SWA_BLOCK)
        sink_ctx = _sink_rows(swa_sink[layer], lc)

        fz, qs, qn, ks, vs, kn, vn = _in_proj(x, sh1, sc1, gn1, w_t, bd_t, cos_t, sin_t,
                                              with_q=True, rope=True, tm=tm)
        if with_ctx_out:
            fz_c, qs_c, qn_c, ks_c, vs_c, kn_c, vn_c = _in_proj(xc, shc1, scc1, gn1, w_t, bd_t, cos_c, sin_c,
                                                                with_q=True, rope=False, tm=lc)
        else:
            ks_c, vs_c, kn_c, vn_c = _in_proj(xc, shc1, scc1, gn1, w_t[KS_LO:], bd_t, cos_c, sin_c,
                                              with_q=False, rope=False, tm=lc)

        yf = _fourier(fz, cn, sn, wf_t, tk=512)
        ys = _swa(qs, ks, vs, ks_c, vs_c, mask, sink_lat)
        yn = _na(qn, kn, vn, kn_c, vn_c, _na_bias(na_rpb[layer], s))
        x1, h2, route = _out_proj(x, yf, ys, yn, wo, g1, sh2, sc2, gn2, w_r, b_r, tm=tm)

        moe_w = (w_exp_gate, w_exp_up, w_exp_down, layer)
        if with_ctx_out:
            yf_c = _fourier(fz_c, cn_c, sn_c, wf_t, tk=lc)
            ys_c, yn_c = _ctx_attn(qs_c, ks_c, vs_c, qn_c, kn_c, vn_c, sink_ctx)
            xc1, hc2, route_c = _out_proj(xc, yf_c, ys_c, yn_c, wo, gc1, shc2, scc2, gn2, w_r, b_r, tm=lc)
            n_lat = b * s
            lat_chunks = n_lat // MOE_CHUNK
            plan = _moe_plan(jnp.concatenate([route.reshape(n_lat, ROUTE_W),
                                              route_c.reshape(b * lc, ROUTE_W)], axis=0))
            xs = _dispatch([h2.reshape(n_lat, d), hc2.reshape(b * lc, d)], plan)
            ye = _experts(xs, plan["te"], plan["n_tiles"], *moe_w)
            x = _combine(x1, route, plan, g2, g_final[None, :], ye, c0=0, final_norm=False)
            ctx_chunks = b * lc // MOE_CHUNK
            xc = _combine(xc1.reshape(ctx_chunks, MOE_CHUNK, d), route_c.reshape(ctx_chunks, MOE_CHUNK, ROUTE_W),
                          plan, gc2[:ctx_chunks], g_final[None, :], ye, c0=lat_chunks,
                          final_norm=False).reshape(b, lc, d)
        else:
            plan = _moe_plan(route.reshape(b * s, ROUTE_W))
            xs = _dispatch([h2.reshape(b * s, d)], plan)
            ye = _experts(xs, plan["te"], plan["n_tiles"], *moe_w)
            x = _combine(x1, route, plan, g2, g_final[None, :], ye, c0=0, final_norm=True)
    return x
```

```python
import functools

import jax
import jax.numpy as jnp
from jax import lax
from jax.experimental import pallas as pl
from jax.experimental.pallas import tpu as pltpu

F32 = jnp.float32
BF16 = jnp.bfloat16

D_MODEL = 1024
GRID_W = 64
HEAD_DIM = 64
FOURIER_WIDTH = D_MODEL // 4
FOURIER_GROUPS = 4
FOURIER_GROUP_DIM = FOURIER_WIDTH // FOURIER_GROUPS
SWA_HEADS = (3 * D_MODEL // 8) // HEAD_DIM
SWA_KV_HEADS = 2
SWA_GROUP = SWA_HEADS // SWA_KV_HEADS
SWA_WINDOW = 128
SWA_BLOCK = 128
NA_HEADS = (3 * D_MODEL // 8) // HEAD_DIM
NA_WIN_R = 8
NA_WIN_C = 16
ROPE_BASE = 10000.0
N_GROUPS = 4
EXPERTS_PER_GROUP = 8
N_EXPERTS = N_GROUPS * EXPERTS_PER_GROUP
D_EXPERT = D_MODEL // 2
EPS = 1e-6
NEG_INF = -1e30

SWA_Q_W = SWA_HEADS * HEAD_DIM
SWA_KV_W = SWA_KV_HEADS * HEAD_DIM
NA_W = NA_HEADS * HEAD_DIM
MIX_WIDTH = FOURIER_WIDTH + SWA_Q_W + NA_W
Q_COLS = MIX_WIDTH
IN_COLS = 2 * MIX_WIDTH

F_LO, F_HI = 0, FOURIER_WIDTH
QS_LO, QS_HI = F_HI, F_HI + SWA_Q_W
QN_LO, QN_HI = QS_HI, QS_HI + NA_W
KS_LO, KS_HI = QN_HI, QN_HI + SWA_KV_W
VS_LO, VS_HI = KS_HI, KS_HI + SWA_KV_W
KN_LO, KN_HI = VS_HI, VS_HI + NA_W
VN_LO, VN_HI = KN_HI, KN_HI + NA_W

LANE = 128
ROUTE_W = LANE
NA_QROWS = 4
NA_KROWS = NA_QROWS + NA_WIN_R
ATTN_KCHUNK = LANE
MOE_TILE = 256
ROW_BLOCK = 8
MOE_CHUNK = 512
MOE_LROWS = 2 * MOE_CHUNK + (ROW_BLOCK - 1) * N_EXPERTS
VMEM_LIMIT = 48 * 1024 * 1024


def _cparams(*sem):
    return pltpu.CompilerParams(dimension_semantics=sem, vmem_limit_bytes=VMEM_LIMIT)


def _dot(a, b):
    return jnp.dot(a, b, preferred_element_type=F32)


def _dot_tn(a, b):
    return lax.dot_general(a, b, (((0,), (0,)), ((), ())), preferred_element_type=F32)


def _dot_nt(a, b):
    return lax.dot_general(a, b, (((1,), (1,)), ((), ())), preferred_element_type=F32)


def _split_dot(a, w):
    a_hi = a.astype(BF16)
    a_lo = (a - a_hi.astype(F32)).astype(BF16)
    w_hi = w.astype(BF16)
    w_lo = (w - w_hi.astype(F32)).astype(BF16)
    return _dot(a_hi, w_hi) + (_dot(a_hi, w_lo) + _dot(a_lo, w_hi))


def _mod_kernel(c_ref, w_ref, b_ref, o_ref):
    c = c_ref[...]
    a = c * (1.0 / (1.0 + jnp.exp(-c)))
    o_ref[0] = _split_dot(a, w_ref[0]) + b_ref[0]


def _modulation(c_rows, w_mod, b_mod):
    depth, d, n6 = w_mod.shape
    r = c_rows.shape[0]
    tn = 1536
    return pl.pallas_call(
        _mod_kernel,
        out_shape=jax.ShapeDtypeStruct((depth, r, n6), F32),
        grid=(depth, n6 // tn),
        in_specs=[
            pl.BlockSpec((r, d), lambda l, j: (0, 0)),
            pl.BlockSpec((1, d, tn), lambda l, j: (l, 0, j)),
            pl.BlockSpec((1, 1, tn), lambda l, j: (l, 0, j)),
        ],
        out_specs=pl.BlockSpec((1, r, tn), lambda l, j: (l, 0, j)),
        compiler_params=_cparams("parallel", "parallel"),
        name="modulation",
    )(c_rows, w_mod, b_mod.reshape(depth, 1, n6))


def _rope_rows(t, cos_t, sin_t, n_heads):
    outs = []
    for h in range(n_heads):
        th = t[HEAD_DIM * h:HEAD_DIM * (h + 1)]
        sw = jnp.concatenate([th[16:32], th[0:16], th[48:64], th[32:48]], axis=0)
        outs.append(th * cos_t + sw * sin_t)
    return jnp.concatenate(outs, axis=0)


def _in_proj_kernel(x_ref, sh_ref, sc_ref, g_ref, wt_ref, bdt_ref, cos_ref, sin_ref, *outs, with_q, rope):
    xf = x_ref[0]
    ms = jnp.mean(xf * xf, axis=-1, keepdims=True)
    y = xf * lax.rsqrt(ms + EPS) * g_ref[...]
    h = y * (1.0 + sc_ref[0]) + sh_ref[0]
    pt = _dot_nt(wt_ref[...], h.astype(BF16))
    q_scale = HEAD_DIM ** -0.5
    if with_q:
        fz_ref, qs_ref, qn_ref, ks_ref, vs_ref, kn_ref, vn_ref = outs
        fz_ref[0] = _dot(bdt_ref[...], pt[F_LO:F_HI].astype(BF16)).astype(BF16)
        qs = pt[QS_LO:QS_HI]
        if rope:
            qs = _rope_rows(qs, cos_ref[...], sin_ref[...], SWA_HEADS)
        qs_ref[0] = (qs * q_scale).astype(BF16)
        qn_ref[0] = (pt[QN_LO:QN_HI] * q_scale).astype(BF16)
        off = 0
    else:
        ks_ref, vs_ref, kn_ref, vn_ref = outs
        off = KS_LO
    ks = pt[KS_LO - off:KS_HI - off]
    if rope:
        ks = _rope_rows(ks, cos_ref[...], sin_ref[...], SWA_KV_HEADS)
    ks_ref[0] = ks.astype(BF16)
    vs_ref[0] = pt[VS_LO - off:VS_HI - off].astype(BF16)
    kn_ref[0] = pt[KN_LO - off:KN_HI - off].astype(BF16)
    vn_ref[0] = pt[VN_LO - off:VN_HI - off].astype(BF16)


def _in_proj(x, shift, scale, gain, w_t, bd_t, cos_t, sin_t, *, with_q, rope, tm):
    b, t, d = x.shape
    nf = w_t.shape[0]
    rows = ([2 * FOURIER_WIDTH, SWA_Q_W, NA_W] if with_q else []) + [SWA_KV_W, SWA_KV_W, NA_W, NA_W]
    return pl.pallas_call(
        functools.partial(_in_proj_kernel, with_q=with_q, rope=rope),
        out_shape=[jax.ShapeDtypeStruct((b, r, t), BF16) for r in rows],
        grid=(b, t // tm),
        in_specs=[
            pl.BlockSpec((1, tm, d), lambda i, j: (i, j, 0)),
            pl.BlockSpec((1, 1, d), lambda i, j: (i, 0, 0)),
            pl.BlockSpec((1, 1, d), lambda i, j: (i, 0, 0)),
            pl.BlockSpec((1, d), lambda i, j: (0, 0)),
            pl.BlockSpec((nf, d), lambda i, j: (0, 0)),
            pl.BlockSpec(bd_t.shape, lambda i, j: (0, 0)),
            pl.BlockSpec((HEAD_DIM, tm), lambda i, j: (0, j)),
            pl.BlockSpec((HEAD_DIM, tm), lambda i, j: (0, j)),
        ],
        out_specs=[pl.BlockSpec((1, r, tm), lambda i, j: (i, 0, j)) for r in rows],
        compiler_params=_cparams("parallel", "parallel"),
        name="in_proj_q" if with_q else "in_proj_kv",
    )(x, shift, scale, gain, w_t, bd_t, cos_t, sin_t)


def _fourier_kernel(fz_ref, cn_ref, sn_ref, wft_ref, o_ref):
    zc = fz_ref[0, 0:FOURIER_WIDTH, :]
    zs = fz_ref[0, FOURIER_WIDTH:2 * FOURIER_WIDTH, :]
    y = _dot(zc, cn_ref[...]) - _dot(zs, sn_ref[...])
    o_ref[0] = _dot(wft_ref[...], y.astype(BF16)).astype(BF16)


def _fourier(fz, cn, sn, wf_t, *, tk):
    b, _, t = fz.shape
    return pl.pallas_call(
        _fourier_kernel,
        out_shape=jax.ShapeDtypeStruct((b, FOURIER_WIDTH, t), BF16),
        grid=(t // tk, b),
        in_specs=[
            pl.BlockSpec((1, 2 * FOURIER_WIDTH, t), lambda k, i: (i, 0, 0)),
            pl.BlockSpec((t, tk), lambda k, i: (0, k)),
            pl.BlockSpec((t, tk), lambda k, i: (0, k)),
            pl.BlockSpec((FOURIER_WIDTH, FOURIER_WIDTH), lambda k, i: (0, 0)),
        ],
        out_specs=pl.BlockSpec((1, FOURIER_WIDTH, tk), lambda k, i: (i, 0, k)),
        compiler_params=_cparams("parallel", "parallel"),
        name="fourier",
    )(fz, cn, sn, wf_t)


def _key_chunks(k, v, bias=None):
    n = k.shape[1] // ATTN_KCHUNK
    cut = lambda a, j, axis: lax.slice_in_dim(a, j * ATTN_KCHUNK, (j + 1) * ATTN_KCHUNK, axis=axis)
    return [(cut(k, j, 1), cut(v, j, 1), None if bias is None else cut(bias, j, 0)) for j in range(n)]


def _logits(q_t, chunks):
    return _dot_tn(jnp.concatenate([k_t for k_t, _, _ in chunks], axis=1), q_t)


def _softmax_pv(s, chunks, sink_row):
    pieces = []
    off = 0
    for k_t, _, bias in chunks:
        piece = s[off:off + k_t.shape[1]]
        pieces.append(piece if bias is None else piece + bias)
        off += k_t.shape[1]
    m = functools.reduce(jnp.maximum, [jnp.max(p, axis=0, keepdims=True) for p in pieces])
    if sink_row is not None:
        m = jnp.maximum(m, sink_row)
    probs = [jnp.exp(p - m) for p in pieces]
    den = functools.reduce(jnp.add, [jnp.sum(p, axis=0, keepdims=True) for p in probs])
    if sink_row is not None:
        den = den + jnp.exp(sink_row - m)
    v_all = jnp.concatenate([v_t for _, v_t, _ in chunks], axis=1)
    p_all = jnp.concatenate([p.astype(BF16) for p in probs], axis=0)
    return _dot(v_all, p_all) / den


def _attend(q_t, chunks, sink_row):
    return _softmax_pv(_logits(q_t, chunks), chunks, sink_row)


def _attend_blocks(n_blocks, make_block, sink_row, store):
    cur = make_block(0)
    s_cur = _logits(*cur)
    for j in range(n_blocks):
        nxt = s_nxt = None
        if j + 1 < n_blocks:
            nxt = make_block(j + 1)
            s_nxt = _logits(*nxt)
        store(j, _softmax_pv(s_cur, cur[1], sink_row))
        cur, s_cur = nxt, s_nxt


def _swa_kernel(q_ref, k_ref, v_ref, kc_ref, vc_ref, mask_ref, sink_ref, o_ref, *, seq):
    nb = seq // SWA_BLOCK
    ctx_chunks = _key_chunks(kc_ref[0], vc_ref[0])

    def make_block(n):
        q0 = n * SWA_BLOCK
        q_t = jnp.concatenate(
            [q_ref[0, HEAD_DIM * h:HEAD_DIM * (h + 1), q0:q0 + SWA_BLOCK] for h in range(SWA_GROUP)], axis=1)
        chunks = [(k_ref[0, :, q0:q0 + SWA_BLOCK], v_ref[0, :, q0:q0 + SWA_BLOCK], None)] + ctx_chunks
        for side, kb in ((0, n - 1), (1, n + 1)):
            if 0 <= kb < nb:
                k0 = kb * SWA_BLOCK
                chunks.append((k_ref[0, :, k0:k0 + SWA_BLOCK], v_ref[0, :, k0:k0 + SWA_BLOCK], mask_ref[side]))
        return q_t, chunks

    def store(n, o):
        q0 = n * SWA_BLOCK
        for h in range(SWA_GROUP):
            o_ref[0, HEAD_DIM * h:HEAD_DIM * (h + 1), q0:q0 + SWA_BLOCK] = (
                o[:, SWA_BLOCK * h:SWA_BLOCK * (h + 1)].astype(BF16))

    _attend_blocks(nb, make_block, sink_ref[0], store)


def _swa(qs, ks, vs, kc, vc, mask, sink_rows):
    b, _, t = qs.shape
    lc = kc.shape[2]
    gw = SWA_GROUP * HEAD_DIM
    return pl.pallas_call(
        functools.partial(_swa_kernel, seq=t),
        out_shape=jax.ShapeDtypeStruct((b, SWA_Q_W, t), BF16),
        grid=(b, SWA_KV_HEADS),
        in_specs=[
            pl.BlockSpec((1, gw, t), lambda i, g: (i, g, 0)),
            pl.BlockSpec((1, HEAD_DIM, t), lambda i, g: (i, g, 0)),
            pl.BlockSpec((1, HEAD_DIM, t), lambda i, g: (i, g, 0)),
            pl.BlockSpec((1, HEAD_DIM, lc), lambda i, g: (i, g, 0)),
            pl.BlockSpec((1, HEAD_DIM, lc), lambda i, g: (i, g, 0)),
            pl.BlockSpec((2, SWA_BLOCK, SWA_GROUP * SWA_BLOCK), lambda i, g: (0, 0, 0)),
            pl.BlockSpec((1, 1, SWA_GROUP * SWA_BLOCK), lambda i, g: (g, 0, 0)),
        ],
        out_specs=pl.BlockSpec((1, gw, t), lambda i, g: (i, g, 0)),
        compiler_params=_cparams("parallel", "parallel"),
        name="swa",
    )(qs, ks, vs, kc, vc, mask, sink_rows)


def _na_bias_kernel(rpb_ref, o_ref, u_ref, *, total_rows):
    hd = pl.program_id(0)
    kc = lax.broadcasted_iota(jnp.int32, (GRID_W, LANE), 0)
    lane = lax.broadcasted_iota(jnp.int32, (GRID_W, LANE), 1)
    qc = lane % GRID_W
    dc = jnp.clip(kc - qc, -(NA_WIN_C - 1), NA_WIN_C - 1) + (NA_WIN_C - 1)
    c0 = jnp.clip(qc - NA_WIN_C // 2, 0, GRID_W - NA_WIN_C)
    valid_c = (kc >= c0) & (kc < c0 + NA_WIN_C)
    n_dr = 2 * NA_WIN_R - 1
    for dr in range(n_dr):
        u = jnp.full((GRID_W, LANE), NEG_INF, F32)
        for d in range(2 * NA_WIN_C - 1):
            u = jnp.where(valid_c & (dc == d), rpb_ref[hd, dr, d], u)
        u_ref[dr] = u
    n_rows = o_ref.shape[2] // GRID_W
    block_types = [(0, 0), (NA_QROWS, 0), (total_rows - NA_QROWS, total_rows - NA_KROWS)]
    neg = jnp.full((GRID_W, LANE), NEG_INF, F32)
    for t, (r_base, k_base) in enumerate(block_types):
        for kl in range(n_rows):
            kr = k_base + kl
            for lg in range(NA_QROWS // 2):
                halves = []
                for rq in (2 * lg, 2 * lg + 1):
                    r = r_base + rq
                    r0 = min(max(r - NA_WIN_R // 2, 0), total_rows - NA_WIN_R)
                    ok = r0 <= kr < r0 + NA_WIN_R
                    halves.append(u_ref[kr - r + NA_WIN_R - 1] if ok else neg)
                o_ref[0, t, GRID_W * kl:GRID_W * (kl + 1), LANE * lg:LANE * (lg + 1)] = jnp.where(
                    lane < GRID_W, halves[0], halves[1])


def _na_bias(rpb, seq):
    nh = rpb.shape[0]
    return pl.pallas_call(
        functools.partial(_na_bias_kernel, total_rows=seq // GRID_W),
        out_shape=jax.ShapeDtypeStruct((nh, 3, NA_KROWS * GRID_W, NA_QROWS * GRID_W), F32),
        grid=(nh,),
        in_specs=[pl.BlockSpec(memory_space=pltpu.SMEM)],
        out_specs=pl.BlockSpec((1, 3, NA_KROWS * GRID_W, NA_QROWS * GRID_W), lambda h: (h, 0, 0, 0)),
        scratch_shapes=[pltpu.VMEM((2 * NA_WIN_R - 1, GRID_W, LANE), F32)],
        compiler_params=_cparams("parallel"),
        name="na_bias",
    )(rpb)


def _na_kernel(q_ref, k_ref, v_ref, kc_ref, vc_ref, bias_ref, o_ref, *, seq):
    n_rows = seq // GRID_W
    qw = NA_QROWS * GRID_W
    kw = NA_KROWS * GRID_W
    ctx_chunks = _key_chunks(kc_ref[0], vc_ref[0])
    nblk = n_rows // NA_QROWS
    rows_per_chunk = ATTN_KCHUNK // GRID_W

    def make_block(j):
        k_row = min(max(j * NA_QROWS - NA_WIN_R // 2, 0), n_rows - NA_KROWS)
        btype = 0 if j == 0 else (2 if j == nblk - 1 else 1)
        q0 = j * qw
        chunks = list(ctx_chunks)
        for cj in range(kw // ATTN_KCHUNK):
            first = k_row + cj * rows_per_chunk
            in_window = False
            for r in range(j * NA_QROWS, (j + 1) * NA_QROWS):
                r0 = min(max(r - NA_WIN_R // 2, 0), n_rows - NA_WIN_R)
                in_window = in_window or (first < r0 + NA_WIN_R and first + rows_per_chunk > r0)
            if in_window:
                k0 = first * GRID_W
                chunks.append((k_ref[0, :, k0:k0 + ATTN_KCHUNK], v_ref[0, :, k0:k0 + ATTN_KCHUNK],
                               bias_ref[0, btype, cj * ATTN_KCHUNK:(cj + 1) * ATTN_KCHUNK, :]))
        return q_ref[0, :, q0:q0 + qw], chunks

    def store(j, o):
        o_ref[0, :, j * qw:(j + 1) * qw] = o.astype(BF16)

    _attend_blocks(nblk, make_block, None, store)


def _na(qn, kn, vn, kc, vc, bias):
    b, _, t = qn.shape
    lc = kc.shape[2]
    head = lambda i, h: (i, h, 0)
    return pl.pallas_call(
        functools.partial(_na_kernel, seq=t),
        out_shape=jax.ShapeDtypeStruct((b, NA_W, t), BF16),
        grid=(b, NA_HEADS),
        in_specs=[
            pl.BlockSpec((1, HEAD_DIM, t), head),
            pl.BlockSpec((1, HEAD_DIM, t), head),
            pl.BlockSpec((1, HEAD_DIM, t), head),
            pl.BlockSpec((1, HEAD_DIM, lc), head),
            pl.BlockSpec((1, HEAD_DIM, lc), head),
            pl.BlockSpec((1,) + bias.shape[1:], lambda i, h: (h, 0, 0, 0)),
        ],
        out_specs=pl.BlockSpec((1, HEAD_DIM, t), head),
        compiler_params=_cparams("parallel", "parallel"),
        name="na",
    )(qn, kn, vn, kc, vc, bias)


def _ctx_attn_kernel(qs_ref, ks_ref, vs_ref, qn_ref, kn_ref, vn_ref, sink_ref, ys_ref, yn_ref):
    lc = qs_ref.shape[2]
    for g in range(SWA_KV_HEADS):
        q_t = jnp.concatenate(
            [qs_ref[0, HEAD_DIM * (SWA_GROUP * g + h):HEAD_DIM * (SWA_GROUP * g + h + 1), :] for h in range(SWA_GROUP)],
            axis=1)
        kv = slice(HEAD_DIM * g, HEAD_DIM * (g + 1))
        o = _attend(q_t, _key_chunks(ks_ref[0, kv, :], vs_ref[0, kv, :]), sink_ref[g])
        for h in range(SWA_GROUP):
            hh = SWA_GROUP * g + h
            ys_ref[0, HEAD_DIM * hh:HEAD_DIM * (hh + 1), :] = o[:, lc * h:lc * (h + 1)].astype(BF16)
    for h in range(NA_HEADS):
        sl = slice(HEAD_DIM * h, HEAD_DIM * (h + 1))
        o = _attend(qn_ref[0, sl, :], _key_chunks(kn_ref[0, sl, :], vn_ref[0, sl, :]), None)
        yn_ref[0, sl, :] = o.astype(BF16)


def _ctx_attn(qs, ks, vs, qn, kn, vn, sink_rows):
    b, _, lc = qs.shape
    full = lambda a: pl.BlockSpec((1,) + a.shape[1:], lambda i: (i, 0, 0))
    return pl.pallas_call(
        _ctx_attn_kernel,
        out_shape=[jax.ShapeDtypeStruct((b, SWA_Q_W, lc), BF16), jax.ShapeDtypeStruct((b, NA_W, lc), BF16)],
        grid=(b,),
        in_specs=[full(qs), full(ks), full(vs), full(qn), full(kn), full(vn),
                  pl.BlockSpec(sink_rows.shape, lambda i: (0, 0, 0))],
        out_specs=[pl.BlockSpec((1, SWA_Q_W, lc), lambda i: (i, 0, 0)),
                   pl.BlockSpec((1, NA_W, lc), lambda i: (i, 0, 0))],
        compiler_params=_cparams("parallel"),
        name="ctx_attn",
    )(qs, ks, vs, qn, kn, vn, sink_rows)


def _route(logits):
    lane = lax.broadcasted_iota(jnp.int32, logits.shape, 1)
    big = jnp.int32(ROUTE_W)
    gmask = lane < N_GROUPS
    gl = jnp.where(gmask, logits, NEG_INF)
    gmax = jnp.max(gl, axis=-1, keepdims=True)
    g_sel = jnp.min(jnp.where(gl == gmax, lane, big), axis=-1, keepdims=True)
    p_g = 1.0 / jnp.sum(jnp.where(gmask, jnp.exp(logits - gmax), 0.0), axis=-1, keepdims=True)
    lo = N_GROUPS + EXPERTS_PER_GROUP * g_sel
    emask = (lane >= lo) & (lane < lo + EXPERTS_PER_GROUP)
    el = jnp.where(emask, logits, NEG_INF)
    v1 = jnp.max(el, axis=-1, keepdims=True)
    i1 = jnp.min(jnp.where(el == v1, lane, big), axis=-1, keepdims=True)
    el2 = jnp.where(lane == i1, NEG_INF, el)
    v2 = jnp.max(el2, axis=-1, keepdims=True)
    i2 = jnp.min(jnp.where(el2 == v2, lane, big), axis=-1, keepdims=True)
    e21 = jnp.exp(v2 - v1)
    w1 = p_g / (1.0 + e21)
    w2 = p_g * e21 / (1.0 + e21)
    out = jnp.where(lane == 0, (i1 - N_GROUPS).astype(F32), 0.0)
    out = jnp.where(lane == 1, (i2 - N_GROUPS).astype(F32), out)
    out = jnp.where(lane == 2, w1, out)
    out = jnp.where(lane == 3, w2, out)
    return out


def _out_proj_kernel(x_ref, yf_ref, ys_ref, yn_ref, wo_ref, g1_ref, sh_ref, sc_ref, gn_ref, wr_ref, br_ref,
                     x1_ref, h2_ref, rt_ref):
    y = (_dot_tn(yf_ref[0], wo_ref[F_LO:F_HI, :]) + _dot_tn(ys_ref[0], wo_ref[QS_LO:QS_HI, :])
         + _dot_tn(yn_ref[0], wo_ref[QN_LO:QN_HI, :]))
    x1 = x_ref[0] + g1_ref[0] * y
    x1_ref[0] = x1
    ms = jnp.mean(x1 * x1, axis=-1, keepdims=True)
    h2 = (x1 * lax.rsqrt(ms + EPS) * gn_ref[...]) * (1.0 + sc_ref[0]) + sh_ref[0]
    h2_ref[0] = h2.astype(BF16)
    rt_ref[0] = _route(_split_dot(h2, wr_ref[...]) + br_ref[...])


def _out_proj(x, yf, ys, yn, w_out, gate, shift, scale, gain, w_route, b_route, *, tm):
    b, t, d = x.shape
    vec = pl.BlockSpec((1, 1, d), lambda i, j: (i, 0, 0))
    feat = lambda r: pl.BlockSpec((1, r, tm), lambda i, j: (i, 0, j))
    tok = lambda w: pl.BlockSpec((1, tm, w), lambda i, j: (i, j, 0))
    return pl.pallas_call(
        _out_proj_kernel,
        out_shape=[jax.ShapeDtypeStruct((b, t, d), F32), jax.ShapeDtypeStruct((b, t, d), BF16),
                   jax.ShapeDtypeStruct((b, t, ROUTE_W), F32)],
        grid=(b, t // tm),
        in_specs=[tok(d), feat(FOURIER_WIDTH), feat(SWA_Q_W), feat(NA_W),
                  pl.BlockSpec((MIX_WIDTH, d), lambda i, j: (0, 0)),
                  vec, vec, vec,
                  pl.BlockSpec((1, d), lambda i, j: (0, 0)),
                  pl.BlockSpec((d, ROUTE_W), lambda i, j: (0, 0)),
                  pl.BlockSpec((1, ROUTE_W), lambda i, j: (0, 0))],
        out_specs=[tok(d), tok(d), tok(ROUTE_W)],
        compiler_params=_cparams("parallel", "parallel"),
        name="out_proj",
    )(x, yf, ys, yn, w_out, gate, shift, scale, gain, w_route, b_route)


def _experts_kernel(te_ref, nt_ref, x_ref, wg_ref, wu_ref, wd_ref, y_ref, wg_s, wu_s, wd_s):
    i = pl.program_id(0)
    fresh = jnp.logical_or(i == 0, te_ref[i] != te_ref[jnp.maximum(i - 1, 0)])

    @pl.when(jnp.logical_and(fresh, i < nt_ref[0]))
    def _():
        wg_s[...] = wg_ref[0, 0].astype(BF16)
        wu_s[...] = wu_ref[0, 0].astype(BF16)
        wd_s[...] = wd_ref[0, 0].astype(BF16)

    @pl.when(i < nt_ref[0])
    def _():
        x = x_ref[...].astype(BF16)
        g = _dot(x, wg_s[...])
        u = _dot(x, wu_s[...])
        hid = (g * (1.0 / (1.0 + jnp.exp(-g)))) * u
        y_ref[...] = _dot(hid.astype(BF16), wd_s[...])

    @pl.when(i >= nt_ref[0])
    def _():
        y_ref[...] = jnp.zeros(y_ref.shape, F32)


def _experts(xs, tile_expert, n_tiles, w_gate, w_up, w_down, layer):
    r, d = xs.shape
    de = w_gate.shape[3]
    last = lambda i, te, nt: jnp.minimum(i, nt[0] - 1)
    wspec = lambda a, b: pl.BlockSpec((1, 1, a, b), lambda i, te, nt: (layer, te[i], 0, 0))
    return pl.pallas_call(
        _experts_kernel,
        out_shape=jax.ShapeDtypeStruct((r, d), F32),
        grid_spec=pltpu.PrefetchScalarGridSpec(
            num_scalar_prefetch=2,
            grid=(r // MOE_TILE,),
            in_specs=[
                pl.BlockSpec((MOE_TILE, d), lambda i, te, nt: (last(i, te, nt), 0)),
                wspec(d, de), wspec(d, de), wspec(de, d),
            ],
            out_specs=pl.BlockSpec((MOE_TILE, d), lambda i, te, nt: (i, 0)),
            scratch_shapes=[pltpu.VMEM((d, de), BF16), pltpu.VMEM((d, de), BF16), pltpu.VMEM((de, d), BF16)],
        ),
        compiler_params=_cparams("arbitrary"),
        name="experts",
    )(tile_expert, n_tiles, xs, w_gate, w_up, w_down)


def _start_or_wait(copy, wait):
    if wait:
        copy.wait()
    else:
        copy.start()


def _run_copies(local, remote, sem, runs_ref, c, *, to_remote, wait):
    def body(e, carry):
        rows = runs_ref[0, c, e]

        @pl.when(rows > 0)
        def _():
            n = pl.multiple_of(rows, ROW_BLOCK)
            loc = local.at[pl.ds(pl.multiple_of(runs_ref[1, c, e], ROW_BLOCK), n)]
            rem = remote.at[pl.ds(pl.multiple_of(runs_ref[2, c, e], ROW_BLOCK), n)]
            _start_or_wait(pltpu.make_async_copy(loc, rem, sem) if to_remote
                           else pltpu.make_async_copy(rem, loc, sem), wait)

        return carry

    lax.fori_loop(0, N_EXPERTS, body, 0)


def _tail_copies(zero, remote, sem, tstart_ref, tn_ref, *, wait):
    def body(e, carry):
        rows = tn_ref[e]

        @pl.when(rows > 0)
        def _():
            n = pl.multiple_of(rows, ROW_BLOCK)
            rem = remote.at[pl.ds(pl.multiple_of(tstart_ref[e], ROW_BLOCK), n)]
            _start_or_wait(pltpu.make_async_copy(zero.at[pl.ds(0, n)], rem, sem), wait)

        return carry

    lax.fori_loop(0, N_EXPERTS, body, 0)


def _unused_tile_copies(zero, remote, sem, nt_ref, total_tiles, *, wait):
    def body(t, carry):
        row = pl.multiple_of(t * MOE_TILE, MOE_TILE)
        cp = pltpu.make_async_copy(zero, remote.at[pl.ds(row, MOE_TILE)], sem)
        if wait:
            cp.wait()
        else:
            cp.start()
        return carry

    lax.fori_loop(nt_ref[0], total_tiles, body, 0)


def _dispatch_kernel(runs_ref, tstart_ref, tn_ref, nt_ref, *rest, part_chunks):
    n_parts = len(part_chunks)
    h_refs = rest[:n_parts]
    pos_ref, xs_ref, buf, zero, sem, zsem = rest[n_parts:]
    n_chunks = sum(part_chunks)
    c = pl.program_id(0)
    slot = c % 2
    row = lax.broadcasted_iota(jnp.int32, (MOE_LROWS, MOE_CHUNK), 0)
    hit = (row == pos_ref[0, 0:1, :]) | (row == pos_ref[0, 1:2, :])
    sel = jnp.where(hit, 1.0, 0.0).astype(BF16)
    first = 0
    for h_ref, n in zip(h_refs, part_chunks):
        @pl.when((c >= first) & (c < first + n))
        def _(h_ref=h_ref):
            buf[slot] = _dot(sel, h_ref[...])
        first += n
    _run_copies(buf.at[slot], xs_ref, sem.at[slot], runs_ref, c, to_remote=True, wait=False)

    @pl.when(c > 0)
    def _():
        _run_copies(buf.at[1 - slot], xs_ref, sem.at[1 - slot], runs_ref, c - 1, to_remote=True, wait=True)

    @pl.when(c == n_chunks - 1)
    def _():
        total_tiles = xs_ref.shape[0] // MOE_TILE
        zero[...] = jnp.zeros(zero.shape, F32)
        _tail_copies(zero, xs_ref, zsem, tstart_ref, tn_ref, wait=False)
        _unused_tile_copies(zero, xs_ref, zsem, nt_ref, total_tiles, wait=False)
        _run_copies(buf.at[slot], xs_ref, sem.at[slot], runs_ref, c, to_remote=True, wait=True)
        _tail_copies(zero, xs_ref, zsem, tstart_ref, tn_ref, wait=True)
        _unused_tile_copies(zero, xs_ref, zsem, nt_ref, total_tiles, wait=True)


def _dispatch(h_parts, plan):
    d = h_parts[0].shape[1]
    part_chunks = tuple(h.shape[0] // MOE_CHUNK for h in h_parts)
    in_specs = []
    first = 0
    for n in part_chunks:
        in_specs.append(pl.BlockSpec(
            (MOE_CHUNK, d), lambda i, *_, first=first, n=n: (jnp.clip(i - first, 0, n - 1), 0)))
        first += n
    in_specs.append(pl.BlockSpec((1, 8, MOE_CHUNK), lambda i, *_: (i, 0, 0)))
    return pl.pallas_call(
        functools.partial(_dispatch_kernel, part_chunks=part_chunks),
        out_shape=jax.ShapeDtypeStruct((plan["rows"], d), F32),
        grid_spec=pltpu.PrefetchScalarGridSpec(
            num_scalar_prefetch=4,
            grid=(sum(part_chunks),),
            in_specs=in_specs,
            out_specs=pl.BlockSpec(memory_space=pl.ANY),
            scratch_shapes=[pltpu.VMEM((2, MOE_LROWS, d), F32), pltpu.VMEM((MOE_TILE, d), F32),
                            pltpu.SemaphoreType.DMA((2,)), pltpu.SemaphoreType.DMA(())],
        ),
        compiler_params=_cparams("arbitrary"),
        name="dispatch",
    )(plan["runs"], plan["tstart"], plan["tn"], plan["n_tiles"], *h_parts, plan["pos"])


def _combine_kernel(runs_ref, x1_ref, pos_ref, wt_ref, g2_ref, gf_ref, ys_ref, o_ref, buf, sem,
                    *, c0, n_chunks, final_norm):
    i = pl.program_id(0)
    c = i + c0
    slot = i % 2
    gather = functools.partial(_run_copies, remote=ys_ref, runs_ref=runs_ref, to_remote=False)

    @pl.when(i == 0)
    def _():
        buf[...] = jnp.zeros(buf.shape, F32)
        gather(buf.at[slot], sem=sem.at[slot], c=c, wait=False)

    @pl.when(i + 1 < n_chunks)
    def _():
        gather(buf.at[1 - slot], sem=sem.at[1 - slot], c=c + 1, wait=False)

    gather(buf.at[slot], sem=sem.at[slot], c=c, wait=True)
    row = lax.broadcasted_iota(jnp.int32, (MOE_LROWS, MOE_CHUNK), 0)
    hit1 = row == pos_ref[0, 0:1, :]
    hit2 = row == pos_ref[0, 1:2, :]
    w_row = jnp.sum(jnp.where(hit1, wt_ref[0, 0:1, :], 0.0) + jnp.where(hit2, wt_ref[0, 1:2, :], 0.0),
                    axis=1, keepdims=True)
    sel = jnp.where(hit1 | hit2, 1.0, 0.0).astype(BF16)
    x2 = x1_ref[0] + g2_ref[0] * _dot_tn(sel, (buf[slot] * w_row).astype(BF16))
    if final_norm:
        ms = jnp.mean(x2 * x2, axis=-1, keepdims=True)
        x2 = x2 * lax.rsqrt(ms + EPS) * gf_ref[...]
    o_ref[0] = x2


def _combine(x1, plan, gate, g_final, ys, *, c0, final_norm):
    b, t, d = x1.shape
    per_batch = t // MOE_CHUNK
    n_chunks = b * per_batch
    tok = lambda w: pl.BlockSpec((1, MOE_CHUNK, w), lambda i, *_: (i // per_batch, i % per_batch, 0))
    return pl.pallas_call(
        functools.partial(_combine_kernel, c0=c0, n_chunks=n_chunks, final_norm=final_norm),
        out_shape=jax.ShapeDtypeStruct((b, t, d), F32),
        grid_spec=pltpu.PrefetchScalarGridSpec(
            num_scalar_prefetch=1,
            grid=(n_chunks,),
            in_specs=[tok(d),
                      pl.BlockSpec((1, 8, MOE_CHUNK), lambda i, *_: (i + c0, 0, 0)),
                      pl.BlockSpec((1, 8, MOE_CHUNK), lambda i, *_: (i + c0, 0, 0)),
                      pl.BlockSpec((1, 1, d), lambda i, *_: (i // per_batch, 0, 0)),
                      pl.BlockSpec((1, d), lambda i, *_: (0, 0)),
                      pl.BlockSpec(memory_space=pl.ANY)],
            out_specs=tok(d),
            scratch_shapes=[pltpu.VMEM((2, MOE_LROWS, d), F32), pltpu.SemaphoreType.DMA((2,))],
        ),
        compiler_params=_cparams("arbitrary"),
        name="combine",
    )(plan["runs"], x1, plan["pos"], plan["wts"], gate, g_final, ys)


def _rope_tables(seq):
    half = HEAD_DIM // 4
    t = jnp.arange(seq)
    rows, cols = t // GRID_W, t % GRID_W
    freqs = ROPE_BASE ** (-jnp.arange(half, dtype=F32) / half)

    def cs(pos):
        ang = pos.astype(F32)[None, :] * freqs[:, None]
        return jnp.cos(ang), jnp.sin(ang)

    cr, sr = cs(rows)
    cc, sc = cs(cols)
    return jnp.concatenate([cr, cr, cc, cc], axis=0), jnp.concatenate([-sr, sr, -sc, sc], axis=0)


def _dft_tables(n):
    k = jnp.arange(n)

    def cs(m):
        ang = ((m[:, None] * k[None, :]) % n).astype(F32) * (2.0 * jnp.pi / n)
        return jnp.cos(ang), jnp.sin(ang)

    if n <= GRID_W:
        return cs(k)
    ca, sa = cs(jnp.arange(n // GRID_W) * GRID_W)
    cb, sb = cs(jnp.arange(GRID_W))
    c = ca[:, None, :] * cb[None, :, :] - sa[:, None, :] * sb[None, :, :]
    s = sa[:, None, :] * cb[None, :, :] + ca[:, None, :] * sb[None, :, :]
    return c.reshape(n, n), s.reshape(n, n)


def _channel_dft_t():
    c, s = _dft_tables(FOURIER_GROUP_DIM)
    eye = jnp.eye(FOURIER_GROUPS, dtype=F32)
    scale = FOURIER_GROUP_DIM ** -0.5
    return jnp.concatenate([jnp.kron(eye, c), jnp.kron(eye, s)], axis=0) * scale


def _position_dft(n):
    c, s = _dft_tables(n)
    scale = n ** -0.5
    return (c * scale).astype(BF16), (s * scale).astype(BF16)


def _swa_mask():
    kk = jnp.arange(SWA_BLOCK)[:, None]
    q = jnp.arange(SWA_BLOCK)[None, :]
    tiles = []
    for block_offset in (-SWA_BLOCK, SWA_BLOCK):
        ok = jnp.abs(block_offset + kk - q) <= SWA_WINDOW
        tiles.append(jnp.tile(jnp.where(ok, 0.0, NEG_INF).astype(F32), (1, SWA_GROUP)))
    return jnp.stack(tiles)


def _sink_rows(sink, width):
    return jnp.repeat(sink.astype(F32).reshape(SWA_KV_HEADS, SWA_GROUP), width, axis=1)[:, None, :]


def _moe_plan(route2d):
    i32 = jnp.int32
    n = route2d.shape[0]
    nc = n // MOE_CHUNK
    ids = route2d[:, 0:2].astype(i32)
    onehot = (ids[:, :, None] == jnp.arange(N_EXPERTS, dtype=i32)).astype(i32)
    onehot = onehot.reshape(nc, 2 * MOE_CHUNK, N_EXPERTS)
    pair = jnp.arange(2 * MOE_CHUNK, dtype=i32)
    earlier = (pair[None, :] < pair[:, None]).astype(BF16)
    before = jnp.einsum("pq,cqe->cpe", earlier, onehot.astype(BF16), preferred_element_type=F32).astype(i32)
    rank = jnp.sum(before * onehot, axis=-1)
    cnt = jnp.sum(onehot, axis=1)
    run = (cnt + ROW_BLOCK - 1) // ROW_BLOCK * ROW_BLOCK
    lo = jnp.cumsum(run, axis=1) - run
    pos = (jnp.sum(onehot * lo[:, None, :], axis=-1) + rank).reshape(nc, MOE_CHUNK, 2)
    pos = jnp.concatenate([jnp.swapaxes(pos, 1, 2), jnp.full((nc, 6, MOE_CHUNK), -1, i32)], axis=1)
    seg = jnp.sum(run, axis=0)
    padded = (seg + MOE_TILE - 1) // MOE_TILE * MOE_TILE
    start = jnp.cumsum(padded) - padded
    off = start[None, :] + jnp.cumsum(run, axis=0) - run
    wts = jnp.swapaxes(route2d[:, 2:4].reshape(nc, MOE_CHUNK, 2), 1, 2)
    wts = jnp.concatenate([wts, jnp.zeros((nc, 6, MOE_CHUNK), F32)], axis=1)
    ends = start + padded
    rows = -(-(2 * n + (ROW_BLOCK - 1) * N_EXPERTS * nc + (MOE_TILE - 1) * N_EXPERTS) // MOE_TILE) * MOE_TILE
    tile_row = jnp.arange(rows // MOE_TILE, dtype=i32) * MOE_TILE
    te = jnp.minimum(jnp.sum((ends[None, :] <= tile_row[:, None]).astype(i32), axis=1), N_EXPERTS - 1)
    n_tiles = ends[-1] // MOE_TILE
    te_last = jnp.sum(jnp.where(tile_row == (n_tiles - 1) * MOE_TILE, te, 0))
    te = jnp.where(tile_row < ends[-1], te, te_last)
    return {"pos": pos, "wts": wts, "runs": jnp.stack([run, lo, off]).astype(i32),
            "tstart": (start + seg).astype(i32), "tn": (padded - seg).astype(i32), "te": te.astype(i32),
            "n_tiles": n_tiles.astype(i32).reshape(1), "rows": rows}


def kernel(x, c, ctx, c_ctx, w_mod, b_mod, g_norm1, g_norm2, w_in, w_four, w_out, swa_sink, na_rpb,
           w_route_group, b_route_group, w_route_expert, b_route_expert, w_exp_gate, w_exp_up,
           w_exp_down, g_final):
    b, s, d = x.shape
    lc = ctx.shape[1]
    depth = w_mod.shape[0]
    tm = 512

    c_rows = jnp.concatenate([c, c_ctx[None, :], jnp.zeros((7, d), F32)], axis=0)
    mod = _modulation(c_rows, w_mod, b_mod)

    cos_t, sin_t = _rope_tables(s)
    cos_c, sin_c = cos_t[:, :lc], sin_t[:, :lc]
    bd_t = _channel_dft_t().astype(BF16)
    cn, sn = _position_dft(s)
    cn_c, sn_c = _position_dft(lc)
    mask = _swa_mask()
    pad = jnp.zeros((d, ROUTE_W - N_GROUPS - N_EXPERTS), F32)

    xc = ctx
    for layer in range(depth):
        with_ctx_out = layer < depth - 1
        lat = [mod[layer, :b, i * d:(i + 1) * d][:, None, :] for i in range(6)]
        cx = [jnp.broadcast_to(mod[layer, b, i * d:(i + 1) * d][None, None, :], (b, 1, d)) for i in range(6)]
        sh1, sc1, g1, sh2, sc2, g2 = lat
        shc1, scc1, gc1, shc2, scc2, gc2 = cx
        gn1 = g_norm1[layer][None, :]
        gn2 = g_norm2[layer][None, :]
        w_t = w_in[layer].T.astype(BF16)
        wf_t = w_four[layer].T.astype(BF16)
        wo = w_out[layer].astype(BF16)
        w_r = jnp.concatenate([w_route_group[layer], w_route_expert[layer], pad], axis=1)
        b_r = jnp.concatenate([b_route_group[layer], b_route_expert[layer],
                               jnp.zeros((ROUTE_W - N_GROUPS - N_EXPERTS,), F32)])[None, :]
        sink_lat = _sink_rows(swa_sink[layer], SWA_BLOCK)
        sink_ctx = _sink_rows(swa_sink[layer], lc)

        fz, qs, qn, ks, vs, kn, vn = _in_proj(x, sh1, sc1, gn1, w_t, bd_t, cos_t, sin_t,
                                              with_q=True, rope=True, tm=tm)
        if with_ctx_out:
            fz_c, qs_c, qn_c, ks_c, vs_c, kn_c, vn_c = _in_proj(xc, shc1, scc1, gn1, w_t, bd_t, cos_c, sin_c,
                                                                with_q=True, rope=False, tm=lc)
        else:
            ks_c, vs_c, kn_c, vn_c = _in_proj(xc, shc1, scc1, gn1, w_t[KS_LO:], bd_t, cos_c, sin_c,
                                              with_q=False, rope=False, tm=lc)

        yf = _fourier(fz, cn, sn, wf_t, tk=512)
        ys = _swa(qs, ks, vs, ks_c, vs_c, mask, sink_lat)
        yn = _na(qn, kn, vn, kn_c, vn_c, _na_bias(na_rpb[layer], s))
        x1, h2, route = _out_proj(x, yf, ys, yn, wo, g1, sh2, sc2, gn2, w_r, b_r, tm=tm)

        moe_w = (w_exp_gate, w_exp_up, w_exp_down, layer)
        if with_ctx_out:
            yf_c = _fourier(fz_c, cn_c, sn_c, wf_t, tk=lc)
            ys_c, yn_c = _ctx_attn(qs_c, ks_c, vs_c, qn_c, kn_c, vn_c, sink_ctx)
            xc1, hc2, route_c = _out_proj(xc, yf_c, ys_c, yn_c, wo, gc1, shc2, scc2, gn2, w_r, b_r, tm=lc)
            n_lat = b * s
            lat_chunks = n_lat // MOE_CHUNK
            plan = _moe_plan(jnp.concatenate([route.reshape(n_lat, ROUTE_W),
                                              route_c.reshape(b * lc, ROUTE_W)], axis=0))
            xs = _dispatch([h2.reshape(n_lat, d), hc2.reshape(b * lc, d)], plan)
            ye = _experts(xs, plan["te"], plan["n_tiles"], *moe_w)
            x = _combine(x1, plan, g2, g_final[None, :], ye, c0=0, final_norm=False)
            ctx_chunks = b * lc // MOE_CHUNK
            xc = _combine(xc1.reshape(ctx_chunks, MOE_CHUNK, d), plan, gc2[:ctx_chunks], g_final[None, :], ye,
                          c0=lat_chunks, final_norm=False).reshape(b, lc, d)
        else:
            plan = _moe_plan(route.reshape(b * s, ROUTE_W))
            xs = _dispatch([h2.reshape(b * s, d)], plan)
            ye = _experts(xs, plan["te"], plan["n_tiles"], *moe_w)
            x = _combine(x1, plan, g2, g_final[None, :], ye, c0=0, final_norm=True)
    return x
```

```python
import functools

import jax
import jax.numpy as jnp
from jax import lax
from jax.experimental import pallas as pl
from jax.experimental.pallas import tpu as pltpu

F32 = jnp.float32
BF16 = jnp.bfloat16

D_MODEL = 1024
GRID_W = 64
HEAD_DIM = 64
FOURIER_WIDTH = D_MODEL // 4
FOURIER_GROUPS = 4
FOURIER_GROUP_DIM = FOURIER_WIDTH // FOURIER_GROUPS
SWA_HEADS = (3 * D_MODEL // 8) // HEAD_DIM
SWA_KV_HEADS = 2
SWA_GROUP = SWA_HEADS // SWA_KV_HEADS
SWA_WINDOW = 128
SWA_BLOCK = 128
NA_HEADS = (3 * D_MODEL // 8) // HEAD_DIM
NA_WIN_R = 8
NA_WIN_C = 16
ROPE_BASE = 10000.0
N_GROUPS = 4
EXPERTS_PER_GROUP = 8
N_EXPERTS = N_GROUPS * EXPERTS_PER_GROUP
D_EXPERT = D_MODEL // 2
EPS = 1e-6
NEG_INF = -1e30
LOG2E = 1.4426950408889634

SWA_Q_W = SWA_HEADS * HEAD_DIM
SWA_KV_W = SWA_KV_HEADS * HEAD_DIM
NA_W = NA_HEADS * HEAD_DIM
MIX_WIDTH = FOURIER_WIDTH + SWA_Q_W + NA_W
Q_COLS = MIX_WIDTH
IN_COLS = 2 * MIX_WIDTH

F_LO, F_HI = 0, FOURIER_WIDTH
QS_LO, QS_HI = F_HI, F_HI + SWA_Q_W
QN_LO, QN_HI = QS_HI, QS_HI + NA_W
KS_LO, KS_HI = QN_HI, QN_HI + SWA_KV_W
VS_LO, VS_HI = KS_HI, KS_HI + SWA_KV_W
KN_LO, KN_HI = VS_HI, VS_HI + NA_W
VN_LO, VN_HI = KN_HI, KN_HI + NA_W

LANE = 128
SUBLANE = 8
ROUTE_ROWS = -(-(N_GROUPS + N_EXPERTS) // SUBLANE) * SUBLANE
ROUTE_OUT = SUBLANE
NA_QROWS = 4
NA_KROWS = NA_QROWS + NA_WIN_R
ATTN_KCHUNK = LANE
ATTN_LOOKAHEAD = 2
PROJ_SUBTILE = LANE
MOE_TILE = 256
ROW_BLOCK = 8
MOE_CHUNK = 512
MOE_LROWS = 2 * MOE_CHUNK + (ROW_BLOCK - 1) * N_EXPERTS
VMEM_LIMIT = 48 * 1024 * 1024


def _cparams(*sem):
    return pltpu.CompilerParams(dimension_semantics=sem, vmem_limit_bytes=VMEM_LIMIT)


def _dot(a, b):
    return jnp.dot(a, b, preferred_element_type=F32)


def _dot_tn(a, b):
    return lax.dot_general(a, b, (((0,), (0,)), ((), ())), preferred_element_type=F32)


def _dot_nt(a, b):
    return lax.dot_general(a, b, (((1,), (1,)), ((), ())), preferred_element_type=F32)


def _split_dot(a, w):
    a_hi = a.astype(BF16)
    a_lo = (a - a_hi.astype(F32)).astype(BF16)
    w_hi = w.astype(BF16)
    w_lo = (w - w_hi.astype(F32)).astype(BF16)
    return _dot(a_hi, w_hi) + (_dot(a_hi, w_lo) + _dot(a_lo, w_hi))


def _mod_kernel(c_ref, w_ref, b_ref, o_ref):
    c = c_ref[...]
    a = c * (1.0 / (1.0 + jnp.exp(-c)))
    o_ref[0] = _split_dot(a, w_ref[0]) + b_ref[0]


def _modulation(c_rows, w_mod, b_mod):
    depth, d, n6 = w_mod.shape
    r = c_rows.shape[0]
    tn = 1536
    return pl.pallas_call(
        _mod_kernel,
        out_shape=jax.ShapeDtypeStruct((depth, r, n6), F32),
        grid=(depth, n6 // tn),
        in_specs=[
            pl.BlockSpec((r, d), lambda l, j: (0, 0)),
            pl.BlockSpec((1, d, tn), lambda l, j: (l, 0, j)),
            pl.BlockSpec((1, 1, tn), lambda l, j: (l, 0, j)),
        ],
        out_specs=pl.BlockSpec((1, r, tn), lambda l, j: (l, 0, j)),
        compiler_params=_cparams("parallel", "parallel"),
        name="modulation",
    )(c_rows, w_mod, b_mod.reshape(depth, 1, n6))


def _rope_rows(t, cos_t, sin_t, n_heads):
    outs = []
    for h in range(n_heads):
        th = t[HEAD_DIM * h:HEAD_DIM * (h + 1)]
        sw = jnp.concatenate([th[16:32], th[0:16], th[48:64], th[32:48]], axis=0)
        outs.append(th * cos_t + sw * sin_t)
    return jnp.concatenate(outs, axis=0)


def _in_proj_kernel(x_ref, sh_ref, sc_ref, g_ref, wt_ref, bdt_ref, cos_ref, sin_ref, *outs, with_q, rope):
    xf = x_ref[0]
    ms = jnp.mean(xf * xf, axis=-1, keepdims=True)
    y = xf * lax.rsqrt(ms + EPS) * g_ref[...]
    h = y * (1.0 + sc_ref[0]) + sh_ref[0]
    pt = _dot_nt(wt_ref[...], h.astype(BF16))
    q_scale = HEAD_DIM ** -0.5 * LOG2E
    if with_q:
        fz_ref, qs_ref, qn_ref, ks_ref, vs_ref, kn_ref, vn_ref = outs
        fz_ref[0] = _dot(bdt_ref[...], pt[F_LO:F_HI].astype(BF16)).astype(BF16)
        qs = pt[QS_LO:QS_HI]
        if rope:
            qs = _rope_rows(qs, cos_ref[...], sin_ref[...], SWA_HEADS)
        qs_ref[0] = (qs * q_scale).astype(BF16)
        qn_ref[0] = (pt[QN_LO:QN_HI] * q_scale).astype(BF16)
        off = 0
    else:
        ks_ref, vs_ref, kn_ref, vn_ref = outs
        off = KS_LO
    ks = pt[KS_LO - off:KS_HI - off]
    if rope:
        ks = _rope_rows(ks, cos_ref[...], sin_ref[...], SWA_KV_HEADS)
    ks_ref[0] = ks.astype(BF16)
    vs_ref[0] = pt[VS_LO - off:VS_HI - off].astype(BF16)
    kn_ref[0] = pt[KN_LO - off:KN_HI - off].astype(BF16)
    vn_ref[0] = pt[VN_LO - off:VN_HI - off].astype(BF16)


def _in_proj(x, shift, scale, gain, w_t, bd_t, cos_t, sin_t, *, with_q, rope, tm):
    b, t, d = x.shape
    nf = w_t.shape[0]
    rows = ([2 * FOURIER_WIDTH, SWA_Q_W, NA_W] if with_q else []) + [SWA_KV_W, SWA_KV_W, NA_W, NA_W]
    return pl.pallas_call(
        functools.partial(_in_proj_kernel, with_q=with_q, rope=rope),
        out_shape=[jax.ShapeDtypeStruct((b, r, t), BF16) for r in rows],
        grid=(b, t // tm),
        in_specs=[
            pl.BlockSpec((1, tm, d), lambda i, j: (i, j, 0)),
            pl.BlockSpec((1, 1, d), lambda i, j: (i, 0, 0)),
            pl.BlockSpec((1, 1, d), lambda i, j: (i, 0, 0)),
            pl.BlockSpec((1, d), lambda i, j: (0, 0)),
            pl.BlockSpec((nf, d), lambda i, j: (0, 0)),
            pl.BlockSpec(bd_t.shape, lambda i, j: (0, 0)),
            pl.BlockSpec((HEAD_DIM, tm), lambda i, j: (0, j)),
            pl.BlockSpec((HEAD_DIM, tm), lambda i, j: (0, j)),
        ],
        out_specs=[pl.BlockSpec((1, r, tm), lambda i, j: (i, 0, j)) for r in rows],
        compiler_params=_cparams("parallel", "parallel"),
        name="in_proj_q" if with_q else "in_proj_kv",
    )(x, shift, scale, gain, w_t, bd_t, cos_t, sin_t)


def _fourier_kernel(fz_ref, cn_ref, sn_ref, wft_ref, o_ref):
    zc = fz_ref[0, 0:FOURIER_WIDTH, :]
    zs = fz_ref[0, FOURIER_WIDTH:2 * FOURIER_WIDTH, :]
    y = _dot(zc, cn_ref[...]) - _dot(zs, sn_ref[...])
    o_ref[0] = _dot(wft_ref[...], y.astype(BF16)).astype(BF16)


def _fourier(fz, cn, sn, wf_t, *, tk):
    b, _, t = fz.shape
    return pl.pallas_call(
        _fourier_kernel,
        out_shape=jax.ShapeDtypeStruct((b, FOURIER_WIDTH, t), BF16),
        grid=(t // tk, b),
        in_specs=[
            pl.BlockSpec((1, 2 * FOURIER_WIDTH, t), lambda k, i: (i, 0, 0)),
            pl.BlockSpec((t, tk), lambda k, i: (0, k)),
            pl.BlockSpec((t, tk), lambda k, i: (0, k)),
            pl.BlockSpec((FOURIER_WIDTH, FOURIER_WIDTH), lambda k, i: (0, 0)),
        ],
        out_specs=pl.BlockSpec((1, FOURIER_WIDTH, tk), lambda k, i: (i, 0, k)),
        compiler_params=_cparams("parallel", "parallel"),
        name="fourier",
    )(fz, cn, sn, wf_t)


def _key_chunks(k, v, bias=None):
    n = k.shape[1] // ATTN_KCHUNK
    cut = lambda a, j, axis: lax.slice_in_dim(a, j * ATTN_KCHUNK, (j + 1) * ATTN_KCHUNK, axis=axis)
    return [(cut(k, j, 1), cut(v, j, 1), None if bias is None else cut(bias, j, 0)) for j in range(n)]


def _logits(q_t, chunks):
    return _dot_tn(jnp.concatenate([k_t for k_t, _, _ in chunks], axis=1), q_t)


def _softmax_pv(s, chunks, sink_row):
    pieces = []
    off = 0
    for k_t, _, bias in chunks:
        piece = s[off:off + k_t.shape[1]]
        pieces.append(piece if bias is None else piece + bias)
        off += k_t.shape[1]
    m = functools.reduce(jnp.maximum, [jnp.max(p, axis=0, keepdims=True) for p in pieces])
    if sink_row is not None:
        m = jnp.maximum(m, sink_row)
    probs = [jnp.exp2(p - m) for p in pieces]
    den = functools.reduce(jnp.add, [jnp.sum(p, axis=0, keepdims=True) for p in probs])
    if sink_row is not None:
        den = den + jnp.exp2(sink_row - m)
    v_all = jnp.concatenate([v_t for _, v_t, _ in chunks], axis=1)
    p_all = jnp.concatenate([p.astype(BF16) for p in probs], axis=0)
    return _dot(v_all, p_all) / den


def _attend(q_t, chunks, sink_row):
    return _softmax_pv(_logits(q_t, chunks), chunks, sink_row)


def _attend_blocks(n_blocks, make_block, sink_row, store):
    blocks, logits = {}, {}
    for j in range(min(ATTN_LOOKAHEAD, n_blocks)):
        blocks[j] = make_block(j)
        logits[j] = _logits(*blocks[j])
    for j in range(n_blocks):
        ahead = j + ATTN_LOOKAHEAD
        if ahead < n_blocks:
            blocks[ahead] = make_block(ahead)
            logits[ahead] = _logits(*blocks[ahead])
        store(j, _softmax_pv(logits.pop(j), blocks.pop(j)[1], sink_row))


def _swa_kernel(q_ref, k_ref, v_ref, kc_ref, vc_ref, mask_ref, sink_ref, o_ref, *, seq):
    nb = seq // SWA_BLOCK
    ctx_chunks = _key_chunks(kc_ref[0], vc_ref[0])

    def make_block(n):
        q0 = n * SWA_BLOCK
        q_t = jnp.concatenate(
            [q_ref[0, HEAD_DIM * h:HEAD_DIM * (h + 1), q0:q0 + SWA_BLOCK] for h in range(SWA_GROUP)], axis=1)
        chunks = [(k_ref[0, :, q0:q0 + SWA_BLOCK], v_ref[0, :, q0:q0 + SWA_BLOCK], None)] + ctx_chunks
        for side, kb in ((0, n - 1), (1, n + 1)):
            if 0 <= kb < nb:
                k0 = kb * SWA_BLOCK
                chunks.append((k_ref[0, :, k0:k0 + SWA_BLOCK], v_ref[0, :, k0:k0 + SWA_BLOCK], mask_ref[side]))
        return q_t, chunks

    def store(n, o):
        q0 = n * SWA_BLOCK
        for h in range(SWA_GROUP):
            o_ref[0, HEAD_DIM * h:HEAD_DIM * (h + 1), q0:q0 + SWA_BLOCK] = (
                o[:, SWA_BLOCK * h:SWA_BLOCK * (h + 1)].astype(BF16))

    _attend_blocks(nb, make_block, sink_ref[0], store)


def _swa(qs, ks, vs, kc, vc, mask, sink_rows):
    b, _, t = qs.shape
    lc = kc.shape[2]
    gw = SWA_GROUP * HEAD_DIM
    return pl.pallas_call(
        functools.partial(_swa_kernel, seq=t),
        out_shape=jax.ShapeDtypeStruct((b, SWA_Q_W, t), BF16),
        grid=(b, SWA_KV_HEADS),
        in_specs=[
            pl.BlockSpec((1, gw, t), lambda i, g: (i, g, 0)),
            pl.BlockSpec((1, HEAD_DIM, t), lambda i, g: (i, g, 0)),
            pl.BlockSpec((1, HEAD_DIM, t), lambda i, g: (i, g, 0)),
            pl.BlockSpec((1, HEAD_DIM, lc), lambda i, g: (i, g, 0)),
            pl.BlockSpec((1, HEAD_DIM, lc), lambda i, g: (i, g, 0)),
            pl.BlockSpec((2, SWA_BLOCK, SWA_GROUP * SWA_BLOCK), lambda i, g: (0, 0, 0)),
            pl.BlockSpec((1, 1, SWA_GROUP * SWA_BLOCK), lambda i, g: (g, 0, 0)),
        ],
        out_specs=pl.BlockSpec((1, gw, t), lambda i, g: (i, g, 0)),
        compiler_params=_cparams("parallel", "parallel"),
        name="swa",
    )(qs, ks, vs, kc, vc, mask, sink_rows)


def _na_bias_kernel(rpb_ref, o_ref, u_ref, *, total_rows):
    hd = pl.program_id(0)
    kc = lax.broadcasted_iota(jnp.int32, (GRID_W, LANE), 0)
    lane = lax.broadcasted_iota(jnp.int32, (GRID_W, LANE), 1)
    qc = lane % GRID_W
    dc = jnp.clip(kc - qc, -(NA_WIN_C - 1), NA_WIN_C - 1) + (NA_WIN_C - 1)
    c0 = jnp.clip(qc - NA_WIN_C // 2, 0, GRID_W - NA_WIN_C)
    valid_c = (kc >= c0) & (kc < c0 + NA_WIN_C)
    n_dr = 2 * NA_WIN_R - 1
    for dr in range(n_dr):
        u = jnp.full((GRID_W, LANE), NEG_INF, F32)
        for d in range(2 * NA_WIN_C - 1):
            u = jnp.where(valid_c & (dc == d), rpb_ref[hd, dr, d] * LOG2E, u)
        u_ref[dr] = u
    n_rows = o_ref.shape[2] // GRID_W
    block_types = [(0, 0), (NA_QROWS, 0), (total_rows - NA_QROWS, total_rows - NA_KROWS)]
    neg = jnp.full((GRID_W, LANE), NEG_INF, F32)
    for t, (r_base, k_base) in enumerate(block_types):
        for kl in range(n_rows):
            kr = k_base + kl
            for lg in range(NA_QROWS // 2):
                halves = []
                for rq in (2 * lg, 2 * lg + 1):
                    r = r_base + rq
                    r0 = min(max(r - NA_WIN_R // 2, 0), total_rows - NA_WIN_R)
                    ok = r0 <= kr < r0 + NA_WIN_R
                    halves.append(u_ref[kr - r + NA_WIN_R - 1] if ok else neg)
                o_ref[0, t, GRID_W * kl:GRID_W * (kl + 1), LANE * lg:LANE * (lg + 1)] = jnp.where(
                    lane < GRID_W, halves[0], halves[1])


def _na_bias(rpb, seq):
    nh = rpb.shape[0]
    return pl.pallas_call(
        functools.partial(_na_bias_kernel, total_rows=seq // GRID_W),
        out_shape=jax.ShapeDtypeStruct((nh, 3, NA_KROWS * GRID_W, NA_QROWS * GRID_W), F32),
        grid=(nh,),
        in_specs=[pl.BlockSpec(memory_space=pltpu.SMEM)],
        out_specs=pl.BlockSpec((1, 3, NA_KROWS * GRID_W, NA_QROWS * GRID_W), lambda h: (h, 0, 0, 0)),
        scratch_shapes=[pltpu.VMEM((2 * NA_WIN_R - 1, GRID_W, LANE), F32)],
        compiler_params=_cparams("parallel"),
        name="na_bias",
    )(rpb)


def _na_kernel(q_ref, k_ref, v_ref, kc_ref, vc_ref, bias_ref, o_ref, *, seq):
    n_rows = seq // GRID_W
    qw = NA_QROWS * GRID_W
    kw = NA_KROWS * GRID_W
    ctx_chunks = _key_chunks(kc_ref[0], vc_ref[0])
    nblk = n_rows // NA_QROWS
    rows_per_chunk = ATTN_KCHUNK // GRID_W

    def make_block(j):
        k_row = min(max(j * NA_QROWS - NA_WIN_R // 2, 0), n_rows - NA_KROWS)
        btype = 0 if j == 0 else (2 if j == nblk - 1 else 1)
        q0 = j * qw
        chunks = list(ctx_chunks)
        for cj in range(kw // ATTN_KCHUNK):
            first = k_row + cj * rows_per_chunk
            in_window = False
            for r in range(j * NA_QROWS, (j + 1) * NA_QROWS):
                r0 = min(max(r - NA_WIN_R // 2, 0), n_rows - NA_WIN_R)
                in_window = in_window or (first < r0 + NA_WIN_R and first + rows_per_chunk > r0)
            if in_window:
                k0 = first * GRID_W
                chunks.append((k_ref[0, :, k0:k0 + ATTN_KCHUNK], v_ref[0, :, k0:k0 + ATTN_KCHUNK],
                               bias_ref[0, btype, cj * ATTN_KCHUNK:(cj + 1) * ATTN_KCHUNK, :]))
        return q_ref[0, :, q0:q0 + qw], chunks

    def store(j, o):
        o_ref[0, :, j * qw:(j + 1) * qw] = o.astype(BF16)

    _attend_blocks(nblk, make_block, None, store)


def _na(qn, kn, vn, kc, vc, bias):
    b, _, t = qn.shape
    lc = kc.shape[2]
    head = lambda i, h: (i, h, 0)
    return pl.pallas_call(
        functools.partial(_na_kernel, seq=t),
        out_shape=jax.ShapeDtypeStruct((b, NA_W, t), BF16),
        grid=(b, NA_HEADS),
        in_specs=[
            pl.BlockSpec((1, HEAD_DIM, t), head),
            pl.BlockSpec((1, HEAD_DIM, t), head),
            pl.BlockSpec((1, HEAD_DIM, t), head),
            pl.BlockSpec((1, HEAD_DIM, lc), head),
            pl.BlockSpec((1, HEAD_DIM, lc), head),
            pl.BlockSpec((1,) + bias.shape[1:], lambda i, h: (h, 0, 0, 0)),
        ],
        out_specs=pl.BlockSpec((1, HEAD_DIM, t), head),
        compiler_params=_cparams("parallel", "parallel"),
        name="na",
    )(qn, kn, vn, kc, vc, bias)


def _ctx_attn_kernel(qs_ref, ks_ref, vs_ref, qn_ref, kn_ref, vn_ref, sink_ref, ys_ref, yn_ref):
    lc = qs_ref.shape[2]
    for g in range(SWA_KV_HEADS):
        q_t = jnp.concatenate(
            [qs_ref[0, HEAD_DIM * (SWA_GROUP * g + h):HEAD_DIM * (SWA_GROUP * g + h + 1), :] for h in range(SWA_GROUP)],
            axis=1)
        kv = slice(HEAD_DIM * g, HEAD_DIM * (g + 1))
        o = _attend(q_t, _key_chunks(ks_ref[0, kv, :], vs_ref[0, kv, :]), sink_ref[g])
        for h in range(SWA_GROUP):
            hh = SWA_GROUP * g + h
            ys_ref[0, HEAD_DIM * hh:HEAD_DIM * (hh + 1), :] = o[:, lc * h:lc * (h + 1)].astype(BF16)
    for h in range(NA_HEADS):
        sl = slice(HEAD_DIM * h, HEAD_DIM * (h + 1))
        o = _attend(qn_ref[0, sl, :], _key_chunks(kn_ref[0, sl, :], vn_ref[0, sl, :]), None)
        yn_ref[0, sl, :] = o.astype(BF16)


def _ctx_attn(qs, ks, vs, qn, kn, vn, sink_rows):
    b, _, lc = qs.shape
    full = lambda a: pl.BlockSpec((1,) + a.shape[1:], lambda i: (i, 0, 0))
    return pl.pallas_call(
        _ctx_attn_kernel,
        out_shape=[jax.ShapeDtypeStruct((b, SWA_Q_W, lc), BF16), jax.ShapeDtypeStruct((b, NA_W, lc), BF16)],
        grid=(b,),
        in_specs=[full(qs), full(ks), full(vs), full(qn), full(kn), full(vn),
                  pl.BlockSpec(sink_rows.shape, lambda i: (0, 0, 0))],
        out_specs=[pl.BlockSpec((1, SWA_Q_W, lc), lambda i: (i, 0, 0)),
                   pl.BlockSpec((1, NA_W, lc), lambda i: (i, 0, 0))],
        compiler_params=_cparams("parallel"),
        name="ctx_attn",
    )(qs, ks, vs, qn, kn, vn, sink_rows)


def _route(logits):
    row = lax.broadcasted_iota(jnp.int32, logits.shape, 0)
    big = jnp.int32(logits.shape[0])
    colmax = lambda a: jnp.max(a, axis=0, keepdims=True)
    first = lambda hit: jnp.min(jnp.where(hit, row, big), axis=0, keepdims=True)
    gmask = row < N_GROUPS
    gl = jnp.where(gmask, logits, NEG_INF)
    gmax = colmax(gl)
    g_sel = first(gl == gmax)
    p_g = 1.0 / jnp.sum(jnp.where(gmask, jnp.exp(logits - gmax), 0.0), axis=0, keepdims=True)
    lo = N_GROUPS + EXPERTS_PER_GROUP * g_sel
    el = jnp.where((row >= lo) & (row < lo + EXPERTS_PER_GROUP), logits, NEG_INF)
    v1 = colmax(el)
    i1 = first(el == v1)
    el2 = jnp.where(row == i1, NEG_INF, el)
    v2 = colmax(el2)
    i2 = first(el2 == v2)
    e21 = jnp.exp(v2 - v1)
    w1 = p_g / (1.0 + e21)
    w2 = p_g * e21 / (1.0 + e21)
    out_row = lax.broadcasted_iota(jnp.int32, (ROUTE_OUT, logits.shape[1]), 0)
    out = jnp.where(out_row == 0, (i1 - N_GROUPS).astype(F32), 0.0)
    out = jnp.where(out_row == 1, (i2 - N_GROUPS).astype(F32), out)
    out = jnp.where(out_row == 2, w1, out)
    return jnp.where(out_row == 3, w2, out)


def _out_proj_kernel(x_ref, yf_ref, ys_ref, yn_ref, wo_ref, g1_ref, sh_ref, sc_ref, gn_ref, wrh_ref, wrl_ref,
                     br_ref, x1_ref, h2_ref, rt_ref):
    toks = [slice(i, i + PROJ_SUBTILE) for i in range(0, x_ref.shape[1], PROJ_SUBTILE)]
    ys = [_dot_tn(jnp.concatenate([yf_ref[0, :, tok], ys_ref[0, :, tok], yn_ref[0, :, tok]], axis=0), wo_ref[...])
          for tok in toks]
    for tok, y in zip(toks, ys):
        x1 = x_ref[0, tok, :] + g1_ref[0] * y
        x1_ref[0, tok, :] = x1
        ms = jnp.mean(x1 * x1, axis=-1, keepdims=True)
        h2 = (x1 * lax.rsqrt(ms + EPS) * gn_ref[...]) * (1.0 + sc_ref[0]) + sh_ref[0]
        h_hi = h2.astype(BF16)
        h2_ref[0, tok, :] = h_hi
        h_lo = (h2 - h_hi.astype(F32)).astype(BF16)
        logits = (_dot_nt(wrh_ref[...], h_hi) + (_dot_nt(wrh_ref[...], h_lo) + _dot_nt(wrl_ref[...], h_hi))
                  + br_ref[...])
        rt_ref[0, :, tok] = _route(logits)


def _out_proj(x, yf, ys, yn, w_out, gate, shift, scale, gain, wr_hi, wr_lo, b_route, *, tm):
    b, t, d = x.shape
    vec = pl.BlockSpec((1, 1, d), lambda i, j: (i, 0, 0))
    feat = lambda r: pl.BlockSpec((1, r, tm), lambda i, j: (i, 0, j))
    tok = lambda w: pl.BlockSpec((1, tm, w), lambda i, j: (i, j, 0))
    whole = lambda a: pl.BlockSpec(a.shape, lambda i, j: (0, 0))
    return pl.pallas_call(
        _out_proj_kernel,
        out_shape=[jax.ShapeDtypeStruct((b, t, d), F32), jax.ShapeDtypeStruct((b, t, d), BF16),
                   jax.ShapeDtypeStruct((b, ROUTE_OUT, t), F32)],
        grid=(b, t // tm),
        in_specs=[tok(d), feat(FOURIER_WIDTH), feat(SWA_Q_W), feat(NA_W), whole(w_out),
                  vec, vec, vec, pl.BlockSpec((1, d), lambda i, j: (0, 0)),
                  whole(wr_hi), whole(wr_lo), whole(b_route)],
        out_specs=[tok(d), tok(d), feat(ROUTE_OUT)],
        compiler_params=_cparams("parallel", "parallel"),
        name="out_proj",
    )(x, yf, ys, yn, w_out, gate, shift, scale, gain, wr_hi, wr_lo, b_route)


def _experts_kernel(te_ref, nt_ref, x_ref, wg_ref, wu_ref, wd_ref, y_ref, wg_s, wu_s, wd_s):
    i = pl.program_id(0)
    fresh = jnp.logical_or(i == 0, te_ref[i] != te_ref[jnp.maximum(i - 1, 0)])

    @pl.when(jnp.logical_and(fresh, i < nt_ref[0]))
    def _():
        wg_s[...] = wg_ref[0, 0].astype(BF16)
        wu_s[...] = wu_ref[0, 0].astype(BF16)
        wd_s[...] = wd_ref[0, 0].astype(BF16)

    @pl.when(i < nt_ref[0])
    def _():
        x = x_ref[...].astype(BF16)
        g = _dot(x, wg_s[...])
        u = _dot(x, wu_s[...])
        hid = (g * (1.0 / (1.0 + jnp.exp(-g)))) * u
        y_ref[...] = _dot(hid.astype(BF16), wd_s[...])

    @pl.when(i >= nt_ref[0])
    def _():
        y_ref[...] = jnp.zeros(y_ref.shape, F32)


def _experts(xs, tile_expert, n_tiles, w_gate, w_up, w_down, layer):
    r, d = xs.shape
    de = w_gate.shape[3]
    last = lambda i, te, nt: jnp.minimum(i, nt[0] - 1)
    wspec = lambda a, b: pl.BlockSpec((1, 1, a, b), lambda i, te, nt: (layer, te[i], 0, 0))
    return pl.pallas_call(
        _experts_kernel,
        out_shape=jax.ShapeDtypeStruct((r, d), F32),
        grid_spec=pltpu.PrefetchScalarGridSpec(
            num_scalar_prefetch=2,
            grid=(r // MOE_TILE,),
            in_specs=[
                pl.BlockSpec((MOE_TILE, d), lambda i, te, nt: (last(i, te, nt), 0)),
                wspec(d, de), wspec(d, de), wspec(de, d),
            ],
            out_specs=pl.BlockSpec((MOE_TILE, d), lambda i, te, nt: (i, 0)),
            scratch_shapes=[pltpu.VMEM((d, de), BF16), pltpu.VMEM((d, de), BF16), pltpu.VMEM((de, d), BF16)],
        ),
        compiler_params=_cparams("arbitrary"),
        name="experts",
    )(tile_expert, n_tiles, xs, w_gate, w_up, w_down)


def _start_or_wait(copy, wait):
    if wait:
        copy.wait()
    else:
        copy.start()


def _run_copies(local, remote, sem, runs_ref, c, *, to_remote, wait):
    def body(e, carry):
        rows = runs_ref[0, c, e]

        @pl.when(rows > 0)
        def _():
            n = pl.multiple_of(rows, ROW_BLOCK)
            loc = local.at[pl.ds(pl.multiple_of(runs_ref[1, c, e], ROW_BLOCK), n)]
            rem = remote.at[pl.ds(pl.multiple_of(runs_ref[2, c, e], ROW_BLOCK), n)]
            _start_or_wait(pltpu.make_async_copy(loc, rem, sem) if to_remote
                           else pltpu.make_async_copy(rem, loc, sem), wait)

        return carry

    lax.fori_loop(0, N_EXPERTS, body, 0)


def _tail_copies(zero, remote, sem, tstart_ref, tn_ref, *, wait):
    def body(e, carry):
        rows = tn_ref[e]

        @pl.when(rows > 0)
        def _():
            n = pl.multiple_of(rows, ROW_BLOCK)
            rem = remote.at[pl.ds(pl.multiple_of(tstart_ref[e], ROW_BLOCK), n)]
            _start_or_wait(pltpu.make_async_copy(zero.at[pl.ds(0, n)], rem, sem), wait)

        return carry

    lax.fori_loop(0, N_EXPERTS, body, 0)


def _unused_tile_copies(zero, remote, sem, nt_ref, total_tiles, *, wait):
    def body(t, carry):
        row = pl.multiple_of(t * MOE_TILE, MOE_TILE)
        cp = pltpu.make_async_copy(zero, remote.at[pl.ds(row, MOE_TILE)], sem)
        if wait:
            cp.wait()
        else:
            cp.start()
        return carry

    lax.fori_loop(nt_ref[0], total_tiles, body, 0)


def _dispatch_kernel(runs_ref, tstart_ref, tn_ref, nt_ref, *rest, part_chunks):
    n_parts = len(part_chunks)
    h_refs = rest[:n_parts]
    pos_ref, xs_ref, buf, zero, sem, zsem = rest[n_parts:]
    n_chunks = sum(part_chunks)
    c = pl.program_id(0)
    slot = c % 2
    row = lax.broadcasted_iota(jnp.int32, (MOE_LROWS, MOE_CHUNK), 0)
    hit = (row == pos_ref[0, 0:1, :]) | (row == pos_ref[0, 1:2, :])
    sel = jnp.where(hit, 1.0, 0.0).astype(BF16)
    first = 0
    for h_ref, n in zip(h_refs, part_chunks):
        @pl.when((c >= first) & (c < first + n))
        def _(h_ref=h_ref):
            buf[slot] = _dot(sel, h_ref[...])
        first += n
    _run_copies(buf.at[slot], xs_ref, sem.at[slot], runs_ref, c, to_remote=True, wait=False)

    @pl.when(c > 0)
    def _():
        _run_copies(buf.at[1 - slot], xs_ref, sem.at[1 - slot], runs_ref, c - 1, to_remote=True, wait=True)

    @pl.when(c == n_chunks - 1)
    def _():
        total_tiles = xs_ref.shape[0] // MOE_TILE
        zero[...] = jnp.zeros(zero.shape, F32)
        _tail_copies(zero, xs_ref, zsem, tstart_ref, tn_ref, wait=False)
        _unused_tile_copies(zero, xs_ref, zsem, nt_ref, total_tiles, wait=False)
        _run_copies(buf.at[slot], xs_ref, sem.at[slot], runs_ref, c, to_remote=True, wait=True)
        _tail_copies(zero, xs_ref, zsem, tstart_ref, tn_ref, wait=True)
        _unused_tile_copies(zero, xs_ref, zsem, nt_ref, total_tiles, wait=True)


def _dispatch(h_parts, plan):
    d = h_parts[0].shape[1]
    part_chunks = tuple(h.shape[0] // MOE_CHUNK for h in h_parts)
    in_specs = []
    first = 0
    for n in part_chunks:
        in_specs.append(pl.BlockSpec(
            (MOE_CHUNK, d), lambda i, *_, first=first, n=n: (jnp.clip(i - first, 0, n - 1), 0)))
        first += n
    in_specs.append(pl.BlockSpec((1, 8, MOE_CHUNK), lambda i, *_: (i, 0, 0)))
    return pl.pallas_call(
        functools.partial(_dispatch_kernel, part_chunks=part_chunks),
        out_shape=jax.ShapeDtypeStruct((plan["rows"], d), F32),
        grid_spec=pltpu.PrefetchScalarGridSpec(
            num_scalar_prefetch=4,
            grid=(sum(part_chunks),),
            in_specs=in_specs,
            out_specs=pl.BlockSpec(memory_space=pl.ANY),
            scratch_shapes=[pltpu.VMEM((2, MOE_LROWS, d), F32), pltpu.VMEM((MOE_TILE, d), F32),
                            pltpu.SemaphoreType.DMA((2,)), pltpu.SemaphoreType.DMA(())],
        ),
        compiler_params=_cparams("arbitrary"),
        name="dispatch",
    )(plan["runs"], plan["tstart"], plan["tn"], plan["n_tiles"], *h_parts, plan["pos"])


def _combine_kernel(runs_ref, x1_ref, pos_ref, wt_ref, g2_ref, gf_ref, ys_ref, o_ref, buf, sem,
                    *, c0, n_chunks, final_norm):
    i = pl.program_id(0)
    c = i + c0
    slot = i % 2
    gather = functools.partial(_run_copies, remote=ys_ref, runs_ref=runs_ref, to_remote=False)

    @pl.when(i == 0)
    def _():
        buf[...] = jnp.zeros(buf.shape, F32)
        gather(buf.at[slot], sem=sem.at[slot], c=c, wait=False)

    @pl.when(i + 1 < n_chunks)
    def _():
        gather(buf.at[1 - slot], sem=sem.at[1 - slot], c=c + 1, wait=False)

    gather(buf.at[slot], sem=sem.at[slot], c=c, wait=True)
    row = lax.broadcasted_iota(jnp.int32, (MOE_LROWS, MOE_CHUNK), 0)
    hit1 = row == pos_ref[0, 0:1, :]
    hit2 = row == pos_ref[0, 1:2, :]
    w_row = jnp.sum(jnp.where(hit1, wt_ref[0, 0:1, :], 0.0) + jnp.where(hit2, wt_ref[0, 1:2, :], 0.0),
                    axis=1, keepdims=True)
    sel = jnp.where(hit1 | hit2, 1.0, 0.0).astype(BF16)
    x2 = x1_ref[0] + g2_ref[0] * _dot_tn(sel, (buf[slot] * w_row).astype(BF16))
    if final_norm:
        ms = jnp.mean(x2 * x2, axis=-1, keepdims=True)
        x2 = x2 * lax.rsqrt(ms + EPS) * gf_ref[...]
    o_ref[0] = x2


def _combine(x1, plan, gate, g_final, ys, *, c0, final_norm):
    b, t, d = x1.shape
    per_batch = t // MOE_CHUNK
    n_chunks = b * per_batch
    tok = lambda w: pl.BlockSpec((1, MOE_CHUNK, w), lambda i, *_: (i // per_batch, i % per_batch, 0))
    return pl.pallas_call(
        functools.partial(_combine_kernel, c0=c0, n_chunks=n_chunks, final_norm=final_norm),
        out_shape=jax.ShapeDtypeStruct((b, t, d), F32),
        grid_spec=pltpu.PrefetchScalarGridSpec(
            num_scalar_prefetch=1,
            grid=(n_chunks,),
            in_specs=[tok(d),
                      pl.BlockSpec((1, 8, MOE_CHUNK), lambda i, *_: (i + c0, 0, 0)),
                      pl.BlockSpec((1, 8, MOE_CHUNK), lambda i, *_: (i + c0, 0, 0)),
                      pl.BlockSpec((1, 1, d), lambda i, *_: (i // per_batch, 0, 0)),
                      pl.BlockSpec((1, d), lambda i, *_: (0, 0)),
                      pl.BlockSpec(memory_space=pl.ANY)],
            out_specs=tok(d),
            scratch_shapes=[pltpu.VMEM((2, MOE_LROWS, d), F32), pltpu.SemaphoreType.DMA((2,))],
        ),
        compiler_params=_cparams("arbitrary"),
        name="combine",
    )(plan["runs"], x1, plan["pos"], plan["wts"], gate, g_final, ys)


def _rope_tables(seq):
    half = HEAD_DIM // 4
    t = jnp.arange(seq)
    rows, cols = t // GRID_W, t % GRID_W
    freqs = ROPE_BASE ** (-jnp.arange(half, dtype=F32) / half)

    def cs(pos):
        ang = pos.astype(F32)[None, :] * freqs[:, None]
        return jnp.cos(ang), jnp.sin(ang)

    cr, sr = cs(rows)
    cc, sc = cs(cols)
    return jnp.concatenate([cr, cr, cc, cc], axis=0), jnp.concatenate([-sr, sr, -sc, sc], axis=0)


def _dft_tables(n):
    k = jnp.arange(n)

    def cs(m):
        ang = ((m[:, None] * k[None, :]) % n).astype(F32) * (2.0 * jnp.pi / n)
        return jnp.cos(ang), jnp.sin(ang)

    if n <= GRID_W:
        return cs(k)
    ca, sa = cs(jnp.arange(n // GRID_W) * GRID_W)
    cb, sb = cs(jnp.arange(GRID_W))
    c = ca[:, None, :] * cb[None, :, :] - sa[:, None, :] * sb[None, :, :]
    s = sa[:, None, :] * cb[None, :, :] + ca[:, None, :] * sb[None, :, :]
    return c.reshape(n, n), s.reshape(n, n)


def _channel_dft_t():
    c, s = _dft_tables(FOURIER_GROUP_DIM)
    eye = jnp.eye(FOURIER_GROUPS, dtype=F32)
    scale = FOURIER_GROUP_DIM ** -0.5
    return jnp.concatenate([jnp.kron(eye, c), jnp.kron(eye, s)], axis=0) * scale


def _position_dft(n):
    c, s = _dft_tables(n)
    scale = n ** -0.5
    return (c * scale).astype(BF16), (s * scale).astype(BF16)


def _swa_mask():
    kk = jnp.arange(SWA_BLOCK)[:, None]
    q = jnp.arange(SWA_BLOCK)[None, :]
    tiles = []
    for block_offset in (-SWA_BLOCK, SWA_BLOCK):
        ok = jnp.abs(block_offset + kk - q) <= SWA_WINDOW
        tiles.append(jnp.tile(jnp.where(ok, 0.0, NEG_INF).astype(F32), (1, SWA_GROUP)))
    return jnp.stack(tiles)


def _sink_rows(sink, width):
    return jnp.repeat(sink.astype(F32).reshape(SWA_KV_HEADS, SWA_GROUP) * LOG2E, width, axis=1)[:, None, :]


def _route_chunks(route):
    r = jnp.moveaxis(route, 1, 0).reshape(ROUTE_OUT, -1, MOE_CHUNK)
    return jnp.swapaxes(r, 0, 1)


def _moe_plan(route_chunks):
    i32 = jnp.int32
    nc = route_chunks.shape[0]
    n = nc * MOE_CHUNK
    ids = route_chunks[:, 0:2, :].astype(i32).reshape(nc, 2 * MOE_CHUNK)
    onehot = (ids[:, :, None] == jnp.arange(N_EXPERTS, dtype=i32)).astype(i32)
    pair = jnp.arange(2 * MOE_CHUNK, dtype=i32)
    earlier = (pair[None, :] < pair[:, None]).astype(BF16)
    before = jnp.einsum("pq,cqe->cpe", earlier, onehot.astype(BF16), preferred_element_type=F32).astype(i32)
    rank = jnp.sum(before * onehot, axis=-1)
    cnt = jnp.sum(onehot, axis=1)
    run = (cnt + ROW_BLOCK - 1) // ROW_BLOCK * ROW_BLOCK
    lo = jnp.cumsum(run, axis=1) - run
    pos = (jnp.sum(onehot * lo[:, None, :], axis=-1) + rank).reshape(nc, 2, MOE_CHUNK)
    pos = jnp.concatenate([pos, jnp.full((nc, 6, MOE_CHUNK), -1, i32)], axis=1)
    seg = jnp.sum(run, axis=0)
    padded = (seg + MOE_TILE - 1) // MOE_TILE * MOE_TILE
    start = jnp.cumsum(padded) - padded
    off = start[None, :] + jnp.cumsum(run, axis=0) - run
    wts = jnp.concatenate([route_chunks[:, 2:4, :], jnp.zeros((nc, 6, MOE_CHUNK), F32)], axis=1)
    ends = start + padded
    rows = -(-(2 * n + (ROW_BLOCK - 1) * N_EXPERTS * nc + (MOE_TILE - 1) * N_EXPERTS) // MOE_TILE) * MOE_TILE
    tile_row = jnp.arange(rows // MOE_TILE, dtype=i32) * MOE_TILE
    te = jnp.minimum(jnp.sum((ends[None, :] <= tile_row[:, None]).astype(i32), axis=1), N_EXPERTS - 1)
    n_tiles = ends[-1] // MOE_TILE
    te_last = jnp.sum(jnp.where(tile_row == (n_tiles - 1) * MOE_TILE, te, 0))
    te = jnp.where(tile_row < ends[-1], te, te_last)
    return {"pos": pos, "wts": wts, "runs": jnp.stack([run, lo, off]).astype(i32),
            "tstart": (start + seg).astype(i32), "tn": (padded - seg).astype(i32), "te": te.astype(i32),
            "n_tiles": n_tiles.astype(i32).reshape(1), "rows": rows}


def kernel(x, c, ctx, c_ctx, w_mod, b_mod, g_norm1, g_norm2, w_in, w_four, w_out, swa_sink, na_rpb,
           w_route_group, b_route_group, w_route_expert, b_route_expert, w_exp_gate, w_exp_up,
           w_exp_down, g_final):
    b, s, d = x.shape
    lc = ctx.shape[1]
    depth = w_mod.shape[0]
    tm = 512

    c_rows = jnp.concatenate([c, c_ctx[None, :], jnp.zeros((7, d), F32)], axis=0)
    mod = _modulation(c_rows, w_mod, b_mod)

    cos_t, sin_t = _rope_tables(s)
    cos_c, sin_c = cos_t[:, :lc], sin_t[:, :lc]
    bd_t = _channel_dft_t().astype(BF16)
    cn, sn = _position_dft(s)
    cn_c, sn_c = _position_dft(lc)
    mask = _swa_mask()
    route_pad = ROUTE_ROWS - N_GROUPS - N_EXPERTS

    xc = ctx
    for layer in range(depth):
        with_ctx_out = layer < depth - 1
        lat = [mod[layer, :b, i * d:(i + 1) * d][:, None, :] for i in range(6)]
        cx = [jnp.broadcast_to(mod[layer, b, i * d:(i + 1) * d][None, None, :], (b, 1, d)) for i in range(6)]
        sh1, sc1, g1, sh2, sc2, g2 = lat
        shc1, scc1, gc1, shc2, scc2, gc2 = cx
        gn1 = g_norm1[layer][None, :]
        gn2 = g_norm2[layer][None, :]
        w_t = w_in[layer].T.astype(BF16)
        wf_t = w_four[layer].T.astype(BF16)
        wo = w_out[layer].astype(BF16)
        w_r = jnp.concatenate([w_route_group[layer].T, w_route_expert[layer].T,
                               jnp.zeros((route_pad, d), F32)], axis=0)
        wr_hi = w_r.astype(BF16)
        wr_lo = (w_r - wr_hi.astype(F32)).astype(BF16)
        b_r = jnp.concatenate([b_route_group[layer], b_route_expert[layer], jnp.zeros((route_pad,), F32)])[:, None]
        sink_lat = _sink_rows(swa_sink[layer], SWA_BLOCK)
        sink_ctx = _sink_rows(swa_sink[layer], lc)

        fz, qs, qn, ks, vs, kn, vn = _in_proj(x, sh1, sc1, gn1, w_t, bd_t, cos_t, sin_t,
                                              with_q=True, rope=True, tm=tm)
        if with_ctx_out:
            fz_c, qs_c, qn_c, ks_c, vs_c, kn_c, vn_c = _in_proj(xc, shc1, scc1, gn1, w_t, bd_t, cos_c, sin_c,
                                                                with_q=True, rope=False, tm=lc)
        else:
            ks_c, vs_c, kn_c, vn_c = _in_proj(xc, shc1, scc1, gn1, w_t[KS_LO:], bd_t, cos_c, sin_c,
                                              with_q=False, rope=False, tm=lc)

        yf = _fourier(fz, cn, sn, wf_t, tk=512)
        ys = _swa(qs, ks, vs, ks_c, vs_c, mask, sink_lat)
        yn = _na(qn, kn, vn, kn_c, vn_c, _na_bias(na_rpb[layer], s))
        x1, h2, route = _out_proj(x, yf, ys, yn, wo, g1, sh2, sc2, gn2, wr_hi, wr_lo, b_r, tm=tm)

        moe_w = (w_exp_gate, w_exp_up, w_exp_down, layer)
        if with_ctx_out:
            yf_c = _fourier(fz_c, cn_c, sn_c, wf_t, tk=lc)
            ys_c, yn_c = _ctx_attn(qs_c, ks_c, vs_c, qn_c, kn_c, vn_c, sink_ctx)
            xc1, hc2, route_c = _out_proj(xc, yf_c, ys_c, yn_c, wo, gc1, shc2, scc2, gn2, wr_hi, wr_lo, b_r, tm=lc)
            n_lat = b * s
            lat_chunks = n_lat // MOE_CHUNK
            plan = _moe_plan(jnp.concatenate([_route_chunks(route), _route_chunks(route_c)], axis=0))
            xs = _dispatch([h2.reshape(n_lat, d), hc2.reshape(b * lc, d)], plan)
            ye = _experts(xs, plan["te"], plan["n_tiles"], *moe_w)
            x = _combine(x1, plan, g2, g_final[None, :], ye, c0=0, final_norm=False)
            ctx_chunks = b * lc // MOE_CHUNK
            xc = _combine(xc1.reshape(ctx_chunks, MOE_CHUNK, d), plan, gc2[:ctx_chunks], g_final[None, :], ye,
                          c0=lat_chunks, final_norm=False).reshape(b, lc, d)
        else:
            plan = _moe_plan(_route_chunks(route))
            xs = _dispatch([h2.reshape(b * s, d)], plan)
            ye = _experts(xs, plan["te"], plan["n_tiles"], *moe_w)
            x = _combine(x1, plan, g2, g_final[None, :], ye, c0=0, final_norm=True)
    return x
```

```python
import functools

import jax
import jax.numpy as jnp
from jax import lax
from jax.experimental import pallas as pl
from jax.experimental.pallas import tpu as pltpu

F32 = jnp.float32
BF16 = jnp.bfloat16

D_MODEL = 1024
GRID_W = 64
HEAD_DIM = 64
FOURIER_WIDTH = D_MODEL // 4
FOURIER_GROUPS = 4
FOURIER_GROUP_DIM = FOURIER_WIDTH // FOURIER_GROUPS
SWA_HEADS = (3 * D_MODEL // 8) // HEAD_DIM
SWA_KV_HEADS = 2
SWA_GROUP = SWA_HEADS // SWA_KV_HEADS
SWA_WINDOW = 128
SWA_BLOCK = 128
NA_HEADS = (3 * D_MODEL // 8) // HEAD_DIM
NA_WIN_R = 8
NA_WIN_C = 16
ROPE_BASE = 10000.0
N_GROUPS = 4
EXPERTS_PER_GROUP = 8
N_EXPERTS = N_GROUPS * EXPERTS_PER_GROUP
D_EXPERT = D_MODEL // 2
EPS = 1e-6
NEG_INF = -1e30
LOG2E = 1.4426950408889634

SWA_Q_W = SWA_HEADS * HEAD_DIM
SWA_KV_W = SWA_KV_HEADS * HEAD_DIM
NA_W = NA_HEADS * HEAD_DIM
MIX_WIDTH = FOURIER_WIDTH + SWA_Q_W + NA_W
Q_COLS = MIX_WIDTH
IN_COLS = 2 * MIX_WIDTH

F_LO, F_HI = 0, FOURIER_WIDTH
QS_LO, QS_HI = F_HI, F_HI + SWA_Q_W
QN_LO, QN_HI = QS_HI, QS_HI + NA_W
KS_LO, KS_HI = QN_HI, QN_HI + SWA_KV_W
VS_LO, VS_HI = KS_HI, KS_HI + SWA_KV_W
KN_LO, KN_HI = VS_HI, VS_HI + NA_W
VN_LO, VN_HI = KN_HI, KN_HI + NA_W

LANE = 128
SUBLANE = 8
ROUTE_ROWS = -(-(N_GROUPS + N_EXPERTS) // SUBLANE) * SUBLANE
ROUTE_OUT = SUBLANE
NA_QROWS = 4
NA_KROWS = NA_QROWS + NA_WIN_R
ATTN_KCHUNK = LANE
ATTN_LOOKAHEAD = 2
PROJ_SUBTILE = LANE
MOE_TILE = 256
ROW_BLOCK = 8
MOE_CHUNK = 512
MOE_LROWS = 2 * MOE_CHUNK + (ROW_BLOCK - 1) * N_EXPERTS
VMEM_LIMIT = 48 * 1024 * 1024


def _cparams(*sem):
    return pltpu.CompilerParams(dimension_semantics=sem, vmem_limit_bytes=VMEM_LIMIT)


def _dot(a, b):
    return jnp.dot(a, b, preferred_element_type=F32)


def _dot_tn(a, b):
    return lax.dot_general(a, b, (((0,), (0,)), ((), ())), preferred_element_type=F32)


def _dot_nt(a, b):
    return lax.dot_general(a, b, (((1,), (1,)), ((), ())), preferred_element_type=F32)


def _split_dot(a, w):
    a_hi = a.astype(BF16)
    a_lo = (a - a_hi.astype(F32)).astype(BF16)
    w_hi = w.astype(BF16)
    w_lo = (w - w_hi.astype(F32)).astype(BF16)
    return _dot(a_hi, w_hi) + (_dot(a_hi, w_lo) + _dot(a_lo, w_hi))


def _mod_kernel(c_ref, w_ref, b_ref, o_ref):
    c = c_ref[...]
    a = c * (1.0 / (1.0 + jnp.exp(-c)))
    o_ref[0] = _split_dot(a, w_ref[0]) + b_ref[0]


def _modulation(c_rows, w_mod, b_mod):
    depth, d, n6 = w_mod.shape
    r = c_rows.shape[0]
    tn = 1536
    return pl.pallas_call(
        _mod_kernel,
        out_shape=jax.ShapeDtypeStruct((depth, r, n6), F32),
        grid=(depth, n6 // tn),
        in_specs=[
            pl.BlockSpec((r, d), lambda l, j: (0, 0)),
            pl.BlockSpec((1, d, tn), lambda l, j: (l, 0, j)),
            pl.BlockSpec((1, 1, tn), lambda l, j: (l, 0, j)),
        ],
        out_specs=pl.BlockSpec((1, r, tn), lambda l, j: (l, 0, j)),
        compiler_params=_cparams("parallel", "parallel"),
        name="modulation",
    )(c_rows, w_mod, b_mod.reshape(depth, 1, n6))


def _rope_rows(t, cos_t, sin_t, n_heads):
    outs = []
    for h in range(n_heads):
        th = t[HEAD_DIM * h:HEAD_DIM * (h + 1)]
        sw = jnp.concatenate([th[16:32], th[0:16], th[48:64], th[32:48]], axis=0)
        outs.append(th * cos_t + sw * sin_t)
    return jnp.concatenate(outs, axis=0)


def _in_proj_kernel(x_ref, sh_ref, sc_ref, g_ref, wt_ref, bdt_ref, cos_ref, sin_ref, *outs, with_q, rope):
    xf = x_ref[0]
    ms = jnp.mean(xf * xf, axis=-1, keepdims=True)
    y = xf * lax.rsqrt(ms + EPS) * g_ref[...]
    h = y * (1.0 + sc_ref[0]) + sh_ref[0]
    pt = _dot_nt(wt_ref[...], h.astype(BF16))
    q_scale = HEAD_DIM ** -0.5 * LOG2E
    if with_q:
        fz_ref, qs_ref, qn_ref, ks_ref, vs_ref, kn_ref, vn_ref = outs
        fz_ref[0] = _dot(bdt_ref[...], pt[F_LO:F_HI].astype(BF16)).astype(BF16)
        qs = pt[QS_LO:QS_HI]
        if rope:
            qs = _rope_rows(qs, cos_ref[...], sin_ref[...], SWA_HEADS)
        qs_ref[0] = (qs * q_scale).astype(BF16)
        qn_ref[0] = (pt[QN_LO:QN_HI] * q_scale).astype(BF16)
        off = 0
    else:
        ks_ref, vs_ref, kn_ref, vn_ref = outs
        off = KS_LO
    ks = pt[KS_LO - off:KS_HI - off]
    if rope:
        ks = _rope_rows(ks, cos_ref[...], sin_ref[...], SWA_KV_HEADS)
    ks_ref[0] = ks.astype(BF16)
    vs_ref[0] = pt[VS_LO - off:VS_HI - off].astype(BF16)
    kn_ref[0] = pt[KN_LO - off:KN_HI - off].astype(BF16)
    vn_ref[0] = pt[VN_LO - off:VN_HI - off].astype(BF16)


def _in_proj(x, shift, scale, gain, w_t, bd_t, cos_t, sin_t, *, with_q, rope, tm):
    b, t, d = x.shape
    nf = w_t.shape[0]
    rows = ([2 * FOURIER_WIDTH, SWA_Q_W, NA_W] if with_q else []) + [SWA_KV_W, SWA_KV_W, NA_W, NA_W]
    return pl.pallas_call(
        functools.partial(_in_proj_kernel, with_q=with_q, rope=rope),
        out_shape=[jax.ShapeDtypeStruct((b, r, t), BF16) for r in rows],
        grid=(b, t // tm),
        in_specs=[
            pl.BlockSpec((1, tm, d), lambda i, j: (i, j, 0)),
            pl.BlockSpec((1, 1, d), lambda i, j: (i, 0, 0)),
            pl.BlockSpec((1, 1, d), lambda i, j: (i, 0, 0)),
            pl.BlockSpec((1, d), lambda i, j: (0, 0)),
            pl.BlockSpec((nf, d), lambda i, j: (0, 0)),
            pl.BlockSpec(bd_t.shape, lambda i, j: (0, 0)),
            pl.BlockSpec((HEAD_DIM, tm), lambda i, j: (0, j)),
            pl.BlockSpec((HEAD_DIM, tm), lambda i, j: (0, j)),
        ],
        out_specs=[pl.BlockSpec((1, r, tm), lambda i, j: (i, 0, j)) for r in rows],
        compiler_params=_cparams("parallel", "parallel"),
        name="in_proj_q" if with_q else "in_proj_kv",
    )(x, shift, scale, gain, w_t, bd_t, cos_t, sin_t)


def _fourier_kernel(fz_ref, cn_ref, sn_ref, wft_ref, o_ref):
    zc = fz_ref[0, 0:FOURIER_WIDTH, :]
    zs = fz_ref[0, FOURIER_WIDTH:2 * FOURIER_WIDTH, :]
    y = _dot(zc, cn_ref[...]) - _dot(zs, sn_ref[...])
    o_ref[0] = _dot(wft_ref[...], y.astype(BF16)).astype(BF16)


def _fourier(fz, cn, sn, wf_t, *, tk):
    b, _, t = fz.shape
    return pl.pallas_call(
        _fourier_kernel,
        out_shape=jax.ShapeDtypeStruct((b, FOURIER_WIDTH, t), BF16),
        grid=(t // tk, b),
        in_specs=[
            pl.BlockSpec((1, 2 * FOURIER_WIDTH, t), lambda k, i: (i, 0, 0)),
            pl.BlockSpec((t, tk), lambda k, i: (0, k)),
            pl.BlockSpec((t, tk), lambda k, i: (0, k)),
            pl.BlockSpec((FOURIER_WIDTH, FOURIER_WIDTH), lambda k, i: (0, 0)),
        ],
        out_specs=pl.BlockSpec((1, FOURIER_WIDTH, tk), lambda k, i: (i, 0, k)),
        compiler_params=_cparams("parallel", "parallel"),
        name="fourier",
    )(fz, cn, sn, wf_t)


def _key_chunks(k, v, bias=None):
    n = k.shape[1] // ATTN_KCHUNK
    cut = lambda a, j, axis: lax.slice_in_dim(a, j * ATTN_KCHUNK, (j + 1) * ATTN_KCHUNK, axis=axis)
    return [(cut(k, j, 1), cut(v, j, 1), None if bias is None else cut(bias, j, 0)) for j in range(n)]


def _logits(q_t, chunks):
    return _dot_tn(jnp.concatenate([k_t for k_t, _, _ in chunks], axis=1), q_t)


def _softmax_pv(s, chunks, sink_row):
    pieces = []
    off = 0
    for k_t, _, bias in chunks:
        piece = s[off:off + k_t.shape[1]]
        pieces.append(piece if bias is None else piece + bias)
        off += k_t.shape[1]
    m = functools.reduce(jnp.maximum, [jnp.max(p, axis=0, keepdims=True) for p in pieces])
    if sink_row is not None:
        m = jnp.maximum(m, sink_row)
    probs = [jnp.exp2(p - m) for p in pieces]
    den = functools.reduce(jnp.add, [jnp.sum(p, axis=0, keepdims=True) for p in probs])
    if sink_row is not None:
        den = den + jnp.exp2(sink_row - m)
    v_all = jnp.concatenate([v_t for _, v_t, _ in chunks], axis=1)
    p_all = jnp.concatenate([p.astype(BF16) for p in probs], axis=0)
    return _dot(v_all, p_all) / den


def _attend(q_t, chunks, sink_row):
    return _softmax_pv(_logits(q_t, chunks), chunks, sink_row)


def _attend_blocks(n_blocks, make_block, sink_row, store):
    blocks, logits = {}, {}
    for j in range(min(ATTN_LOOKAHEAD, n_blocks)):
        blocks[j] = make_block(j)
        logits[j] = _logits(*blocks[j])
    for j in range(n_blocks):
        ahead = j + ATTN_LOOKAHEAD
        if ahead < n_blocks:
            blocks[ahead] = make_block(ahead)
            logits[ahead] = _logits(*blocks[ahead])
        store(j, _softmax_pv(logits.pop(j), blocks.pop(j)[1], sink_row))


def _swa_kernel(q_ref, k_ref, v_ref, kc_ref, vc_ref, mask_ref, sink_ref, o_ref, *, seq):
    nb = seq // SWA_BLOCK
    ctx_chunks = _key_chunks(kc_ref[0], vc_ref[0])

    def make_block(n):
        q0 = n * SWA_BLOCK
        q_t = jnp.concatenate(
            [q_ref[0, HEAD_DIM * h:HEAD_DIM * (h + 1), q0:q0 + SWA_BLOCK] for h in range(SWA_GROUP)], axis=1)
        chunks = [(k_ref[0, :, q0:q0 + SWA_BLOCK], v_ref[0, :, q0:q0 + SWA_BLOCK], None)] + ctx_chunks
        for side, kb in ((0, n - 1), (1, n + 1)):
            if 0 <= kb < nb:
                k0 = kb * SWA_BLOCK
                chunks.append((k_ref[0, :, k0:k0 + SWA_BLOCK], v_ref[0, :, k0:k0 + SWA_BLOCK], mask_ref[side]))
        return q_t, chunks

    def store(n, o):
        q0 = n * SWA_BLOCK
        for h in range(SWA_GROUP):
            o_ref[0, HEAD_DIM * h:HEAD_DIM * (h + 1), q0:q0 + SWA_BLOCK] = (
                o[:, SWA_BLOCK * h:SWA_BLOCK * (h + 1)].astype(BF16))

    _attend_blocks(nb, make_block, sink_ref[0], store)


def _swa(qs, ks, vs, kc, vc, mask, sink_rows):
    b, _, t = qs.shape
    lc = kc.shape[2]
    gw = SWA_GROUP * HEAD_DIM
    return pl.pallas_call(
        functools.partial(_swa_kernel, seq=t),
        out_shape=jax.ShapeDtypeStruct((b, SWA_Q_W, t), BF16),
        grid=(b, SWA_KV_HEADS),
        in_specs=[
            pl.BlockSpec((1, gw, t), lambda i, g: (i, g, 0)),
            pl.BlockSpec((1, HEAD_DIM, t), lambda i, g: (i, g, 0)),
            pl.BlockSpec((1, HEAD_DIM, t), lambda i, g: (i, g, 0)),
            pl.BlockSpec((1, HEAD_DIM, lc), lambda i, g: (i, g, 0)),
            pl.BlockSpec((1, HEAD_DIM, lc), lambda i, g: (i, g, 0)),
            pl.BlockSpec((2, SWA_BLOCK, SWA_GROUP * SWA_BLOCK), lambda i, g: (0, 0, 0)),
            pl.BlockSpec((1, 1, SWA_GROUP * SWA_BLOCK), lambda i, g: (g, 0, 0)),
        ],
        out_specs=pl.BlockSpec((1, gw, t), lambda i, g: (i, g, 0)),
        compiler_params=_cparams("parallel", "parallel"),
        name="swa",
    )(qs, ks, vs, kc, vc, mask, sink_rows)


def _na_bias_kernel(rpb_ref, o_ref, u_ref, *, total_rows):
    hd = pl.program_id(0)
    kc = lax.broadcasted_iota(jnp.int32, (GRID_W, LANE), 0)
    lane = lax.broadcasted_iota(jnp.int32, (GRID_W, LANE), 1)
    qc = lane % GRID_W
    dc = jnp.clip(kc - qc, -(NA_WIN_C - 1), NA_WIN_C - 1) + (NA_WIN_C - 1)
    c0 = jnp.clip(qc - NA_WIN_C // 2, 0, GRID_W - NA_WIN_C)
    valid_c = (kc >= c0) & (kc < c0 + NA_WIN_C)
    n_dr = 2 * NA_WIN_R - 1
    for dr in range(n_dr):
        u = jnp.full((GRID_W, LANE), NEG_INF, F32)
        for d in range(2 * NA_WIN_C - 1):
            u = jnp.where(valid_c & (dc == d), rpb_ref[hd, dr, d] * LOG2E, u)
        u_ref[dr] = u
    n_rows = o_ref.shape[2] // GRID_W
    block_types = [(0, 0), (NA_QROWS, 0), (total_rows - NA_QROWS, total_rows - NA_KROWS)]
    neg = jnp.full((GRID_W, LANE), NEG_INF, F32)
    for t, (r_base, k_base) in enumerate(block_types):
        for kl in range(n_rows):
            kr = k_base + kl
            for lg in range(NA_QROWS // 2):
                halves = []
                for rq in (2 * lg, 2 * lg + 1):
                    r = r_base + rq
                    r0 = min(max(r - NA_WIN_R // 2, 0), total_rows - NA_WIN_R)
                    ok = r0 <= kr < r0 + NA_WIN_R
                    halves.append(u_ref[kr - r + NA_WIN_R - 1] if ok else neg)
                o_ref[0, t, GRID_W * kl:GRID_W * (kl + 1), LANE * lg:LANE * (lg + 1)] = jnp.where(
                    lane < GRID_W, halves[0], halves[1])


def _na_bias(rpb, seq):
    nh = rpb.shape[0]
    return pl.pallas_call(
        functools.partial(_na_bias_kernel, total_rows=seq // GRID_W),
        out_shape=jax.ShapeDtypeStruct((nh, 3, NA_KROWS * GRID_W, NA_QROWS * GRID_W), F32),
        grid=(nh,),
        in_specs=[pl.BlockSpec(memory_space=pltpu.SMEM)],
        out_specs=pl.BlockSpec((1, 3, NA_KROWS * GRID_W, NA_QROWS * GRID_W), lambda h: (h, 0, 0, 0)),
        scratch_shapes=[pltpu.VMEM((2 * NA_WIN_R - 1, GRID_W, LANE), F32)],
        compiler_params=_cparams("parallel"),
        name="na_bias",
    )(rpb)


def _na_kernel(q_ref, k_ref, v_ref, kc_ref, vc_ref, bias_ref, o_ref, *, seq):
    n_rows = seq // GRID_W
    qw = NA_QROWS * GRID_W
    kw = NA_KROWS * GRID_W
    ctx_chunks = _key_chunks(kc_ref[0], vc_ref[0])
    nblk = n_rows // NA_QROWS
    rows_per_chunk = ATTN_KCHUNK // GRID_W

    def make_block(j):
        k_row = min(max(j * NA_QROWS - NA_WIN_R // 2, 0), n_rows - NA_KROWS)
        btype = 0 if j == 0 else (2 if j == nblk - 1 else 1)
        q0 = j * qw
        chunks = list(ctx_chunks)
        for cj in range(kw // ATTN_KCHUNK):
            first = k_row + cj * rows_per_chunk
            in_window = False
            for r in range(j * NA_QROWS, (j + 1) * NA_QROWS):
                r0 = min(max(r - NA_WIN_R // 2, 0), n_rows - NA_WIN_R)
                in_window = in_window or (first < r0 + NA_WIN_R and first + rows_per_chunk > r0)
            if in_window:
                k0 = first * GRID_W
                chunks.append((k_ref[0, :, k0:k0 + ATTN_KCHUNK], v_ref[0, :, k0:k0 + ATTN_KCHUNK],
                               bias_ref[0, btype, cj * ATTN_KCHUNK:(cj + 1) * ATTN_KCHUNK, :]))
        return q_ref[0, :, q0:q0 + qw], chunks

    def store(j, o):
        o_ref[0, :, j * qw:(j + 1) * qw] = o.astype(BF16)

    _attend_blocks(nblk, make_block, None, store)


def _na(qn, kn, vn, kc, vc, bias):
    b, _, t = qn.shape
    lc = kc.shape[2]
    head = lambda i, h: (i, h, 0)
    return pl.pallas_call(
        functools.partial(_na_kernel, seq=t),
        out_shape=jax.ShapeDtypeStruct((b, NA_W, t), BF16),
        grid=(b, NA_HEADS),
        in_specs=[
            pl.BlockSpec((1, HEAD_DIM, t), head),
            pl.BlockSpec((1, HEAD_DIM, t), head),
            pl.BlockSpec((1, HEAD_DIM, t), head),
            pl.BlockSpec((1, HEAD_DIM, lc), head),
            pl.BlockSpec((1, HEAD_DIM, lc), head),
            pl.BlockSpec((1,) + bias.shape[1:], lambda i, h: (h, 0, 0, 0)),
        ],
        out_specs=pl.BlockSpec((1, HEAD_DIM, t), head),
        compiler_params=_cparams("parallel", "parallel"),
        name="na",
    )(qn, kn, vn, kc, vc, bias)


def _ctx_attn_kernel(qs_ref, ks_ref, vs_ref, qn_ref, kn_ref, vn_ref, sink_ref, ys_ref, yn_ref):
    lc = qs_ref.shape[2]
    for g in range(SWA_KV_HEADS):
        q_t = jnp.concatenate(
            [qs_ref[0, HEAD_DIM * (SWA_GROUP * g + h):HEAD_DIM * (SWA_GROUP * g + h + 1), :] for h in range(SWA_GROUP)],
            axis=1)
        kv = slice(HEAD_DIM * g, HEAD_DIM * (g + 1))
        o = _attend(q_t, _key_chunks(ks_ref[0, kv, :], vs_ref[0, kv, :]), sink_ref[g])
        for h in range(SWA_GROUP):
            hh = SWA_GROUP * g + h
            ys_ref[0, HEAD_DIM * hh:HEAD_DIM * (hh + 1), :] = o[:, lc * h:lc * (h + 1)].astype(BF16)
    for h in range(NA_HEADS):
        sl = slice(HEAD_DIM * h, HEAD_DIM * (h + 1))
        o = _attend(qn_ref[0, sl, :], _key_chunks(kn_ref[0, sl, :], vn_ref[0, sl, :]), None)
        yn_ref[0, sl, :] = o.astype(BF16)


def _ctx_attn(qs, ks, vs, qn, kn, vn, sink_rows):
    b, _, lc = qs.shape
    full = lambda a: pl.BlockSpec((1,) + a.shape[1:], lambda i: (i, 0, 0))
    return pl.pallas_call(
        _ctx_attn_kernel,
        out_shape=[jax.ShapeDtypeStruct((b, SWA_Q_W, lc), BF16), jax.ShapeDtypeStruct((b, NA_W, lc), BF16)],
        grid=(b,),
        in_specs=[full(qs), full(ks), full(vs), full(qn), full(kn), full(vn),
                  pl.BlockSpec(sink_rows.shape, lambda i: (0, 0, 0))],
        out_specs=[pl.BlockSpec((1, SWA_Q_W, lc), lambda i: (i, 0, 0)),
                   pl.BlockSpec((1, NA_W, lc), lambda i: (i, 0, 0))],
        compiler_params=_cparams("parallel"),
        name="ctx_attn",
    )(qs, ks, vs, qn, kn, vn, sink_rows)


def _route(logits):
    row = lax.broadcasted_iota(jnp.int32, logits.shape, 0)
    big = jnp.int32(logits.shape[0])
    colmax = lambda a: jnp.max(a, axis=0, keepdims=True)
    first = lambda hit: jnp.min(jnp.where(hit, row, big), axis=0, keepdims=True)
    gmask = row < N_GROUPS
    gl = jnp.where(gmask, logits, NEG_INF)
    gmax = colmax(gl)
    g_sel = first(gl == gmax)
    p_g = 1.0 / jnp.sum(jnp.where(gmask, jnp.exp(logits - gmax), 0.0), axis=0, keepdims=True)
    lo = N_GROUPS + EXPERTS_PER_GROUP * g_sel
    el = jnp.where((row >= lo) & (row < lo + EXPERTS_PER_GROUP), logits, NEG_INF)
    v1 = colmax(el)
    i1 = first(el == v1)
    el2 = jnp.where(row == i1, NEG_INF, el)
    v2 = colmax(el2)
    i2 = first(el2 == v2)
    e21 = jnp.exp(v2 - v1)
    w1 = p_g / (1.0 + e21)
    w2 = p_g * e21 / (1.0 + e21)
    out_row = lax.broadcasted_iota(jnp.int32, (ROUTE_OUT, logits.shape[1]), 0)
    out = jnp.where(out_row == 0, (i1 - N_GROUPS).astype(F32), 0.0)
    out = jnp.where(out_row == 1, (i2 - N_GROUPS).astype(F32), out)
    out = jnp.where(out_row == 2, w1, out)
    return jnp.where(out_row == 3, w2, out)


def _out_proj_kernel(x_ref, yf_ref, ys_ref, yn_ref, wo_ref, g1_ref, sh_ref, sc_ref, gn_ref, wrh_ref, wrl_ref,
                     br_ref, x1_ref, h2_ref, rt_ref):
    toks = [slice(i, i + PROJ_SUBTILE) for i in range(0, x_ref.shape[1], PROJ_SUBTILE)]
    ys = [_dot_tn(jnp.concatenate([yf_ref[0, :, tok], ys_ref[0, :, tok], yn_ref[0, :, tok]], axis=0), wo_ref[...])
          for tok in toks]
    for tok, y in zip(toks, ys):
        x1 = x_ref[0, tok, :] + g1_ref[0] * y
        x1_ref[0, tok, :] = x1
        ms = jnp.mean(x1 * x1, axis=-1, keepdims=True)
        h2 = (x1 * lax.rsqrt(ms + EPS) * gn_ref[...]) * (1.0 + sc_ref[0]) + sh_ref[0]
        h_hi = h2.astype(BF16)
        h2_ref[0, tok, :] = h_hi
        h_lo = (h2 - h_hi.astype(F32)).astype(BF16)
        logits = (_dot_nt(wrh_ref[...], h_hi) + (_dot_nt(wrh_ref[...], h_lo) + _dot_nt(wrl_ref[...], h_hi))
                  + br_ref[...])
        rt_ref[0, :, tok] = _route(logits)


def _out_proj(x, yf, ys, yn, w_out, gate, shift, scale, gain, wr_hi, wr_lo, b_route, *, tm):
    b, t, d = x.shape
    vec = pl.BlockSpec((1, 1, d), lambda i, j: (i, 0, 0))
    feat = lambda r: pl.BlockSpec((1, r, tm), lambda i, j: (i, 0, j))
    tok = lambda w: pl.BlockSpec((1, tm, w), lambda i, j: (i, j, 0))
    whole = lambda a: pl.BlockSpec(a.shape, lambda i, j: (0, 0))
    return pl.pallas_call(
        _out_proj_kernel,
        out_shape=[jax.ShapeDtypeStruct((b, t, d), F32), jax.ShapeDtypeStruct((b, t, d), BF16),
                   jax.ShapeDtypeStruct((b, ROUTE_OUT, t), F32)],
        grid=(b, t // tm),
        in_specs=[tok(d), feat(FOURIER_WIDTH), feat(SWA_Q_W), feat(NA_W), whole(w_out),
                  vec, vec, vec, pl.BlockSpec((1, d), lambda i, j: (0, 0)),
                  whole(wr_hi), whole(wr_lo), whole(b_route)],
        out_specs=[tok(d), tok(d), feat(ROUTE_OUT)],
        compiler_params=_cparams("parallel", "parallel"),
        name="out_proj",
    )(x, yf, ys, yn, w_out, gate, shift, scale, gain, wr_hi, wr_lo, b_route)


def _start_or_wait(copy, wait):
    if wait:
        copy.wait()
    else:
        copy.start()


def _run_copies(local, remote, sem, runs_ref, c, *, to_remote, wait):
    def body(e, carry):
        rows = runs_ref[0, c, e]

        @pl.when(rows > 0)
        def _():
            n = pl.multiple_of(rows, ROW_BLOCK)
            loc = local.at[pl.ds(pl.multiple_of(runs_ref[1, c, e], ROW_BLOCK), n)]
            rem = remote.at[pl.ds(pl.multiple_of(runs_ref[2, c, e], ROW_BLOCK), n)]
            _start_or_wait(pltpu.make_async_copy(loc, rem, sem) if to_remote
                           else pltpu.make_async_copy(rem, loc, sem), wait)

        return carry

    lax.fori_loop(0, N_EXPERTS, body, 0)


def _tail_copies(zero, remote, sem, tstart_ref, tn_ref, *, wait):
    def body(e, carry):
        rows = tn_ref[e]

        @pl.when(rows > 0)
        def _():
            n = pl.multiple_of(rows, ROW_BLOCK)
            rem = remote.at[pl.ds(pl.multiple_of(tstart_ref[e], ROW_BLOCK), n)]
            _start_or_wait(pltpu.make_async_copy(zero.at[pl.ds(0, n)], rem, sem), wait)

        return carry

    lax.fori_loop(0, N_EXPERTS, body, 0)


def _unused_tile_copies(zero, remote, sem, nt_ref, total_tiles, *, wait):
    def body(t, carry):
        row = pl.multiple_of(t * MOE_TILE, MOE_TILE)
        cp = pltpu.make_async_copy(zero, remote.at[pl.ds(row, MOE_TILE)], sem)
        if wait:
            cp.wait()
        else:
            cp.start()
        return carry

    lax.fori_loop(nt_ref[0], total_tiles, body, 0)


def _experts_kernel(t0_ref, tcnt_ref, nt_ref, xs_ref, wg_ref, wu_ref, wd_ref, ys_ref,
                    xbuf, ybuf, zero, wg_s, wu_s, wd_s, xsem, ysem, zsem):
    e = pl.program_id(0)
    n = tcnt_ref[e]
    total_tiles = xs_ref.shape[0] // MOE_TILE

    def tile_rows(ref, t):
        return ref.at[pl.ds(pl.multiple_of((t0_ref[e] + t) * MOE_TILE, MOE_TILE), MOE_TILE)]

    x_copy = lambda t, slot: pltpu.make_async_copy(tile_rows(xs_ref, t), xbuf.at[slot], xsem.at[slot])
    y_copy = lambda t, slot: pltpu.make_async_copy(ybuf.at[slot], tile_rows(ys_ref, t), ysem.at[slot])

    @pl.when(e == 0)
    def _():
        zero[...] = jnp.zeros(zero.shape, F32)
        _unused_tile_copies(zero, ys_ref, zsem, nt_ref, total_tiles, wait=False)

    @pl.when(n > 0)
    def _():
        x_copy(0, 0).start()
        wg_s[...] = wg_ref[0, 0].astype(BF16)
        wu_s[...] = wu_ref[0, 0].astype(BF16)
        wd_s[...] = wd_ref[0, 0].astype(BF16)

        def body(t, carry):
            slot = t % 2
            x_copy(t, slot).wait()

            @pl.when(t + 1 < n)
            def _():
                x_copy(t + 1, 1 - slot).start()

            @pl.when(t >= 2)
            def _():
                y_copy(t - 2, slot).wait()

            x = xbuf[slot].astype(BF16)
            g = _dot(x, wg_s[...])
            u = _dot(x, wu_s[...])
            hid = (g * (1.0 / (1.0 + jnp.exp(-g)))) * u
            ybuf[slot] = _dot(hid.astype(BF16), wd_s[...])
            y_copy(t, slot).start()
            return carry

        lax.fori_loop(0, n, body, 0)
        y_copy(n - 1, (n - 1) % 2).wait()

        @pl.when(n >= 2)
        def _():
            y_copy(n - 2, n % 2).wait()

    @pl.when(e == N_EXPERTS - 1)
    def _():
        _unused_tile_copies(zero, ys_ref, zsem, nt_ref, total_tiles, wait=True)


def _experts(xs, plan, w_gate, w_up, w_down, layer):
    r, d = xs.shape
    de = w_gate.shape[3]
    wspec = lambda a, b: pl.BlockSpec((1, 1, a, b), lambda e, *_: (layer, e, 0, 0))
    any_spec = pl.BlockSpec(memory_space=pl.ANY)
    return pl.pallas_call(
        _experts_kernel,
        out_shape=jax.ShapeDtypeStruct((r, d), F32),
        grid_spec=pltpu.PrefetchScalarGridSpec(
            num_scalar_prefetch=3,
            grid=(N_EXPERTS,),
            in_specs=[any_spec, wspec(d, de), wspec(d, de), wspec(de, d)],
            out_specs=any_spec,
            scratch_shapes=[pltpu.VMEM((2, MOE_TILE, d), F32), pltpu.VMEM((2, MOE_TILE, d), F32),
                            pltpu.VMEM((MOE_TILE, d), F32),
                            pltpu.VMEM((d, de), BF16), pltpu.VMEM((d, de), BF16), pltpu.VMEM((de, d), BF16),
                            pltpu.SemaphoreType.DMA((2,)), pltpu.SemaphoreType.DMA((2,)),
                            pltpu.SemaphoreType.DMA(())],
        ),
        compiler_params=_cparams("arbitrary"),
        name="experts",
    )(plan["tile0"], plan["tcnt"], plan["n_tiles"], xs, w_gate, w_up, w_down)


def _dispatch_kernel(runs_ref, tstart_ref, tn_ref, nt_ref, *rest, part_chunks):
    n_parts = len(part_chunks)
    h_refs = rest[:n_parts]
    pos_ref, xs_ref, buf, zero, sem, zsem = rest[n_parts:]
    n_chunks = sum(part_chunks)
    c = pl.program_id(0)
    slot = c % 2
    row = lax.broadcasted_iota(jnp.int32, (MOE_LROWS, MOE_CHUNK), 0)
    hit = (row == pos_ref[0, 0:1, :]) | (row == pos_ref[0, 1:2, :])
    sel = jnp.where(hit, 1.0, 0.0).astype(BF16)
    first = 0
    for h_ref, n in zip(h_refs, part_chunks):
        @pl.when((c >= first) & (c < first + n))
        def _(h_ref=h_ref):
            buf[slot] = _dot(sel, h_ref[...])
        first += n
    _run_copies(buf.at[slot], xs_ref, sem.at[slot], runs_ref, c, to_remote=True, wait=False)

    @pl.when(c > 0)
    def _():
        _run_copies(buf.at[1 - slot], xs_ref, sem.at[1 - slot], runs_ref, c - 1, to_remote=True, wait=True)

    @pl.when(c == n_chunks - 1)
    def _():
        total_tiles = xs_ref.shape[0] // MOE_TILE
        zero[...] = jnp.zeros(zero.shape, F32)
        _tail_copies(zero, xs_ref, zsem, tstart_ref, tn_ref, wait=False)
        _unused_tile_copies(zero, xs_ref, zsem, nt_ref, total_tiles, wait=False)
        _run_copies(buf.at[slot], xs_ref, sem.at[slot], runs_ref, c, to_remote=True, wait=True)
        _tail_copies(zero, xs_ref, zsem, tstart_ref, tn_ref, wait=True)
        _unused_tile_copies(zero, xs_ref, zsem, nt_ref, total_tiles, wait=True)


def _dispatch(h_parts, plan):
    d = h_parts[0].shape[1]
    part_chunks = tuple(h.shape[0] // MOE_CHUNK for h in h_parts)
    in_specs = []
    first = 0
    for n in part_chunks:
        in_specs.append(pl.BlockSpec(
            (MOE_CHUNK, d), lambda i, *_, first=first, n=n: (jnp.clip(i - first, 0, n - 1), 0)))
        first += n
    in_specs.append(pl.BlockSpec((1, 8, MOE_CHUNK), lambda i, *_: (i, 0, 0)))
    return pl.pallas_call(
        functools.partial(_dispatch_kernel, part_chunks=part_chunks),
        out_shape=jax.ShapeDtypeStruct((plan["rows"], d), F32),
        grid_spec=pltpu.PrefetchScalarGridSpec(
            num_scalar_prefetch=4,
            grid=(sum(part_chunks),),
            in_specs=in_specs,
            out_specs=pl.BlockSpec(memory_space=pl.ANY),
            scratch_shapes=[pltpu.VMEM((2, MOE_LROWS, d), F32), pltpu.VMEM((MOE_TILE, d), F32),
                            pltpu.SemaphoreType.DMA((2,)), pltpu.SemaphoreType.DMA(())],
        ),
        compiler_params=_cparams("arbitrary"),
        name="dispatch",
    )(plan["runs"], plan["tstart"], plan["tn"], plan["n_tiles"], *h_parts, plan["pos"])


def _combine_kernel(runs_ref, x1_ref, pos_ref, wt_ref, g2_ref, gf_ref, ys_ref, o_ref, buf, sem,
                    *, c0, n_chunks, final_norm):
    i = pl.program_id(0)
    c = i + c0
    slot = i % 2
    gather = functools.partial(_run_copies, remote=ys_ref, runs_ref=runs_ref, to_remote=False)

    @pl.when(i == 0)
    def _():
        buf[...] = jnp.zeros(buf.shape, F32)
        gather(buf.at[slot], sem=sem.at[slot], c=c, wait=False)

    @pl.when(i + 1 < n_chunks)
    def _():
        gather(buf.at[1 - slot], sem=sem.at[1 - slot], c=c + 1, wait=False)

    gather(buf.at[slot], sem=sem.at[slot], c=c, wait=True)
    row = lax.broadcasted_iota(jnp.int32, (MOE_LROWS, MOE_CHUNK), 0)
    hit1 = row == pos_ref[0, 0:1, :]
    hit2 = row == pos_ref[0, 1:2, :]
    w_row = jnp.sum(jnp.where(hit1, wt_ref[0, 0:1, :], 0.0) + jnp.where(hit2, wt_ref[0, 1:2, :], 0.0),
                    axis=1, keepdims=True)
    sel = jnp.where(hit1 | hit2, 1.0, 0.0).astype(BF16)
    x2 = x1_ref[0] + g2_ref[0] * _dot_tn(sel, (buf[slot] * w_row).astype(BF16))
    if final_norm:
        ms = jnp.mean(x2 * x2, axis=-1, keepdims=True)
        x2 = x2 * lax.rsqrt(ms + EPS) * gf_ref[...]
    o_ref[0] = x2


def _combine(x1, plan, gate, g_final, ys, *, c0, final_norm):
    b, t, d = x1.shape
    per_batch = t // MOE_CHUNK
    n_chunks = b * per_batch
    tok = lambda w: pl.BlockSpec((1, MOE_CHUNK, w), lambda i, *_: (i // per_batch, i % per_batch, 0))
    return pl.pallas_call(
        functools.partial(_combine_kernel, c0=c0, n_chunks=n_chunks, final_norm=final_norm),
        out_shape=jax.ShapeDtypeStruct((b, t, d), F32),
        grid_spec=pltpu.PrefetchScalarGridSpec(
            num_scalar_prefetch=1,
            grid=(n_chunks,),
            in_specs=[tok(d),
                      pl.BlockSpec((1, 8, MOE_CHUNK), lambda i, *_: (i + c0, 0, 0)),
                      pl.BlockSpec((1, 8, MOE_CHUNK), lambda i, *_: (i + c0, 0, 0)),
                      pl.BlockSpec((1, 1, d), lambda i, *_: (i // per_batch, 0, 0)),
                      pl.BlockSpec((1, d), lambda i, *_: (0, 0)),
                      pl.BlockSpec(memory_space=pl.ANY)],
            out_specs=tok(d),
            scratch_shapes=[pltpu.VMEM((2, MOE_LROWS, d), F32), pltpu.SemaphoreType.DMA((2,))],
        ),
        compiler_params=_cparams("arbitrary"),
        name="combine",
    )(plan["runs"], x1, plan["pos"], plan["wts"], gate, g_final, ys)


def _rope_tables(seq):
    half = HEAD_DIM // 4
    t = jnp.arange(seq)
    rows, cols = t // GRID_W, t % GRID_W
    freqs = ROPE_BASE ** (-jnp.arange(half, dtype=F32) / half)

    def cs(pos):
        ang = pos.astype(F32)[None, :] * freqs[:, None]
        return jnp.cos(ang), jnp.sin(ang)

    cr, sr = cs(rows)
    cc, sc = cs(cols)
    return jnp.concatenate([cr, cr, cc, cc], axis=0), jnp.concatenate([-sr, sr, -sc, sc], axis=0)


def _dft_tables(n):
    k = jnp.arange(n)

    def cs(m):
        ang = ((m[:, None] * k[None, :]) % n).astype(F32) * (2.0 * jnp.pi / n)
        return jnp.cos(ang), jnp.sin(ang)

    if n <= GRID_W:
        return cs(k)
    ca, sa = cs(jnp.arange(n // GRID_W) * GRID_W)
    cb, sb = cs(jnp.arange(GRID_W))
    c = ca[:, None, :] * cb[None, :, :] - sa[:, None, :] * sb[None, :, :]
    s = sa[:, None, :] * cb[None, :, :] + ca[:, None, :] * sb[None, :, :]
    return c.reshape(n, n), s.reshape(n, n)


def _channel_dft_t():
    c, s = _dft_tables(FOURIER_GROUP_DIM)
    eye = jnp.eye(FOURIER_GROUPS, dtype=F32)
    scale = FOURIER_GROUP_DIM ** -0.5
    return jnp.concatenate([jnp.kron(eye, c), jnp.kron(eye, s)], axis=0) * scale


def _position_dft(n):
    c, s = _dft_tables(n)
    scale = n ** -0.5
    return (c * scale).astype(BF16), (s * scale).astype(BF16)


def _swa_mask():
    kk = jnp.arange(SWA_BLOCK)[:, None]
    q = jnp.arange(SWA_BLOCK)[None, :]
    tiles = []
    for block_offset in (-SWA_BLOCK, SWA_BLOCK):
        ok = jnp.abs(block_offset + kk - q) <= SWA_WINDOW
        tiles.append(jnp.tile(jnp.where(ok, 0.0, NEG_INF).astype(F32), (1, SWA_GROUP)))
    return jnp.stack(tiles)


def _sink_rows(sink, width):
    return jnp.repeat(sink.astype(F32).reshape(SWA_KV_HEADS, SWA_GROUP) * LOG2E, width, axis=1)[:, None, :]


def _route_chunks(route):
    r = jnp.moveaxis(route, 1, 0).reshape(ROUTE_OUT, -1, MOE_CHUNK)
    return jnp.swapaxes(r, 0, 1)


def _moe_plan(route_chunks):
    i32 = jnp.int32
    nc = route_chunks.shape[0]
    n = nc * MOE_CHUNK
    ids = route_chunks[:, 0:2, :].astype(i32).reshape(nc, 2 * MOE_CHUNK)
    onehot = (ids[:, :, None] == jnp.arange(N_EXPERTS, dtype=i32)).astype(i32)
    pair = jnp.arange(2 * MOE_CHUNK, dtype=i32)
    earlier = (pair[None, :] < pair[:, None]).astype(BF16)
    before = jnp.einsum("pq,cqe->cpe", earlier, onehot.astype(BF16), preferred_element_type=F32).astype(i32)
    rank = jnp.sum(before * onehot, axis=-1)
    cnt = jnp.sum(onehot, axis=1)
    run = (cnt + ROW_BLOCK - 1) // ROW_BLOCK * ROW_BLOCK
    lo = jnp.cumsum(run, axis=1) - run
    pos = (jnp.sum(onehot * lo[:, None, :], axis=-1) + rank).reshape(nc, 2, MOE_CHUNK)
    pos = jnp.concatenate([pos, jnp.full((nc, 6, MOE_CHUNK), -1, i32)], axis=1)
    seg = jnp.sum(run, axis=0)
    padded = (seg + MOE_TILE - 1) // MOE_TILE * MOE_TILE
    start = jnp.cumsum(padded) - padded
    off = start[None, :] + jnp.cumsum(run, axis=0) - run
    wts = jnp.concatenate([route_chunks[:, 2:4, :], jnp.zeros((nc, 6, MOE_CHUNK), F32)], axis=1)
    rows = -(-(2 * n + (ROW_BLOCK - 1) * N_EXPERTS * nc + (MOE_TILE - 1) * N_EXPERTS) // MOE_TILE) * MOE_TILE
    n_tiles = jnp.sum(padded) // MOE_TILE
    return {"pos": pos, "wts": wts, "runs": jnp.stack([run, lo, off]).astype(i32),
            "tstart": (start + seg).astype(i32), "tn": (padded - seg).astype(i32),
            "tile0": (start // MOE_TILE).astype(i32), "tcnt": (padded // MOE_TILE).astype(i32),
            "n_tiles": n_tiles.astype(i32).reshape(1), "rows": rows}


def kernel(x, c, ctx, c_ctx, w_mod, b_mod, g_norm1, g_norm2, w_in, w_four, w_out, swa_sink, na_rpb,
           w_route_group, b_route_group, w_route_expert, b_route_expert, w_exp_gate, w_exp_up,
           w_exp_down, g_final):
    b, s, d = x.shape
    lc = ctx.shape[1]
    depth = w_mod.shape[0]
    tm = 512

    c_rows = jnp.concatenate([c, c_ctx[None, :], jnp.zeros((7, d), F32)], axis=0)
    mod = _modulation(c_rows, w_mod, b_mod)

    cos_t, sin_t = _rope_tables(s)
    cos_c, sin_c = cos_t[:, :lc], sin_t[:, :lc]
    bd_t = _channel_dft_t().astype(BF16)
    cn, sn = _position_dft(s)
    cn_c, sn_c = _position_dft(lc)
    mask = _swa_mask()
    route_pad = ROUTE_ROWS - N_GROUPS - N_EXPERTS

    xc = ctx
    for layer in range(depth):
        with_ctx_out = layer < depth - 1
        lat = [mod[layer, :b, i * d:(i + 1) * d][:, None, :] for i in range(6)]
        cx = [jnp.broadcast_to(mod[layer, b, i * d:(i + 1) * d][None, None, :], (b, 1, d)) for i in range(6)]
        sh1, sc1, g1, sh2, sc2, g2 = lat
        shc1, scc1, gc1, shc2, scc2, gc2 = cx
        gn1 = g_norm1[layer][None, :]
        gn2 = g_norm2[layer][None, :]
        w_t = w_in[layer].T.astype(BF16)
        wf_t = w_four[layer].T.astype(BF16)
        wo = w_out[layer].astype(BF16)
        w_r = jnp.concatenate([w_route_group[layer].T, w_route_expert[layer].T,
                               jnp.zeros((route_pad, d), F32)], axis=0)
        wr_hi = w_r.astype(BF16)
        wr_lo = (w_r - wr_hi.astype(F32)).astype(BF16)
        b_r = jnp.concatenate([b_route_group[layer], b_route_expert[layer], jnp.zeros((route_pad,), F32)])[:, None]
        sink_lat = _sink_rows(swa_sink[layer], SWA_BLOCK)
        sink_ctx = _sink_rows(swa_sink[layer], lc)

        fz, qs, qn, ks, vs, kn, vn = _in_proj(x, sh1, sc1, gn1, w_t, bd_t, cos_t, sin_t,
                                              with_q=True, rope=True, tm=tm)
        if with_ctx_out:
            fz_c, qs_c, qn_c, ks_c, vs_c, kn_c, vn_c = _in_proj(xc, shc1, scc1, gn1, w_t, bd_t, cos_c, sin_c,
                                                                with_q=True, rope=False, tm=lc)
        else:
            ks_c, vs_c, kn_c, vn_c = _in_proj(xc, shc1, scc1, gn1, w_t[KS_LO:], bd_t, cos_c, sin_c,
                                              with_q=False, rope=False, tm=lc)

        yf = _fourier(fz, cn, sn, wf_t, tk=512)
        ys = _swa(qs, ks, vs, ks_c, vs_c, mask, sink_lat)
        yn = _na(qn, kn, vn, kn_c, vn_c, _na_bias(na_rpb[layer], s))
        x1, h2, route = _out_proj(x, yf, ys, yn, wo, g1, sh2, sc2, gn2, wr_hi, wr_lo, b_r, tm=tm)

        moe_w = (w_exp_gate, w_exp_up, w_exp_down, layer)
        if with_ctx_out:
            yf_c = _fourier(fz_c, cn_c, sn_c, wf_t, tk=lc)
            ys_c, yn_c = _ctx_attn(qs_c, ks_c, vs_c, qn_c, kn_c, vn_c, sink_ctx)
            xc1, hc2, route_c = _out_proj(xc, yf_c, ys_c, yn_c, wo, gc1, shc2, scc2, gn2, wr_hi, wr_lo, b_r, tm=lc)
            n_lat = b * s
            lat_chunks = n_lat // MOE_CHUNK
            plan = _moe_plan(jnp.concatenate([_route_chunks(route), _route_chunks(route_c)], axis=0))
            xs = _dispatch([h2.reshape(n_lat, d), hc2.reshape(b * lc, d)], plan)
            ye = _experts(xs, plan, *moe_w)
            x = _combine(x1, plan, g2, g_final[None, :], ye, c0=0, final_norm=False)
            ctx_chunks = b * lc // MOE_CHUNK
            xc = _combine(xc1.reshape(ctx_chunks, MOE_CHUNK, d), plan, gc2[:ctx_chunks], g_final[None, :], ye,
                          c0=lat_chunks, final_norm=False).reshape(b, lc, d)
        else:
            plan = _moe_plan(_route_chunks(route))
            xs = _dispatch([h2.reshape(b * s, d)], plan)
            ye = _experts(xs, plan, *moe_w)
            x = _combine(x1, plan, g2, g_final[None, :], ye, c0=0, final_norm=True)
    return x
```

```python
import functools

import jax
import jax.numpy as jnp
from jax import lax
from jax.experimental import pallas as pl
from jax.experimental.pallas import tpu as pltpu

F32 = jnp.float32
BF16 = jnp.bfloat16

D_MODEL = 1024
GRID_W = 64
HEAD_DIM = 64
FOURIER_WIDTH = D_MODEL // 4
FOURIER_GROUPS = 4
FOURIER_GROUP_DIM = FOURIER_WIDTH // FOURIER_GROUPS
SWA_HEADS = (3 * D_MODEL // 8) // HEAD_DIM
SWA_KV_HEADS = 2
SWA_GROUP = SWA_HEADS // SWA_KV_HEADS
SWA_WINDOW = 128
SWA_BLOCK = 128
NA_HEADS = (3 * D_MODEL // 8) // HEAD_DIM
NA_WIN_R = 8
NA_WIN_C = 16
ROPE_BASE = 10000.0
N_GROUPS = 4
EXPERTS_PER_GROUP = 8
N_EXPERTS = N_GROUPS * EXPERTS_PER_GROUP
D_EXPERT = D_MODEL // 2
EPS = 1e-6
NEG_INF = -1e30
LOG2E = 1.4426950408889634

SWA_Q_W = SWA_HEADS * HEAD_DIM
SWA_KV_W = SWA_KV_HEADS * HEAD_DIM
NA_W = NA_HEADS * HEAD_DIM
MIX_WIDTH = FOURIER_WIDTH + SWA_Q_W + NA_W
Q_COLS = MIX_WIDTH
IN_COLS = 2 * MIX_WIDTH

F_LO, F_HI = 0, FOURIER_WIDTH
QS_LO, QS_HI = F_HI, F_HI + SWA_Q_W
QN_LO, QN_HI = QS_HI, QS_HI + NA_W
KS_LO, KS_HI = QN_HI, QN_HI + SWA_KV_W
VS_LO, VS_HI = KS_HI, KS_HI + SWA_KV_W
KN_LO, KN_HI = VS_HI, VS_HI + NA_W
VN_LO, VN_HI = KN_HI, KN_HI + NA_W

LANE = 128
SUBLANE = 8
ROUTE_ROWS = -(-(N_GROUPS + N_EXPERTS) // SUBLANE) * SUBLANE
ROUTE_OUT = SUBLANE
NA_QROWS = 4
NA_KROWS = NA_QROWS + NA_WIN_R
ATTN_KCHUNK = LANE
ATTN_LOOKAHEAD = 2
PROJ_SUBTILE = LANE
MOE_TILE = 256
ROW_BLOCK = 8
MOE_CHUNK = 512
MOE_LROWS = 2 * MOE_CHUNK + (ROW_BLOCK - 1) * N_EXPERTS
VMEM_LIMIT = 48 * 1024 * 1024


def _cparams(*sem):
    return pltpu.CompilerParams(dimension_semantics=sem, vmem_limit_bytes=VMEM_LIMIT)


def _dot(a, b):
    return jnp.dot(a, b, preferred_element_type=F32)


def _dot_tn(a, b):
    return lax.dot_general(a, b, (((0,), (0,)), ((), ())), preferred_element_type=F32)


def _dot_nt(a, b):
    return lax.dot_general(a, b, (((1,), (1,)), ((), ())), preferred_element_type=F32)


def _split_dot(a, w):
    a_hi = a.astype(BF16)
    a_lo = (a - a_hi.astype(F32)).astype(BF16)
    w_hi = w.astype(BF16)
    w_lo = (w - w_hi.astype(F32)).astype(BF16)
    return _dot(a_hi, w_hi) + (_dot(a_hi, w_lo) + _dot(a_lo, w_hi))


def _mod_kernel(c_ref, w_ref, b_ref, o_ref):
    c = c_ref[...]
    a = c * (1.0 / (1.0 + jnp.exp(-c)))
    o_ref[0] = _split_dot(a, w_ref[0]) + b_ref[0]


def _modulation(c_rows, w_mod, b_mod):
    depth, d, n6 = w_mod.shape
    r = c_rows.shape[0]
    tn = 1536
    return pl.pallas_call(
        _mod_kernel,
        out_shape=jax.ShapeDtypeStruct((depth, r, n6), F32),
        grid=(depth, n6 // tn),
        in_specs=[
            pl.BlockSpec((r, d), lambda l, j: (0, 0)),
            pl.BlockSpec((1, d, tn), lambda l, j: (l, 0, j)),
            pl.BlockSpec((1, 1, tn), lambda l, j: (l, 0, j)),
        ],
        out_specs=pl.BlockSpec((1, r, tn), lambda l, j: (l, 0, j)),
        compiler_params=_cparams("parallel", "parallel"),
        name="modulation",
    )(c_rows, w_mod, b_mod.reshape(depth, 1, n6))


def _rope_rows(t, cos_t, sin_t, n_heads):
    outs = []
    for h in range(n_heads):
        th = t[HEAD_DIM * h:HEAD_DIM * (h + 1)]
        sw = jnp.concatenate([th[16:32], th[0:16], th[48:64], th[32:48]], axis=0)
        outs.append(th * cos_t + sw * sin_t)
    return jnp.concatenate(outs, axis=0)


def _in_proj_kernel(x_ref, sh_ref, sc_ref, g_ref, wt_ref, bdt_ref, cos_ref, sin_ref, *outs, with_q, rope):
    xf = x_ref[0]
    ms = jnp.mean(xf * xf, axis=-1, keepdims=True)
    y = xf * lax.rsqrt(ms + EPS) * g_ref[...]
    h = y * (1.0 + sc_ref[0]) + sh_ref[0]
    pt = _dot_nt(wt_ref[...], h.astype(BF16))
    q_scale = HEAD_DIM ** -0.5 * LOG2E
    if with_q:
        fz_ref, qs_ref, qn_ref, ks_ref, vs_ref, kn_ref, vn_ref = outs
        fz_ref[0] = _dot(bdt_ref[...], pt[F_LO:F_HI].astype(BF16)).astype(BF16)
        qs = pt[QS_LO:QS_HI]
        if rope:
            qs = _rope_rows(qs, cos_ref[...], sin_ref[...], SWA_HEADS)
        qs_ref[0] = (qs * q_scale).astype(BF16)
        qn_ref[0] = (pt[QN_LO:QN_HI] * q_scale).astype(BF16)
        off = 0
    else:
        ks_ref, vs_ref, kn_ref, vn_ref = outs
        off = KS_LO
    ks = pt[KS_LO - off:KS_HI - off]
    if rope:
        ks = _rope_rows(ks, cos_ref[...], sin_ref[...], SWA_KV_HEADS)
    ks_ref[0] = ks.astype(BF16)
    vs_ref[0] = pt[VS_LO - off:VS_HI - off].astype(BF16)
    kn_ref[0] = pt[KN_LO - off:KN_HI - off].astype(BF16)
    vn_ref[0] = pt[VN_LO - off:VN_HI - off].astype(BF16)


def _in_proj(x, shift, scale, gain, w_t, bd_t, cos_t, sin_t, *, with_q, rope, tm):
    b, t, d = x.shape
    nf = w_t.shape[0]
    rows = ([2 * FOURIER_WIDTH, SWA_Q_W, NA_W] if with_q else []) + [SWA_KV_W, SWA_KV_W, NA_W, NA_W]
    return pl.pallas_call(
        functools.partial(_in_proj_kernel, with_q=with_q, rope=rope),
        out_shape=[jax.ShapeDtypeStruct((b, r, t), BF16) for r in rows],
        grid=(b, t // tm),
        in_specs=[
            pl.BlockSpec((1, tm, d), lambda i, j: (i, j, 0)),
            pl.BlockSpec((1, 1, d), lambda i, j: (i, 0, 0)),
            pl.BlockSpec((1, 1, d), lambda i, j: (i, 0, 0)),
            pl.BlockSpec((1, d), lambda i, j: (0, 0)),
            pl.BlockSpec((nf, d), lambda i, j: (0, 0)),
            pl.BlockSpec(bd_t.shape, lambda i, j: (0, 0)),
            pl.BlockSpec((HEAD_DIM, tm), lambda i, j: (0, j)),
            pl.BlockSpec((HEAD_DIM, tm), lambda i, j: (0, j)),
        ],
        out_specs=[pl.BlockSpec((1, r, tm), lambda i, j: (i, 0, j)) for r in rows],
        compiler_params=_cparams("parallel", "parallel"),
        name="in_proj_q" if with_q else "in_proj_kv",
    )(x, shift, scale, gain, w_t, bd_t, cos_t, sin_t)


def _fourier_kernel(fz_ref, cn_ref, sn_ref, wft_ref, o_ref):
    zc = fz_ref[0, 0:FOURIER_WIDTH, :]
    zs = fz_ref[0, FOURIER_WIDTH:2 * FOURIER_WIDTH, :]
    y = _dot(zc, cn_ref[...]) - _dot(zs, sn_ref[...])
    o_ref[0] = _dot(wft_ref[...], y.astype(BF16)).astype(BF16)


def _fourier(fz, cn, sn, wf_t, *, tk):
    b, _, t = fz.shape
    return pl.pallas_call(
        _fourier_kernel,
        out_shape=jax.ShapeDtypeStruct((b, FOURIER_WIDTH, t), BF16),
        grid=(t // tk, b),
        in_specs=[
            pl.BlockSpec((1, 2 * FOURIER_WIDTH, t), lambda k, i: (i, 0, 0)),
            pl.BlockSpec((t, tk), lambda k, i: (0, k)),
            pl.BlockSpec((t, tk), lambda k, i: (0, k)),
            pl.BlockSpec((FOURIER_WIDTH, FOURIER_WIDTH), lambda k, i: (0, 0)),
        ],
        out_specs=pl.BlockSpec((1, FOURIER_WIDTH, tk), lambda k, i: (i, 0, k)),
        compiler_params=_cparams("parallel", "parallel"),
        name="fourier",
    )(fz, cn, sn, wf_t)


def _key_chunks(k, v, bias=None):
    n = k.shape[1] // ATTN_KCHUNK
    cut = lambda a, j, axis: lax.slice_in_dim(a, j * ATTN_KCHUNK, (j + 1) * ATTN_KCHUNK, axis=axis)
    return [(cut(k, j, 1), cut(v, j, 1), None if bias is None else cut(bias, j, 0)) for j in range(n)]


def _logits(q_t, chunks):
    return _dot_tn(jnp.concatenate([k_t for k_t, _, _ in chunks], axis=1), q_t)


def _softmax_pv(s, chunks, sink_row):
    pieces = []
    off = 0
    for k_t, _, bias in chunks:
        piece = s[off:off + k_t.shape[1]]
        pieces.append(piece if bias is None else piece + bias)
        off += k_t.shape[1]
    m = functools.reduce(jnp.maximum, [jnp.max(p, axis=0, keepdims=True) for p in pieces])
    if sink_row is not None:
        m = jnp.maximum(m, sink_row)
    probs = [jnp.exp2(p - m) for p in pieces]
    den = functools.reduce(jnp.add, [jnp.sum(p, axis=0, keepdims=True) for p in probs])
    if sink_row is not None:
        den = den + jnp.exp2(sink_row - m)
    v_all = jnp.concatenate([v_t for _, v_t, _ in chunks], axis=1)
    p_all = jnp.concatenate([p.astype(BF16) for p in probs], axis=0)
    return _dot(v_all, p_all) / den


def _attend(q_t, chunks, sink_row):
    return _softmax_pv(_logits(q_t, chunks), chunks, sink_row)


def _attend_blocks(n_blocks, make_block, sink_row, store):
    blocks, logits = {}, {}
    for j in range(min(ATTN_LOOKAHEAD, n_blocks)):
        blocks[j] = make_block(j)
        logits[j] = _logits(*blocks[j])
    for j in range(n_blocks):
        ahead = j + ATTN_LOOKAHEAD
        if ahead < n_blocks:
            blocks[ahead] = make_block(ahead)
            logits[ahead] = _logits(*blocks[ahead])
        store(j, _softmax_pv(logits.pop(j), blocks.pop(j)[1], sink_row))


def _swa_kernel(q_ref, k_ref, v_ref, kc_ref, vc_ref, mask_ref, sink_ref, o_ref, *, seq):
    nb = seq // SWA_BLOCK
    ctx_chunks = _key_chunks(kc_ref[0], vc_ref[0])

    def make_block(n):
        q0 = n * SWA_BLOCK
        q_t = jnp.concatenate(
            [q_ref[0, HEAD_DIM * h:HEAD_DIM * (h + 1), q0:q0 + SWA_BLOCK] for h in range(SWA_GROUP)], axis=1)
        chunks = [(k_ref[0, :, q0:q0 + SWA_BLOCK], v_ref[0, :, q0:q0 + SWA_BLOCK], None)] + ctx_chunks
        for side, kb in ((0, n - 1), (1, n + 1)):
            if 0 <= kb < nb:
                k0 = kb * SWA_BLOCK
                chunks.append((k_ref[0, :, k0:k0 + SWA_BLOCK], v_ref[0, :, k0:k0 + SWA_BLOCK], mask_ref[side]))
        return q_t, chunks

    def store(n, o):
        q0 = n * SWA_BLOCK
        for h in range(SWA_GROUP):
            o_ref[0, HEAD_DIM * h:HEAD_DIM * (h + 1), q0:q0 + SWA_BLOCK] = (
                o[:, SWA_BLOCK * h:SWA_BLOCK * (h + 1)].astype(BF16))

    _attend_blocks(nb, make_block, sink_ref[0], store)


def _swa(qs, ks, vs, kc, vc, mask, sink_rows):
    b, _, t = qs.shape
    lc = kc.shape[2]
    gw = SWA_GROUP * HEAD_DIM
    return pl.pallas_call(
        functools.partial(_swa_kernel, seq=t),
        out_shape=jax.ShapeDtypeStruct((b, SWA_Q_W, t), BF16),
        grid=(b, SWA_KV_HEADS),
        in_specs=[
            pl.BlockSpec((1, gw, t), lambda i, g: (i, g, 0)),
            pl.BlockSpec((1, HEAD_DIM, t), lambda i, g: (i, g, 0)),
            pl.BlockSpec((1, HEAD_DIM, t), lambda i, g: (i, g, 0)),
            pl.BlockSpec((1, HEAD_DIM, lc), lambda i, g: (i, g, 0)),
            pl.BlockSpec((1, HEAD_DIM, lc), lambda i, g: (i, g, 0)),
            pl.BlockSpec((2, SWA_BLOCK, SWA_GROUP * SWA_BLOCK), lambda i, g: (0, 0, 0)),
            pl.BlockSpec((1, 1, SWA_GROUP * SWA_BLOCK), lambda i, g: (g, 0, 0)),
        ],
        out_specs=pl.BlockSpec((1, gw, t), lambda i, g: (i, g, 0)),
        compiler_params=_cparams("parallel", "parallel"),
        name="swa",
    )(qs, ks, vs, kc, vc, mask, sink_rows)


def _na_bias_kernel(rpb_ref, o_ref, u_ref, *, total_rows):
    hd = pl.program_id(0)
    kc = lax.broadcasted_iota(jnp.int32, (GRID_W, LANE), 0)
    lane = lax.broadcasted_iota(jnp.int32, (GRID_W, LANE), 1)
    qc = lane % GRID_W
    dc = jnp.clip(kc - qc, -(NA_WIN_C - 1), NA_WIN_C - 1) + (NA_WIN_C - 1)
    c0 = jnp.clip(qc - NA_WIN_C // 2, 0, GRID_W - NA_WIN_C)
    valid_c = (kc >= c0) & (kc < c0 + NA_WIN_C)
    n_dr = 2 * NA_WIN_R - 1
    for dr in range(n_dr):
        u = jnp.full((GRID_W, LANE), NEG_INF, F32)
        for d in range(2 * NA_WIN_C - 1):
            u = jnp.where(valid_c & (dc == d), rpb_ref[hd, dr, d] * LOG2E, u)
        u_ref[dr] = u
    n_rows = o_ref.shape[2] // GRID_W
    block_types = [(0, 0), (NA_QROWS, 0), (total_rows - NA_QROWS, total_rows - NA_KROWS)]
    neg = jnp.full((GRID_W, LANE), NEG_INF, F32)
    for t, (r_base, k_base) in enumerate(block_types):
        for kl in range(n_rows):
            kr = k_base + kl
            for lg in range(NA_QROWS // 2):
                halves = []
                for rq in (2 * lg, 2 * lg + 1):
                    r = r_base + rq
                    r0 = min(max(r - NA_WIN_R // 2, 0), total_rows - NA_WIN_R)
                    ok = r0 <= kr < r0 + NA_WIN_R
                    halves.append(u_ref[kr - r + NA_WIN_R - 1] if ok else neg)
                o_ref[0, t, GRID_W * kl:GRID_W * (kl + 1), LANE * lg:LANE * (lg + 1)] = jnp.where(
                    lane < GRID_W, halves[0], halves[1])


def _na_bias(rpb, seq):
    nh = rpb.shape[0]
    return pl.pallas_call(
        functools.partial(_na_bias_kernel, total_rows=seq // GRID_W),
        out_shape=jax.ShapeDtypeStruct((nh, 3, NA_KROWS * GRID_W, NA_QROWS * GRID_W), F32),
        grid=(nh,),
        in_specs=[pl.BlockSpec(memory_space=pltpu.SMEM)],
        out_specs=pl.BlockSpec((1, 3, NA_KROWS * GRID_W, NA_QROWS * GRID_W), lambda h: (h, 0, 0, 0)),
        scratch_shapes=[pltpu.VMEM((2 * NA_WIN_R - 1, GRID_W, LANE), F32)],
        compiler_params=_cparams("parallel"),
        name="na_bias",
    )(rpb)


def _na_kernel(q_ref, k_ref, v_ref, kc_ref, vc_ref, bias_ref, o_ref, *, seq):
    n_rows = seq // GRID_W
    qw = NA_QROWS * GRID_W
    kw = NA_KROWS * GRID_W
    ctx_chunks = _key_chunks(kc_ref[0], vc_ref[0])
    nblk = n_rows // NA_QROWS
    rows_per_chunk = ATTN_KCHUNK // GRID_W

    def make_block(j):
        k_row = min(max(j * NA_QROWS - NA_WIN_R // 2, 0), n_rows - NA_KROWS)
        btype = 0 if j == 0 else (2 if j == nblk - 1 else 1)
        q0 = j * qw
        chunks = list(ctx_chunks)
        for cj in range(kw // ATTN_KCHUNK):
            first = k_row + cj * rows_per_chunk
            in_window = False
            for r in range(j * NA_QROWS, (j + 1) * NA_QROWS):
                r0 = min(max(r - NA_WIN_R // 2, 0), n_rows - NA_WIN_R)
                in_window = in_window or (first < r0 + NA_WIN_R and first + rows_per_chunk > r0)
            if in_window:
                k0 = first * GRID_W
                chunks.append((k_ref[0, :, k0:k0 + ATTN_KCHUNK], v_ref[0, :, k0:k0 + ATTN_KCHUNK],
                               bias_ref[0, btype, cj * ATTN_KCHUNK:(cj + 1) * ATTN_KCHUNK, :]))
        return q_ref[0, :, q0:q0 + qw], chunks

    def store(j, o):
        o_ref[0, :, j * qw:(j + 1) * qw] = o.astype(BF16)

    _attend_blocks(nblk, make_block, None, store)


def _na(qn, kn, vn, kc, vc, bias):
    b, _, t = qn.shape
    lc = kc.shape[2]
    head = lambda i, h: (i, h, 0)
    return pl.pallas_call(
        functools.partial(_na_kernel, seq=t),
        out_shape=jax.ShapeDtypeStruct((b, NA_W, t), BF16),
        grid=(b, NA_HEADS),
        in_specs=[
            pl.BlockSpec((1, HEAD_DIM, t), head),
            pl.BlockSpec((1, HEAD_DIM, t), head),
            pl.BlockSpec((1, HEAD_DIM, t), head),
            pl.BlockSpec((1, HEAD_DIM, lc), head),
            pl.BlockSpec((1, HEAD_DIM, lc), head),
            pl.BlockSpec((1,) + bias.shape[1:], lambda i, h: (h, 0, 0, 0)),
        ],
        out_specs=pl.BlockSpec((1, HEAD_DIM, t), head),
        compiler_params=_cparams("parallel", "parallel"),
        name="na",
    )(qn, kn, vn, kc, vc, bias)


def _ctx_attn_kernel(qs_ref, ks_ref, vs_ref, qn_ref, kn_ref, vn_ref, sink_ref, ys_ref, yn_ref):
    lc = qs_ref.shape[2]
    for g in range(SWA_KV_HEADS):
        q_t = jnp.concatenate(
            [qs_ref[0, HEAD_DIM * (SWA_GROUP * g + h):HEAD_DIM * (SWA_GROUP * g + h + 1), :] for h in range(SWA_GROUP)],
            axis=1)
        kv = slice(HEAD_DIM * g, HEAD_DIM * (g + 1))
        o = _attend(q_t, _key_chunks(ks_ref[0, kv, :], vs_ref[0, kv, :]), sink_ref[g])
        for h in range(SWA_GROUP):
            hh = SWA_GROUP * g + h
            ys_ref[0, HEAD_DIM * hh:HEAD_DIM * (hh + 1), :] = o[:, lc * h:lc * (h + 1)].astype(BF16)
    for h in range(NA_HEADS):
        sl = slice(HEAD_DIM * h, HEAD_DIM * (h + 1))
        o = _attend(qn_ref[0, sl, :], _key_chunks(kn_ref[0, sl, :], vn_ref[0, sl, :]), None)
        yn_ref[0, sl, :] = o.astype(BF16)


def _ctx_attn(qs, ks, vs, qn, kn, vn, sink_rows):
    b, _, lc = qs.shape
    full = lambda a: pl.BlockSpec((1,) + a.shape[1:], lambda i: (i, 0, 0))
    return pl.pallas_call(
        _ctx_attn_kernel,
        out_shape=[jax.ShapeDtypeStruct((b, SWA_Q_W, lc), BF16), jax.ShapeDtypeStruct((b, NA_W, lc), BF16)],
        grid=(b,),
        in_specs=[full(qs), full(ks), full(vs), full(qn), full(kn), full(vn),
                  pl.BlockSpec(sink_rows.shape, lambda i: (0, 0, 0))],
        out_specs=[pl.BlockSpec((1, SWA_Q_W, lc), lambda i: (i, 0, 0)),
                   pl.BlockSpec((1, NA_W, lc), lambda i: (i, 0, 0))],
        compiler_params=_cparams("parallel"),
        name="ctx_attn",
    )(qs, ks, vs, qn, kn, vn, sink_rows)


def _route(logits):
    row = lax.broadcasted_iota(jnp.int32, logits.shape, 0)
    big = jnp.int32(logits.shape[0])
    colmax = lambda a: jnp.max(a, axis=0, keepdims=True)
    first = lambda hit: jnp.min(jnp.where(hit, row, big), axis=0, keepdims=True)
    gmask = row < N_GROUPS
    gl = jnp.where(gmask, logits, NEG_INF)
    gmax = colmax(gl)
    g_sel = first(gl == gmax)
    p_g = 1.0 / jnp.sum(jnp.where(gmask, jnp.exp(logits - gmax), 0.0), axis=0, keepdims=True)
    lo = N_GROUPS + EXPERTS_PER_GROUP * g_sel
    el = jnp.where((row >= lo) & (row < lo + EXPERTS_PER_GROUP), logits, NEG_INF)
    v1 = colmax(el)
    i1 = first(el == v1)
    el2 = jnp.where(row == i1, NEG_INF, el)
    v2 = colmax(el2)
    i2 = first(el2 == v2)
    e21 = jnp.exp(v2 - v1)
    w1 = p_g / (1.0 + e21)
    w2 = p_g * e21 / (1.0 + e21)
    out_row = lax.broadcasted_iota(jnp.int32, (ROUTE_OUT, logits.shape[1]), 0)
    out = jnp.where(out_row == 0, (i1 - N_GROUPS).astype(F32), 0.0)
    out = jnp.where(out_row == 1, (i2 - N_GROUPS).astype(F32), out)
    out = jnp.where(out_row == 2, w1, out)
    return jnp.where(out_row == 3, w2, out)


def _out_proj_kernel(x_ref, yf_ref, ys_ref, yn_ref, wo_ref, g1_ref, sh_ref, sc_ref, gn_ref, wrh_ref, wrl_ref,
                     br_ref, x1_ref, h2_ref, rt_ref):
    toks = [slice(i, i + PROJ_SUBTILE) for i in range(0, x_ref.shape[1], PROJ_SUBTILE)]
    ys = [_dot_tn(jnp.concatenate([yf_ref[0, :, tok], ys_ref[0, :, tok], yn_ref[0, :, tok]], axis=0), wo_ref[...])
          for tok in toks]
    for tok, y in zip(toks, ys):
        x1 = x_ref[0, tok, :] + g1_ref[0] * y
        x1_ref[0, tok, :] = x1
        ms = jnp.mean(x1 * x1, axis=-1, keepdims=True)
        h2 = (x1 * lax.rsqrt(ms + EPS) * gn_ref[...]) * (1.0 + sc_ref[0]) + sh_ref[0]
        h_hi = h2.astype(BF16)
        h2_ref[0, tok, :] = h_hi
        h_lo = (h2 - h_hi.astype(F32)).astype(BF16)
        logits = (_dot_nt(wrh_ref[...], h_hi) + (_dot_nt(wrh_ref[...], h_lo) + _dot_nt(wrl_ref[...], h_hi))
                  + br_ref[...])
        rt_ref[0, :, tok] = _route(logits)


def _out_proj(x, yf, ys, yn, w_out, gate, shift, scale, gain, wr_hi, wr_lo, b_route, *, tm):
    b, t, d = x.shape
    vec = pl.BlockSpec((1, 1, d), lambda i, j: (i, 0, 0))
    feat = lambda r: pl.BlockSpec((1, r, tm), lambda i, j: (i, 0, j))
    tok = lambda w: pl.BlockSpec((1, tm, w), lambda i, j: (i, j, 0))
    whole = lambda a: pl.BlockSpec(a.shape, lambda i, j: (0, 0))
    return pl.pallas_call(
        _out_proj_kernel,
        out_shape=[jax.ShapeDtypeStruct((b, t, d), F32), jax.ShapeDtypeStruct((b, t, d), BF16),
                   jax.ShapeDtypeStruct((b, ROUTE_OUT, t), F32)],
        grid=(b, t // tm),
        in_specs=[tok(d), feat(FOURIER_WIDTH), feat(SWA_Q_W), feat(NA_W), whole(w_out),
                  vec, vec, vec, pl.BlockSpec((1, d), lambda i, j: (0, 0)),
                  whole(wr_hi), whole(wr_lo), whole(b_route)],
        out_specs=[tok(d), tok(d), feat(ROUTE_OUT)],
        compiler_params=_cparams("parallel", "parallel"),
        name="out_proj",
    )(x, yf, ys, yn, w_out, gate, shift, scale, gain, wr_hi, wr_lo, b_route)


def _start_or_wait(copy, wait):
    if wait:
        copy.wait()
    else:
        copy.start()


def _run_copies(local, remote, sem, runs_ref, c, *, to_remote, wait):
    def body(e, carry):
        rows = runs_ref[0, c, e]

        @pl.when(rows > 0)
        def _():
            n = pl.multiple_of(rows, ROW_BLOCK)
            loc = local.at[pl.ds(pl.multiple_of(runs_ref[1, c, e], ROW_BLOCK), n)]
            rem = remote.at[pl.ds(pl.multiple_of(runs_ref[2, c, e], ROW_BLOCK), n)]
            _start_or_wait(pltpu.make_async_copy(loc, rem, sem) if to_remote
                           else pltpu.make_async_copy(rem, loc, sem), wait)

        return carry

    lax.fori_loop(0, N_EXPERTS, body, 0)


def _tail_copies(zero, remote, sem, tstart_ref, tn_ref, *, wait):
    def body(e, carry):
        rows = tn_ref[e]

        @pl.when(rows > 0)
        def _():
            n = pl.multiple_of(rows, ROW_BLOCK)
            rem = remote.at[pl.ds(pl.multiple_of(tstart_ref[e], ROW_BLOCK), n)]
            _start_or_wait(pltpu.make_async_copy(zero.at[pl.ds(0, n)], rem, sem), wait)

        return carry

    lax.fori_loop(0, N_EXPERTS, body, 0)


def _unused_tile_copies(zero, remote, sem, nt_ref, total_tiles, *, wait):
    def body(t, carry):
        row = pl.multiple_of(t * MOE_TILE, MOE_TILE)
        cp = pltpu.make_async_copy(zero, remote.at[pl.ds(row, MOE_TILE)], sem)
        if wait:
            cp.wait()
        else:
            cp.start()
        return carry

    lax.fori_loop(nt_ref[0], total_tiles, body, 0)


def _experts_kernel(t0_ref, tcnt_ref, nt_ref, xs_ref, wg_ref, wu_ref, wd_ref, ys_ref,
                    xbuf, ybuf, zero, wg_s, wu_s, wd_s, xsem, ysem, zsem):
    e = pl.program_id(0)
    n = tcnt_ref[e]
    total_tiles = xs_ref.shape[0] // MOE_TILE

    def tile_rows(ref, t):
        return ref.at[pl.ds(pl.multiple_of((t0_ref[e] + t) * MOE_TILE, MOE_TILE), MOE_TILE)]

    x_copy = lambda t, slot: pltpu.make_async_copy(tile_rows(xs_ref, t), xbuf.at[slot], xsem.at[slot])
    y_copy = lambda t, slot: pltpu.make_async_copy(ybuf.at[slot], tile_rows(ys_ref, t), ysem.at[slot])

    last = e == N_EXPERTS - 1

    @pl.when(last)
    def _():
        zero[...] = jnp.zeros(zero.shape, F32)
        _unused_tile_copies(zero, ys_ref, zsem, nt_ref, total_tiles, wait=False)

    @pl.when(n > 0)
    def _():
        @pl.when(t0_ref[e] == 0)
        def _():
            x_copy(0, 0).start(priority=1)

        wg_s[...] = wg_ref[0, 0].astype(BF16)
        wu_s[...] = wu_ref[0, 0].astype(BF16)
        wd_s[...] = wd_ref[0, 0].astype(BF16)

        def body(t, carry):
            slot = t % 2
            x_copy(t, slot).wait()

            @pl.when(t + 1 < n)
            def _():
                x_copy(t + 1, 1 - slot).start(priority=1)

            @pl.when(t >= 2)
            def _():
                y_copy(t - 2, slot).wait()

            x = xbuf[slot].astype(BF16)
            g = _dot(x, wg_s[...])
            u = _dot(x, wu_s[...])
            hid = (g * (1.0 / (1.0 + jnp.exp(-g)))) * u
            ybuf[slot] = _dot(hid.astype(BF16), wd_s[...])
            y_copy(t, slot).start(priority=1)
            return carry

        lax.fori_loop(0, n, body, 0)

        @pl.when(t0_ref[e] + n < nt_ref[0])
        def _():
            x_copy(n, 0).start(priority=1)

        y_copy(n - 1, (n - 1) % 2).wait()

        @pl.when(n >= 2)
        def _():
            y_copy(n - 2, n % 2).wait()

    @pl.when(last)
    def _():
        _unused_tile_copies(zero, ys_ref, zsem, nt_ref, total_tiles, wait=True)


def _experts(xs, plan, w_gate, w_up, w_down, layer):
    r, d = xs.shape
    de = w_gate.shape[3]
    wspec = lambda a, b: pl.BlockSpec((1, 1, a, b), lambda e, *_: (layer, e, 0, 0))
    any_spec = pl.BlockSpec(memory_space=pl.ANY)
    return pl.pallas_call(
        _experts_kernel,
        out_shape=jax.ShapeDtypeStruct((r, d), F32),
        grid_spec=pltpu.PrefetchScalarGridSpec(
            num_scalar_prefetch=3,
            grid=(N_EXPERTS,),
            in_specs=[any_spec, wspec(d, de), wspec(d, de), wspec(de, d)],
            out_specs=any_spec,
            scratch_shapes=[pltpu.VMEM((2, MOE_TILE, d), F32), pltpu.VMEM((2, MOE_TILE, d), F32),
                            pltpu.VMEM((MOE_TILE, d), F32),
                            pltpu.VMEM((d, de), BF16), pltpu.VMEM((d, de), BF16), pltpu.VMEM((de, d), BF16),
                            pltpu.SemaphoreType.DMA((2,)), pltpu.SemaphoreType.DMA((2,)),
                            pltpu.SemaphoreType.DMA(())],
        ),
        compiler_params=_cparams("arbitrary"),
        name="experts",
    )(plan["tile0"], plan["tcnt"], plan["n_tiles"], xs, w_gate, w_up, w_down)


def _dispatch_kernel(runs_ref, tstart_ref, tn_ref, nt_ref, *rest, part_chunks):
    n_parts = len(part_chunks)
    h_refs = rest[:n_parts]
    pos_ref, xs_ref, buf, zero, sem, zsem = rest[n_parts:]
    n_chunks = sum(part_chunks)
    c = pl.program_id(0)
    slot = c % 2
    row = lax.broadcasted_iota(jnp.int32, (MOE_LROWS, MOE_CHUNK), 0)
    hit = (row == pos_ref[0, 0:1, :]) | (row == pos_ref[0, 1:2, :])
    sel = jnp.where(hit, 1.0, 0.0).astype(BF16)
    first = 0
    for h_ref, n in zip(h_refs, part_chunks):
        @pl.when((c >= first) & (c < first + n))
        def _(h_ref=h_ref):
            buf[slot] = _dot(sel, h_ref[...])
        first += n
    _run_copies(buf.at[slot], xs_ref, sem.at[slot], runs_ref, c, to_remote=True, wait=False)

    @pl.when(c > 0)
    def _():
        _run_copies(buf.at[1 - slot], xs_ref, sem.at[1 - slot], runs_ref, c - 1, to_remote=True, wait=True)

    @pl.when(c == n_chunks - 1)
    def _():
        total_tiles = xs_ref.shape[0] // MOE_TILE
        zero[...] = jnp.zeros(zero.shape, F32)
        _tail_copies(zero, xs_ref, zsem, tstart_ref, tn_ref, wait=False)
        _unused_tile_copies(zero, xs_ref, zsem, nt_ref, total_tiles, wait=False)
        _run_copies(buf.at[slot], xs_ref, sem.at[slot], runs_ref, c, to_remote=True, wait=True)
        _tail_copies(zero, xs_ref, zsem, tstart_ref, tn_ref, wait=True)
        _unused_tile_copies(zero, xs_ref, zsem, nt_ref, total_tiles, wait=True)


def _dispatch(h_parts, plan):
    d = h_parts[0].shape[1]
    part_chunks = tuple(h.shape[0] // MOE_CHUNK for h in h_parts)
    in_specs = []
    first = 0
    for n in part_chunks:
        in_specs.append(pl.BlockSpec(
            (MOE_CHUNK, d), lambda i, *_, first=first, n=n: (jnp.clip(i - first, 0, n - 1), 0)))
        first += n
    in_specs.append(pl.BlockSpec((1, 8, MOE_CHUNK), lambda i, *_: (i, 0, 0)))
    return pl.pallas_call(
        functools.partial(_dispatch_kernel, part_chunks=part_chunks),
        out_shape=jax.ShapeDtypeStruct((plan["rows"], d), F32),
        grid_spec=pltpu.PrefetchScalarGridSpec(
            num_scalar_prefetch=4,
            grid=(sum(part_chunks),),
            in_specs=in_specs,
            out_specs=pl.BlockSpec(memory_space=pl.ANY),
            scratch_shapes=[pltpu.VMEM((2, MOE_LROWS, d), F32), pltpu.VMEM((MOE_TILE, d), F32),
                            pltpu.SemaphoreType.DMA((2,)), pltpu.SemaphoreType.DMA(())],
        ),
        compiler_params=_cparams("arbitrary"),
        name="dispatch",
    )(plan["runs"], plan["tstart"], plan["tn"], plan["n_tiles"], *h_parts, plan["pos"])


def _combine_kernel(runs_ref, x1_ref, pos_ref, wt_ref, g2_ref, gf_ref, ys_ref, o_ref, buf, sem,
                    *, c0, n_chunks, final_norm):
    i = pl.program_id(0)
    c = i + c0
    slot = i % 2
    gather = functools.partial(_run_copies, remote=ys_ref, runs_ref=runs_ref, to_remote=False)

    @pl.when(i == 0)
    def _():
        buf[...] = jnp.zeros(buf.shape, F32)
        gather(buf.at[slot], sem=sem.at[slot], c=c, wait=False)

    @pl.when(i + 1 < n_chunks)
    def _():
        gather(buf.at[1 - slot], sem=sem.at[1 - slot], c=c + 1, wait=False)

    gather(buf.at[slot], sem=sem.at[slot], c=c, wait=True)
    row = lax.broadcasted_iota(jnp.int32, (MOE_LROWS, MOE_CHUNK), 0)
    hit1 = row == pos_ref[0, 0:1, :]
    hit2 = row == pos_ref[0, 1:2, :]
    w_row = jnp.sum(jnp.where(hit1, wt_ref[0, 0:1, :], 0.0) + jnp.where(hit2, wt_ref[0, 1:2, :], 0.0),
                    axis=1, keepdims=True)
    sel = jnp.where(hit1 | hit2, 1.0, 0.0).astype(BF16)
    x2 = x1_ref[0] + g2_ref[0] * _dot_tn(sel, (buf[slot] * w_row).astype(BF16))
    if final_norm:
        ms = jnp.mean(x2 * x2, axis=-1, keepdims=True)
        x2 = x2 * lax.rsqrt(ms + EPS) * gf_ref[...]
    o_ref[0] = x2


def _combine(x1, plan, gate, g_final, ys, *, c0, final_norm):
    b, t, d = x1.shape
    per_batch = t // MOE_CHUNK
    n_chunks = b * per_batch
    tok = lambda w: pl.BlockSpec((1, MOE_CHUNK, w), lambda i, *_: (i // per_batch, i % per_batch, 0))
    return pl.pallas_call(
        functools.partial(_combine_kernel, c0=c0, n_chunks=n_chunks, final_norm=final_norm),
        out_shape=jax.ShapeDtypeStruct((b, t, d), F32),
        grid_spec=pltpu.PrefetchScalarGridSpec(
            num_scalar_prefetch=1,
            grid=(n_chunks,),
            in_specs=[tok(d),
                      pl.BlockSpec((1, 8, MOE_CHUNK), lambda i, *_: (i + c0, 0, 0)),
                      pl.BlockSpec((1, 8, MOE_CHUNK), lambda i, *_: (i + c0, 0, 0)),
                      pl.BlockSpec((1, 1, d), lambda i, *_: (i // per_batch, 0, 0)),
                      pl.BlockSpec((1, d), lambda i, *_: (0, 0)),
                      pl.BlockSpec(memory_space=pl.ANY)],
            out_specs=tok(d),
            scratch_shapes=[pltpu.VMEM((2, MOE_LROWS, d), F32), pltpu.SemaphoreType.DMA((2,))],
        ),
        compiler_params=_cparams("arbitrary"),
        name="combine",
    )(plan["runs"], x1, plan["pos"], plan["wts"], gate, g_final, ys)


def _rope_tables(seq):
    half = HEAD_DIM // 4
    t = jnp.arange(seq)
    rows, cols = t // GRID_W, t % GRID_W
    freqs = ROPE_BASE ** (-jnp.arange(half, dtype=F32) / half)

    def cs(pos):
        ang = pos.astype(F32)[None, :] * freqs[:, None]
        return jnp.cos(ang), jnp.sin(ang)

    cr, sr = cs(rows)
    cc, sc = cs(cols)
    return jnp.concatenate([cr, cr, cc, cc], axis=0), jnp.concatenate([-sr, sr, -sc, sc], axis=0)


def _dft_tables(n):
    k = jnp.arange(n)

    def cs(m):
        ang = ((m[:, None] * k[None, :]) % n).astype(F32) * (2.0 * jnp.pi / n)
        return jnp.cos(ang), jnp.sin(ang)

    if n <= GRID_W:
        return cs(k)
    ca, sa = cs(jnp.arange(n // GRID_W) * GRID_W)
    cb, sb = cs(jnp.arange(GRID_W))
    c = ca[:, None, :] * cb[None, :, :] - sa[:, None, :] * sb[None, :, :]
    s = sa[:, None, :] * cb[None, :, :] + ca[:, None, :] * sb[None, :, :]
    return c.reshape(n, n), s.reshape(n, n)


def _channel_dft_t():
    c, s = _dft_tables(FOURIER_GROUP_DIM)
    eye = jnp.eye(FOURIER_GROUPS, dtype=F32)
    scale = FOURIER_GROUP_DIM ** -0.5
    return jnp.concatenate([jnp.kron(eye, c), jnp.kron(eye, s)], axis=0) * scale


def _position_dft(n):
    c, s = _dft_tables(n)
    scale = n ** -0.5
    return (c * scale).astype(BF16), (s * scale).astype(BF16)


def _swa_mask():
    kk = jnp.arange(SWA_BLOCK)[:, None]
    q = jnp.arange(SWA_BLOCK)[None, :]
    tiles = []
    for block_offset in (-SWA_BLOCK, SWA_BLOCK):
        ok = jnp.abs(block_offset + kk - q) <= SWA_WINDOW
        tiles.append(jnp.tile(jnp.where(ok, 0.0, NEG_INF).astype(F32), (1, SWA_GROUP)))
    return jnp.stack(tiles)


def _sink_rows(sink, width):
    return jnp.repeat(sink.astype(F32).reshape(SWA_KV_HEADS, SWA_GROUP) * LOG2E, width, axis=1)[:, None, :]


def _route_chunks(route):
    r = jnp.moveaxis(route, 1, 0).reshape(ROUTE_OUT, -1, MOE_CHUNK)
    return jnp.swapaxes(r, 0, 1)


def _moe_plan(route_chunks):
    i32 = jnp.int32
    nc = route_chunks.shape[0]
    n = nc * MOE_CHUNK
    ids = route_chunks[:, 0:2, :].astype(i32).reshape(nc, 2 * MOE_CHUNK)
    onehot = (ids[:, :, None] == jnp.arange(N_EXPERTS, dtype=i32)).astype(i32)
    pair = jnp.arange(2 * MOE_CHUNK, dtype=i32)
    earlier = (pair[None, :] < pair[:, None]).astype(BF16)
    before = jnp.einsum("pq,cqe->cpe", earlier, onehot.astype(BF16), preferred_element_type=F32).astype(i32)
    rank = jnp.sum(before * onehot, axis=-1)
    cnt = jnp.sum(onehot, axis=1)
    run = (cnt + ROW_BLOCK - 1) // ROW_BLOCK * ROW_BLOCK
    lo = jnp.cumsum(run, axis=1) - run
    pos = (jnp.sum(onehot * lo[:, None, :], axis=-1) + rank).reshape(nc, 2, MOE_CHUNK)
    pos = jnp.concatenate([pos, jnp.full((nc, 6, MOE_CHUNK), -1, i32)], axis=1)
    seg = jnp.sum(run, axis=0)
    padded = (seg + MOE_TILE - 1) // MOE_TILE * MOE_TILE
    start = jnp.cumsum(padded) - padded
    off = start[None, :] + jnp.cumsum(run, axis=0) - run
    wts = jnp.concatenate([route_chunks[:, 2:4, :], jnp.zeros((nc, 6, MOE_CHUNK), F32)], axis=1)
    rows = -(-(2 * n + (ROW_BLOCK - 1) * N_EXPERTS * nc + (MOE_TILE - 1) * N_EXPERTS) // MOE_TILE) * MOE_TILE
    n_tiles = jnp.sum(padded) // MOE_TILE
    return {"pos": pos, "wts": wts, "runs": jnp.stack([run, lo, off]).astype(i32),
            "tstart": (start + seg).astype(i32), "tn": (padded - seg).astype(i32),
            "tile0": (start // MOE_TILE).astype(i32), "tcnt": (padded // MOE_TILE).astype(i32),
            "n_tiles": n_tiles.astype(i32).reshape(1), "rows": rows}


def kernel(x, c, ctx, c_ctx, w_mod, b_mod, g_norm1, g_norm2, w_in, w_four, w_out, swa_sink, na_rpb,
           w_route_group, b_route_group, w_route_expert, b_route_expert, w_exp_gate, w_exp_up,
           w_exp_down, g_final):
    b, s, d = x.shape
    lc = ctx.shape[1]
    depth = w_mod.shape[0]
    tm = 512

    c_rows = jnp.concatenate([c, c_ctx[None, :], jnp.zeros((7, d), F32)], axis=0)
    mod = _modulation(c_rows, w_mod, b_mod)

    cos_t, sin_t = _rope_tables(s)
    cos_c, sin_c = cos_t[:, :lc], sin_t[:, :lc]
    bd_t = _channel_dft_t().astype(BF16)
    cn, sn = _position_dft(s)
    cn_c, sn_c = _position_dft(lc)
    mask = _swa_mask()
    route_pad = ROUTE_ROWS - N_GROUPS - N_EXPERTS

    xc = ctx
    for layer in range(depth):
        with_ctx_out = layer < depth - 1
        lat = [mod[layer, :b, i * d:(i + 1) * d][:, None, :] for i in range(6)]
        cx = [jnp.broadcast_to(mod[layer, b, i * d:(i + 1) * d][None, None, :], (b, 1, d)) for i in range(6)]
        sh1, sc1, g1, sh2, sc2, g2 = lat
        shc1, scc1, gc1, shc2, scc2, gc2 = cx
        gn1 = g_norm1[layer][None, :]
        gn2 = g_norm2[layer][None, :]
        w_t = w_in[layer].T.astype(BF16)
        wf_t = w_four[layer].T.astype(BF16)
        wo = w_out[layer].astype(BF16)
        w_r = jnp.concatenate([w_route_group[layer].T, w_route_expert[layer].T,
                               jnp.zeros((route_pad, d), F32)], axis=0)
        wr_hi = w_r.astype(BF16)
        wr_lo = (w_r - wr_hi.astype(F32)).astype(BF16)
        b_r = jnp.concatenate([b_route_group[layer], b_route_expert[layer], jnp.zeros((route_pad,), F32)])[:, None]
        sink_lat = _sink_rows(swa_sink[layer], SWA_BLOCK)
        sink_ctx = _sink_rows(swa_sink[layer], lc)

        fz, qs, qn, ks, vs, kn, vn = _in_proj(x, sh1, sc1, gn1, w_t, bd_t, cos_t, sin_t,
                                              with_q=True, rope=True, tm=tm)
        if with_ctx_out:
            fz_c, qs_c, qn_c, ks_c, vs_c, kn_c, vn_c = _in_proj(xc, shc1, scc1, gn1, w_t, bd_t, cos_c, sin_c,
                                                                with_q=True, rope=False, tm=lc)
        else:
            ks_c, vs_c, kn_c, vn_c = _in_proj(xc, shc1, scc1, gn1, w_t[KS_LO:], bd_t, cos_c, sin_c,
                                              with_q=False, rope=False, tm=lc)

        yf = _fourier(fz, cn, sn, wf_t, tk=512)
        ys = _swa(qs, ks, vs, ks_c, vs_c, mask, sink_lat)
        yn = _na(qn, kn, vn, kn_c, vn_c, _na_bias(na_rpb[layer], s))
        x1, h2, route = _out_proj(x, yf, ys, yn, wo, g1, sh2, sc2, gn2, wr_hi, wr_lo, b_r, tm=tm)

        moe_w = (w_exp_gate, w_exp_up, w_exp_down, layer)
        if with_ctx_out:
            yf_c = _fourier(fz_c, cn_c, sn_c, wf_t, tk=lc)
            ys_c, yn_c = _ctx_attn(qs_c, ks_c, vs_c, qn_c, kn_c, vn_c, sink_ctx)
            xc1, hc2, route_c = _out_proj(xc, yf_c, ys_c, yn_c, wo, gc1, shc2, scc2, gn2, wr_hi, wr_lo, b_r, tm=lc)
            n_lat = b * s
            lat_chunks = n_lat // MOE_CHUNK
            plan = _moe_plan(jnp.concatenate([_route_chunks(route), _route_chunks(route_c)], axis=0))
            xs = _dispatch([h2.reshape(n_lat, d), hc2.reshape(b * lc, d)], plan)
            ye = _experts(xs, plan, *moe_w)
            x = _combine(x1, plan, g2, g_final[None, :], ye, c0=0, final_norm=False)
            ctx_chunks = b * lc // MOE_CHUNK
            xc = _combine(xc1.reshape(ctx_chunks, MOE_CHUNK, d), plan, gc2[:ctx_chunks], g_final[None, :], ye,
                          c0=lat_chunks, final_norm=False).reshape(b, lc, d)
        else:
            plan = _moe_plan(_route_chunks(route))
            xs = _dispatch([h2.reshape(b * s, d)], plan)
            ye = _experts(xs, plan, *moe_w)
            x = _combine(x1, plan, g2, g_final[None, :], ye, c0=0, final_norm=True)
    return x
```

```python
import functools

import jax
import jax.numpy as jnp
from jax import lax
from jax.experimental import pallas as pl
from jax.experimental.pallas import tpu as pltpu

F32 = jnp.float32
BF16 = jnp.bfloat16

D_MODEL = 1024
GRID_W = 64
HEAD_DIM = 64
FOURIER_WIDTH = D_MODEL // 4
FOURIER_GROUPS = 4
FOURIER_GROUP_DIM = FOURIER_WIDTH // FOURIER_GROUPS
SWA_HEADS = (3 * D_MODEL // 8) // HEAD_DIM
SWA_KV_HEADS = 2
SWA_GROUP = SWA_HEADS // SWA_KV_HEADS
SWA_WINDOW = 128
SWA_BLOCK = 128
NA_HEADS = (3 * D_MODEL // 8) // HEAD_DIM
NA_WIN_R = 8
NA_WIN_C = 16
ROPE_BASE = 10000.0
N_GROUPS = 4
EXPERTS_PER_GROUP = 8
N_EXPERTS = N_GROUPS * EXPERTS_PER_GROUP
D_EXPERT = D_MODEL // 2
EPS = 1e-6
NEG_INF = -1e30
LOG2E = 1.4426950408889634

SWA_Q_W = SWA_HEADS * HEAD_DIM
SWA_KV_W = SWA_KV_HEADS * HEAD_DIM
NA_W = NA_HEADS * HEAD_DIM
MIX_WIDTH = FOURIER_WIDTH + SWA_Q_W + NA_W
Q_COLS = MIX_WIDTH
IN_COLS = 2 * MIX_WIDTH

F_LO, F_HI = 0, FOURIER_WIDTH
QS_LO, QS_HI = F_HI, F_HI + SWA_Q_W
QN_LO, QN_HI = QS_HI, QS_HI + NA_W
KS_LO, KS_HI = QN_HI, QN_HI + SWA_KV_W
VS_LO, VS_HI = KS_HI, KS_HI + SWA_KV_W
KN_LO, KN_HI = VS_HI, VS_HI + NA_W
VN_LO, VN_HI = KN_HI, KN_HI + NA_W

LANE = 128
SUBLANE = 8
ROUTE_ROWS = -(-(N_GROUPS + N_EXPERTS) // SUBLANE) * SUBLANE
ROUTE_OUT = SUBLANE
NA_QROWS = 4
NA_KROWS = NA_QROWS + NA_WIN_R
ATTN_KCHUNK = LANE
ATTN_LOOKAHEAD = 2
PROJ_SUBTILE = LANE
MOE_TILE = 512
ROW_BLOCK = 8
MOE_CHUNK = 512
MOE_LROWS = 2 * MOE_CHUNK + (ROW_BLOCK - 1) * N_EXPERTS
VMEM_LIMIT = 48 * 1024 * 1024


def _cparams(*sem):
    return pltpu.CompilerParams(dimension_semantics=sem, vmem_limit_bytes=VMEM_LIMIT)


def _dot(a, b):
    return jnp.dot(a, b, preferred_element_type=F32)


def _dot_tn(a, b):
    return lax.dot_general(a, b, (((0,), (0,)), ((), ())), preferred_element_type=F32)


def _dot_nt(a, b):
    return lax.dot_general(a, b, (((1,), (1,)), ((), ())), preferred_element_type=F32)


def _split_dot(a, w):
    a_hi = a.astype(BF16)
    a_lo = (a - a_hi.astype(F32)).astype(BF16)
    w_hi = w.astype(BF16)
    w_lo = (w - w_hi.astype(F32)).astype(BF16)
    return _dot(a_hi, w_hi) + (_dot(a_hi, w_lo) + _dot(a_lo, w_hi))


def _mod_kernel(c_ref, w_ref, b_ref, o_ref):
    c = c_ref[...]
    a = c * (1.0 / (1.0 + jnp.exp(-c)))
    o_ref[0] = _split_dot(a, w_ref[0]) + b_ref[0]


def _modulation(c_rows, w_mod, b_mod):
    depth, d, n6 = w_mod.shape
    r = c_rows.shape[0]
    tn = 1536
    return pl.pallas_call(
        _mod_kernel,
        out_shape=jax.ShapeDtypeStruct((depth, r, n6), F32),
        grid=(depth, n6 // tn),
        in_specs=[
            pl.BlockSpec((r, d), lambda l, j: (0, 0)),
            pl.BlockSpec((1, d, tn), lambda l, j: (l, 0, j)),
            pl.BlockSpec((1, 1, tn), lambda l, j: (l, 0, j)),
        ],
        out_specs=pl.BlockSpec((1, r, tn), lambda l, j: (l, 0, j)),
        compiler_params=_cparams("parallel", "parallel"),
        name="modulation",
    )(c_rows, w_mod, b_mod.reshape(depth, 1, n6))


def _rope_rows(t, cos_t, sin_t, n_heads):
    outs = []
    for h in range(n_heads):
        th = t[HEAD_DIM * h:HEAD_DIM * (h + 1)]
        sw = jnp.concatenate([th[16:32], th[0:16], th[48:64], th[32:48]], axis=0)
        outs.append(th * cos_t + sw * sin_t)
    return jnp.concatenate(outs, axis=0)


def _in_proj_kernel(x_ref, sh_ref, sc_ref, g_ref, wt_ref, bdt_ref, cos_ref, sin_ref, *outs, with_q, rope):
    xf = x_ref[0]
    ms = jnp.mean(xf * xf, axis=-1, keepdims=True)
    y = xf * lax.rsqrt(ms + EPS) * g_ref[...]
    h = y * (1.0 + sc_ref[0]) + sh_ref[0]
    pt = _dot_nt(wt_ref[...], h.astype(BF16))
    q_scale = HEAD_DIM ** -0.5 * LOG2E
    if with_q:
        fz_ref, qs_ref, qn_ref, ks_ref, vs_ref, kn_ref, vn_ref = outs
        fz_ref[0] = _dot(bdt_ref[...], pt[F_LO:F_HI].astype(BF16)).astype(BF16)
        qs = pt[QS_LO:QS_HI]
        if rope:
            qs = _rope_rows(qs, cos_ref[...], sin_ref[...], SWA_HEADS)
        qs_ref[0] = (qs * q_scale).astype(BF16)
        qn_ref[0] = (pt[QN_LO:QN_HI] * q_scale).astype(BF16)
        off = 0
    else:
        ks_ref, vs_ref, kn_ref, vn_ref = outs
        off = KS_LO
    ks = pt[KS_LO - off:KS_HI - off]
    if rope:
        ks = _rope_rows(ks, cos_ref[...], sin_ref[...], SWA_KV_HEADS)
    ks_ref[0] = ks.astype(BF16)
    vs_ref[0] = pt[VS_LO - off:VS_HI - off].astype(BF16)
    kn_ref[0] = pt[KN_LO - off:KN_HI - off].astype(BF16)
    vn_ref[0] = pt[VN_LO - off:VN_HI - off].astype(BF16)


def _in_proj(x, shift, scale, gain, w_t, bd_t, cos_t, sin_t, *, with_q, rope, tm):
    b, t, d = x.shape
    nf = w_t.shape[0]
    rows = ([2 * FOURIER_WIDTH, SWA_Q_W, NA_W] if with_q else []) + [SWA_KV_W, SWA_KV_W, NA_W, NA_W]
    return pl.pallas_call(
        functools.partial(_in_proj_kernel, with_q=with_q, rope=rope),
        out_shape=[jax.ShapeDtypeStruct((b, r, t), BF16) for r in rows],
        grid=(b, t // tm),
        in_specs=[
            pl.BlockSpec((1, tm, d), lambda i, j: (i, j, 0)),
            pl.BlockSpec((1, 1, d), lambda i, j: (i, 0, 0)),
            pl.BlockSpec((1, 1, d), lambda i, j: (i, 0, 0)),
            pl.BlockSpec((1, d), lambda i, j: (0, 0)),
            pl.BlockSpec((nf, d), lambda i, j: (0, 0)),
            pl.BlockSpec(bd_t.shape, lambda i, j: (0, 0)),
            pl.BlockSpec((HEAD_DIM, tm), lambda i, j: (0, j)),
            pl.BlockSpec((HEAD_DIM, tm), lambda i, j: (0, j)),
        ],
        out_specs=[pl.BlockSpec((1, r, tm), lambda i, j: (i, 0, j)) for r in rows],
        compiler_params=_cparams("parallel", "parallel"),
        name="in_proj_q" if with_q else "in_proj_kv",
    )(x, shift, scale, gain, w_t, bd_t, cos_t, sin_t)


def _fourier_kernel(fz_ref, cn_ref, sn_ref, wft_ref, o_ref):
    zc = fz_ref[0, 0:FOURIER_WIDTH, :]
    zs = fz_ref[0, FOURIER_WIDTH:2 * FOURIER_WIDTH, :]
    y = _dot(zc, cn_ref[...]) - _dot(zs, sn_ref[...])
    o_ref[0] = _dot(wft_ref[...], y.astype(BF16)).astype(BF16)


def _fourier(fz, cn, sn, wf_t, *, tk):
    b, _, t = fz.shape
    return pl.pallas_call(
        _fourier_kernel,
        out_shape=jax.ShapeDtypeStruct((b, FOURIER_WIDTH, t), BF16),
        grid=(t // tk, b),
        in_specs=[
            pl.BlockSpec((1, 2 * FOURIER_WIDTH, t), lambda k, i: (i, 0, 0)),
            pl.BlockSpec((t, tk), lambda k, i: (0, k)),
            pl.BlockSpec((t, tk), lambda k, i: (0, k)),
            pl.BlockSpec((FOURIER_WIDTH, FOURIER_WIDTH), lambda k, i: (0, 0)),
        ],
        out_specs=pl.BlockSpec((1, FOURIER_WIDTH, tk), lambda k, i: (i, 0, k)),
        compiler_params=_cparams("parallel", "parallel"),
        name="fourier",
    )(fz, cn, sn, wf_t)


def _key_chunks(k, v, bias=None):
    n = k.shape[1] // ATTN_KCHUNK
    cut = lambda a, j, axis: lax.slice_in_dim(a, j * ATTN_KCHUNK, (j + 1) * ATTN_KCHUNK, axis=axis)
    return [(cut(k, j, 1), cut(v, j, 1), None if bias is None else cut(bias, j, 0)) for j in range(n)]


def _logits(q_t, chunks):
    return _dot_tn(jnp.concatenate([k_t for k_t, _, _ in chunks], axis=1), q_t)


def _softmax_pv(s, chunks, sink_row):
    pieces = []
    off = 0
    for k_t, _, bias in chunks:
        piece = s[off:off + k_t.shape[1]]
        pieces.append(piece if bias is None else piece + bias)
        off += k_t.shape[1]
    m = functools.reduce(jnp.maximum, [jnp.max(p, axis=0, keepdims=True) for p in pieces])
    if sink_row is not None:
        m = jnp.maximum(m, sink_row)
    probs = [jnp.exp2(p - m) for p in pieces]
    den = functools.reduce(jnp.add, [jnp.sum(p, axis=0, keepdims=True) for p in probs])
    if sink_row is not None:
        den = den + jnp.exp2(sink_row - m)
    v_all = jnp.concatenate([v_t for _, v_t, _ in chunks], axis=1)
    p_all = jnp.concatenate([p.astype(BF16) for p in probs], axis=0)
    return _dot(v_all, p_all) / den


def _attend(q_t, chunks, sink_row):
    return _softmax_pv(_logits(q_t, chunks), chunks, sink_row)


def _attend_blocks(n_blocks, make_block, sink_row, store):
    blocks, logits = {}, {}
    for j in range(min(ATTN_LOOKAHEAD, n_blocks)):
        blocks[j] = make_block(j)
        logits[j] = _logits(*blocks[j])
    for j in range(n_blocks):
        ahead = j + ATTN_LOOKAHEAD
        if ahead < n_blocks:
            blocks[ahead] = make_block(ahead)
            logits[ahead] = _logits(*blocks[ahead])
        store(j, _softmax_pv(logits.pop(j), blocks.pop(j)[1], sink_row))


def _swa_kernel(q_ref, k_ref, v_ref, kc_ref, vc_ref, mask_ref, sink_ref, o_ref, *, seq):
    nb = seq // SWA_BLOCK
    ctx_chunks = _key_chunks(kc_ref[0], vc_ref[0])

    def make_block(n):
        q0 = n * SWA_BLOCK
        q_t = jnp.concatenate(
            [q_ref[0, HEAD_DIM * h:HEAD_DIM * (h + 1), q0:q0 + SWA_BLOCK] for h in range(SWA_GROUP)], axis=1)
        chunks = [(k_ref[0, :, q0:q0 + SWA_BLOCK], v_ref[0, :, q0:q0 + SWA_BLOCK], None)] + ctx_chunks
        for side, kb in ((0, n - 1), (1, n + 1)):
            if 0 <= kb < nb:
                k0 = kb * SWA_BLOCK
                chunks.append((k_ref[0, :, k0:k0 + SWA_BLOCK], v_ref[0, :, k0:k0 + SWA_BLOCK], mask_ref[side]))
        return q_t, chunks

    def store(n, o):
        q0 = n * SWA_BLOCK
        for h in range(SWA_GROUP):
            o_ref[0, HEAD_DIM * h:HEAD_DIM * (h + 1), q0:q0 + SWA_BLOCK] = (
                o[:, SWA_BLOCK * h:SWA_BLOCK * (h + 1)].astype(BF16))

    _attend_blocks(nb, make_block, sink_ref[0], store)


def _swa(qs, ks, vs, kc, vc, mask, sink_rows):
    b, _, t = qs.shape
    lc = kc.shape[2]
    gw = SWA_GROUP * HEAD_DIM
    return pl.pallas_call(
        functools.partial(_swa_kernel, seq=t),
        out_shape=jax.ShapeDtypeStruct((b, SWA_Q_W, t), BF16),
        grid=(b, SWA_KV_HEADS),
        in_specs=[
            pl.BlockSpec((1, gw, t), lambda i, g: (i, g, 0)),
            pl.BlockSpec((1, HEAD_DIM, t), lambda i, g: (i, g, 0)),
            pl.BlockSpec((1, HEAD_DIM, t), lambda i, g: (i, g, 0)),
            pl.BlockSpec((1, HEAD_DIM, lc), lambda i, g: (i, g, 0)),
            pl.BlockSpec((1, HEAD_DIM, lc), lambda i, g: (i, g, 0)),
            pl.BlockSpec((2, SWA_BLOCK, SWA_GROUP * SWA_BLOCK), lambda i, g: (0, 0, 0)),
            pl.BlockSpec((1, 1, SWA_GROUP * SWA_BLOCK), lambda i, g: (g, 0, 0)),
        ],
        out_specs=pl.BlockSpec((1, gw, t), lambda i, g: (i, g, 0)),
        compiler_params=_cparams("parallel", "parallel"),
        name="swa",
    )(qs, ks, vs, kc, vc, mask, sink_rows)


def _na_bias_kernel(rpb_ref, o_ref, u_ref, *, total_rows):
    hd = pl.program_id(0)
    kc = lax.broadcasted_iota(jnp.int32, (GRID_W, LANE), 0)
    lane = lax.broadcasted_iota(jnp.int32, (GRID_W, LANE), 1)
    qc = lane % GRID_W
    dc = jnp.clip(kc - qc, -(NA_WIN_C - 1), NA_WIN_C - 1) + (NA_WIN_C - 1)
    c0 = jnp.clip(qc - NA_WIN_C // 2, 0, GRID_W - NA_WIN_C)
    valid_c = (kc >= c0) & (kc < c0 + NA_WIN_C)
    n_dr = 2 * NA_WIN_R - 1
    for dr in range(n_dr):
        u = jnp.full((GRID_W, LANE), NEG_INF, F32)
        for d in range(2 * NA_WIN_C - 1):
            u = jnp.where(valid_c & (dc == d), rpb_ref[hd, dr, d] * LOG2E, u)
        u_ref[dr] = u
    n_rows = o_ref.shape[2] // GRID_W
    block_types = [(0, 0), (NA_QROWS, 0), (total_rows - NA_QROWS, total_rows - NA_KROWS)]
    neg = jnp.full((GRID_W, LANE), NEG_INF, F32)
    for t, (r_base, k_base) in enumerate(block_types):
        for kl in range(n_rows):
            kr = k_base + kl
            for lg in range(NA_QROWS // 2):
                halves = []
                for rq in (2 * lg, 2 * lg + 1):
                    r = r_base + rq
                    r0 = min(max(r - NA_WIN_R // 2, 0), total_rows - NA_WIN_R)
                    ok = r0 <= kr < r0 + NA_WIN_R
                    halves.append(u_ref[kr - r + NA_WIN_R - 1] if ok else neg)
                o_ref[0, t, GRID_W * kl:GRID_W * (kl + 1), LANE * lg:LANE * (lg + 1)] = jnp.where(
                    lane < GRID_W, halves[0], halves[1])


def _na_bias(rpb, seq):
    nh = rpb.shape[0]
    return pl.pallas_call(
        functools.partial(_na_bias_kernel, total_rows=seq // GRID_W),
        out_shape=jax.ShapeDtypeStruct((nh, 3, NA_KROWS * GRID_W, NA_QROWS * GRID_W), F32),
        grid=(nh,),
        in_specs=[pl.BlockSpec(memory_space=pltpu.SMEM)],
        out_specs=pl.BlockSpec((1, 3, NA_KROWS * GRID_W, NA_QROWS * GRID_W), lambda h: (h, 0, 0, 0)),
        scratch_shapes=[pltpu.VMEM((2 * NA_WIN_R - 1, GRID_W, LANE), F32)],
        compiler_params=_cparams("parallel"),
        name="na_bias",
    )(rpb)


def _na_kernel(q_ref, k_ref, v_ref, kc_ref, vc_ref, bias_ref, o_ref, *, seq):
    n_rows = seq // GRID_W
    qw = NA_QROWS * GRID_W
    kw = NA_KROWS * GRID_W
    ctx_chunks = _key_chunks(kc_ref[0], vc_ref[0])
    nblk = n_rows // NA_QROWS
    rows_per_chunk = ATTN_KCHUNK // GRID_W

    def make_block(j):
        k_row = min(max(j * NA_QROWS - NA_WIN_R // 2, 0), n_rows - NA_KROWS)
        btype = 0 if j == 0 else (2 if j == nblk - 1 else 1)
        q0 = j * qw
        chunks = list(ctx_chunks)
        for cj in range(kw // ATTN_KCHUNK):
            first = k_row + cj * rows_per_chunk
            in_window = False
            for r in range(j * NA_QROWS, (j + 1) * NA_QROWS):
                r0 = min(max(r - NA_WIN_R // 2, 0), n_rows - NA_WIN_R)
                in_window = in_window or (first < r0 + NA_WIN_R and first + rows_per_chunk > r0)
            if in_window:
                k0 = first * GRID_W
                chunks.append((k_ref[0, :, k0:k0 + ATTN_KCHUNK], v_ref[0, :, k0:k0 + ATTN_KCHUNK],
                               bias_ref[0, btype, cj * ATTN_KCHUNK:(cj + 1) * ATTN_KCHUNK, :]))
        return q_ref[0, :, q0:q0 + qw], chunks

    def store(j, o):
        o_ref[0, :, j * qw:(j + 1) * qw] = o.astype(BF16)

    _attend_blocks(nblk, make_block, None, store)


def _na(qn, kn, vn, kc, vc, bias):
    b, _, t = qn.shape
    lc = kc.shape[2]
    head = lambda i, h: (i, h, 0)
    return pl.pallas_call(
        functools.partial(_na_kernel, seq=t),
        out_shape=jax.ShapeDtypeStruct((b, NA_W, t), BF16),
        grid=(b, NA_HEADS),
        in_specs=[
            pl.BlockSpec((1, HEAD_DIM, t), head),
            pl.BlockSpec((1, HEAD_DIM, t), head),
            pl.BlockSpec((1, HEAD_DIM, t), head),
            pl.BlockSpec((1, HEAD_DIM, lc), head),
            pl.BlockSpec((1, HEAD_DIM, lc), head),
            pl.BlockSpec((1,) + bias.shape[1:], lambda i, h: (h, 0, 0, 0)),
        ],
        out_specs=pl.BlockSpec((1, HEAD_DIM, t), head),
        compiler_params=_cparams("parallel", "parallel"),
        name="na",
    )(qn, kn, vn, kc, vc, bias)


def _ctx_attn_kernel(qs_ref, ks_ref, vs_ref, qn_ref, kn_ref, vn_ref, sink_ref, ys_ref, yn_ref):
    lc = qs_ref.shape[2]
    for g in range(SWA_KV_HEADS):
        q_t = jnp.concatenate(
            [qs_ref[0, HEAD_DIM * (SWA_GROUP * g + h):HEAD_DIM * (SWA_GROUP * g + h + 1), :] for h in range(SWA_GROUP)],
            axis=1)
        kv = slice(HEAD_DIM * g, HEAD_DIM * (g + 1))
        o = _attend(q_t, _key_chunks(ks_ref[0, kv, :], vs_ref[0, kv, :]), sink_ref[g])
        for h in range(SWA_GROUP):
            hh = SWA_GROUP * g + h
            ys_ref[0, HEAD_DIM * hh:HEAD_DIM * (hh + 1), :] = o[:, lc * h:lc * (h + 1)].astype(BF16)
    for h in range(NA_HEADS):
        sl = slice(HEAD_DIM * h, HEAD_DIM * (h + 1))
        o = _attend(qn_ref[0, sl, :], _key_chunks(kn_ref[0, sl, :], vn_ref[0, sl, :]), None)
        yn_ref[0, sl, :] = o.astype(BF16)


def _ctx_attn(qs, ks, vs, qn, kn, vn, sink_rows):
    b, _, lc = qs.shape
    full = lambda a: pl.BlockSpec((1,) + a.shape[1:], lambda i: (i, 0, 0))
    return pl.pallas_call(
        _ctx_attn_kernel,
        out_shape=[jax.ShapeDtypeStruct((b, SWA_Q_W, lc), BF16), jax.ShapeDtypeStruct((b, NA_W, lc), BF16)],
        grid=(b,),
        in_specs=[full(qs), full(ks), full(vs), full(qn), full(kn), full(vn),
                  pl.BlockSpec(sink_rows.shape, lambda i: (0, 0, 0))],
        out_specs=[pl.BlockSpec((1, SWA_Q_W, lc), lambda i: (i, 0, 0)),
                   pl.BlockSpec((1, NA_W, lc), lambda i: (i, 0, 0))],
        compiler_params=_cparams("parallel"),
        name="ctx_attn",
    )(qs, ks, vs, qn, kn, vn, sink_rows)


def _route(logits):
    row = lax.broadcasted_iota(jnp.int32, logits.shape, 0)
    big = jnp.int32(logits.shape[0])
    colmax = lambda a: jnp.max(a, axis=0, keepdims=True)
    first = lambda hit: jnp.min(jnp.where(hit, row, big), axis=0, keepdims=True)
    gmask = row < N_GROUPS
    gl = jnp.where(gmask, logits, NEG_INF)
    gmax = colmax(gl)
    g_sel = first(gl == gmax)
    p_g = 1.0 / jnp.sum(jnp.where(gmask, jnp.exp(logits - gmax), 0.0), axis=0, keepdims=True)
    lo = N_GROUPS + EXPERTS_PER_GROUP * g_sel
    el = jnp.where((row >= lo) & (row < lo + EXPERTS_PER_GROUP), logits, NEG_INF)
    v1 = colmax(el)
    i1 = first(el == v1)
    el2 = jnp.where(row == i1, NEG_INF, el)
    v2 = colmax(el2)
    i2 = first(el2 == v2)
    e21 = jnp.exp(v2 - v1)
    w1 = p_g / (1.0 + e21)
    w2 = p_g * e21 / (1.0 + e21)
    out_row = lax.broadcasted_iota(jnp.int32, (ROUTE_OUT, logits.shape[1]), 0)
    out = jnp.where(out_row == 0, (i1 - N_GROUPS).astype(F32), 0.0)
    out = jnp.where(out_row == 1, (i2 - N_GROUPS).astype(F32), out)
    out = jnp.where(out_row == 2, w1, out)
    return jnp.where(out_row == 3, w2, out)


def _out_proj_kernel(x_ref, yf_ref, ys_ref, yn_ref, wo_ref, g1_ref, sh_ref, sc_ref, gn_ref, wrh_ref, wrl_ref,
                     br_ref, x1_ref, h2_ref, rt_ref):
    toks = [slice(i, i + PROJ_SUBTILE) for i in range(0, x_ref.shape[1], PROJ_SUBTILE)]
    ys = [_dot_tn(jnp.concatenate([yf_ref[0, :, tok], ys_ref[0, :, tok], yn_ref[0, :, tok]], axis=0), wo_ref[...])
          for tok in toks]
    for tok, y in zip(toks, ys):
        x1 = x_ref[0, tok, :] + g1_ref[0] * y
        x1_ref[0, tok, :] = x1
        ms = jnp.mean(x1 * x1, axis=-1, keepdims=True)
        h2 = (x1 * lax.rsqrt(ms + EPS) * gn_ref[...]) * (1.0 + sc_ref[0]) + sh_ref[0]
        h_hi = h2.astype(BF16)
        h2_ref[0, tok, :] = h_hi
        h_lo = (h2 - h_hi.astype(F32)).astype(BF16)
        logits = (_dot_nt(wrh_ref[...], h_hi) + (_dot_nt(wrh_ref[...], h_lo) + _dot_nt(wrl_ref[...], h_hi))
                  + br_ref[...])
        rt_ref[0, :, tok] = _route(logits)


def _out_proj(x, yf, ys, yn, w_out, gate, shift, scale, gain, wr_hi, wr_lo, b_route, *, tm):
    b, t, d = x.shape
    vec = pl.BlockSpec((1, 1, d), lambda i, j: (i, 0, 0))
    feat = lambda r: pl.BlockSpec((1, r, tm), lambda i, j: (i, 0, j))
    tok = lambda w: pl.BlockSpec((1, tm, w), lambda i, j: (i, j, 0))
    whole = lambda a: pl.BlockSpec(a.shape, lambda i, j: (0, 0))
    return pl.pallas_call(
        _out_proj_kernel,
        out_shape=[jax.ShapeDtypeStruct((b, t, d), F32), jax.ShapeDtypeStruct((b, t, d), BF16),
                   jax.ShapeDtypeStruct((b, ROUTE_OUT, t), F32)],
        grid=(b, t // tm),
        in_specs=[tok(d), feat(FOURIER_WIDTH), feat(SWA_Q_W), feat(NA_W), whole(w_out),
                  vec, vec, vec, pl.BlockSpec((1, d), lambda i, j: (0, 0)),
                  whole(wr_hi), whole(wr_lo), whole(b_route)],
        out_specs=[tok(d), tok(d), feat(ROUTE_OUT)],
        compiler_params=_cparams("parallel", "parallel"),
        name="out_proj",
    )(x, yf, ys, yn, w_out, gate, shift, scale, gain, wr_hi, wr_lo, b_route)


def _start_or_wait(copy, wait):
    if wait:
        copy.wait()
    else:
        copy.start()


def _run_copies(local, remote, sem, runs_ref, c, *, to_remote, wait):
    def body(e, carry):
        rows = runs_ref[0, c, e]

        @pl.when(rows > 0)
        def _():
            n = pl.multiple_of(rows, ROW_BLOCK)
            loc = local.at[pl.ds(pl.multiple_of(runs_ref[1, c, e], ROW_BLOCK), n)]
            rem = remote.at[pl.ds(pl.multiple_of(runs_ref[2, c, e], ROW_BLOCK), n)]
            _start_or_wait(pltpu.make_async_copy(loc, rem, sem) if to_remote
                           else pltpu.make_async_copy(rem, loc, sem), wait)

        return carry

    lax.fori_loop(0, N_EXPERTS, body, 0)


def _tail_copies(zero, remote, sem, tstart_ref, tn_ref, *, wait):
    def body(e, carry):
        rows = tn_ref[e]

        @pl.when(rows > 0)
        def _():
            n = pl.multiple_of(rows, ROW_BLOCK)
            rem = remote.at[pl.ds(pl.multiple_of(tstart_ref[e], ROW_BLOCK), n)]
            _start_or_wait(pltpu.make_async_copy(zero.at[pl.ds(0, n)], rem, sem), wait)

        return carry

    lax.fori_loop(0, N_EXPERTS, body, 0)


def _unused_tile_copies(zero, remote, sem, nt_ref, total_tiles, *, wait):
    def body(t, carry):
        row = pl.multiple_of(t * MOE_TILE, MOE_TILE)
        cp = pltpu.make_async_copy(zero, remote.at[pl.ds(row, MOE_TILE)], sem)
        if wait:
            cp.wait()
        else:
            cp.start()
        return carry

    lax.fori_loop(nt_ref[0], total_tiles, body, 0)


def _experts_kernel(t0_ref, tcnt_ref, nt_ref, xs_ref, wg_ref, wu_ref, wd_ref, ys_ref,
                    xbuf, ybuf, zero, wg_s, wu_s, wd_s, xsem, ysem, zsem):
    e = pl.program_id(0)
    n = tcnt_ref[e]
    total_tiles = xs_ref.shape[0] // MOE_TILE

    def tile_rows(ref, t):
        return ref.at[pl.ds(pl.multiple_of((t0_ref[e] + t) * MOE_TILE, MOE_TILE), MOE_TILE)]

    x_copy = lambda t, slot: pltpu.make_async_copy(tile_rows(xs_ref, t), xbuf.at[slot], xsem.at[slot])
    y_copy = lambda t, slot: pltpu.make_async_copy(ybuf.at[slot], tile_rows(ys_ref, t), ysem.at[slot])

    last = e == N_EXPERTS - 1

    @pl.when(last)
    def _():
        zero[...] = jnp.zeros(zero.shape, F32)
        _unused_tile_copies(zero, ys_ref, zsem, nt_ref, total_tiles, wait=False)

    @pl.when(n > 0)
    def _():
        @pl.when(t0_ref[e] == 0)
        def _():
            x_copy(0, 0).start(priority=1)

        wg_s[...] = wg_ref[0, 0].astype(BF16)
        wu_s[...] = wu_ref[0, 0].astype(BF16)
        wd_s[...] = wd_ref[0, 0].astype(BF16)

        def body(t, carry):
            slot = t % 2
            x_copy(t, slot).wait()

            @pl.when(t + 1 < n)
            def _():
                x_copy(t + 1, 1 - slot).start(priority=1)

            @pl.when(t >= 2)
            def _():
                y_copy(t - 2, slot).wait()

            x = xbuf[slot].astype(BF16)
            g = _dot(x, wg_s[...])
            u = _dot(x, wu_s[...])
            hid = (g * (1.0 / (1.0 + jnp.exp(-g)))) * u
            ybuf[slot] = _dot(hid.astype(BF16), wd_s[...])
            y_copy(t, slot).start(priority=1)
            return carry

        lax.fori_loop(0, n, body, 0)

        @pl.when(t0_ref[e] + n < nt_ref[0])
        def _():
            x_copy(n, 0).start(priority=1)

        y_copy(n - 1, (n - 1) % 2).wait()

        @pl.when(n >= 2)
        def _():
            y_copy(n - 2, n % 2).wait()

    @pl.when(last)
    def _():
        _unused_tile_copies(zero, ys_ref, zsem, nt_ref, total_tiles, wait=True)


def _experts(xs, plan, w_gate, w_up, w_down, layer):
    r, d = xs.shape
    de = w_gate.shape[3]
    wspec = lambda a, b: pl.BlockSpec((1, 1, a, b), lambda e, *_: (layer, e, 0, 0))
    any_spec = pl.BlockSpec(memory_space=pl.ANY)
    return pl.pallas_call(
        _experts_kernel,
        out_shape=jax.ShapeDtypeStruct((r, d), F32),
        grid_spec=pltpu.PrefetchScalarGridSpec(
            num_scalar_prefetch=3,
            grid=(N_EXPERTS,),
            in_specs=[any_spec, wspec(d, de), wspec(d, de), wspec(de, d)],
            out_specs=any_spec,
            scratch_shapes=[pltpu.VMEM((2, MOE_TILE, d), F32), pltpu.VMEM((2, MOE_TILE, d), F32),
                            pltpu.VMEM((MOE_TILE, d), F32),
                            pltpu.VMEM((d, de), BF16), pltpu.VMEM((d, de), BF16), pltpu.VMEM((de, d), BF16),
                            pltpu.SemaphoreType.DMA((2,)), pltpu.SemaphoreType.DMA((2,)),
                            pltpu.SemaphoreType.DMA(())],
        ),
        compiler_params=_cparams("arbitrary"),
        name="experts",
    )(plan["tile0"], plan["tcnt"], plan["n_tiles"], xs, w_gate, w_up, w_down)


def _dispatch_kernel(runs_ref, tstart_ref, tn_ref, nt_ref, *rest, part_chunks):
    n_parts = len(part_chunks)
    h_refs = rest[:n_parts]
    pos_ref, xs_ref, buf, zero, sem, zsem = rest[n_parts:]
    n_chunks = sum(part_chunks)
    c = pl.program_id(0)
    slot = c % 2
    row = lax.broadcasted_iota(jnp.int32, (MOE_LROWS, MOE_CHUNK), 0)
    hit = (row == pos_ref[0, 0:1, :]) | (row == pos_ref[0, 1:2, :])
    sel = jnp.where(hit, 1.0, 0.0).astype(BF16)
    first = 0
    for h_ref, n in zip(h_refs, part_chunks):
        @pl.when((c >= first) & (c < first + n))
        def _(h_ref=h_ref):
            buf[slot] = _dot(sel, h_ref[...])
        first += n
    _run_copies(buf.at[slot], xs_ref, sem.at[slot], runs_ref, c, to_remote=True, wait=False)

    @pl.when(c > 0)
    def _():
        _run_copies(buf.at[1 - slot], xs_ref, sem.at[1 - slot], runs_ref, c - 1, to_remote=True, wait=True)

    @pl.when(c == n_chunks - 1)
    def _():
        total_tiles = xs_ref.shape[0] // MOE_TILE
        zero[...] = jnp.zeros(zero.shape, F32)
        _tail_copies(zero, xs_ref, zsem, tstart_ref, tn_ref, wait=False)
        _unused_tile_copies(zero, xs_ref, zsem, nt_ref, total_tiles, wait=False)
        _run_copies(buf.at[slot], xs_ref, sem.at[slot], runs_ref, c, to_remote=True, wait=True)
        _tail_copies(zero, xs_ref, zsem, tstart_ref, tn_ref, wait=True)
        _unused_tile_copies(zero, xs_ref, zsem, nt_ref, total_tiles, wait=True)


def _dispatch(h_parts, plan):
    d = h_parts[0].shape[1]
    part_chunks = tuple(h.shape[0] // MOE_CHUNK for h in h_parts)
    in_specs = []
    first = 0
    for n in part_chunks:
        in_specs.append(pl.BlockSpec(
            (MOE_CHUNK, d), lambda i, *_, first=first, n=n: (jnp.clip(i - first, 0, n - 1), 0)))
        first += n
    in_specs.append(pl.BlockSpec((1, 8, MOE_CHUNK), lambda i, *_: (i, 0, 0)))
    return pl.pallas_call(
        functools.partial(_dispatch_kernel, part_chunks=part_chunks),
        out_shape=jax.ShapeDtypeStruct((plan["rows"], d), F32),
        grid_spec=pltpu.PrefetchScalarGridSpec(
            num_scalar_prefetch=4,
            grid=(sum(part_chunks),),
            in_specs=in_specs,
            out_specs=pl.BlockSpec(memory_space=pl.ANY),
            scratch_shapes=[pltpu.VMEM((2, MOE_LROWS, d), F32), pltpu.VMEM((MOE_TILE, d), F32),
                            pltpu.SemaphoreType.DMA((2,)), pltpu.SemaphoreType.DMA(())],
        ),
        compiler_params=_cparams("arbitrary"),
        name="dispatch",
    )(plan["runs"], plan["tstart"], plan["tn"], plan["n_tiles"], *h_parts, plan["pos"])


def _combine_kernel(runs_ref, x1_ref, pos_ref, wt_ref, g2_ref, gf_ref, ys_ref, o_ref, buf, sem,
                    *, c0, n_chunks, final_norm):
    i = pl.program_id(0)
    c = i + c0
    slot = i % 2
    gather = functools.partial(_run_copies, remote=ys_ref, runs_ref=runs_ref, to_remote=False)

    @pl.when(i == 0)
    def _():
        buf[...] = jnp.zeros(buf.shape, F32)
        gather(buf.at[slot], sem=sem.at[slot], c=c, wait=False)

    @pl.when(i + 1 < n_chunks)
    def _():
        gather(buf.at[1 - slot], sem=sem.at[1 - slot], c=c + 1, wait=False)

    gather(buf.at[slot], sem=sem.at[slot], c=c, wait=True)
    row = lax.broadcasted_iota(jnp.int32, (MOE_LROWS, MOE_CHUNK), 0)
    hit1 = row == pos_ref[0, 0:1, :]
    hit2 = row == pos_ref[0, 1:2, :]
    w_row = jnp.sum(jnp.where(hit1, wt_ref[0, 0:1, :], 0.0) + jnp.where(hit2, wt_ref[0, 1:2, :], 0.0),
                    axis=1, keepdims=True)
    sel = jnp.where(hit1 | hit2, 1.0, 0.0).astype(BF16)
    x2 = x1_ref[0] + g2_ref[0] * _dot_tn(sel, (buf[slot] * w_row).astype(BF16))
    if final_norm:
        ms = jnp.mean(x2 * x2, axis=-1, keepdims=True)
        x2 = x2 * lax.rsqrt(ms + EPS) * gf_ref[...]
    o_ref[0] = x2


def _combine(x1, plan, gate, g_final, ys, *, c0, final_norm):
    b, t, d = x1.shape
    per_batch = t // MOE_CHUNK
    n_chunks = b * per_batch
    tok = lambda w: pl.BlockSpec((1, MOE_CHUNK, w), lambda i, *_: (i // per_batch, i % per_batch, 0))
    return pl.pallas_call(
        functools.partial(_combine_kernel, c0=c0, n_chunks=n_chunks, final_norm=final_norm),
        out_shape=jax.ShapeDtypeStruct((b, t, d), F32),
        grid_spec=pltpu.PrefetchScalarGridSpec(
            num_scalar_prefetch=1,
            grid=(n_chunks,),
            in_specs=[tok(d),
                      pl.BlockSpec((1, 8, MOE_CHUNK), lambda i, *_: (i + c0, 0, 0)),
                      pl.BlockSpec((1, 8, MOE_CHUNK), lambda i, *_: (i + c0, 0, 0)),
                      pl.BlockSpec((1, 1, d), lambda i, *_: (i // per_batch, 0, 0)),
                      pl.BlockSpec((1, d), lambda i, *_: (0, 0)),
                      pl.BlockSpec(memory_space=pl.ANY)],
            out_specs=tok(d),
            scratch_shapes=[pltpu.VMEM((2, MOE_LROWS, d), F32), pltpu.SemaphoreType.DMA((2,))],
        ),
        compiler_params=_cparams("arbitrary"),
        name="combine",
    )(plan["runs"], x1, plan["pos"], plan["wts"], gate, g_final, ys)


def _rope_tables(seq):
    half = HEAD_DIM // 4
    t = jnp.arange(seq)
    rows, cols = t // GRID_W, t % GRID_W
    freqs = ROPE_BASE ** (-jnp.arange(half, dtype=F32) / half)

    def cs(pos):
        ang = pos.astype(F32)[None, :] * freqs[:, None]
        return jnp.cos(ang), jnp.sin(ang)

    cr, sr = cs(rows)
    cc, sc = cs(cols)
    return jnp.concatenate([cr, cr, cc, cc], axis=0), jnp.concatenate([-sr, sr, -sc, sc], axis=0)


def _dft_tables(n):
    k = jnp.arange(n)

    def cs(m):
        ang = ((m[:, None] * k[None, :]) % n).astype(F32) * (2.0 * jnp.pi / n)
        return jnp.cos(ang), jnp.sin(ang)

    if n <= GRID_W:
        return cs(k)
    ca, sa = cs(jnp.arange(n // GRID_W) * GRID_W)
    cb, sb = cs(jnp.arange(GRID_W))
    c = ca[:, None, :] * cb[None, :, :] - sa[:, None, :] * sb[None, :, :]
    s = sa[:, None, :] * cb[None, :, :] + ca[:, None, :] * sb[None, :, :]
    return c.reshape(n, n), s.reshape(n, n)


def _channel_dft_t():
    c, s = _dft_tables(FOURIER_GROUP_DIM)
    eye = jnp.eye(FOURIER_GROUPS, dtype=F32)
    scale = FOURIER_GROUP_DIM ** -0.5
    return jnp.concatenate([jnp.kron(eye, c), jnp.kron(eye, s)], axis=0) * scale


def _position_dft(n):
    c, s = _dft_tables(n)
    scale = n ** -0.5
    return (c * scale).astype(BF16), (s * scale).astype(BF16)


def _swa_mask():
    kk = jnp.arange(SWA_BLOCK)[:, None]
    q = jnp.arange(SWA_BLOCK)[None, :]
    tiles = []
    for block_offset in (-SWA_BLOCK, SWA_BLOCK):
        ok = jnp.abs(block_offset + kk - q) <= SWA_WINDOW
        tiles.append(jnp.tile(jnp.where(ok, 0.0, NEG_INF).astype(F32), (1, SWA_GROUP)))
    return jnp.stack(tiles)


def _sink_rows(sink, width):
    return jnp.repeat(sink.astype(F32).reshape(SWA_KV_HEADS, SWA_GROUP) * LOG2E, width, axis=1)[:, None, :]


def _route_chunks(route):
    r = jnp.moveaxis(route, 1, 0).reshape(ROUTE_OUT, -1, MOE_CHUNK)
    return jnp.swapaxes(r, 0, 1)


def _moe_plan(route_chunks):
    i32 = jnp.int32
    nc = route_chunks.shape[0]
    n = nc * MOE_CHUNK
    ids = route_chunks[:, 0:2, :].astype(i32).reshape(nc, 2 * MOE_CHUNK)
    onehot = (ids[:, :, None] == jnp.arange(N_EXPERTS, dtype=i32)).astype(i32)
    pair = jnp.arange(2 * MOE_CHUNK, dtype=i32)
    earlier = (pair[None, :] < pair[:, None]).astype(BF16)
    before = jnp.einsum("pq,cqe->cpe", earlier, onehot.astype(BF16), preferred_element_type=F32).astype(i32)
    rank = jnp.sum(before * onehot, axis=-1)
    cnt = jnp.sum(onehot, axis=1)
    run = (cnt + ROW_BLOCK - 1) // ROW_BLOCK * ROW_BLOCK
    lo = jnp.cumsum(run, axis=1) - run
    pos = (jnp.sum(onehot * lo[:, None, :], axis=-1) + rank).reshape(nc, 2, MOE_CHUNK)
    pos = jnp.concatenate([pos, jnp.full((nc, 6, MOE_CHUNK), -1, i32)], axis=1)
    seg = jnp.sum(run, axis=0)
    padded = (seg + MOE_TILE - 1) // MOE_TILE * MOE_TILE
    start = jnp.cumsum(padded) - padded
    off = start[None, :] + jnp.cumsum(run, axis=0) - run
    wts = jnp.concatenate([route_chunks[:, 2:4, :], jnp.zeros((nc, 6, MOE_CHUNK), F32)], axis=1)
    rows = -(-(2 * n + (ROW_BLOCK - 1) * N_EXPERTS * nc + (MOE_TILE - 1) * N_EXPERTS) // MOE_TILE) * MOE_TILE
    n_tiles = jnp.sum(padded) // MOE_TILE
    return {"pos": pos, "wts": wts, "runs": jnp.stack([run, lo, off]).astype(i32),
            "tstart": (start + seg).astype(i32), "tn": (padded - seg).astype(i32),
            "tile0": (start // MOE_TILE).astype(i32), "tcnt": (padded // MOE_TILE).astype(i32),
            "n_tiles": n_tiles.astype(i32).reshape(1), "rows": rows}


def kernel(x, c, ctx, c_ctx, w_mod, b_mod, g_norm1, g_norm2, w_in, w_four, w_out, swa_sink, na_rpb,
           w_route_group, b_route_group, w_route_expert, b_route_expert, w_exp_gate, w_exp_up,
           w_exp_down, g_final):
    b, s, d = x.shape
    lc = ctx.shape[1]
    depth = w_mod.shape[0]
    tm = 512

    c_rows = jnp.concatenate([c, c_ctx[None, :], jnp.zeros((7, d), F32)], axis=0)
    mod = _modulation(c_rows, w_mod, b_mod)

    cos_t, sin_t = _rope_tables(s)
    cos_c, sin_c = cos_t[:, :lc], sin_t[:, :lc]
    bd_t = _channel_dft_t().astype(BF16)
    cn, sn = _position_dft(s)
    cn_c, sn_c = _position_dft(lc)
    mask = _swa_mask()
    route_pad = ROUTE_ROWS - N_GROUPS - N_EXPERTS

    xc = ctx
    for layer in range(depth):
        with_ctx_out = layer < depth - 1
        lat = [mod[layer, :b, i * d:(i + 1) * d][:, None, :] for i in range(6)]
        cx = [jnp.broadcast_to(mod[layer, b, i * d:(i + 1) * d][None, None, :], (b, 1, d)) for i in range(6)]
        sh1, sc1, g1, sh2, sc2, g2 = lat
        shc1, scc1, gc1, shc2, scc2, gc2 = cx
        gn1 = g_norm1[layer][None, :]
        gn2 = g_norm2[layer][None, :]
        w_t = w_in[layer].T.astype(BF16)
        wf_t = w_four[layer].T.astype(BF16)
        wo = w_out[layer].astype(BF16)
        w_r = jnp.concatenate([w_route_group[layer].T, w_route_expert[layer].T,
                               jnp.zeros((route_pad, d), F32)], axis=0)
        wr_hi = w_r.astype(BF16)
        wr_lo = (w_r - wr_hi.astype(F32)).astype(BF16)
        b_r = jnp.concatenate([b_route_group[layer], b_route_expert[layer], jnp.zeros((route_pad,), F32)])[:, None]
        sink_lat = _sink_rows(swa_sink[layer], SWA_BLOCK)
        sink_ctx = _sink_rows(swa_sink[layer], lc)

        fz, qs, qn, ks, vs, kn, vn = _in_proj(x, sh1, sc1, gn1, w_t, bd_t, cos_t, sin_t,
                                              with_q=True, rope=True, tm=tm)
        if with_ctx_out:
            fz_c, qs_c, qn_c, ks_c, vs_c, kn_c, vn_c = _in_proj(xc, shc1, scc1, gn1, w_t, bd_t, cos_c, sin_c,
                                                                with_q=True, rope=False, tm=lc)
        else:
            ks_c, vs_c, kn_c, vn_c = _in_proj(xc, shc1, scc1, gn1, w_t[KS_LO:], bd_t, cos_c, sin_c,
                                              with_q=False, rope=False, tm=lc)

        yf = _fourier(fz, cn, sn, wf_t, tk=512)
        ys = _swa(qs, ks, vs, ks_c, vs_c, mask, sink_lat)
        yn = _na(qn, kn, vn, kn_c, vn_c, _na_bias(na_rpb[layer], s))
        x1, h2, route = _out_proj(x, yf, ys, yn, wo, g1, sh2, sc2, gn2, wr_hi, wr_lo, b_r, tm=tm)

        moe_w = (w_exp_gate, w_exp_up, w_exp_down, layer)
        if with_ctx_out:
            yf_c = _fourier(fz_c, cn_c, sn_c, wf_t, tk=lc)
            ys_c, yn_c = _ctx_attn(qs_c, ks_c, vs_c, qn_c, kn_c, vn_c, sink_ctx)
            xc1, hc2, route_c = _out_proj(xc, yf_c, ys_c, yn_c, wo, gc1, shc2, scc2, gn2, wr_hi, wr_lo, b_r, tm=lc)
            n_lat = b * s
            lat_chunks = n_lat // MOE_CHUNK
            plan = _moe_plan(jnp.concatenate([_route_chunks(route), _route_chunks(route_c)], axis=0))
            xs = _dispatch([h2.reshape(n_lat, d), hc2.reshape(b * lc, d)], plan)
            ye = _experts(xs, plan, *moe_w)
            x = _combine(x1, plan, g2, g_final[None, :], ye, c0=0, final_norm=False)
            ctx_chunks = b * lc // MOE_CHUNK
            xc = _combine(xc1.reshape(ctx_chunks, MOE_CHUNK, d), plan, gc2[:ctx_chunks], g_final[None, :], ye,
                          c0=lat_chunks, final_norm=False).reshape(b, lc, d)
        else:
            plan = _moe_plan(_route_chunks(route))
            xs = _dispatch([h2.reshape(b * s, d)], plan)
            ye = _experts(xs, plan, *moe_w)
            x = _combine(x1, plan, g2, g_final[None, :], ye, c0=0, final_norm=True)
    return x
```

```python
import functools

import jax
import jax.numpy as jnp
from jax import lax
from jax.experimental import pallas as pl
from jax.experimental.pallas import tpu as pltpu

F32 = jnp.float32
BF16 = jnp.bfloat16

D_MODEL = 1024
GRID_W = 64
HEAD_DIM = 64
FOURIER_WIDTH = D_MODEL // 4
FOURIER_GROUPS = 4
FOURIER_GROUP_DIM = FOURIER_WIDTH // FOURIER_GROUPS
SWA_HEADS = (3 * D_MODEL // 8) // HEAD_DIM
SWA_KV_HEADS = 2
SWA_GROUP = SWA_HEADS // SWA_KV_HEADS
SWA_WINDOW = 128
SWA_BLOCK = 128
NA_HEADS = (3 * D_MODEL // 8) // HEAD_DIM
NA_WIN_R = 8
NA_WIN_C = 16
ROPE_BASE = 10000.0
N_GROUPS = 4
EXPERTS_PER_GROUP = 8
N_EXPERTS = N_GROUPS * EXPERTS_PER_GROUP
D_EXPERT = D_MODEL // 2
EPS = 1e-6
NEG_INF = -1e30
LOG2E = 1.4426950408889634

SWA_Q_W = SWA_HEADS * HEAD_DIM
SWA_KV_W = SWA_KV_HEADS * HEAD_DIM
NA_W = NA_HEADS * HEAD_DIM
MIX_WIDTH = FOURIER_WIDTH + SWA_Q_W + NA_W
Q_COLS = MIX_WIDTH
IN_COLS = 2 * MIX_WIDTH

F_LO, F_HI = 0, FOURIER_WIDTH
QS_LO, QS_HI = F_HI, F_HI + SWA_Q_W
QN_LO, QN_HI = QS_HI, QS_HI + NA_W
KS_LO, KS_HI = QN_HI, QN_HI + SWA_KV_W
VS_LO, VS_HI = KS_HI, KS_HI + SWA_KV_W
KN_LO, KN_HI = VS_HI, VS_HI + NA_W
VN_LO, VN_HI = KN_HI, KN_HI + NA_W

LANE = 128
SUBLANE = 8
ROUTE_ROWS = -(-(N_GROUPS + N_EXPERTS) // SUBLANE) * SUBLANE
ROUTE_OUT = SUBLANE
NA_QROWS = 4
NA_KROWS = NA_QROWS + NA_WIN_R
ATTN_KCHUNK = LANE
ATTN_LOOKAHEAD = 2
PROJ_SUBTILE = LANE
MOE_TILE = 512
MOE_DT = BF16
ROW_BLOCK = 16
MOE_CHUNK = 512
MOE_LROWS = 2 * MOE_CHUNK + (ROW_BLOCK - 1) * N_EXPERTS
VMEM_LIMIT = 48 * 1024 * 1024


def _cparams(*sem):
    return pltpu.CompilerParams(dimension_semantics=sem, vmem_limit_bytes=VMEM_LIMIT)


def _dot(a, b):
    return jnp.dot(a, b, preferred_element_type=F32)


def _dot_tn(a, b):
    return lax.dot_general(a, b, (((0,), (0,)), ((), ())), preferred_element_type=F32)


def _dot_nt(a, b):
    return lax.dot_general(a, b, (((1,), (1,)), ((), ())), preferred_element_type=F32)


def _split_dot(a, w):
    a_hi = a.astype(BF16)
    a_lo = (a - a_hi.astype(F32)).astype(BF16)
    w_hi = w.astype(BF16)
    w_lo = (w - w_hi.astype(F32)).astype(BF16)
    return _dot(a_hi, w_hi) + (_dot(a_hi, w_lo) + _dot(a_lo, w_hi))


def _mod_kernel(c_ref, w_ref, b_ref, o_ref):
    c = c_ref[...]
    a = c * (1.0 / (1.0 + jnp.exp(-c)))
    o_ref[0] = _split_dot(a, w_ref[0]) + b_ref[0]


def _modulation(c_rows, w_mod, b_mod):
    depth, d, n6 = w_mod.shape
    r = c_rows.shape[0]
    tn = 1536
    return pl.pallas_call(
        _mod_kernel,
        out_shape=jax.ShapeDtypeStruct((depth, r, n6), F32),
        grid=(depth, n6 // tn),
        in_specs=[
            pl.BlockSpec((r, d), lambda l, j: (0, 0)),
            pl.BlockSpec((1, d, tn), lambda l, j: (l, 0, j)),
            pl.BlockSpec((1, 1, tn), lambda l, j: (l, 0, j)),
        ],
        out_specs=pl.BlockSpec((1, r, tn), lambda l, j: (l, 0, j)),
        compiler_params=_cparams("parallel", "parallel"),
        name="modulation",
    )(c_rows, w_mod, b_mod.reshape(depth, 1, n6))


def _rope_rows(t, cos_t, sin_t, n_heads):
    outs = []
    for h in range(n_heads):
        th = t[HEAD_DIM * h:HEAD_DIM * (h + 1)]
        sw = jnp.concatenate([th[16:32], th[0:16], th[48:64], th[32:48]], axis=0)
        outs.append(th * cos_t + sw * sin_t)
    return jnp.concatenate(outs, axis=0)


def _in_proj_kernel(x_ref, sh_ref, sc_ref, g_ref, wt_ref, bdt_ref, cos_ref, sin_ref, *outs, with_q, rope):
    xf = x_ref[0]
    ms = jnp.mean(xf * xf, axis=-1, keepdims=True)
    y = xf * lax.rsqrt(ms + EPS) * g_ref[...]
    h = y * (1.0 + sc_ref[0]) + sh_ref[0]
    pt = _dot_nt(wt_ref[...], h.astype(BF16))
    q_scale = HEAD_DIM ** -0.5 * LOG2E
    if with_q:
        fz_ref, qs_ref, qn_ref, ks_ref, vs_ref, kn_ref, vn_ref = outs
        fz_ref[0] = _dot(bdt_ref[...], pt[F_LO:F_HI].astype(BF16)).astype(BF16)
        qs = pt[QS_LO:QS_HI]
        if rope:
            qs = _rope_rows(qs, cos_ref[...], sin_ref[...], SWA_HEADS)
        qs_ref[0] = (qs * q_scale).astype(BF16)
        qn_ref[0] = (pt[QN_LO:QN_HI] * q_scale).astype(BF16)
        off = 0
    else:
        ks_ref, vs_ref, kn_ref, vn_ref = outs
        off = KS_LO
    ks = pt[KS_LO - off:KS_HI - off]
    if rope:
        ks = _rope_rows(ks, cos_ref[...], sin_ref[...], SWA_KV_HEADS)
    ks_ref[0] = ks.astype(BF16)
    vs_ref[0] = pt[VS_LO - off:VS_HI - off].astype(BF16)
    kn_ref[0] = pt[KN_LO - off:KN_HI - off].astype(BF16)
    vn_ref[0] = pt[VN_LO - off:VN_HI - off].astype(BF16)


def _in_proj(x, shift, scale, gain, w_t, bd_t, cos_t, sin_t, *, with_q, rope, tm):
    b, t, d = x.shape
    nf = w_t.shape[0]
    rows = ([2 * FOURIER_WIDTH, SWA_Q_W, NA_W] if with_q else []) + [SWA_KV_W, SWA_KV_W, NA_W, NA_W]
    return pl.pallas_call(
        functools.partial(_in_proj_kernel, with_q=with_q, rope=rope),
        out_shape=[jax.ShapeDtypeStruct((b, r, t), BF16) for r in rows],
        grid=(b, t // tm),
        in_specs=[
            pl.BlockSpec((1, tm, d), lambda i, j: (i, j, 0)),
            pl.BlockSpec((1, 1, d), lambda i, j: (i, 0, 0)),
            pl.BlockSpec((1, 1, d), lambda i, j: (i, 0, 0)),
            pl.BlockSpec((1, d), lambda i, j: (0, 0)),
            pl.BlockSpec((nf, d), lambda i, j: (0, 0)),
            pl.BlockSpec(bd_t.shape, lambda i, j: (0, 0)),
            pl.BlockSpec((HEAD_DIM, tm), lambda i, j: (0, j)),
            pl.BlockSpec((HEAD_DIM, tm), lambda i, j: (0, j)),
        ],
        out_specs=[pl.BlockSpec((1, r, tm), lambda i, j: (i, 0, j)) for r in rows],
        compiler_params=_cparams("parallel", "parallel"),
        name="in_proj_q" if with_q else "in_proj_kv",
    )(x, shift, scale, gain, w_t, bd_t, cos_t, sin_t)


def _fourier_kernel(fz_ref, cn_ref, sn_ref, wft_ref, o_ref):
    zc = fz_ref[0, 0:FOURIER_WIDTH, :]
    zs = fz_ref[0, FOURIER_WIDTH:2 * FOURIER_WIDTH, :]
    y = _dot(zc, cn_ref[...]) - _dot(zs, sn_ref[...])
    o_ref[0] = _dot(wft_ref[...], y.astype(BF16)).astype(BF16)


def _fourier(fz, cn, sn, wf_t, *, tk):
    b, _, t = fz.shape
    return pl.pallas_call(
        _fourier_kernel,
        out_shape=jax.ShapeDtypeStruct((b, FOURIER_WIDTH, t), BF16),
        grid=(t // tk, b),
        in_specs=[
            pl.BlockSpec((1, 2 * FOURIER_WIDTH, t), lambda k, i: (i, 0, 0)),
            pl.BlockSpec((t, tk), lambda k, i: (0, k)),
            pl.BlockSpec((t, tk), lambda k, i: (0, k)),
            pl.BlockSpec((FOURIER_WIDTH, FOURIER_WIDTH), lambda k, i: (0, 0)),
        ],
        out_specs=pl.BlockSpec((1, FOURIER_WIDTH, tk), lambda k, i: (i, 0, k)),
        compiler_params=_cparams("parallel", "parallel"),
        name="fourier",
    )(fz, cn, sn, wf_t)


def _key_chunks(k, v, bias=None):
    n = k.shape[1] // ATTN_KCHUNK
    cut = lambda a, j, axis: lax.slice_in_dim(a, j * ATTN_KCHUNK, (j + 1) * ATTN_KCHUNK, axis=axis)
    return [(cut(k, j, 1), cut(v, j, 1), None if bias is None else cut(bias, j, 0)) for j in range(n)]


def _logits(q_t, chunks):
    return _dot_tn(jnp.concatenate([k_t for k_t, _, _ in chunks], axis=1), q_t)


def _softmax_pv(s, chunks, sink_row):
    pieces = []
    off = 0
    for k_t, _, bias in chunks:
        piece = s[off:off + k_t.shape[1]]
        pieces.append(piece if bias is None else piece + bias)
        off += k_t.shape[1]
    m = functools.reduce(jnp.maximum, [jnp.max(p, axis=0, keepdims=True) for p in pieces])
    if sink_row is not None:
        m = jnp.maximum(m, sink_row)
    probs = [jnp.exp2(p - m) for p in pieces]
    den = functools.reduce(jnp.add, [jnp.sum(p, axis=0, keepdims=True) for p in probs])
    if sink_row is not None:
        den = den + jnp.exp2(sink_row - m)
    v_all = jnp.concatenate([v_t for _, v_t, _ in chunks], axis=1)
    p_all = jnp.concatenate([p.astype(BF16) for p in probs], axis=0)
    return _dot(v_all, p_all) / den


def _attend(q_t, chunks, sink_row):
    return _softmax_pv(_logits(q_t, chunks), chunks, sink_row)


def _attend_blocks(n_blocks, make_block, sink_row, store):
    blocks, logits = {}, {}
    for j in range(min(ATTN_LOOKAHEAD, n_blocks)):
        blocks[j] = make_block(j)
        logits[j] = _logits(*blocks[j])
    for j in range(n_blocks):
        ahead = j + ATTN_LOOKAHEAD
        if ahead < n_blocks:
            blocks[ahead] = make_block(ahead)
            logits[ahead] = _logits(*blocks[ahead])
        store(j, _softmax_pv(logits.pop(j), blocks.pop(j)[1], sink_row))


def _swa_kernel(q_ref, k_ref, v_ref, kc_ref, vc_ref, mask_ref, sink_ref, o_ref, *, seq):
    nb = seq // SWA_BLOCK
    ctx_chunks = _key_chunks(kc_ref[0], vc_ref[0])

    def make_block(n):
        q0 = n * SWA_BLOCK
        q_t = jnp.concatenate(
            [q_ref[0, HEAD_DIM * h:HEAD_DIM * (h + 1), q0:q0 + SWA_BLOCK] for h in range(SWA_GROUP)], axis=1)
        chunks = [(k_ref[0, :, q0:q0 + SWA_BLOCK], v_ref[0, :, q0:q0 + SWA_BLOCK], None)] + ctx_chunks
        for side, kb in ((0, n - 1), (1, n + 1)):
            if 0 <= kb < nb:
                k0 = kb * SWA_BLOCK
                chunks.append((k_ref[0, :, k0:k0 + SWA_BLOCK], v_ref[0, :, k0:k0 + SWA_BLOCK], mask_ref[side]))
        return q_t, chunks

    def store(n, o):
        q0 = n * SWA_BLOCK
        for h in range(SWA_GROUP):
            o_ref[0, HEAD_DIM * h:HEAD_DIM * (h + 1), q0:q0 + SWA_BLOCK] = (
                o[:, SWA_BLOCK * h:SWA_BLOCK * (h + 1)].astype(BF16))

    _attend_blocks(nb, make_block, sink_ref[0], store)


def _swa(qs, ks, vs, kc, vc, mask, sink_rows):
    b, _, t = qs.shape
    lc = kc.shape[2]
    gw = SWA_GROUP * HEAD_DIM
    return pl.pallas_call(
        functools.partial(_swa_kernel, seq=t),
        out_shape=jax.ShapeDtypeStruct((b, SWA_Q_W, t), BF16),
        grid=(b, SWA_KV_HEADS),
        in_specs=[
            pl.BlockSpec((1, gw, t), lambda i, g: (i, g, 0)),
            pl.BlockSpec((1, HEAD_DIM, t), lambda i, g: (i, g, 0)),
            pl.BlockSpec((1, HEAD_DIM, t), lambda i, g: (i, g, 0)),
            pl.BlockSpec((1, HEAD_DIM, lc), lambda i, g: (i, g, 0)),
            pl.BlockSpec((1, HEAD_DIM, lc), lambda i, g: (i, g, 0)),
            pl.BlockSpec((2, SWA_BLOCK, SWA_GROUP * SWA_BLOCK), lambda i, g: (0, 0, 0)),
            pl.BlockSpec((1, 1, SWA_GROUP * SWA_BLOCK), lambda i, g: (g, 0, 0)),
        ],
        out_specs=pl.BlockSpec((1, gw, t), lambda i, g: (i, g, 0)),
        compiler_params=_cparams("parallel", "parallel"),
        name="swa",
    )(qs, ks, vs, kc, vc, mask, sink_rows)


def _na_bias_kernel(rpb_ref, o_ref, u_ref, *, total_rows):
    hd = pl.program_id(0)
    kc = lax.broadcasted_iota(jnp.int32, (GRID_W, LANE), 0)
    lane = lax.broadcasted_iota(jnp.int32, (GRID_W, LANE), 1)
    qc = lane % GRID_W
    dc = jnp.clip(kc - qc, -(NA_WIN_C - 1), NA_WIN_C - 1) + (NA_WIN_C - 1)
    c0 = jnp.clip(qc - NA_WIN_C // 2, 0, GRID_W - NA_WIN_C)
    valid_c = (kc >= c0) & (kc < c0 + NA_WIN_C)
    n_dr = 2 * NA_WIN_R - 1
    for dr in range(n_dr):
        u = jnp.full((GRID_W, LANE), NEG_INF, F32)
        for d in range(2 * NA_WIN_C - 1):
            u = jnp.where(valid_c & (dc == d), rpb_ref[hd, dr, d] * LOG2E, u)
        u_ref[dr] = u
    n_rows = o_ref.shape[2] // GRID_W
    block_types = [(0, 0), (NA_QROWS, 0), (total_rows - NA_QROWS, total_rows - NA_KROWS)]
    neg = jnp.full((GRID_W, LANE), NEG_INF, F32)
    for t, (r_base, k_base) in enumerate(block_types):
        for kl in range(n_rows):
            kr = k_base + kl
            for lg in range(NA_QROWS // 2):
                halves = []
                for rq in (2 * lg, 2 * lg + 1):
                    r = r_base + rq
                    r0 = min(max(r - NA_WIN_R // 2, 0), total_rows - NA_WIN_R)
                    ok = r0 <= kr < r0 + NA_WIN_R
                    halves.append(u_ref[kr - r + NA_WIN_R - 1] if ok else neg)
                o_ref[0, t, GRID_W * kl:GRID_W * (kl + 1), LANE * lg:LANE * (lg + 1)] = jnp.where(
                    lane < GRID_W, halves[0], halves[1])


def _na_bias(rpb, seq):
    nh = rpb.shape[0]
    return pl.pallas_call(
        functools.partial(_na_bias_kernel, total_rows=seq // GRID_W),
        out_shape=jax.ShapeDtypeStruct((nh, 3, NA_KROWS * GRID_W, NA_QROWS * GRID_W), F32),
        grid=(nh,),
        in_specs=[pl.BlockSpec(memory_space=pltpu.SMEM)],
        out_specs=pl.BlockSpec((1, 3, NA_KROWS * GRID_W, NA_QROWS * GRID_W), lambda h: (h, 0, 0, 0)),
        scratch_shapes=[pltpu.VMEM((2 * NA_WIN_R - 1, GRID_W, LANE), F32)],
        compiler_params=_cparams("parallel"),
        name="na_bias",
    )(rpb)


def _na_kernel(q_ref, k_ref, v_ref, kc_ref, vc_ref, bias_ref, o_ref, *, seq):
    n_rows = seq // GRID_W
    qw = NA_QROWS * GRID_W
    kw = NA_KROWS * GRID_W
    ctx_chunks = _key_chunks(kc_ref[0], vc_ref[0])
    nblk = n_rows // NA_QROWS
    rows_per_chunk = ATTN_KCHUNK // GRID_W

    def make_block(j):
        k_row = min(max(j * NA_QROWS - NA_WIN_R // 2, 0), n_rows - NA_KROWS)
        btype = 0 if j == 0 else (2 if j == nblk - 1 else 1)
        q0 = j * qw
        chunks = list(ctx_chunks)
        for cj in range(kw // ATTN_KCHUNK):
            first = k_row + cj * rows_per_chunk
            in_window = False
            for r in range(j * NA_QROWS, (j + 1) * NA_QROWS):
                r0 = min(max(r - NA_WIN_R // 2, 0), n_rows - NA_WIN_R)
                in_window = in_window or (first < r0 + NA_WIN_R and first + rows_per_chunk > r0)
            if in_window:
                k0 = first * GRID_W
                chunks.append((k_ref[0, :, k0:k0 + ATTN_KCHUNK], v_ref[0, :, k0:k0 + ATTN_KCHUNK],
                               bias_ref[0, btype, cj * ATTN_KCHUNK:(cj + 1) * ATTN_KCHUNK, :]))
        return q_ref[0, :, q0:q0 + qw], chunks

    def store(j, o):
        o_ref[0, :, j * qw:(j + 1) * qw] = o.astype(BF16)

    _attend_blocks(nblk, make_block, None, store)


def _na(qn, kn, vn, kc, vc, bias):
    b, _, t = qn.shape
    lc = kc.shape[2]
    head = lambda i, h: (i, h, 0)
    return pl.pallas_call(
        functools.partial(_na_kernel, seq=t),
        out_shape=jax.ShapeDtypeStruct((b, NA_W, t), BF16),
        grid=(b, NA_HEADS),
        in_specs=[
            pl.BlockSpec((1, HEAD_DIM, t), head),
            pl.BlockSpec((1, HEAD_DIM, t), head),
            pl.BlockSpec((1, HEAD_DIM, t), head),
            pl.BlockSpec((1, HEAD_DIM, lc), head),
            pl.BlockSpec((1, HEAD_DIM, lc), head),
            pl.BlockSpec((1,) + bias.shape[1:], lambda i, h: (h, 0, 0, 0)),
        ],
        out_specs=pl.BlockSpec((1, HEAD_DIM, t), head),
        compiler_params=_cparams("parallel", "parallel"),
        name="na",
    )(qn, kn, vn, kc, vc, bias)


def _ctx_attn_kernel(qs_ref, ks_ref, vs_ref, qn_ref, kn_ref, vn_ref, sink_ref, ys_ref, yn_ref):
    lc = qs_ref.shape[2]
    for g in range(SWA_KV_HEADS):
        q_t = jnp.concatenate(
            [qs_ref[0, HEAD_DIM * (SWA_GROUP * g + h):HEAD_DIM * (SWA_GROUP * g + h + 1), :] for h in range(SWA_GROUP)],
            axis=1)
        kv = slice(HEAD_DIM * g, HEAD_DIM * (g + 1))
        o = _attend(q_t, _key_chunks(ks_ref[0, kv, :], vs_ref[0, kv, :]), sink_ref[g])
        for h in range(SWA_GROUP):
            hh = SWA_GROUP * g + h
            ys_ref[0, HEAD_DIM * hh:HEAD_DIM * (hh + 1), :] = o[:, lc * h:lc * (h + 1)].astype(BF16)
    for h in range(NA_HEADS):
        sl = slice(HEAD_DIM * h, HEAD_DIM * (h + 1))
        o = _attend(qn_ref[0, sl, :], _key_chunks(kn_ref[0, sl, :], vn_ref[0, sl, :]), None)
        yn_ref[0, sl, :] = o.astype(BF16)


def _ctx_attn(qs, ks, vs, qn, kn, vn, sink_rows):
    b, _, lc = qs.shape
    full = lambda a: pl.BlockSpec((1,) + a.shape[1:], lambda i: (i, 0, 0))
    return pl.pallas_call(
        _ctx_attn_kernel,
        out_shape=[jax.ShapeDtypeStruct((b, SWA_Q_W, lc), BF16), jax.ShapeDtypeStruct((b, NA_W, lc), BF16)],
        grid=(b,),
        in_specs=[full(qs), full(ks), full(vs), full(qn), full(kn), full(vn),
                  pl.BlockSpec(sink_rows.shape, lambda i: (0, 0, 0))],
        out_specs=[pl.BlockSpec((1, SWA_Q_W, lc), lambda i: (i, 0, 0)),
                   pl.BlockSpec((1, NA_W, lc), lambda i: (i, 0, 0))],
        compiler_params=_cparams("parallel"),
        name="ctx_attn",
    )(qs, ks, vs, qn, kn, vn, sink_rows)


def _route(logits):
    row = lax.broadcasted_iota(jnp.int32, logits.shape, 0)
    big = jnp.int32(logits.shape[0])
    colmax = lambda a: jnp.max(a, axis=0, keepdims=True)
    first = lambda hit: jnp.min(jnp.where(hit, row, big), axis=0, keepdims=True)
    gmask = row < N_GROUPS
    gl = jnp.where(gmask, logits, NEG_INF)
    gmax = colmax(gl)
    g_sel = first(gl == gmax)
    p_g = 1.0 / jnp.sum(jnp.where(gmask, jnp.exp(logits - gmax), 0.0), axis=0, keepdims=True)
    lo = N_GROUPS + EXPERTS_PER_GROUP * g_sel
    el = jnp.where((row >= lo) & (row < lo + EXPERTS_PER_GROUP), logits, NEG_INF)
    v1 = colmax(el)
    i1 = first(el == v1)
    el2 = jnp.where(row == i1, NEG_INF, el)
    v2 = colmax(el2)
    i2 = first(el2 == v2)
    e21 = jnp.exp(v2 - v1)
    w1 = p_g / (1.0 + e21)
    w2 = p_g * e21 / (1.0 + e21)
    out_row = lax.broadcasted_iota(jnp.int32, (ROUTE_OUT, logits.shape[1]), 0)
    out = jnp.where(out_row == 0, (i1 - N_GROUPS).astype(F32), 0.0)
    out = jnp.where(out_row == 1, (i2 - N_GROUPS).astype(F32), out)
    out = jnp.where(out_row == 2, w1, out)
    return jnp.where(out_row == 3, w2, out)


def _out_proj_kernel(x_ref, yf_ref, ys_ref, yn_ref, wo_ref, g1_ref, sh_ref, sc_ref, gn_ref, wrh_ref, wrl_ref,
                     br_ref, x1_ref, h2_ref, rt_ref):
    toks = [slice(i, i + PROJ_SUBTILE) for i in range(0, x_ref.shape[1], PROJ_SUBTILE)]
    ys = [_dot_tn(jnp.concatenate([yf_ref[0, :, tok], ys_ref[0, :, tok], yn_ref[0, :, tok]], axis=0), wo_ref[...])
          for tok in toks]
    for tok, y in zip(toks, ys):
        x1 = x_ref[0, tok, :] + g1_ref[0] * y
        x1_ref[0, tok, :] = x1
        ms = jnp.mean(x1 * x1, axis=-1, keepdims=True)
        h2 = (x1 * lax.rsqrt(ms + EPS) * gn_ref[...]) * (1.0 + sc_ref[0]) + sh_ref[0]
        h_hi = h2.astype(BF16)
        h2_ref[0, tok, :] = h_hi
        h_lo = (h2 - h_hi.astype(F32)).astype(BF16)
        logits = (_dot_nt(wrh_ref[...], h_hi) + (_dot_nt(wrh_ref[...], h_lo) + _dot_nt(wrl_ref[...], h_hi))
                  + br_ref[...])
        rt_ref[0, :, tok] = _route(logits)


def _out_proj(x, yf, ys, yn, w_out, gate, shift, scale, gain, wr_hi, wr_lo, b_route, *, tm):
    b, t, d = x.shape
    vec = pl.BlockSpec((1, 1, d), lambda i, j: (i, 0, 0))
    feat = lambda r: pl.BlockSpec((1, r, tm), lambda i, j: (i, 0, j))
    tok = lambda w: pl.BlockSpec((1, tm, w), lambda i, j: (i, j, 0))
    whole = lambda a: pl.BlockSpec(a.shape, lambda i, j: (0, 0))
    return pl.pallas_call(
        _out_proj_kernel,
        out_shape=[jax.ShapeDtypeStruct((b, t, d), F32), jax.ShapeDtypeStruct((b, t, d), BF16),
                   jax.ShapeDtypeStruct((b, ROUTE_OUT, t), F32)],
        grid=(b, t // tm),
        in_specs=[tok(d), feat(FOURIER_WIDTH), feat(SWA_Q_W), feat(NA_W), whole(w_out),
                  vec, vec, vec, pl.BlockSpec((1, d), lambda i, j: (0, 0)),
                  whole(wr_hi), whole(wr_lo), whole(b_route)],
        out_specs=[tok(d), tok(d), feat(ROUTE_OUT)],
        compiler_params=_cparams("parallel", "parallel"),
        name="out_proj",
    )(x, yf, ys, yn, w_out, gate, shift, scale, gain, wr_hi, wr_lo, b_route)


def _start_or_wait(copy, wait):
    if wait:
        copy.wait()
    else:
        copy.start()


def _run_copies(local, remote, sem, runs_ref, c, *, to_remote, wait):
    def body(e, carry):
        rows = runs_ref[0, c, e]

        @pl.when(rows > 0)
        def _():
            n = pl.multiple_of(rows, ROW_BLOCK)
            loc = local.at[pl.ds(pl.multiple_of(runs_ref[1, c, e], ROW_BLOCK), n)]
            rem = remote.at[pl.ds(pl.multiple_of(runs_ref[2, c, e], ROW_BLOCK), n)]
            _start_or_wait(pltpu.make_async_copy(loc, rem, sem) if to_remote
                           else pltpu.make_async_copy(rem, loc, sem), wait)

        return carry

    lax.fori_loop(0, N_EXPERTS, body, 0)


def _tail_copies(zero, remote, sem, tstart_ref, tn_ref, *, wait):
    def body(e, carry):
        rows = tn_ref[e]

        @pl.when(rows > 0)
        def _():
            n = pl.multiple_of(rows, ROW_BLOCK)
            rem = remote.at[pl.ds(pl.multiple_of(tstart_ref[e], ROW_BLOCK), n)]
            _start_or_wait(pltpu.make_async_copy(zero.at[pl.ds(0, n)], rem, sem), wait)

        return carry

    lax.fori_loop(0, N_EXPERTS, body, 0)


def _unused_tile_copies(zero, remote, sem, nt_ref, total_tiles, *, wait):
    def body(t, carry):
        row = pl.multiple_of(t * MOE_TILE, MOE_TILE)
        cp = pltpu.make_async_copy(zero, remote.at[pl.ds(row, MOE_TILE)], sem)
        if wait:
            cp.wait()
        else:
            cp.start()
        return carry

    lax.fori_loop(nt_ref[0], total_tiles, body, 0)


def _experts_kernel(t0_ref, tcnt_ref, nt_ref, xs_ref, wg_ref, wu_ref, wd_ref, ys_ref,
                    xbuf, ybuf, zero, wg_s, wu_s, wd_s, xsem, ysem, zsem):
    e = pl.program_id(0)
    n = tcnt_ref[e]
    total_tiles = xs_ref.shape[0] // MOE_TILE

    def tile_rows(ref, t):
        return ref.at[pl.ds(pl.multiple_of((t0_ref[e] + t) * MOE_TILE, MOE_TILE), MOE_TILE)]

    x_copy = lambda t, slot: pltpu.make_async_copy(tile_rows(xs_ref, t), xbuf.at[slot], xsem.at[slot])
    y_copy = lambda t, slot: pltpu.make_async_copy(ybuf.at[slot], tile_rows(ys_ref, t), ysem.at[slot])

    last = e == N_EXPERTS - 1

    @pl.when(last)
    def _():
        zero[...] = jnp.zeros(zero.shape, MOE_DT)
        _unused_tile_copies(zero, ys_ref, zsem, nt_ref, total_tiles, wait=False)

    @pl.when(n > 0)
    def _():
        @pl.when(t0_ref[e] == 0)
        def _():
            x_copy(0, 0).start(priority=1)

        wg_s[...] = wg_ref[0, 0].astype(BF16)
        wu_s[...] = wu_ref[0, 0].astype(BF16)
        wd_s[...] = wd_ref[0, 0].astype(BF16)

        def body(t, carry):
            slot = t % 2
            x_copy(t, slot).wait()

            @pl.when(t + 1 < n)
            def _():
                x_copy(t + 1, 1 - slot).start(priority=1)

            @pl.when(t >= 2)
            def _():
                y_copy(t - 2, slot).wait()

            x = xbuf[slot]
            g = _dot(x, wg_s[...])
            u = _dot(x, wu_s[...])
            hid = (g * (1.0 / (1.0 + jnp.exp(-g)))) * u
            ybuf[slot] = _dot(hid.astype(BF16), wd_s[...]).astype(MOE_DT)
            y_copy(t, slot).start(priority=1)
            return carry

        lax.fori_loop(0, n, body, 0)

        @pl.when(t0_ref[e] + n < nt_ref[0])
        def _():
            x_copy(n, 0).start(priority=1)

        y_copy(n - 1, (n - 1) % 2).wait()

        @pl.when(n >= 2)
        def _():
            y_copy(n - 2, n % 2).wait()

    @pl.when(last)
    def _():
        _unused_tile_copies(zero, ys_ref, zsem, nt_ref, total_tiles, wait=True)


def _experts(xs, plan, w_gate, w_up, w_down, layer):
    r, d = xs.shape
    de = w_gate.shape[3]
    wspec = lambda a, b: pl.BlockSpec((1, 1, a, b), lambda e, *_: (layer, e, 0, 0))
    any_spec = pl.BlockSpec(memory_space=pl.ANY)
    return pl.pallas_call(
        _experts_kernel,
        out_shape=jax.ShapeDtypeStruct((r, d), MOE_DT),
        grid_spec=pltpu.PrefetchScalarGridSpec(
            num_scalar_prefetch=3,
            grid=(N_EXPERTS,),
            in_specs=[any_spec, wspec(d, de), wspec(d, de), wspec(de, d)],
            out_specs=any_spec,
            scratch_shapes=[pltpu.VMEM((2, MOE_TILE, d), MOE_DT), pltpu.VMEM((2, MOE_TILE, d), MOE_DT),
                            pltpu.VMEM((MOE_TILE, d), MOE_DT),
                            pltpu.VMEM((d, de), BF16), pltpu.VMEM((d, de), BF16), pltpu.VMEM((de, d), BF16),
                            pltpu.SemaphoreType.DMA((2,)), pltpu.SemaphoreType.DMA((2,)),
                            pltpu.SemaphoreType.DMA(())],
        ),
        compiler_params=_cparams("arbitrary"),
        name="experts",
    )(plan["tile0"], plan["tcnt"], plan["n_tiles"], xs, w_gate, w_up, w_down)


def _dispatch_kernel(runs_ref, tstart_ref, tn_ref, nt_ref, *rest, part_chunks):
    n_parts = len(part_chunks)
    h_refs = rest[:n_parts]
    pos_ref, xs_ref, buf, zero, sem, zsem = rest[n_parts:]
    n_chunks = sum(part_chunks)
    c = pl.program_id(0)
    slot = c % 2
    row = lax.broadcasted_iota(jnp.int32, (MOE_LROWS, MOE_CHUNK), 0)
    hit = (row == pos_ref[0, 0:1, :]) | (row == pos_ref[0, 1:2, :])
    sel = jnp.where(hit, 1.0, 0.0).astype(BF16)
    first = 0
    for h_ref, n in zip(h_refs, part_chunks):
        @pl.when((c >= first) & (c < first + n))
        def _(h_ref=h_ref):
            buf[slot] = _dot(sel, h_ref[...]).astype(MOE_DT)
        first += n
    _run_copies(buf.at[slot], xs_ref, sem.at[slot], runs_ref, c, to_remote=True, wait=False)

    @pl.when(c > 0)
    def _():
        _run_copies(buf.at[1 - slot], xs_ref, sem.at[1 - slot], runs_ref, c - 1, to_remote=True, wait=True)

    @pl.when(c == n_chunks - 1)
    def _():
        total_tiles = xs_ref.shape[0] // MOE_TILE
        zero[...] = jnp.zeros(zero.shape, MOE_DT)
        _tail_copies(zero, xs_ref, zsem, tstart_ref, tn_ref, wait=False)
        _unused_tile_copies(zero, xs_ref, zsem, nt_ref, total_tiles, wait=False)
        _run_copies(buf.at[slot], xs_ref, sem.at[slot], runs_ref, c, to_remote=True, wait=True)
        _tail_copies(zero, xs_ref, zsem, tstart_ref, tn_ref, wait=True)
        _unused_tile_copies(zero, xs_ref, zsem, nt_ref, total_tiles, wait=True)


def _dispatch(h_parts, plan):
    d = h_parts[0].shape[1]
    part_chunks = tuple(h.shape[0] // MOE_CHUNK for h in h_parts)
    in_specs = []
    first = 0
    for n in part_chunks:
        in_specs.append(pl.BlockSpec(
            (MOE_CHUNK, d), lambda i, *_, first=first, n=n: (jnp.clip(i - first, 0, n - 1), 0)))
        first += n
    in_specs.append(pl.BlockSpec((1, 8, MOE_CHUNK), lambda i, *_: (i, 0, 0)))
    return pl.pallas_call(
        functools.partial(_dispatch_kernel, part_chunks=part_chunks),
        out_shape=jax.ShapeDtypeStruct((plan["rows"], d), MOE_DT),
        grid_spec=pltpu.PrefetchScalarGridSpec(
            num_scalar_prefetch=4,
            grid=(sum(part_chunks),),
            in_specs=in_specs,
            out_specs=pl.BlockSpec(memory_space=pl.ANY),
            scratch_shapes=[pltpu.VMEM((2, MOE_LROWS, d), MOE_DT), pltpu.VMEM((MOE_TILE, d), MOE_DT),
                            pltpu.SemaphoreType.DMA((2,)), pltpu.SemaphoreType.DMA(())],
        ),
        compiler_params=_cparams("arbitrary"),
        name="dispatch",
    )(plan["runs"], plan["tstart"], plan["tn"], plan["n_tiles"], *h_parts, plan["pos"])


def _combine_kernel(runs_ref, x1_ref, pos_ref, wt_ref, g2_ref, gf_ref, ys_ref, o_ref, buf, sem,
                    *, c0, n_chunks, final_norm):
    i = pl.program_id(0)
    c = i + c0
    slot = i % 2
    gather = functools.partial(_run_copies, remote=ys_ref, runs_ref=runs_ref, to_remote=False)

    @pl.when(i == 0)
    def _():
        buf[...] = jnp.zeros(buf.shape, MOE_DT)
        gather(buf.at[slot], sem=sem.at[slot], c=c, wait=False)

    @pl.when(i + 1 < n_chunks)
    def _():
        gather(buf.at[1 - slot], sem=sem.at[1 - slot], c=c + 1, wait=False)

    gather(buf.at[slot], sem=sem.at[slot], c=c, wait=True)
    row = lax.broadcasted_iota(jnp.int32, (MOE_LROWS, MOE_CHUNK), 0)
    hit1 = row == pos_ref[0, 0:1, :]
    hit2 = row == pos_ref[0, 1:2, :]
    w_row = jnp.sum(jnp.where(hit1, wt_ref[0, 0:1, :], 0.0) + jnp.where(hit2, wt_ref[0, 1:2, :], 0.0),
                    axis=1, keepdims=True)
    sel = jnp.where(hit1 | hit2, 1.0, 0.0).astype(BF16)
    x2 = x1_ref[0] + g2_ref[0] * _dot_tn(sel, (buf[slot] * w_row).astype(BF16))
    if final_norm:
        ms = jnp.mean(x2 * x2, axis=-1, keepdims=True)
        x2 = x2 * lax.rsqrt(ms + EPS) * gf_ref[...]
    o_ref[0] = x2


def _combine(x1, plan, gate, g_final, ys, *, c0, final_norm):
    b, t, d = x1.shape
    per_batch = t // MOE_CHUNK
    n_chunks = b * per_batch
    tok = lambda w: pl.BlockSpec((1, MOE_CHUNK, w), lambda i, *_: (i // per_batch, i % per_batch, 0))
    return pl.pallas_call(
        functools.partial(_combine_kernel, c0=c0, n_chunks=n_chunks, final_norm=final_norm),
        out_shape=jax.ShapeDtypeStruct((b, t, d), F32),
        grid_spec=pltpu.PrefetchScalarGridSpec(
            num_scalar_prefetch=1,
            grid=(n_chunks,),
            in_specs=[tok(d),
                      pl.BlockSpec((1, 8, MOE_CHUNK), lambda i, *_: (i + c0, 0, 0)),
                      pl.BlockSpec((1, 8, MOE_CHUNK), lambda i, *_: (i + c0, 0, 0)),
                      pl.BlockSpec((1, 1, d), lambda i, *_: (i // per_batch, 0, 0)),
                      pl.BlockSpec((1, d), lambda i, *_: (0, 0)),
                      pl.BlockSpec(memory_space=pl.ANY)],
            out_specs=tok(d),
            scratch_shapes=[pltpu.VMEM((2, MOE_LROWS, d), MOE_DT), pltpu.SemaphoreType.DMA((2,))],
        ),
        compiler_params=_cparams("arbitrary"),
        name="combine",
    )(plan["runs"], x1, plan["pos"], plan["wts"], gate, g_final, ys)


def _rope_tables(seq):
    half = HEAD_DIM // 4
    t = jnp.arange(seq)
    rows, cols = t // GRID_W, t % GRID_W
    freqs = ROPE_BASE ** (-jnp.arange(half, dtype=F32) / half)

    def cs(pos):
        ang = pos.astype(F32)[None, :] * freqs[:, None]
        return jnp.cos(ang), jnp.sin(ang)

    cr, sr = cs(rows)
    cc, sc = cs(cols)
    return jnp.concatenate([cr, cr, cc, cc], axis=0), jnp.concatenate([-sr, sr, -sc, sc], axis=0)


def _dft_tables(n):
    k = jnp.arange(n)

    def cs(m):
        ang = ((m[:, None] * k[None, :]) % n).astype(F32) * (2.0 * jnp.pi / n)
        return jnp.cos(ang), jnp.sin(ang)

    if n <= GRID_W:
        return cs(k)
    ca, sa = cs(jnp.arange(n // GRID_W) * GRID_W)
    cb, sb = cs(jnp.arange(GRID_W))
    c = ca[:, None, :] * cb[None, :, :] - sa[:, None, :] * sb[None, :, :]
    s = sa[:, None, :] * cb[None, :, :] + ca[:, None, :] * sb[None, :, :]
    return c.reshape(n, n), s.reshape(n, n)


def _channel_dft_t():
    c, s = _dft_tables(FOURIER_GROUP_DIM)
    eye = jnp.eye(FOURIER_GROUPS, dtype=F32)
    scale = FOURIER_GROUP_DIM ** -0.5
    return jnp.concatenate([jnp.kron(eye, c), jnp.kron(eye, s)], axis=0) * scale


def _position_dft(n):
    c, s = _dft_tables(n)
    scale = n ** -0.5
    return (c * scale).astype(BF16), (s * scale).astype(BF16)


def _swa_mask():
    kk = jnp.arange(SWA_BLOCK)[:, None]
    q = jnp.arange(SWA_BLOCK)[None, :]
    tiles = []
    for block_offset in (-SWA_BLOCK, SWA_BLOCK):
        ok = jnp.abs(block_offset + kk - q) <= SWA_WINDOW
        tiles.append(jnp.tile(jnp.where(ok, 0.0, NEG_INF).astype(F32), (1, SWA_GROUP)))
    return jnp.stack(tiles)


def _sink_rows(sink, width):
    return jnp.repeat(sink.astype(F32).reshape(SWA_KV_HEADS, SWA_GROUP) * LOG2E, width, axis=1)[:, None, :]


def _route_chunks(route):
    r = jnp.moveaxis(route, 1, 0).reshape(ROUTE_OUT, -1, MOE_CHUNK)
    return jnp.swapaxes(r, 0, 1)


def _moe_plan(route_chunks):
    i32 = jnp.int32
    nc = route_chunks.shape[0]
    n = nc * MOE_CHUNK
    ids = route_chunks[:, 0:2, :].astype(i32).reshape(nc, 2 * MOE_CHUNK)
    onehot = (ids[:, :, None] == jnp.arange(N_EXPERTS, dtype=i32)).astype(i32)
    pair = jnp.arange(2 * MOE_CHUNK, dtype=i32)
    earlier = (pair[None, :] < pair[:, None]).astype(BF16)
    before = jnp.einsum("pq,cqe->cpe", earlier, onehot.astype(BF16), preferred_element_type=F32).astype(i32)
    rank = jnp.sum(before * onehot, axis=-1)
    cnt = jnp.sum(onehot, axis=1)
    run = (cnt + ROW_BLOCK - 1) // ROW_BLOCK * ROW_BLOCK
    lo = jnp.cumsum(run, axis=1) - run
    pos = (jnp.sum(onehot * lo[:, None, :], axis=-1) + rank).reshape(nc, 2, MOE_CHUNK)
    pos = jnp.concatenate([pos, jnp.full((nc, 6, MOE_CHUNK), -1, i32)], axis=1)
    seg = jnp.sum(run, axis=0)
    padded = (seg + MOE_TILE - 1) // MOE_TILE * MOE_TILE
    start = jnp.cumsum(padded) - padded
    off = start[None, :] + jnp.cumsum(run, axis=0) - run
    wts = jnp.concatenate([route_chunks[:, 2:4, :], jnp.zeros((nc, 6, MOE_CHUNK), F32)], axis=1)
    rows = -(-(2 * n + (ROW_BLOCK - 1) * N_EXPERTS * nc + (MOE_TILE - 1) * N_EXPERTS) // MOE_TILE) * MOE_TILE
    n_tiles = jnp.sum(padded) // MOE_TILE
    return {"pos": pos, "wts": wts, "runs": jnp.stack([run, lo, off]).astype(i32),
            "tstart": (start + seg).astype(i32), "tn": (padded - seg).astype(i32),
            "tile0": (start // MOE_TILE).astype(i32), "tcnt": (padded // MOE_TILE).astype(i32),
            "n_tiles": n_tiles.astype(i32).reshape(1), "rows": rows}


def kernel(x, c, ctx, c_ctx, w_mod, b_mod, g_norm1, g_norm2, w_in, w_four, w_out, swa_sink, na_rpb,
           w_route_group, b_route_group, w_route_expert, b_route_expert, w_exp_gate, w_exp_up,
           w_exp_down, g_final):
    b, s, d = x.shape
    lc = ctx.shape[1]
    depth = w_mod.shape[0]
    tm = 512

    c_rows = jnp.concatenate([c, c_ctx[None, :], jnp.zeros((7, d), F32)], axis=0)
    mod = _modulation(c_rows, w_mod, b_mod)

    cos_t, sin_t = _rope_tables(s)
    cos_c, sin_c = cos_t[:, :lc], sin_t[:, :lc]
    bd_t = _channel_dft_t().astype(BF16)
    cn, sn = _position_dft(s)
    cn_c, sn_c = _position_dft(lc)
    mask = _swa_mask()
    route_pad = ROUTE_ROWS - N_GROUPS - N_EXPERTS

    xc = ctx
    for layer in range(depth):
        with_ctx_out = layer < depth - 1
        lat = [mod[layer, :b, i * d:(i + 1) * d][:, None, :] for i in range(6)]
        cx = [jnp.broadcast_to(mod[layer, b, i * d:(i + 1) * d][None, None, :], (b, 1, d)) for i in range(6)]
        sh1, sc1, g1, sh2, sc2, g2 = lat
        shc1, scc1, gc1, shc2, scc2, gc2 = cx
        gn1 = g_norm1[layer][None, :]
        gn2 = g_norm2[layer][None, :]
        w_t = w_in[layer].T.astype(BF16)
        wf_t = w_four[layer].T.astype(BF16)
        wo = w_out[layer].astype(BF16)
        w_r = jnp.concatenate([w_route_group[layer].T, w_route_expert[layer].T,
                               jnp.zeros((route_pad, d), F32)], axis=0)
        wr_hi = w_r.astype(BF16)
        wr_lo = (w_r - wr_hi.astype(F32)).astype(BF16)
        b_r = jnp.concatenate([b_route_group[layer], b_route_expert[layer], jnp.zeros((route_pad,), F32)])[:, None]
        sink_lat = _sink_rows(swa_sink[layer], SWA_BLOCK)
        sink_ctx = _sink_rows(swa_sink[layer], lc)

        fz, qs, qn, ks, vs, kn, vn = _in_proj(x, sh1, sc1, gn1, w_t, bd_t, cos_t, sin_t,
                                              with_q=True, rope=True, tm=tm)
        if with_ctx_out:
            fz_c, qs_c, qn_c, ks_c, vs_c, kn_c, vn_c = _in_proj(xc, shc1, scc1, gn1, w_t, bd_t, cos_c, sin_c,
                                                                with_q=True, rope=False, tm=lc)
        else:
            ks_c, vs_c, kn_c, vn_c = _in_proj(xc, shc1, scc1, gn1, w_t[KS_LO:], bd_t, cos_c, sin_c,
                                              with_q=False, rope=False, tm=lc)

        yf = _fourier(fz, cn, sn, wf_t, tk=512)
        ys = _swa(qs, ks, vs, ks_c, vs_c, mask, sink_lat)
        yn = _na(qn, kn, vn, kn_c, vn_c, _na_bias(na_rpb[layer], s))
        x1, h2, route = _out_proj(x, yf, ys, yn, wo, g1, sh2, sc2, gn2, wr_hi, wr_lo, b_r, tm=tm)

        moe_w = (w_exp_gate, w_exp_up, w_exp_down, layer)
        if with_ctx_out:
            yf_c = _fourier(fz_c, cn_c, sn_c, wf_t, tk=lc)
            ys_c, yn_c = _ctx_attn(qs_c, ks_c, vs_c, qn_c, kn_c, vn_c, sink_ctx)
            xc1, hc2, route_c = _out_proj(xc, yf_c, ys_c, yn_c, wo, gc1, shc2, scc2, gn2, wr_hi, wr_lo, b_r, tm=lc)
            n_lat = b * s
            lat_chunks = n_lat // MOE_CHUNK
            plan = _moe_plan(jnp.concatenate([_route_chunks(route), _route_chunks(route_c)], axis=0))
            xs = _dispatch([h2.reshape(n_lat, d), hc2.reshape(b * lc, d)], plan)
            ye = _experts(xs, plan, *moe_w)
            x = _combine(x1, plan, g2, g_final[None, :], ye, c0=0, final_norm=False)
            ctx_chunks = b * lc // MOE_CHUNK
            xc = _combine(xc1.reshape(ctx_chunks, MOE_CHUNK, d), plan, gc2[:ctx_chunks], g_final[None, :], ye,
                          c0=lat_chunks, final_norm=False).reshape(b, lc, d)
        else:
            plan = _moe_plan(_route_chunks(route))
            xs = _dispatch([h2.reshape(b * s, d)], plan)
            ye = _experts(xs, plan, *moe_w)
            x = _combine(x1, plan, g2, g_final[None, :], ye, c0=0, final_norm=True)
    return x
```

```python
import functools

import jax
import jax.numpy as jnp
from jax import lax
from jax.experimental import pallas as pl
from jax.experimental.pallas import tpu as pltpu

F32 = jnp.float32
BF16 = jnp.bfloat16

D_MODEL = 1024
GRID_W = 64
HEAD_DIM = 64
FOURIER_WIDTH = D_MODEL // 4
FOURIER_GROUPS = 4
FOURIER_GROUP_DIM = FOURIER_WIDTH // FOURIER_GROUPS
SWA_HEADS = (3 * D_MODEL // 8) // HEAD_DIM
SWA_KV_HEADS = 2
SWA_GROUP = SWA_HEADS // SWA_KV_HEADS
SWA_WINDOW = 128
SWA_BLOCK = 128
NA_HEADS = (3 * D_MODEL // 8) // HEAD_DIM
NA_WIN_R = 8
NA_WIN_C = 16
ROPE_BASE = 10000.0
N_GROUPS = 4
EXPERTS_PER_GROUP = 8
N_EXPERTS = N_GROUPS * EXPERTS_PER_GROUP
D_EXPERT = D_MODEL // 2
EPS = 1e-6
NEG_INF = -1e30
LOG2E = 1.4426950408889634

SWA_Q_W = SWA_HEADS * HEAD_DIM
SWA_KV_W = SWA_KV_HEADS * HEAD_DIM
NA_W = NA_HEADS * HEAD_DIM
MIX_WIDTH = FOURIER_WIDTH + SWA_Q_W + NA_W
Q_COLS = MIX_WIDTH
IN_COLS = 2 * MIX_WIDTH

F_LO, F_HI = 0, FOURIER_WIDTH
QS_LO, QS_HI = F_HI, F_HI + SWA_Q_W
QN_LO, QN_HI = QS_HI, QS_HI + NA_W
KS_LO, KS_HI = QN_HI, QN_HI + SWA_KV_W
VS_LO, VS_HI = KS_HI, KS_HI + SWA_KV_W
KN_LO, KN_HI = VS_HI, VS_HI + NA_W
VN_LO, VN_HI = KN_HI, KN_HI + NA_W

LANE = 128
SUBLANE = 8
ROUTE_ROWS = -(-(N_GROUPS + N_EXPERTS) // SUBLANE) * SUBLANE
ROUTE_OUT = SUBLANE
NA_QROWS = 4
NA_KROWS = NA_QROWS + NA_WIN_R
ATTN_KCHUNK = LANE
ATTN_LOOKAHEAD = 2
PROJ_SUBTILE = LANE
MOE_TILE = 512
MOE_DT = F32
ROW_BLOCK = 8
MOE_CHUNK = 512
MOE_LROWS = 2 * MOE_CHUNK + (ROW_BLOCK - 1) * N_EXPERTS
VMEM_LIMIT = 48 * 1024 * 1024


def _cparams(*sem):
    return pltpu.CompilerParams(dimension_semantics=sem, vmem_limit_bytes=VMEM_LIMIT)


def _dot(a, b):
    return jnp.dot(a, b, preferred_element_type=F32)


def _dot_tn(a, b):
    return lax.dot_general(a, b, (((0,), (0,)), ((), ())), preferred_element_type=F32)


def _dot_nt(a, b):
    return lax.dot_general(a, b, (((1,), (1,)), ((), ())), preferred_element_type=F32)


def _split_dot(a, w):
    a_hi = a.astype(BF16)
    a_lo = (a - a_hi.astype(F32)).astype(BF16)
    w_hi = w.astype(BF16)
    w_lo = (w - w_hi.astype(F32)).astype(BF16)
    return _dot(a_hi, w_hi) + (_dot(a_hi, w_lo) + _dot(a_lo, w_hi))


def _mod_kernel(c_ref, w_ref, b_ref, o_ref):
    c = c_ref[...]
    a = c * (1.0 / (1.0 + jnp.exp(-c)))
    o_ref[0] = _split_dot(a, w_ref[0]) + b_ref[0]


def _modulation(c_rows, w_mod, b_mod):
    depth, d, n6 = w_mod.shape
    r = c_rows.shape[0]
    tn = 1536
    return pl.pallas_call(
        _mod_kernel,
        out_shape=jax.ShapeDtypeStruct((depth, r, n6), F32),
        grid=(depth, n6 // tn),
        in_specs=[
            pl.BlockSpec((r, d), lambda l, j: (0, 0)),
            pl.BlockSpec((1, d, tn), lambda l, j: (l, 0, j)),
            pl.BlockSpec((1, 1, tn), lambda l, j: (l, 0, j)),
        ],
        out_specs=pl.BlockSpec((1, r, tn), lambda l, j: (l, 0, j)),
        compiler_params=_cparams("parallel", "parallel"),
        name="modulation",
    )(c_rows, w_mod, b_mod.reshape(depth, 1, n6))


def _rope_rows(t, cos_t, sin_t, n_heads):
    outs = []
    for h in range(n_heads):
        th = t[HEAD_DIM * h:HEAD_DIM * (h + 1)]
        sw = jnp.concatenate([th[16:32], th[0:16], th[48:64], th[32:48]], axis=0)
        outs.append(th * cos_t + sw * sin_t)
    return jnp.concatenate(outs, axis=0)


def _in_proj_kernel(x_ref, sh_ref, sc_ref, g_ref, wt_ref, bdt_ref, cos_ref, sin_ref, *outs, with_q, rope):
    xf = x_ref[0]
    ms = jnp.mean(xf * xf, axis=-1, keepdims=True)
    y = xf * lax.rsqrt(ms + EPS) * g_ref[...]
    h = y * (1.0 + sc_ref[0]) + sh_ref[0]
    pt = _dot_nt(wt_ref[...], h.astype(BF16))
    q_scale = HEAD_DIM ** -0.5 * LOG2E
    if with_q:
        fz_ref, qs_ref, qn_ref, ks_ref, vs_ref, kn_ref, vn_ref = outs
        fz_ref[0] = _dot(bdt_ref[...], pt[F_LO:F_HI].astype(BF16)).astype(BF16)
        qs = pt[QS_LO:QS_HI]
        if rope:
            qs = _rope_rows(qs, cos_ref[...], sin_ref[...], SWA_HEADS)
        qs_ref[0] = (qs * q_scale).astype(BF16)
        qn_ref[0] = (pt[QN_LO:QN_HI] * q_scale).astype(BF16)
        off = 0
    else:
        ks_ref, vs_ref, kn_ref, vn_ref = outs
        off = KS_LO
    ks = pt[KS_LO - off:KS_HI - off]
    if rope:
        ks = _rope_rows(ks, cos_ref[...], sin_ref[...], SWA_KV_HEADS)
    ks_ref[0] = ks.astype(BF16)
    vs_ref[0] = pt[VS_LO - off:VS_HI - off].astype(BF16)
    kn_ref[0] = pt[KN_LO - off:KN_HI - off].astype(BF16)
    vn_ref[0] = pt[VN_LO - off:VN_HI - off].astype(BF16)


def _in_proj(x, shift, scale, gain, w_t, bd_t, cos_t, sin_t, *, with_q, rope, tm):
    b, t, d = x.shape
    nf = w_t.shape[0]
    rows = ([2 * FOURIER_WIDTH, SWA_Q_W, NA_W] if with_q else []) + [SWA_KV_W, SWA_KV_W, NA_W, NA_W]
    return pl.pallas_call(
        functools.partial(_in_proj_kernel, with_q=with_q, rope=rope),
        out_shape=[jax.ShapeDtypeStruct((b, r, t), BF16) for r in rows],
        grid=(b, t // tm),
        in_specs=[
            pl.BlockSpec((1, tm, d), lambda i, j: (i, j, 0)),
            pl.BlockSpec((1, 1, d), lambda i, j: (i, 0, 0)),
            pl.BlockSpec((1, 1, d), lambda i, j: (i, 0, 0)),
            pl.BlockSpec((1, d), lambda i, j: (0, 0)),
            pl.BlockSpec((nf, d), lambda i, j: (0, 0)),
            pl.BlockSpec(bd_t.shape, lambda i, j: (0, 0)),
            pl.BlockSpec((HEAD_DIM, tm), lambda i, j: (0, j)),
            pl.BlockSpec((HEAD_DIM, tm), lambda i, j: (0, j)),
        ],
        out_specs=[pl.BlockSpec((1, r, tm), lambda i, j: (i, 0, j)) for r in rows],
        compiler_params=_cparams("parallel", "parallel"),
        name="in_proj_q" if with_q else "in_proj_kv",
    )(x, shift, scale, gain, w_t, bd_t, cos_t, sin_t)


def _fourier_kernel(fz_ref, cn_ref, sn_ref, wft_ref, o_ref):
    zc = fz_ref[0, 0:FOURIER_WIDTH, :]
    zs = fz_ref[0, FOURIER_WIDTH:2 * FOURIER_WIDTH, :]
    y = _dot(zc, cn_ref[...]) - _dot(zs, sn_ref[...])
    o_ref[0] = _dot(wft_ref[...], y.astype(BF16)).astype(BF16)


def _fourier(fz, cn, sn, wf_t, *, tk):
    b, _, t = fz.shape
    return pl.pallas_call(
        _fourier_kernel,
        out_shape=jax.ShapeDtypeStruct((b, FOURIER_WIDTH, t), BF16),
        grid=(t // tk, b),
        in_specs=[
            pl.BlockSpec((1, 2 * FOURIER_WIDTH, t), lambda k, i: (i, 0, 0)),
            pl.BlockSpec((t, tk), lambda k, i: (0, k)),
            pl.BlockSpec((t, tk), lambda k, i: (0, k)),
            pl.BlockSpec((FOURIER_WIDTH, FOURIER_WIDTH), lambda k, i: (0, 0)),
        ],
        out_specs=pl.BlockSpec((1, FOURIER_WIDTH, tk), lambda k, i: (i, 0, k)),
        compiler_params=_cparams("parallel", "parallel"),
        name="fourier",
    )(fz, cn, sn, wf_t)


def _key_chunks(k, v, bias=None):
    n = k.shape[1] // ATTN_KCHUNK
    cut = lambda a, j, axis: lax.slice_in_dim(a, j * ATTN_KCHUNK, (j + 1) * ATTN_KCHUNK, axis=axis)
    return [(cut(k, j, 1), cut(v, j, 1), None if bias is None else cut(bias, j, 0)) for j in range(n)]


def _logits(q_t, chunks):
    return _dot_tn(jnp.concatenate([k_t for k_t, _, _ in chunks], axis=1), q_t)


def _softmax_pv(s, chunks, sink_row):
    pieces = []
    off = 0
    for k_t, _, bias in chunks:
        piece = s[off:off + k_t.shape[1]]
        pieces.append(piece if bias is None else piece + bias)
        off += k_t.shape[1]
    m = functools.reduce(jnp.maximum, [jnp.max(p, axis=0, keepdims=True) for p in pieces])
    if sink_row is not None:
        m = jnp.maximum(m, sink_row)
    probs = [jnp.exp2(p - m) for p in pieces]
    den = functools.reduce(jnp.add, [jnp.sum(p, axis=0, keepdims=True) for p in probs])
    if sink_row is not None:
        den = den + jnp.exp2(sink_row - m)
    v_all = jnp.concatenate([v_t for _, v_t, _ in chunks], axis=1)
    p_all = jnp.concatenate([p.astype(BF16) for p in probs], axis=0)
    return _dot(v_all, p_all) / den


def _attend(q_t, chunks, sink_row):
    return _softmax_pv(_logits(q_t, chunks), chunks, sink_row)


def _attend_blocks(n_blocks, make_block, sink_row, store):
    blocks, logits = {}, {}
    for j in range(min(ATTN_LOOKAHEAD, n_blocks)):
        blocks[j] = make_block(j)
        logits[j] = _logits(*blocks[j])
    for j in range(n_blocks):
        ahead = j + ATTN_LOOKAHEAD
        if ahead < n_blocks:
            blocks[ahead] = make_block(ahead)
            logits[ahead] = _logits(*blocks[ahead])
        store(j, _softmax_pv(logits.pop(j), blocks.pop(j)[1], sink_row))


def _swa_kernel(q_ref, k_ref, v_ref, kc_ref, vc_ref, mask_ref, sink_ref, o_ref, *, seq):
    nb = seq // SWA_BLOCK
    ctx_chunks = _key_chunks(kc_ref[0], vc_ref[0])

    def make_block(n):
        q0 = n * SWA_BLOCK
        q_t = jnp.concatenate(
            [q_ref[0, HEAD_DIM * h:HEAD_DIM * (h + 1), q0:q0 + SWA_BLOCK] for h in range(SWA_GROUP)], axis=1)
        chunks = [(k_ref[0, :, q0:q0 + SWA_BLOCK], v_ref[0, :, q0:q0 + SWA_BLOCK], None)] + ctx_chunks
        for side, kb in ((0, n - 1), (1, n + 1)):
            if 0 <= kb < nb:
                k0 = kb * SWA_BLOCK
                chunks.append((k_ref[0, :, k0:k0 + SWA_BLOCK], v_ref[0, :, k0:k0 + SWA_BLOCK], mask_ref[side]))
        return q_t, chunks

    def store(n, o):
        q0 = n * SWA_BLOCK
        for h in range(SWA_GROUP):
            o_ref[0, HEAD_DIM * h:HEAD_DIM * (h + 1), q0:q0 + SWA_BLOCK] = (
                o[:, SWA_BLOCK * h:SWA_BLOCK * (h + 1)].astype(BF16))

    _attend_blocks(nb, make_block, sink_ref[0], store)


def _swa(qs, ks, vs, kc, vc, mask, sink_rows):
    b, _, t = qs.shape
    lc = kc.shape[2]
    gw = SWA_GROUP * HEAD_DIM
    return pl.pallas_call(
        functools.partial(_swa_kernel, seq=t),
        out_shape=jax.ShapeDtypeStruct((b, SWA_Q_W, t), BF16),
        grid=(b, SWA_KV_HEADS),
        in_specs=[
            pl.BlockSpec((1, gw, t), lambda i, g: (i, g, 0)),
            pl.BlockSpec((1, HEAD_DIM, t), lambda i, g: (i, g, 0)),
            pl.BlockSpec((1, HEAD_DIM, t), lambda i, g: (i, g, 0)),
            pl.BlockSpec((1, HEAD_DIM, lc), lambda i, g: (i, g, 0)),
            pl.BlockSpec((1, HEAD_DIM, lc), lambda i, g: (i, g, 0)),
            pl.BlockSpec((2, SWA_BLOCK, SWA_GROUP * SWA_BLOCK), lambda i, g: (0, 0, 0)),
            pl.BlockSpec((1, 1, SWA_GROUP * SWA_BLOCK), lambda i, g: (g, 0, 0)),
        ],
        out_specs=pl.BlockSpec((1, gw, t), lambda i, g: (i, g, 0)),
        compiler_params=_cparams("parallel", "parallel"),
        name="swa",
    )(qs, ks, vs, kc, vc, mask, sink_rows)


def _na_bias_kernel(rpb_ref, o_ref, u_ref, *, total_rows):
    hd = pl.program_id(0)
    kc = lax.broadcasted_iota(jnp.int32, (GRID_W, LANE), 0)
    lane = lax.broadcasted_iota(jnp.int32, (GRID_W, LANE), 1)
    qc = lane % GRID_W
    dc = jnp.clip(kc - qc, -(NA_WIN_C - 1), NA_WIN_C - 1) + (NA_WIN_C - 1)
    c0 = jnp.clip(qc - NA_WIN_C // 2, 0, GRID_W - NA_WIN_C)
    valid_c = (kc >= c0) & (kc < c0 + NA_WIN_C)
    n_dr = 2 * NA_WIN_R - 1
    for dr in range(n_dr):
        u = jnp.full((GRID_W, LANE), NEG_INF, F32)
        for d in range(2 * NA_WIN_C - 1):
            u = jnp.where(valid_c & (dc == d), rpb_ref[hd, dr, d] * LOG2E, u)
        u_ref[dr] = u
    n_rows = o_ref.shape[2] // GRID_W
    block_types = [(0, 0), (NA_QROWS, 0), (total_rows - NA_QROWS, total_rows - NA_KROWS)]
    neg = jnp.full((GRID_W, LANE), NEG_INF, F32)
    for t, (r_base, k_base) in enumerate(block_types):
        for kl in range(n_rows):
            kr = k_base + kl
            for lg in range(NA_QROWS // 2):
                halves = []
                for rq in (2 * lg, 2 * lg + 1):
                    r = r_base + rq
                    r0 = min(max(r - NA_WIN_R // 2, 0), total_rows - NA_WIN_R)
                    ok = r0 <= kr < r0 + NA_WIN_R
                    halves.append(u_ref[kr - r + NA_WIN_R - 1] if ok else neg)
                o_ref[0, t, GRID_W * kl:GRID_W * (kl + 1), LANE * lg:LANE * (lg + 1)] = jnp.where(
                    lane < GRID_W, halves[0], halves[1])


def _na_bias(rpb, seq):
    nh = rpb.shape[0]
    return pl.pallas_call(
        functools.partial(_na_bias_kernel, total_rows=seq // GRID_W),
        out_shape=jax.ShapeDtypeStruct((nh, 3, NA_KROWS * GRID_W, NA_QROWS * GRID_W), F32),
        grid=(nh,),
        in_specs=[pl.BlockSpec(memory_space=pltpu.SMEM)],
        out_specs=pl.BlockSpec((1, 3, NA_KROWS * GRID_W, NA_QROWS * GRID_W), lambda h: (h, 0, 0, 0)),
        scratch_shapes=[pltpu.VMEM((2 * NA_WIN_R - 1, GRID_W, LANE), F32)],
        compiler_params=_cparams("parallel"),
        name="na_bias",
    )(rpb)


def _na_kernel(q_ref, k_ref, v_ref, kc_ref, vc_ref, bias_ref, o_ref, *, seq):
    n_rows = seq // GRID_W
    qw = NA_QROWS * GRID_W
    kw = NA_KROWS * GRID_W
    ctx_chunks = _key_chunks(kc_ref[0], vc_ref[0])
    nblk = n_rows // NA_QROWS
    rows_per_chunk = ATTN_KCHUNK // GRID_W

    def make_block(j):
        k_row = min(max(j * NA_QROWS - NA_WIN_R // 2, 0), n_rows - NA_KROWS)
        btype = 0 if j == 0 else (2 if j == nblk - 1 else 1)
        q0 = j * qw
        chunks = list(ctx_chunks)
        for cj in range(kw // ATTN_KCHUNK):
            first = k_row + cj * rows_per_chunk
            in_window = False
            for r in range(j * NA_QROWS, (j + 1) * NA_QROWS):
                r0 = min(max(r - NA_WIN_R // 2, 0), n_rows - NA_WIN_R)
                in_window = in_window or (first < r0 + NA_WIN_R and first + rows_per_chunk > r0)
            if in_window:
                k0 = first * GRID_W
                chunks.append((k_ref[0, :, k0:k0 + ATTN_KCHUNK], v_ref[0, :, k0:k0 + ATTN_KCHUNK],
                               bias_ref[0, btype, cj * ATTN_KCHUNK:(cj + 1) * ATTN_KCHUNK, :]))
        return q_ref[0, :, q0:q0 + qw], chunks

    def store(j, o):
        o_ref[0, :, j * qw:(j + 1) * qw] = o.astype(BF16)

    _attend_blocks(nblk, make_block, None, store)


def _na(qn, kn, vn, kc, vc, bias):
    b, _, t = qn.shape
    lc = kc.shape[2]
    head = lambda i, h: (i, h, 0)
    return pl.pallas_call(
        functools.partial(_na_kernel, seq=t),
        out_shape=jax.ShapeDtypeStruct((b, NA_W, t), BF16),
        grid=(b, NA_HEADS),
        in_specs=[
            pl.BlockSpec((1, HEAD_DIM, t), head),
            pl.BlockSpec((1, HEAD_DIM, t), head),
            pl.BlockSpec((1, HEAD_DIM, t), head),
            pl.BlockSpec((1, HEAD_DIM, lc), head),
            pl.BlockSpec((1, HEAD_DIM, lc), head),
            pl.BlockSpec((1,) + bias.shape[1:], lambda i, h: (h, 0, 0, 0)),
        ],
        out_specs=pl.BlockSpec((1, HEAD_DIM, t), head),
        compiler_params=_cparams("parallel", "parallel"),
        name="na",
    )(qn, kn, vn, kc, vc, bias)


def _ctx_attn_kernel(qs_ref, ks_ref, vs_ref, qn_ref, kn_ref, vn_ref, sink_ref, ys_ref, yn_ref):
    lc = qs_ref.shape[2]
    for g in range(SWA_KV_HEADS):
        q_t = jnp.concatenate(
            [qs_ref[0, HEAD_DIM * (SWA_GROUP * g + h):HEAD_DIM * (SWA_GROUP * g + h + 1), :] for h in range(SWA_GROUP)],
            axis=1)
        kv = slice(HEAD_DIM * g, HEAD_DIM * (g + 1))
        o = _attend(q_t, _key_chunks(ks_ref[0, kv, :], vs_ref[0, kv, :]), sink_ref[g])
        for h in range(SWA_GROUP):
            hh = SWA_GROUP * g + h
            ys_ref[0, HEAD_DIM * hh:HEAD_DIM * (hh + 1), :] = o[:, lc * h:lc * (h + 1)].astype(BF16)
    for h in range(NA_HEADS):
        sl = slice(HEAD_DIM * h, HEAD_DIM * (h + 1))
        o = _attend(qn_ref[0, sl, :], _key_chunks(kn_ref[0, sl, :], vn_ref[0, sl, :]), None)
        yn_ref[0, sl, :] = o.astype(BF16)


def _ctx_attn(qs, ks, vs, qn, kn, vn, sink_rows):
    b, _, lc = qs.shape
    full = lambda a: pl.BlockSpec((1,) + a.shape[1:], lambda i: (i, 0, 0))
    return pl.pallas_call(
        _ctx_attn_kernel,
        out_shape=[jax.ShapeDtypeStruct((b, SWA_Q_W, lc), BF16), jax.ShapeDtypeStruct((b, NA_W, lc), BF16)],
        grid=(b,),
        in_specs=[full(qs), full(ks), full(vs), full(qn), full(kn), full(vn),
                  pl.BlockSpec(sink_rows.shape, lambda i: (0, 0, 0))],
        out_specs=[pl.BlockSpec((1, SWA_Q_W, lc), lambda i: (i, 0, 0)),
                   pl.BlockSpec((1, NA_W, lc), lambda i: (i, 0, 0))],
        compiler_params=_cparams("parallel"),
        name="ctx_attn",
    )(qs, ks, vs, qn, kn, vn, sink_rows)


def _route(logits):
    row = lax.broadcasted_iota(jnp.int32, logits.shape, 0)
    big = jnp.int32(logits.shape[0])
    colmax = lambda a: jnp.max(a, axis=0, keepdims=True)
    first = lambda hit: jnp.min(jnp.where(hit, row, big), axis=0, keepdims=True)
    gmask = row < N_GROUPS
    gl = jnp.where(gmask, logits, NEG_INF)
    gmax = colmax(gl)
    g_sel = first(gl == gmax)
    p_g = 1.0 / jnp.sum(jnp.where(gmask, jnp.exp(logits - gmax), 0.0), axis=0, keepdims=True)
    lo = N_GROUPS + EXPERTS_PER_GROUP * g_sel
    el = jnp.where((row >= lo) & (row < lo + EXPERTS_PER_GROUP), logits, NEG_INF)
    v1 = colmax(el)
    i1 = first(el == v1)
    el2 = jnp.where(row == i1, NEG_INF, el)
    v2 = colmax(el2)
    i2 = first(el2 == v2)
    e21 = jnp.exp(v2 - v1)
    w1 = p_g / (1.0 + e21)
    w2 = p_g * e21 / (1.0 + e21)
    out_row = lax.broadcasted_iota(jnp.int32, (ROUTE_OUT, logits.shape[1]), 0)
    out = jnp.where(out_row == 0, (i1 - N_GROUPS).astype(F32), 0.0)
    out = jnp.where(out_row == 1, (i2 - N_GROUPS).astype(F32), out)
    out = jnp.where(out_row == 2, w1, out)
    return jnp.where(out_row == 3, w2, out)


def _out_proj_kernel(x_ref, yf_ref, ys_ref, yn_ref, wo_ref, g1_ref, sh_ref, sc_ref, gn_ref, wrh_ref, wrl_ref,
                     br_ref, x1_ref, h2_ref, rt_ref):
    toks = [slice(i, i + PROJ_SUBTILE) for i in range(0, x_ref.shape[1], PROJ_SUBTILE)]
    ys = [_dot_tn(jnp.concatenate([yf_ref[0, :, tok], ys_ref[0, :, tok], yn_ref[0, :, tok]], axis=0), wo_ref[...])
          for tok in toks]
    for tok, y in zip(toks, ys):
        x1 = x_ref[0, tok, :] + g1_ref[0] * y
        x1_ref[0, tok, :] = x1
        ms = jnp.mean(x1 * x1, axis=-1, keepdims=True)
        h2 = (x1 * lax.rsqrt(ms + EPS) * gn_ref[...]) * (1.0 + sc_ref[0]) + sh_ref[0]
        h_hi = h2.astype(BF16)
        h2_ref[0, tok, :] = h_hi
        h_lo = (h2 - h_hi.astype(F32)).astype(BF16)
        logits = (_dot_nt(wrh_ref[...], h_hi) + (_dot_nt(wrh_ref[...], h_lo) + _dot_nt(wrl_ref[...], h_hi))
                  + br_ref[...])
        rt_ref[0, :, tok] = _route(logits)


def _out_proj(x, yf, ys, yn, w_out, gate, shift, scale, gain, wr_hi, wr_lo, b_route, *, tm):
    b, t, d = x.shape
    vec = pl.BlockSpec((1, 1, d), lambda i, j: (i, 0, 0))
    feat = lambda r: pl.BlockSpec((1, r, tm), lambda i, j: (i, 0, j))
    tok = lambda w: pl.BlockSpec((1, tm, w), lambda i, j: (i, j, 0))
    whole = lambda a: pl.BlockSpec(a.shape, lambda i, j: (0, 0))
    return pl.pallas_call(
        _out_proj_kernel,
        out_shape=[jax.ShapeDtypeStruct((b, t, d), F32), jax.ShapeDtypeStruct((b, t, d), BF16),
                   jax.ShapeDtypeStruct((b, ROUTE_OUT, t), F32)],
        grid=(b, t // tm),
        in_specs=[tok(d), feat(FOURIER_WIDTH), feat(SWA_Q_W), feat(NA_W), whole(w_out),
                  vec, vec, vec, pl.BlockSpec((1, d), lambda i, j: (0, 0)),
                  whole(wr_hi), whole(wr_lo), whole(b_route)],
        out_specs=[tok(d), tok(d), feat(ROUTE_OUT)],
        compiler_params=_cparams("parallel", "parallel"),
        name="out_proj",
    )(x, yf, ys, yn, w_out, gate, shift, scale, gain, wr_hi, wr_lo, b_route)


def _start_or_wait(copy, wait):
    if wait:
        copy.wait()
    else:
        copy.start()


def _run_copies(local, remote, sem, runs_ref, c, *, to_remote, wait):
    for e in range(N_EXPERTS):
        rows = runs_ref[0, c, e]

        @pl.when(rows > 0)
        def _(e=e, rows=rows):
            n = pl.multiple_of(rows, ROW_BLOCK)
            loc = local.at[pl.ds(pl.multiple_of(runs_ref[1, c, e], ROW_BLOCK), n)]
            rem = remote.at[pl.ds(pl.multiple_of(runs_ref[2, c, e], ROW_BLOCK), n)]
            _start_or_wait(pltpu.make_async_copy(loc, rem, sem) if to_remote
                           else pltpu.make_async_copy(rem, loc, sem), wait)


def _tail_copies(zero, remote, sem, tstart_ref, tn_ref, *, wait):
    def body(e, carry):
        rows = tn_ref[e]

        @pl.when(rows > 0)
        def _():
            n = pl.multiple_of(rows, ROW_BLOCK)
            rem = remote.at[pl.ds(pl.multiple_of(tstart_ref[e], ROW_BLOCK), n)]
            _start_or_wait(pltpu.make_async_copy(zero.at[pl.ds(0, n)], rem, sem), wait)

        return carry

    lax.fori_loop(0, N_EXPERTS, body, 0)


def _unused_tile_copies(zero, remote, sem, nt_ref, total_tiles, *, wait):
    def body(t, carry):
        row = pl.multiple_of(t * MOE_TILE, MOE_TILE)
        cp = pltpu.make_async_copy(zero, remote.at[pl.ds(row, MOE_TILE)], sem)
        if wait:
            cp.wait()
        else:
            cp.start()
        return carry

    lax.fori_loop(nt_ref[0], total_tiles, body, 0)


def _experts_kernel(t0_ref, tcnt_ref, nt_ref, xs_ref, wg_ref, wu_ref, wd_ref, ys_ref,
                    xbuf, ybuf, zero, wg_s, wu_s, wd_s, xsem, ysem, zsem):
    e = pl.program_id(0)
    first_tile = t0_ref[e]
    n = tcnt_ref[e]
    n_tiles = nt_ref[0]
    total_tiles = xs_ref.shape[0] // MOE_TILE

    def tile_rows(ref, a):
        return ref.at[pl.ds(pl.multiple_of(a * MOE_TILE, MOE_TILE), MOE_TILE)]

    x_copy = lambda a: pltpu.make_async_copy(tile_rows(xs_ref, a), xbuf.at[a % 2], xsem.at[a % 2])
    y_copy = lambda a: pltpu.make_async_copy(ybuf.at[a % 2], tile_rows(ys_ref, a), ysem.at[a % 2])

    last = e == N_EXPERTS - 1

    @pl.when(last)
    def _():
        zero[...] = jnp.zeros(zero.shape, MOE_DT)
        _unused_tile_copies(zero, ys_ref, zsem, nt_ref, total_tiles, wait=False)

    @pl.when(n > 0)
    def _():
        @pl.when(first_tile == 0)
        def _():
            x_copy(first_tile).start(priority=1)

        wg_s[...] = wg_ref[0, 0].astype(BF16)
        wu_s[...] = wu_ref[0, 0].astype(BF16)
        wd_s[...] = wd_ref[0, 0].astype(BF16)

        def body(a, carry):
            slot = a % 2
            x_copy(a).wait()

            @pl.when(a + 1 < n_tiles)
            def _():
                x_copy(a + 1).start(priority=1)

            @pl.when(a >= 2)
            def _():
                y_copy(a - 2).wait()

            x = xbuf[slot].astype(BF16)
            g = _dot(x, wg_s[...])
            u = _dot(x, wu_s[...])
            hid = (g * (1.0 / (1.0 + jnp.exp(-g)))) * u
            ybuf[slot] = _dot(hid.astype(BF16), wd_s[...]).astype(MOE_DT)
            y_copy(a).start(priority=1)
            return carry

        lax.fori_loop(first_tile, first_tile + n, body, 0)

    @pl.when(last)
    def _():
        y_copy(n_tiles - 1).wait()

        @pl.when(n_tiles >= 2)
        def _():
            y_copy(n_tiles - 2).wait()

        _unused_tile_copies(zero, ys_ref, zsem, nt_ref, total_tiles, wait=True)


def _experts(xs, plan, w_gate, w_up, w_down, layer):
    r, d = xs.shape
    de = w_gate.shape[3]
    wspec = lambda a, b: pl.BlockSpec((1, 1, a, b), lambda e, *_: (layer, e, 0, 0))
    any_spec = pl.BlockSpec(memory_space=pl.ANY)
    return pl.pallas_call(
        _experts_kernel,
        out_shape=jax.ShapeDtypeStruct((r, d), MOE_DT),
        grid_spec=pltpu.PrefetchScalarGridSpec(
            num_scalar_prefetch=3,
            grid=(N_EXPERTS,),
            in_specs=[any_spec, wspec(d, de), wspec(d, de), wspec(de, d)],
            out_specs=any_spec,
            scratch_shapes=[pltpu.VMEM((2, MOE_TILE, d), MOE_DT), pltpu.VMEM((2, MOE_TILE, d), MOE_DT),
                            pltpu.VMEM((MOE_TILE, d), MOE_DT),
                            pltpu.VMEM((d, de), BF16), pltpu.VMEM((d, de), BF16), pltpu.VMEM((de, d), BF16),
                            pltpu.SemaphoreType.DMA((2,)), pltpu.SemaphoreType.DMA((2,)),
                            pltpu.SemaphoreType.DMA(())],
        ),
        compiler_params=_cparams("arbitrary"),
        name="experts",
    )(plan["tile0"], plan["tcnt"], plan["n_tiles"], xs, w_gate, w_up, w_down)


def _dispatch_kernel(runs_ref, tstart_ref, tn_ref, nt_ref, *rest, part_chunks):
    n_parts = len(part_chunks)
    h_refs = rest[:n_parts]
    pos_ref, xs_ref, buf, zero, sem, zsem = rest[n_parts:]
    n_chunks = sum(part_chunks)
    c = pl.program_id(0)
    slot = c % 2
    row = lax.broadcasted_iota(jnp.int32, (MOE_LROWS, MOE_CHUNK), 0)
    hit = (row == pos_ref[0, 0:1, :]) | (row == pos_ref[0, 1:2, :])
    sel = jnp.where(hit, 1.0, 0.0).astype(BF16)
    first = 0
    for h_ref, n in zip(h_refs, part_chunks):
        @pl.when((c >= first) & (c < first + n))
        def _(h_ref=h_ref):
            buf[slot] = _dot(sel, h_ref[...]).astype(MOE_DT)
        first += n
    _run_copies(buf.at[slot], xs_ref, sem.at[slot], runs_ref, c, to_remote=True, wait=False)

    @pl.when(c > 0)
    def _():
        _run_copies(buf.at[1 - slot], xs_ref, sem.at[1 - slot], runs_ref, c - 1, to_remote=True, wait=True)

    @pl.when(c == n_chunks - 1)
    def _():
        total_tiles = xs_ref.shape[0] // MOE_TILE
        zero[...] = jnp.zeros(zero.shape, MOE_DT)
        _tail_copies(zero, xs_ref, zsem, tstart_ref, tn_ref, wait=False)
        _unused_tile_copies(zero, xs_ref, zsem, nt_ref, total_tiles, wait=False)
        _run_copies(buf.at[slot], xs_ref, sem.at[slot], runs_ref, c, to_remote=True, wait=True)
        _tail_copies(zero, xs_ref, zsem, tstart_ref, tn_ref, wait=True)
        _unused_tile_copies(zero, xs_ref, zsem, nt_ref, total_tiles, wait=True)


def _dispatch(h_parts, plan):
    d = h_parts[0].shape[1]
    part_chunks = tuple(h.shape[0] // MOE_CHUNK for h in h_parts)
    in_specs = []
    first = 0
    for n in part_chunks:
        in_specs.append(pl.BlockSpec(
            (MOE_CHUNK, d), lambda i, *_, first=first, n=n: (jnp.clip(i - first, 0, n - 1), 0)))
        first += n
    in_specs.append(pl.BlockSpec((1, 8, MOE_CHUNK), lambda i, *_: (i, 0, 0)))
    return pl.pallas_call(
        functools.partial(_dispatch_kernel, part_chunks=part_chunks),
        out_shape=jax.ShapeDtypeStruct((plan["rows"], d), MOE_DT),
        grid_spec=pltpu.PrefetchScalarGridSpec(
            num_scalar_prefetch=4,
            grid=(sum(part_chunks),),
            in_specs=in_specs,
            out_specs=pl.BlockSpec(memory_space=pl.ANY),
            scratch_shapes=[pltpu.VMEM((2, MOE_LROWS, d), MOE_DT), pltpu.VMEM((MOE_TILE, d), MOE_DT),
                            pltpu.SemaphoreType.DMA((2,)), pltpu.SemaphoreType.DMA(())],
        ),
        compiler_params=_cparams("arbitrary"),
        name="dispatch",
    )(plan["runs"], plan["tstart"], plan["tn"], plan["n_tiles"], *h_parts, plan["pos"])


def _combine_kernel(runs_ref, x1_ref, pos_ref, wt_ref, g2_ref, gf_ref, ys_ref, o_ref, buf, sem,
                    *, c0, n_chunks, final_norm):
    i = pl.program_id(0)
    c = i + c0
    slot = i % 2
    gather = functools.partial(_run_copies, remote=ys_ref, runs_ref=runs_ref, to_remote=False)

    @pl.when(i == 0)
    def _():
        buf[...] = jnp.zeros(buf.shape, MOE_DT)
        gather(buf.at[slot], sem=sem.at[slot], c=c, wait=False)

    @pl.when(i + 1 < n_chunks)
    def _():
        gather(buf.at[1 - slot], sem=sem.at[1 - slot], c=c + 1, wait=False)

    gather(buf.at[slot], sem=sem.at[slot], c=c, wait=True)
    row = lax.broadcasted_iota(jnp.int32, (MOE_LROWS, MOE_CHUNK), 0)
    hit1 = row == pos_ref[0, 0:1, :]
    hit2 = row == pos_ref[0, 1:2, :]
    w_row = jnp.sum(jnp.where(hit1, wt_ref[0, 0:1, :], 0.0) + jnp.where(hit2, wt_ref[0, 1:2, :], 0.0),
                    axis=1, keepdims=True)
    sel = jnp.where(hit1 | hit2, 1.0, 0.0).astype(BF16)
    x2 = x1_ref[0] + g2_ref[0] * _dot_tn(sel, (buf[slot] * w_row).astype(BF16))
    if final_norm:
        ms = jnp.mean(x2 * x2, axis=-1, keepdims=True)
        x2 = x2 * lax.rsqrt(ms + EPS) * gf_ref[...]
    o_ref[0] = x2


def _combine(x1, plan, gate, g_final, ys, *, c0, final_norm):
    b, t, d = x1.shape
    per_batch = t // MOE_CHUNK
    n_chunks = b * per_batch
    tok = lambda w: pl.BlockSpec((1, MOE_CHUNK, w), lambda i, *_: (i // per_batch, i % per_batch, 0))
    return pl.pallas_call(
        functools.partial(_combine_kernel, c0=c0, n_chunks=n_chunks, final_norm=final_norm),
        out_shape=jax.ShapeDtypeStruct((b, t, d), F32),
        grid_spec=pltpu.PrefetchScalarGridSpec(
            num_scalar_prefetch=1,
            grid=(n_chunks,),
            in_specs=[tok(d),
                      pl.BlockSpec((1, 8, MOE_CHUNK), lambda i, *_: (i + c0, 0, 0)),
                      pl.BlockSpec((1, 8, MOE_CHUNK), lambda i, *_: (i + c0, 0, 0)),
                      pl.BlockSpec((1, 1, d), lambda i, *_: (i // per_batch, 0, 0)),
                      pl.BlockSpec((1, d), lambda i, *_: (0, 0)),
                      pl.BlockSpec(memory_space=pl.ANY)],
            out_specs=tok(d),
            scratch_shapes=[pltpu.VMEM((2, MOE_LROWS, d), MOE_DT), pltpu.SemaphoreType.DMA((2,))],
        ),
        compiler_params=_cparams("arbitrary"),
        name="combine",
    )(plan["runs"], x1, plan["pos"], plan["wts"], gate, g_final, ys)


def _rope_tables(seq):
    half = HEAD_DIM // 4
    t = jnp.arange(seq)
    rows, cols = t // GRID_W, t % GRID_W
    freqs = ROPE_BASE ** (-jnp.arange(half, dtype=F32) / half)

    def cs(pos):
        ang = pos.astype(F32)[None, :] * freqs[:, None]
        return jnp.cos(ang), jnp.sin(ang)

    cr, sr = cs(rows)
    cc, sc = cs(cols)
    return jnp.concatenate([cr, cr, cc, cc], axis=0), jnp.concatenate([-sr, sr, -sc, sc], axis=0)


def _dft_tables(n):
    k = jnp.arange(n)

    def cs(m):
        ang = ((m[:, None] * k[None, :]) % n).astype(F32) * (2.0 * jnp.pi / n)
        return jnp.cos(ang), jnp.sin(ang)

    if n <= GRID_W:
        return cs(k)
    ca, sa = cs(jnp.arange(n // GRID_W) * GRID_W)
    cb, sb = cs(jnp.arange(GRID_W))
    c = ca[:, None, :] * cb[None, :, :] - sa[:, None, :] * sb[None, :, :]
    s = sa[:, None, :] * cb[None, :, :] + ca[:, None, :] * sb[None, :, :]
    return c.reshape(n, n), s.reshape(n, n)


def _channel_dft_t():
    c, s = _dft_tables(FOURIER_GROUP_DIM)
    eye = jnp.eye(FOURIER_GROUPS, dtype=F32)
    scale = FOURIER_GROUP_DIM ** -0.5
    return jnp.concatenate([jnp.kron(eye, c), jnp.kron(eye, s)], axis=0) * scale


def _position_dft(n):
    c, s = _dft_tables(n)
    scale = n ** -0.5
    return (c * scale).astype(BF16), (s * scale).astype(BF16)


def _swa_mask():
    kk = jnp.arange(SWA_BLOCK)[:, None]
    q = jnp.arange(SWA_BLOCK)[None, :]
    tiles = []
    for block_offset in (-SWA_BLOCK, SWA_BLOCK):
        ok = jnp.abs(block_offset + kk - q) <= SWA_WINDOW
        tiles.append(jnp.tile(jnp.where(ok, 0.0, NEG_INF).astype(F32), (1, SWA_GROUP)))
    return jnp.stack(tiles)


def _sink_rows(sink, width):
    return jnp.repeat(sink.astype(F32).reshape(SWA_KV_HEADS, SWA_GROUP) * LOG2E, width, axis=1)[:, None, :]


def _route_chunks(route):
    r = jnp.moveaxis(route, 1, 0).reshape(ROUTE_OUT, -1, MOE_CHUNK)
    return jnp.swapaxes(r, 0, 1)


def _moe_plan(route_chunks):
    i32 = jnp.int32
    nc = route_chunks.shape[0]
    n = nc * MOE_CHUNK
    ids = route_chunks[:, 0:2, :].astype(i32).reshape(nc, 2 * MOE_CHUNK)
    onehot = (ids[:, :, None] == jnp.arange(N_EXPERTS, dtype=i32)).astype(i32)
    pair = jnp.arange(2 * MOE_CHUNK, dtype=i32)
    earlier = (pair[None, :] < pair[:, None]).astype(BF16)
    before = jnp.einsum("pq,cqe->cpe", earlier, onehot.astype(BF16), preferred_element_type=F32).astype(i32)
    rank = jnp.sum(before * onehot, axis=-1)
    cnt = jnp.sum(onehot, axis=1)
    run = (cnt + ROW_BLOCK - 1) // ROW_BLOCK * ROW_BLOCK
    lo = jnp.cumsum(run, axis=1) - run
    pos = (jnp.sum(onehot * lo[:, None, :], axis=-1) + rank).reshape(nc, 2, MOE_CHUNK)
    pos = jnp.concatenate([pos, jnp.full((nc, 6, MOE_CHUNK), -1, i32)], axis=1)
    seg = jnp.sum(run, axis=0)
    padded = (seg + MOE_TILE - 1) // MOE_TILE * MOE_TILE
    start = jnp.cumsum(padded) - padded
    off = start[None, :] + jnp.cumsum(run, axis=0) - run
    wts = jnp.concatenate([route_chunks[:, 2:4, :], jnp.zeros((nc, 6, MOE_CHUNK), F32)], axis=1)
    rows = -(-(2 * n + (ROW_BLOCK - 1) * N_EXPERTS * nc + (MOE_TILE - 1) * N_EXPERTS) // MOE_TILE) * MOE_TILE
    n_tiles = jnp.sum(padded) // MOE_TILE
    return {"pos": pos, "wts": wts, "runs": jnp.stack([run, lo, off]).astype(i32),
            "tstart": (start + seg).astype(i32), "tn": (padded - seg).astype(i32),
            "tile0": (start // MOE_TILE).astype(i32), "tcnt": (padded // MOE_TILE).astype(i32),
            "n_tiles": n_tiles.astype(i32).reshape(1), "rows": rows}


def kernel(x, c, ctx, c_ctx, w_mod, b_mod, g_norm1, g_norm2, w_in, w_four, w_out, swa_sink, na_rpb,
           w_route_group, b_route_group, w_route_expert, b_route_expert, w_exp_gate, w_exp_up,
           w_exp_down, g_final):
    b, s, d = x.shape
    lc = ctx.shape[1]
    depth = w_mod.shape[0]
    tm = 512

    c_rows = jnp.concatenate([c, c_ctx[None, :], jnp.zeros((7, d), F32)], axis=0)
    mod = _modulation(c_rows, w_mod, b_mod)

    cos_t, sin_t = _rope_tables(s)
    cos_c, sin_c = cos_t[:, :lc], sin_t[:, :lc]
    bd_t = _channel_dft_t().astype(BF16)
    cn, sn = _position_dft(s)
    cn_c, sn_c = _position_dft(lc)
    mask = _swa_mask()
    route_pad = ROUTE_ROWS - N_GROUPS - N_EXPERTS

    xc = ctx
    for layer in range(depth):
        with_ctx_out = layer < depth - 1
        lat = [mod[layer, :b, i * d:(i + 1) * d][:, None, :] for i in range(6)]
        cx = [jnp.broadcast_to(mod[layer, b, i * d:(i + 1) * d][None, None, :], (b, 1, d)) for i in range(6)]
        sh1, sc1, g1, sh2, sc2, g2 = lat
        shc1, scc1, gc1, shc2, scc2, gc2 = cx
        gn1 = g_norm1[layer][None, :]
        gn2 = g_norm2[layer][None, :]
        w_t = w_in[layer].T.astype(BF16)
        wf_t = w_four[layer].T.astype(BF16)
        wo = w_out[layer].astype(BF16)
        w_r = jnp.concatenate([w_route_group[layer].T, w_route_expert[layer].T,
                               jnp.zeros((route_pad, d), F32)], axis=0)
        wr_hi = w_r.astype(BF16)
        wr_lo = (w_r - wr_hi.astype(F32)).astype(BF16)
        b_r = jnp.concatenate([b_route_group[layer], b_route_expert[layer], jnp.zeros((route_pad,), F32)])[:, None]
        sink_lat = _sink_rows(swa_sink[layer], SWA_BLOCK)
        sink_ctx = _sink_rows(swa_sink[layer], lc)

        fz, qs, qn, ks, vs, kn, vn = _in_proj(x, sh1, sc1, gn1, w_t, bd_t, cos_t, sin_t,
                                              with_q=True, rope=True, tm=tm)
        if with_ctx_out:
            fz_c, qs_c, qn_c, ks_c, vs_c, kn_c, vn_c = _in_proj(xc, shc1, scc1, gn1, w_t, bd_t, cos_c, sin_c,
                                                                with_q=True, rope=False, tm=lc)
        else:
            ks_c, vs_c, kn_c, vn_c = _in_proj(xc, shc1, scc1, gn1, w_t[KS_LO:], bd_t, cos_c, sin_c,
                                              with_q=False, rope=False, tm=lc)

        yf = _fourier(fz, cn, sn, wf_t, tk=512)
        ys = _swa(qs, ks, vs, ks_c, vs_c, mask, sink_lat)
        yn = _na(qn, kn, vn, kn_c, vn_c, _na_bias(na_rpb[layer], s))
        x1, h2, route = _out_proj(x, yf, ys, yn, wo, g1, sh2, sc2, gn2, wr_hi, wr_lo, b_r, tm=tm)

        moe_w = (w_exp_gate, w_exp_up, w_exp_down, layer)
        if with_ctx_out:
            yf_c = _fourier(fz_c, cn_c, sn_c, wf_t, tk=lc)
            ys_c, yn_c = _ctx_attn(qs_c, ks_c, vs_c, qn_c, kn_c, vn_c, sink_ctx)
            xc1, hc2, route_c = _out_proj(xc, yf_c, ys_c, yn_c, wo, gc1, shc2, scc2, gn2, wr_hi, wr_lo, b_r, tm=lc)
            n_lat = b * s
            lat_chunks = n_lat // MOE_CHUNK
            plan = _moe_plan(jnp.concatenate([_route_chunks(route), _route_chunks(route_c)], axis=0))
            xs = _dispatch([h2.reshape(n_lat, d), hc2.reshape(b * lc, d)], plan)
            ye = _experts(xs, plan, *moe_w)
            x = _combine(x1, plan, g2, g_final[None, :], ye, c0=0, final_norm=False)
            ctx_chunks = b * lc // MOE_CHUNK
            xc = _combine(xc1.reshape(ctx_chunks, MOE_CHUNK, d), plan, gc2[:ctx_chunks], g_final[None, :], ye,
                          c0=lat_chunks, final_norm=False).reshape(b, lc, d)
        else:
            plan = _moe_plan(_route_chunks(route))
            xs = _dispatch([h2.reshape(b * s, d)], plan)
            ye = _experts(xs, plan, *moe_w)
            x = _combine(x1, plan, g2, g_final[None, :], ye, c0=0, final_norm=True)
    return x
```

```python
import functools

import jax
import jax.numpy as jnp
from jax import lax
from jax.experimental import pallas as pl
from jax.experimental.pallas import tpu as pltpu

F32 = jnp.float32
BF16 = jnp.bfloat16

D_MODEL = 1024
GRID_W = 64
HEAD_DIM = 64
FOURIER_WIDTH = D_MODEL // 4
FOURIER_GROUPS = 4
FOURIER_GROUP_DIM = FOURIER_WIDTH // FOURIER_GROUPS
SWA_HEADS = (3 * D_MODEL // 8) // HEAD_DIM
SWA_KV_HEADS = 2
SWA_GROUP = SWA_HEADS // SWA_KV_HEADS
SWA_WINDOW = 128
SWA_BLOCK = 128
NA_HEADS = (3 * D_MODEL // 8) // HEAD_DIM
NA_WIN_R = 8
NA_WIN_C = 16
ROPE_BASE = 10000.0
N_GROUPS = 4
EXPERTS_PER_GROUP = 8
N_EXPERTS = N_GROUPS * EXPERTS_PER_GROUP
D_EXPERT = D_MODEL // 2
EPS = 1e-6
NEG_INF = -1e30
LOG2E = 1.4426950408889634

SWA_Q_W = SWA_HEADS * HEAD_DIM
SWA_KV_W = SWA_KV_HEADS * HEAD_DIM
NA_W = NA_HEADS * HEAD_DIM
MIX_WIDTH = FOURIER_WIDTH + SWA_Q_W + NA_W
Q_COLS = MIX_WIDTH
IN_COLS = 2 * MIX_WIDTH

F_LO, F_HI = 0, FOURIER_WIDTH
QS_LO, QS_HI = F_HI, F_HI + SWA_Q_W
QN_LO, QN_HI = QS_HI, QS_HI + NA_W
KS_LO, KS_HI = QN_HI, QN_HI + SWA_KV_W
VS_LO, VS_HI = KS_HI, KS_HI + SWA_KV_W
KN_LO, KN_HI = VS_HI, VS_HI + NA_W
VN_LO, VN_HI = KN_HI, KN_HI + NA_W

LANE = 128
SUBLANE = 8
ROUTE_ROWS = -(-(N_GROUPS + N_EXPERTS) // SUBLANE) * SUBLANE
ROUTE_OUT = SUBLANE
NA_QROWS = 4
NA_KROWS = NA_QROWS + NA_WIN_R
ATTN_KCHUNK = LANE
ATTN_LOOKAHEAD = 2
NA_HEADS_PER_STEP = 2
PROJ_SUBTILE = LANE
MOE_TILE = 512
MOE_DT = F32
ROW_BLOCK = 8
MOE_CHUNK = 512
MOE_LROWS = 2 * MOE_CHUNK + (ROW_BLOCK - 1) * N_EXPERTS
VMEM_LIMIT = 48 * 1024 * 1024


def _cparams(*sem):
    return pltpu.CompilerParams(dimension_semantics=sem, vmem_limit_bytes=VMEM_LIMIT)


def _dot(a, b):
    return jnp.dot(a, b, preferred_element_type=F32)


def _dot_tn(a, b):
    return lax.dot_general(a, b, (((0,), (0,)), ((), ())), preferred_element_type=F32)


def _dot_nt(a, b):
    return lax.dot_general(a, b, (((1,), (1,)), ((), ())), preferred_element_type=F32)


def _split_dot(a, w):
    a_hi = a.astype(BF16)
    a_lo = (a - a_hi.astype(F32)).astype(BF16)
    w_hi = w.astype(BF16)
    w_lo = (w - w_hi.astype(F32)).astype(BF16)
    return _dot(a_hi, w_hi) + (_dot(a_hi, w_lo) + _dot(a_lo, w_hi))


def _mod_kernel(c_ref, w_ref, b_ref, o_ref):
    c = c_ref[...]
    a = c * (1.0 / (1.0 + jnp.exp(-c)))
    o_ref[0] = _split_dot(a, w_ref[0]) + b_ref[0]


def _modulation(c_rows, w_mod, b_mod):
    depth, d, n6 = w_mod.shape
    r = c_rows.shape[0]
    tn = 1536
    return pl.pallas_call(
        _mod_kernel,
        out_shape=jax.ShapeDtypeStruct((depth, r, n6), F32),
        grid=(depth, n6 // tn),
        in_specs=[
            pl.BlockSpec((r, d), lambda l, j: (0, 0)),
            pl.BlockSpec((1, d, tn), lambda l, j: (l, 0, j)),
            pl.BlockSpec((1, 1, tn), lambda l, j: (l, 0, j)),
        ],
        out_specs=pl.BlockSpec((1, r, tn), lambda l, j: (l, 0, j)),
        compiler_params=_cparams("parallel", "parallel"),
        name="modulation",
    )(c_rows, w_mod, b_mod.reshape(depth, 1, n6))


def _rope_rows(t, cos_t, sin_t, n_heads):
    outs = []
    for h in range(n_heads):
        th = t[HEAD_DIM * h:HEAD_DIM * (h + 1)]
        sw = jnp.concatenate([th[16:32], th[0:16], th[48:64], th[32:48]], axis=0)
        outs.append(th * cos_t + sw * sin_t)
    return jnp.concatenate(outs, axis=0)


def _in_proj_kernel(x_ref, sh_ref, sc_ref, g_ref, wt_ref, bdt_ref, cos_ref, sin_ref, *outs, with_q, rope):
    xf = x_ref[0]
    ms = jnp.mean(xf * xf, axis=-1, keepdims=True)
    y = xf * lax.rsqrt(ms + EPS) * g_ref[...]
    h = y * (1.0 + sc_ref[0]) + sh_ref[0]
    pt = _dot_nt(wt_ref[...], h.astype(BF16))
    q_scale = HEAD_DIM ** -0.5 * LOG2E
    if with_q:
        fz_ref, qs_ref, qn_ref, ks_ref, vs_ref, kn_ref, vn_ref = outs
        fz_ref[0] = _dot(bdt_ref[...], pt[F_LO:F_HI].astype(BF16)).astype(BF16)
        qs = pt[QS_LO:QS_HI]
        if rope:
            qs = _rope_rows(qs, cos_ref[...], sin_ref[...], SWA_HEADS)
        qs_ref[0] = (qs * q_scale).astype(BF16)
        qn_ref[0] = (pt[QN_LO:QN_HI] * q_scale).astype(BF16)
        off = 0
    else:
        ks_ref, vs_ref, kn_ref, vn_ref = outs
        off = KS_LO
    ks = pt[KS_LO - off:KS_HI - off]
    if rope:
        ks = _rope_rows(ks, cos_ref[...], sin_ref[...], SWA_KV_HEADS)
    ks_ref[0] = ks.astype(BF16)
    vs_ref[0] = pt[VS_LO - off:VS_HI - off].astype(BF16)
    kn_ref[0] = pt[KN_LO - off:KN_HI - off].astype(BF16)
    vn_ref[0] = pt[VN_LO - off:VN_HI - off].astype(BF16)


def _in_proj(x, shift, scale, gain, w_t, bd_t, cos_t, sin_t, *, with_q, rope, tm):
    b, t, d = x.shape
    nf = w_t.shape[0]
    rows = ([2 * FOURIER_WIDTH, SWA_Q_W, NA_W] if with_q else []) + [SWA_KV_W, SWA_KV_W, NA_W, NA_W]
    return pl.pallas_call(
        functools.partial(_in_proj_kernel, with_q=with_q, rope=rope),
        out_shape=[jax.ShapeDtypeStruct((b, r, t), BF16) for r in rows],
        grid=(b, t // tm),
        in_specs=[
            pl.BlockSpec((1, tm, d), lambda i, j: (i, j, 0)),
            pl.BlockSpec((1, 1, d), lambda i, j: (i, 0, 0)),
            pl.BlockSpec((1, 1, d), lambda i, j: (i, 0, 0)),
            pl.BlockSpec((1, d), lambda i, j: (0, 0)),
            pl.BlockSpec((nf, d), lambda i, j: (0, 0)),
            pl.BlockSpec(bd_t.shape, lambda i, j: (0, 0)),
            pl.BlockSpec((HEAD_DIM, tm), lambda i, j: (0, j)),
            pl.BlockSpec((HEAD_DIM, tm), lambda i, j: (0, j)),
        ],
        out_specs=[pl.BlockSpec((1, r, tm), lambda i, j: (i, 0, j)) for r in rows],
        compiler_params=_cparams("parallel", "parallel"),
        name="in_proj_q" if with_q else "in_proj_kv",
    )(x, shift, scale, gain, w_t, bd_t, cos_t, sin_t)


def _fourier_kernel(fz_ref, cn_ref, sn_ref, wft_ref, o_ref):
    zc = fz_ref[0, 0:FOURIER_WIDTH, :]
    zs = fz_ref[0, FOURIER_WIDTH:2 * FOURIER_WIDTH, :]
    y = _dot(zc, cn_ref[...]) - _dot(zs, sn_ref[...])
    o_ref[0] = _dot(wft_ref[...], y.astype(BF16)).astype(BF16)


def _fourier(fz, cn, sn, wf_t, *, tk):
    b, _, t = fz.shape
    return pl.pallas_call(
        _fourier_kernel,
        out_shape=jax.ShapeDtypeStruct((b, FOURIER_WIDTH, t), BF16),
        grid=(t // tk, b),
        in_specs=[
            pl.BlockSpec((1, 2 * FOURIER_WIDTH, t), lambda k, i: (i, 0, 0)),
            pl.BlockSpec((t, tk), lambda k, i: (0, k)),
            pl.BlockSpec((t, tk), lambda k, i: (0, k)),
            pl.BlockSpec((FOURIER_WIDTH, FOURIER_WIDTH), lambda k, i: (0, 0)),
        ],
        out_specs=pl.BlockSpec((1, FOURIER_WIDTH, tk), lambda k, i: (i, 0, k)),
        compiler_params=_cparams("parallel", "parallel"),
        name="fourier",
    )(fz, cn, sn, wf_t)


def _key_chunks(k, v, bias=None):
    n = k.shape[1] // ATTN_KCHUNK
    cut = lambda a, j, axis: lax.slice_in_dim(a, j * ATTN_KCHUNK, (j + 1) * ATTN_KCHUNK, axis=axis)
    return [(cut(k, j, 1), cut(v, j, 1), None if bias is None else cut(bias, j, 0)) for j in range(n)]


def _logits(q_t, chunks):
    return _dot_tn(jnp.concatenate([k_t for k_t, _, _ in chunks], axis=1), q_t)


def _softmax_pv(s, chunks, sink_row):
    pieces = []
    off = 0
    for k_t, _, bias in chunks:
        piece = s[off:off + k_t.shape[1]]
        pieces.append(piece if bias is None else piece + bias)
        off += k_t.shape[1]
    m = functools.reduce(jnp.maximum, [jnp.max(p, axis=0, keepdims=True) for p in pieces])
    if sink_row is not None:
        m = jnp.maximum(m, sink_row)
    probs = [jnp.exp2(p - m) for p in pieces]
    den = functools.reduce(jnp.add, [jnp.sum(p, axis=0, keepdims=True) for p in probs])
    if sink_row is not None:
        den = den + jnp.exp2(sink_row - m)
    v_all = jnp.concatenate([v_t for _, v_t, _ in chunks], axis=1)
    p_all = jnp.concatenate([p.astype(BF16) for p in probs], axis=0)
    return _dot(v_all, p_all) / den


def _attend(q_t, chunks, sink_row):
    return _softmax_pv(_logits(q_t, chunks), chunks, sink_row)


def _attend_blocks(n_blocks, make_block, sink_row, store):
    blocks, logits = {}, {}
    for j in range(min(ATTN_LOOKAHEAD, n_blocks)):
        blocks[j] = make_block(j)
        logits[j] = _logits(*blocks[j])
    for j in range(n_blocks):
        ahead = j + ATTN_LOOKAHEAD
        if ahead < n_blocks:
            blocks[ahead] = make_block(ahead)
            logits[ahead] = _logits(*blocks[ahead])
        store(j, _softmax_pv(logits.pop(j), blocks.pop(j)[1], sink_row))


def _swa_kernel(q_ref, k_ref, v_ref, kc_ref, vc_ref, mask_ref, sink_ref, o_ref, *, seq):
    nb = seq // SWA_BLOCK
    ctx_chunks = _key_chunks(kc_ref[0], vc_ref[0])

    def make_block(n):
        q0 = n * SWA_BLOCK
        q_t = jnp.concatenate(
            [q_ref[0, HEAD_DIM * h:HEAD_DIM * (h + 1), q0:q0 + SWA_BLOCK] for h in range(SWA_GROUP)], axis=1)
        chunks = [(k_ref[0, :, q0:q0 + SWA_BLOCK], v_ref[0, :, q0:q0 + SWA_BLOCK], None)] + ctx_chunks
        for side, kb in ((0, n - 1), (1, n + 1)):
            if 0 <= kb < nb:
                k0 = kb * SWA_BLOCK
                chunks.append((k_ref[0, :, k0:k0 + SWA_BLOCK], v_ref[0, :, k0:k0 + SWA_BLOCK], mask_ref[side]))
        return q_t, chunks

    def store(n, o):
        q0 = n * SWA_BLOCK
        for h in range(SWA_GROUP):
            o_ref[0, HEAD_DIM * h:HEAD_DIM * (h + 1), q0:q0 + SWA_BLOCK] = (
                o[:, SWA_BLOCK * h:SWA_BLOCK * (h + 1)].astype(BF16))

    _attend_blocks(nb, make_block, sink_ref[0], store)


def _swa(qs, ks, vs, kc, vc, mask, sink_rows):
    b, _, t = qs.shape
    lc = kc.shape[2]
    gw = SWA_GROUP * HEAD_DIM
    return pl.pallas_call(
        functools.partial(_swa_kernel, seq=t),
        out_shape=jax.ShapeDtypeStruct((b, SWA_Q_W, t), BF16),
        grid=(b, SWA_KV_HEADS),
        in_specs=[
            pl.BlockSpec((1, gw, t), lambda i, g: (i, g, 0)),
            pl.BlockSpec((1, HEAD_DIM, t), lambda i, g: (i, g, 0)),
            pl.BlockSpec((1, HEAD_DIM, t), lambda i, g: (i, g, 0)),
            pl.BlockSpec((1, HEAD_DIM, lc), lambda i, g: (i, g, 0)),
            pl.BlockSpec((1, HEAD_DIM, lc), lambda i, g: (i, g, 0)),
            pl.BlockSpec((2, SWA_BLOCK, SWA_GROUP * SWA_BLOCK), lambda i, g: (0, 0, 0)),
            pl.BlockSpec((1, 1, SWA_GROUP * SWA_BLOCK), lambda i, g: (g, 0, 0)),
        ],
        out_specs=pl.BlockSpec((1, gw, t), lambda i, g: (i, g, 0)),
        compiler_params=_cparams("parallel", "parallel"),
        name="swa",
    )(qs, ks, vs, kc, vc, mask, sink_rows)


def _na_bias_kernel(rpb_ref, o_ref, u_ref, *, total_rows):
    hd = pl.program_id(0)
    kc = lax.broadcasted_iota(jnp.int32, (GRID_W, LANE), 0)
    lane = lax.broadcasted_iota(jnp.int32, (GRID_W, LANE), 1)
    qc = lane % GRID_W
    dc = jnp.clip(kc - qc, -(NA_WIN_C - 1), NA_WIN_C - 1) + (NA_WIN_C - 1)
    c0 = jnp.clip(qc - NA_WIN_C // 2, 0, GRID_W - NA_WIN_C)
    valid_c = (kc >= c0) & (kc < c0 + NA_WIN_C)
    n_dr = 2 * NA_WIN_R - 1
    for dr in range(n_dr):
        u = jnp.full((GRID_W, LANE), NEG_INF, F32)
        for d in range(2 * NA_WIN_C - 1):
            u = jnp.where(valid_c & (dc == d), rpb_ref[hd, dr, d] * LOG2E, u)
        u_ref[dr] = u
    n_rows = o_ref.shape[2] // GRID_W
    block_types = [(0, 0), (NA_QROWS, 0), (total_rows - NA_QROWS, total_rows - NA_KROWS)]
    neg = jnp.full((GRID_W, LANE), NEG_INF, F32)
    for t, (r_base, k_base) in enumerate(block_types):
        for kl in range(n_rows):
            kr = k_base + kl
            for lg in range(NA_QROWS // 2):
                halves = []
                for rq in (2 * lg, 2 * lg + 1):
                    r = r_base + rq
                    r0 = min(max(r - NA_WIN_R // 2, 0), total_rows - NA_WIN_R)
                    ok = r0 <= kr < r0 + NA_WIN_R
                    halves.append(u_ref[kr - r + NA_WIN_R - 1] if ok else neg)
                o_ref[0, t, GRID_W * kl:GRID_W * (kl + 1), LANE * lg:LANE * (lg + 1)] = jnp.where(
                    lane < GRID_W, halves[0], halves[1])


def _na_bias(rpb, seq):
    nh = rpb.shape[0]
    return pl.pallas_call(
        functools.partial(_na_bias_kernel, total_rows=seq // GRID_W),
        out_shape=jax.ShapeDtypeStruct((nh, 3, NA_KROWS * GRID_W, NA_QROWS * GRID_W), F32),
        grid=(nh,),
        in_specs=[pl.BlockSpec(memory_space=pltpu.SMEM)],
        out_specs=pl.BlockSpec((1, 3, NA_KROWS * GRID_W, NA_QROWS * GRID_W), lambda h: (h, 0, 0, 0)),
        scratch_shapes=[pltpu.VMEM((2 * NA_WIN_R - 1, GRID_W, LANE), F32)],
        compiler_params=_cparams("parallel"),
        name="na_bias",
    )(rpb)


def _na_kernel(q_ref, k_ref, v_ref, kc_ref, vc_ref, bias_ref, o_ref, *, seq):
    n_rows = seq // GRID_W
    qw = NA_QROWS * GRID_W
    kw = NA_KROWS * GRID_W
    nblk = n_rows // NA_QROWS
    rows_per_chunk = ATTN_KCHUNK // GRID_W
    head_rows = [slice(HEAD_DIM * h, HEAD_DIM * (h + 1)) for h in range(NA_HEADS_PER_STEP)]
    ctx_chunks = [_key_chunks(kc_ref[0, hs, :], vc_ref[0, hs, :]) for hs in head_rows]

    def make_block(jj):
        h, j = divmod(jj, nblk)
        hs = head_rows[h]
        k_row = min(max(j * NA_QROWS - NA_WIN_R // 2, 0), n_rows - NA_KROWS)
        btype = 0 if j == 0 else (2 if j == nblk - 1 else 1)
        q0 = j * qw
        chunks = list(ctx_chunks[h])
        for cj in range(kw // ATTN_KCHUNK):
            first = k_row + cj * rows_per_chunk
            in_window = False
            for r in range(j * NA_QROWS, (j + 1) * NA_QROWS):
                r0 = min(max(r - NA_WIN_R // 2, 0), n_rows - NA_WIN_R)
                in_window = in_window or (first < r0 + NA_WIN_R and first + rows_per_chunk > r0)
            if in_window:
                k0 = first * GRID_W
                chunks.append((k_ref[0, hs, k0:k0 + ATTN_KCHUNK], v_ref[0, hs, k0:k0 + ATTN_KCHUNK],
                               bias_ref[h, btype, cj * ATTN_KCHUNK:(cj + 1) * ATTN_KCHUNK, :]))
        return q_ref[0, hs, q0:q0 + qw], chunks

    def store(jj, o):
        h, j = divmod(jj, nblk)
        o_ref[0, head_rows[h], j * qw:(j + 1) * qw] = o.astype(BF16)

    _attend_blocks(nblk * NA_HEADS_PER_STEP, make_block, None, store)


def _na(qn, kn, vn, kc, vc, bias):
    b, _, t = qn.shape
    lc = kc.shape[2]
    hw = HEAD_DIM * NA_HEADS_PER_STEP
    head = lambda i, h: (i, h, 0)
    return pl.pallas_call(
        functools.partial(_na_kernel, seq=t),
        out_shape=jax.ShapeDtypeStruct((b, NA_W, t), BF16),
        grid=(b, NA_HEADS // NA_HEADS_PER_STEP),
        in_specs=[
            pl.BlockSpec((1, hw, t), head),
            pl.BlockSpec((1, hw, t), head),
            pl.BlockSpec((1, hw, t), head),
            pl.BlockSpec((1, hw, lc), head),
            pl.BlockSpec((1, hw, lc), head),
            pl.BlockSpec((NA_HEADS_PER_STEP,) + bias.shape[1:], lambda i, h: (h, 0, 0, 0)),
        ],
        out_specs=pl.BlockSpec((1, hw, t), head),
        compiler_params=_cparams("parallel", "parallel"),
        name="na",
    )(qn, kn, vn, kc, vc, bias)


def _ctx_attn_kernel(qs_ref, ks_ref, vs_ref, qn_ref, kn_ref, vn_ref, sink_ref, ys_ref, yn_ref):
    lc = qs_ref.shape[2]
    for g in range(SWA_KV_HEADS):
        q_t = jnp.concatenate(
            [qs_ref[0, HEAD_DIM * (SWA_GROUP * g + h):HEAD_DIM * (SWA_GROUP * g + h + 1), :] for h in range(SWA_GROUP)],
            axis=1)
        kv = slice(HEAD_DIM * g, HEAD_DIM * (g + 1))
        o = _attend(q_t, _key_chunks(ks_ref[0, kv, :], vs_ref[0, kv, :]), sink_ref[g])
        for h in range(SWA_GROUP):
            hh = SWA_GROUP * g + h
            ys_ref[0, HEAD_DIM * hh:HEAD_DIM * (hh + 1), :] = o[:, lc * h:lc * (h + 1)].astype(BF16)
    for h in range(NA_HEADS):
        sl = slice(HEAD_DIM * h, HEAD_DIM * (h + 1))
        o = _attend(qn_ref[0, sl, :], _key_chunks(kn_ref[0, sl, :], vn_ref[0, sl, :]), None)
        yn_ref[0, sl, :] = o.astype(BF16)


def _ctx_attn(qs, ks, vs, qn, kn, vn, sink_rows):
    b, _, lc = qs.shape
    full = lambda a: pl.BlockSpec((1,) + a.shape[1:], lambda i: (i, 0, 0))
    return pl.pallas_call(
        _ctx_attn_kernel,
        out_shape=[jax.ShapeDtypeStruct((b, SWA_Q_W, lc), BF16), jax.ShapeDtypeStruct((b, NA_W, lc), BF16)],
        grid=(b,),
        in_specs=[full(qs), full(ks), full(vs), full(qn), full(kn), full(vn),
                  pl.BlockSpec(sink_rows.shape, lambda i: (0, 0, 0))],
        out_specs=[pl.BlockSpec((1, SWA_Q_W, lc), lambda i: (i, 0, 0)),
                   pl.BlockSpec((1, NA_W, lc), lambda i: (i, 0, 0))],
        compiler_params=_cparams("parallel"),
        name="ctx_attn",
    )(qs, ks, vs, qn, kn, vn, sink_rows)


def _route(logits):
    row = lax.broadcasted_iota(jnp.int32, logits.shape, 0)
    big = jnp.int32(logits.shape[0])
    colmax = lambda a: jnp.max(a, axis=0, keepdims=True)
    first = lambda hit: jnp.min(jnp.where(hit, row, big), axis=0, keepdims=True)
    gmask = row < N_GROUPS
    gl = jnp.where(gmask, logits, NEG_INF)
    gmax = colmax(gl)
    g_sel = first(gl == gmax)
    p_g = 1.0 / jnp.sum(jnp.where(gmask, jnp.exp(logits - gmax), 0.0), axis=0, keepdims=True)
    lo = N_GROUPS + EXPERTS_PER_GROUP * g_sel
    el = jnp.where((row >= lo) & (row < lo + EXPERTS_PER_GROUP), logits, NEG_INF)
    v1 = colmax(el)
    i1 = first(el == v1)
    el2 = jnp.where(row == i1, NEG_INF, el)
    v2 = colmax(el2)
    i2 = first(el2 == v2)
    e21 = jnp.exp(v2 - v1)
    w1 = p_g / (1.0 + e21)
    w2 = p_g * e21 / (1.0 + e21)
    out_row = lax.broadcasted_iota(jnp.int32, (ROUTE_OUT, logits.shape[1]), 0)
    out = jnp.where(out_row == 0, (i1 - N_GROUPS).astype(F32), 0.0)
    out = jnp.where(out_row == 1, (i2 - N_GROUPS).astype(F32), out)
    out = jnp.where(out_row == 2, w1, out)
    return jnp.where(out_row == 3, w2, out)


def _out_proj_kernel(x_ref, yf_ref, ys_ref, yn_ref, wo_ref, g1_ref, sh_ref, sc_ref, gn_ref, wrh_ref, wrl_ref,
                     br_ref, x1_ref, h2_ref, rt_ref):
    toks = [slice(i, i + PROJ_SUBTILE) for i in range(0, x_ref.shape[1], PROJ_SUBTILE)]
    ys = [_dot_tn(jnp.concatenate([yf_ref[0, :, tok], ys_ref[0, :, tok], yn_ref[0, :, tok]], axis=0), wo_ref[...])
          for tok in toks]
    for tok, y in zip(toks, ys):
        x1 = x_ref[0, tok, :] + g1_ref[0] * y
        x1_ref[0, tok, :] = x1
        ms = jnp.mean(x1 * x1, axis=-1, keepdims=True)
        h2 = (x1 * lax.rsqrt(ms + EPS) * gn_ref[...]) * (1.0 + sc_ref[0]) + sh_ref[0]
        h_hi = h2.astype(BF16)
        h2_ref[0, tok, :] = h_hi
        h_lo = (h2 - h_hi.astype(F32)).astype(BF16)
        logits = (_dot_nt(wrh_ref[...], h_hi) + (_dot_nt(wrh_ref[...], h_lo) + _dot_nt(wrl_ref[...], h_hi))
                  + br_ref[...])
        rt_ref[0, :, tok] = _route(logits)


def _out_proj(x, yf, ys, yn, w_out, gate, shift, scale, gain, wr_hi, wr_lo, b_route, *, tm):
    b, t, d = x.shape
    vec = pl.BlockSpec((1, 1, d), lambda i, j: (i, 0, 0))
    feat = lambda r: pl.BlockSpec((1, r, tm), lambda i, j: (i, 0, j))
    tok = lambda w: pl.BlockSpec((1, tm, w), lambda i, j: (i, j, 0))
    whole = lambda a: pl.BlockSpec(a.shape, lambda i, j: (0, 0))
    return pl.pallas_call(
        _out_proj_kernel,
        out_shape=[jax.ShapeDtypeStruct((b, t, d), F32), jax.ShapeDtypeStruct((b, t, d), BF16),
                   jax.ShapeDtypeStruct((b, ROUTE_OUT, t), F32)],
        grid=(b, t // tm),
        in_specs=[tok(d), feat(FOURIER_WIDTH), feat(SWA_Q_W), feat(NA_W), whole(w_out),
                  vec, vec, vec, pl.BlockSpec((1, d), lambda i, j: (0, 0)),
                  whole(wr_hi), whole(wr_lo), whole(b_route)],
        out_specs=[tok(d), tok(d), feat(ROUTE_OUT)],
        compiler_params=_cparams("parallel", "parallel"),
        name="out_proj",
    )(x, yf, ys, yn, w_out, gate, shift, scale, gain, wr_hi, wr_lo, b_route)


def _start_or_wait(copy, wait, priority=0):
    if wait:
        copy.wait()
    else:
        copy.start(priority=priority)


def _run_copies(local, remote, sem, runs_ref, c, *, to_remote, wait):
    if wait:
        n = pl.multiple_of(runs_ref[3, c, 0], ROW_BLOCK)
        loc, rem = local.at[pl.ds(0, n)], remote.at[pl.ds(0, n)]
        (pltpu.make_async_copy(loc, rem, sem) if to_remote else pltpu.make_async_copy(rem, loc, sem)).wait()
        return
    for e in range(N_EXPERTS):
        rows = runs_ref[0, c, e]

        @pl.when(rows > 0)
        def _(e=e, rows=rows):
            n = pl.multiple_of(rows, ROW_BLOCK)
            loc = local.at[pl.ds(pl.multiple_of(runs_ref[1, c, e], ROW_BLOCK), n)]
            rem = remote.at[pl.ds(pl.multiple_of(runs_ref[2, c, e], ROW_BLOCK), n)]
            (pltpu.make_async_copy(loc, rem, sem) if to_remote else pltpu.make_async_copy(rem, loc, sem)).start()


def _tail_copies(zero, remote, sem, tstart_ref, tn_ref, *, wait, priority=0):
    def body(e, carry):
        rows = tn_ref[e]

        @pl.when(rows > 0)
        def _():
            n = pl.multiple_of(rows, ROW_BLOCK)
            rem = remote.at[pl.ds(pl.multiple_of(tstart_ref[e], ROW_BLOCK), n)]
            _start_or_wait(pltpu.make_async_copy(zero.at[pl.ds(0, n)], rem, sem), wait, priority)

        return carry

    lax.fori_loop(0, N_EXPERTS, body, 0)


def _unused_tile_copies(zero, remote, sem, nt_ref, total_tiles, *, wait, priority=0):
    def body(t, carry):
        row = pl.multiple_of(t * MOE_TILE, MOE_TILE)
        _start_or_wait(pltpu.make_async_copy(zero, remote.at[pl.ds(row, MOE_TILE)], sem), wait, priority)
        return carry

    lax.fori_loop(nt_ref[0], total_tiles, body, 0)


def _experts_kernel(t0_ref, tcnt_ref, nt_ref, xs_ref, wg_ref, wu_ref, wd_ref, ys_ref,
                    xbuf, ybuf, zero, wg_s, wu_s, wd_s, xsem, ysem, zsem):
    e = pl.program_id(0)
    first_tile = t0_ref[e]
    n = tcnt_ref[e]
    n_tiles = nt_ref[0]
    total_tiles = xs_ref.shape[0] // MOE_TILE

    def tile_rows(ref, a):
        return ref.at[pl.ds(pl.multiple_of(a * MOE_TILE, MOE_TILE), MOE_TILE)]

    x_copy = lambda a: pltpu.make_async_copy(tile_rows(xs_ref, a), xbuf.at[a % 2], xsem.at[a % 2])
    y_copy = lambda a: pltpu.make_async_copy(ybuf.at[a % 2], tile_rows(ys_ref, a), ysem.at[a % 2])

    last = e == N_EXPERTS - 1

    @pl.when(last)
    def _():
        zero[...] = jnp.zeros(zero.shape, MOE_DT)
        _unused_tile_copies(zero, ys_ref, zsem, nt_ref, total_tiles, wait=False)

    @pl.when(n > 0)
    def _():
        @pl.when(first_tile == 0)
        def _():
            x_copy(first_tile).start(priority=1)

        wg_s[...] = wg_ref[0, 0].astype(BF16)
        wu_s[...] = wu_ref[0, 0].astype(BF16)
        wd_s[...] = wd_ref[0, 0].astype(BF16)

        def body(a, carry):
            slot = a % 2
            x_copy(a).wait()

            @pl.when(a + 1 < n_tiles)
            def _():
                x_copy(a + 1).start(priority=1)

            @pl.when(a >= 2)
            def _():
                y_copy(a - 2).wait()

            x = xbuf[slot].astype(BF16)
            g = _dot(x, wg_s[...])
            u = _dot(x, wu_s[...])
            hid = (g * (1.0 / (1.0 + jnp.exp(-g)))) * u
            ybuf[slot] = _dot(hid.astype(BF16), wd_s[...]).astype(MOE_DT)
            y_copy(a).start(priority=1)
            return carry

        lax.fori_loop(first_tile, first_tile + n, body, 0)

    @pl.when(last)
    def _():
        y_copy(n_tiles - 1).wait()

        @pl.when(n_tiles >= 2)
        def _():
            y_copy(n_tiles - 2).wait()

        _unused_tile_copies(zero, ys_ref, zsem, nt_ref, total_tiles, wait=True)


def _experts(xs, plan, w_gate, w_up, w_down, layer):
    r, d = xs.shape
    de = w_gate.shape[3]
    wspec = lambda a, b: pl.BlockSpec((1, 1, a, b), lambda e, *_: (layer, e, 0, 0))
    any_spec = pl.BlockSpec(memory_space=pl.ANY)
    return pl.pallas_call(
        _experts_kernel,
        out_shape=jax.ShapeDtypeStruct((r, d), MOE_DT),
        grid_spec=pltpu.PrefetchScalarGridSpec(
            num_scalar_prefetch=3,
            grid=(N_EXPERTS,),
            in_specs=[any_spec, wspec(d, de), wspec(d, de), wspec(de, d)],
            out_specs=any_spec,
            scratch_shapes=[pltpu.VMEM((2, MOE_TILE, d), MOE_DT), pltpu.VMEM((2, MOE_TILE, d), MOE_DT),
                            pltpu.VMEM((MOE_TILE, d), MOE_DT),
                            pltpu.VMEM((d, de), BF16), pltpu.VMEM((d, de), BF16), pltpu.VMEM((de, d), BF16),
                            pltpu.SemaphoreType.DMA((2,)), pltpu.SemaphoreType.DMA((2,)),
                            pltpu.SemaphoreType.DMA(())],
        ),
        compiler_params=_cparams("arbitrary"),
        name="experts",
    )(plan["tile0"], plan["tcnt"], plan["n_tiles"], xs, w_gate, w_up, w_down)


def _dispatch_kernel(runs_ref, tstart_ref, tn_ref, nt_ref, *rest, part_chunks):
    n_parts = len(part_chunks)
    h_refs = rest[:n_parts]
    pos_ref, xs_ref, buf, zero, sem, zsem = rest[n_parts:]
    n_chunks = sum(part_chunks)
    c = pl.program_id(0)
    slot = c % 2
    total_tiles = xs_ref.shape[0] // MOE_TILE

    @pl.when(c == 0)
    def _():
        zero[...] = jnp.zeros(zero.shape, MOE_DT)
        _tail_copies(zero, xs_ref, zsem, tstart_ref, tn_ref, wait=False, priority=1)
        _unused_tile_copies(zero, xs_ref, zsem, nt_ref, total_tiles, wait=False, priority=1)

    row = lax.broadcasted_iota(jnp.int32, (MOE_LROWS, MOE_CHUNK), 0)
    hit = (row == pos_ref[0, 0:1, :]) | (row == pos_ref[0, 1:2, :])
    sel = jnp.where(hit, 1.0, 0.0).astype(BF16)
    first = 0
    for h_ref, n in zip(h_refs, part_chunks):
        @pl.when((c >= first) & (c < first + n))
        def _(h_ref=h_ref):
            buf[slot] = _dot(sel, h_ref[...]).astype(MOE_DT)
        first += n
    _run_copies(buf.at[slot], xs_ref, sem.at[slot], runs_ref, c, to_remote=True, wait=False)

    @pl.when(c > 0)
    def _():
        _run_copies(buf.at[1 - slot], xs_ref, sem.at[1 - slot], runs_ref, c - 1, to_remote=True, wait=True)

    @pl.when(c == n_chunks - 1)
    def _():
        _run_copies(buf.at[slot], xs_ref, sem.at[slot], runs_ref, c, to_remote=True, wait=True)
        _tail_copies(zero, xs_ref, zsem, tstart_ref, tn_ref, wait=True)
        _unused_tile_copies(zero, xs_ref, zsem, nt_ref, total_tiles, wait=True)


def _dispatch(h_parts, plan):
    d = h_parts[0].shape[1]
    part_chunks = tuple(h.shape[0] // MOE_CHUNK for h in h_parts)
    in_specs = []
    first = 0
    for n in part_chunks:
        in_specs.append(pl.BlockSpec(
            (MOE_CHUNK, d), lambda i, *_, first=first, n=n: (jnp.clip(i - first, 0, n - 1), 0)))
        first += n
    in_specs.append(pl.BlockSpec((1, 8, MOE_CHUNK), lambda i, *_: (i, 0, 0)))
    return pl.pallas_call(
        functools.partial(_dispatch_kernel, part_chunks=part_chunks),
        out_shape=jax.ShapeDtypeStruct((plan["rows"], d), MOE_DT),
        grid_spec=pltpu.PrefetchScalarGridSpec(
            num_scalar_prefetch=4,
            grid=(sum(part_chunks),),
            in_specs=in_specs,
            out_specs=pl.BlockSpec(memory_space=pl.ANY),
            scratch_shapes=[pltpu.VMEM((2, MOE_LROWS, d), MOE_DT), pltpu.VMEM((MOE_TILE, d), MOE_DT),
                            pltpu.SemaphoreType.DMA((2,)), pltpu.SemaphoreType.DMA(())],
        ),
        compiler_params=_cparams("arbitrary"),
        name="dispatch",
    )(plan["runs"], plan["tstart"], plan["tn"], plan["n_tiles"], *h_parts, plan["pos"])


def _combine_kernel(runs_ref, x1_ref, pos_ref, wt_ref, g2_ref, gf_ref, ys_ref, o_ref, buf, sem,
                    *, c0, n_chunks, final_norm):
    i = pl.program_id(0)
    c = i + c0
    slot = i % 2
    gather = functools.partial(_run_copies, remote=ys_ref, runs_ref=runs_ref, to_remote=False)

    @pl.when(i == 0)
    def _():
        buf[...] = jnp.zeros(buf.shape, MOE_DT)
        gather(buf.at[slot], sem=sem.at[slot], c=c, wait=False)

    @pl.when(i + 1 < n_chunks)
    def _():
        gather(buf.at[1 - slot], sem=sem.at[1 - slot], c=c + 1, wait=False)

    gather(buf.at[slot], sem=sem.at[slot], c=c, wait=True)
    row = lax.broadcasted_iota(jnp.int32, (MOE_LROWS, MOE_CHUNK), 0)
    hit1 = row == pos_ref[0, 0:1, :]
    hit2 = row == pos_ref[0, 1:2, :]
    w_row = jnp.sum(jnp.where(hit1, wt_ref[0, 0:1, :], 0.0) + jnp.where(hit2, wt_ref[0, 1:2, :], 0.0),
                    axis=1, keepdims=True)
    sel = jnp.where(hit1 | hit2, 1.0, 0.0).astype(BF16)
    x2 = x1_ref[0] + g2_ref[0] * _dot_tn(sel, (buf[slot] * w_row).astype(BF16))
    if final_norm:
        ms = jnp.mean(x2 * x2, axis=-1, keepdims=True)
        x2 = x2 * lax.rsqrt(ms + EPS) * gf_ref[...]
    o_ref[0] = x2


def _combine(x1, plan, gate, g_final, ys, *, c0, final_norm):
    b, t, d = x1.shape
    per_batch = t // MOE_CHUNK
    n_chunks = b * per_batch
    tok = lambda w: pl.BlockSpec((1, MOE_CHUNK, w), lambda i, *_: (i // per_batch, i % per_batch, 0))
    return pl.pallas_call(
        functools.partial(_combine_kernel, c0=c0, n_chunks=n_chunks, final_norm=final_norm),
        out_shape=jax.ShapeDtypeStruct((b, t, d), F32),
        grid_spec=pltpu.PrefetchScalarGridSpec(
            num_scalar_prefetch=1,
            grid=(n_chunks,),
            in_specs=[tok(d),
                      pl.BlockSpec((1, 8, MOE_CHUNK), lambda i, *_: (i + c0, 0, 0)),
                      pl.BlockSpec((1, 8, MOE_CHUNK), lambda i, *_: (i + c0, 0, 0)),
                      pl.BlockSpec((1, 1, d), lambda i, *_: (i // per_batch, 0, 0)),
                      pl.BlockSpec((1, d), lambda i, *_: (0, 0)),
                      pl.BlockSpec(memory_space=pl.ANY)],
            out_specs=tok(d),
            scratch_shapes=[pltpu.VMEM((2, MOE_LROWS, d), MOE_DT), pltpu.SemaphoreType.DMA((2,))],
        ),
        compiler_params=_cparams("arbitrary"),
        name="combine",
    )(plan["runs"], x1, plan["pos"], plan["wts"], gate, g_final, ys)


def _rope_tables(seq):
    half = HEAD_DIM // 4
    t = jnp.arange(seq)
    rows, cols = t // GRID_W, t % GRID_W
    freqs = ROPE_BASE ** (-jnp.arange(half, dtype=F32) / half)

    def cs(pos):
        ang = pos.astype(F32)[None, :] * freqs[:, None]
        return jnp.cos(ang), jnp.sin(ang)

    cr, sr = cs(rows)
    cc, sc = cs(cols)
    return jnp.concatenate([cr, cr, cc, cc], axis=0), jnp.concatenate([-sr, sr, -sc, sc], axis=0)


def _dft_tables(n):
    k = jnp.arange(n)

    def cs(m):
        ang = ((m[:, None] * k[None, :]) % n).astype(F32) * (2.0 * jnp.pi / n)
        return jnp.cos(ang), jnp.sin(ang)

    if n <= GRID_W:
        return cs(k)
    ca, sa = cs(jnp.arange(n // GRID_W) * GRID_W)
    cb, sb = cs(jnp.arange(GRID_W))
    c = ca[:, None, :] * cb[None, :, :] - sa[:, None, :] * sb[None, :, :]
    s = sa[:, None, :] * cb[None, :, :] + ca[:, None, :] * sb[None, :, :]
    return c.reshape(n, n), s.reshape(n, n)


def _channel_dft_t():
    c, s = _dft_tables(FOURIER_GROUP_DIM)
    eye = jnp.eye(FOURIER_GROUPS, dtype=F32)
    scale = FOURIER_GROUP_DIM ** -0.5
    return jnp.concatenate([jnp.kron(eye, c), jnp.kron(eye, s)], axis=0) * scale


def _position_dft(n):
    c, s = _dft_tables(n)
    scale = n ** -0.5
    return (c * scale).astype(BF16), (s * scale).astype(BF16)


def _swa_mask():
    kk = jnp.arange(SWA_BLOCK)[:, None]
    q = jnp.arange(SWA_BLOCK)[None, :]
    tiles = []
    for block_offset in (-SWA_BLOCK, SWA_BLOCK):
        ok = jnp.abs(block_offset + kk - q) <= SWA_WINDOW
        tiles.append(jnp.tile(jnp.where(ok, 0.0, NEG_INF).astype(F32), (1, SWA_GROUP)))
    return jnp.stack(tiles)


def _sink_rows(sink, width):
    return jnp.repeat(sink.astype(F32).reshape(SWA_KV_HEADS, SWA_GROUP) * LOG2E, width, axis=1)[:, None, :]


def _route_chunks(route):
    r = jnp.moveaxis(route, 1, 0).reshape(ROUTE_OUT, -1, MOE_CHUNK)
    return jnp.swapaxes(r, 0, 1)


def _moe_plan(route_chunks):
    i32 = jnp.int32
    nc = route_chunks.shape[0]
    n = nc * MOE_CHUNK
    ids = route_chunks[:, 0:2, :].astype(i32).reshape(nc, 2 * MOE_CHUNK)
    onehot = (ids[:, :, None] == jnp.arange(N_EXPERTS, dtype=i32)).astype(i32)
    pair = jnp.arange(2 * MOE_CHUNK, dtype=i32)
    earlier = (pair[None, :] < pair[:, None]).astype(BF16)
    before = jnp.einsum("pq,cqe->cpe", earlier, onehot.astype(BF16), preferred_element_type=F32).astype(i32)
    rank = jnp.sum(before * onehot, axis=-1)
    cnt = jnp.sum(onehot, axis=1)
    run = (cnt + ROW_BLOCK - 1) // ROW_BLOCK * ROW_BLOCK
    lo = jnp.cumsum(run, axis=1) - run
    pos = (jnp.sum(onehot * lo[:, None, :], axis=-1) + rank).reshape(nc, 2, MOE_CHUNK)
    pos = jnp.concatenate([pos, jnp.full((nc, 6, MOE_CHUNK), -1, i32)], axis=1)
    seg = jnp.sum(run, axis=0)
    padded = (seg + MOE_TILE - 1) // MOE_TILE * MOE_TILE
    start = jnp.cumsum(padded) - padded
    off = start[None, :] + jnp.cumsum(run, axis=0) - run
    wts = jnp.concatenate([route_chunks[:, 2:4, :], jnp.zeros((nc, 6, MOE_CHUNK), F32)], axis=1)
    rows = -(-(2 * n + (ROW_BLOCK - 1) * N_EXPERTS * nc + (MOE_TILE - 1) * N_EXPERTS) // MOE_TILE) * MOE_TILE
    n_tiles = jnp.sum(padded) // MOE_TILE
    total = jnp.broadcast_to(jnp.sum(run, axis=1, keepdims=True), run.shape)
    return {"pos": pos, "wts": wts, "runs": jnp.stack([run, lo, off, total]).astype(i32),
            "tstart": (start + seg).astype(i32), "tn": (padded - seg).astype(i32),
            "tile0": (start // MOE_TILE).astype(i32), "tcnt": (padded // MOE_TILE).astype(i32),
            "n_tiles": n_tiles.astype(i32).reshape(1), "rows": rows}


def kernel(x, c, ctx, c_ctx, w_mod, b_mod, g_norm1, g_norm2, w_in, w_four, w_out, swa_sink, na_rpb,
           w_route_group, b_route_group, w_route_expert, b_route_expert, w_exp_gate, w_exp_up,
           w_exp_down, g_final):
    b, s, d = x.shape
    lc = ctx.shape[1]
    depth = w_mod.shape[0]
    tm = 512

    c_rows = jnp.concatenate([c, c_ctx[None, :], jnp.zeros((7, d), F32)], axis=0)
    mod = _modulation(c_rows, w_mod, b_mod)

    cos_t, sin_t = _rope_tables(s)
    cos_c, sin_c = cos_t[:, :lc], sin_t[:, :lc]
    bd_t = _channel_dft_t().astype(BF16)
    cn, sn = _position_dft(s)
    cn_c, sn_c = _position_dft(lc)
    mask = _swa_mask()
    route_pad = ROUTE_ROWS - N_GROUPS - N_EXPERTS

    xc = ctx
    for layer in range(depth):
        with_ctx_out = layer < depth - 1
        lat = [mod[layer, :b, i * d:(i + 1) * d][:, None, :] for i in range(6)]
        cx = [jnp.broadcast_to(mod[layer, b, i * d:(i + 1) * d][None, None, :], (b, 1, d)) for i in range(6)]
        sh1, sc1, g1, sh2, sc2, g2 = lat
        shc1, scc1, gc1, shc2, scc2, gc2 = cx
        gn1 = g_norm1[layer][None, :]
        gn2 = g_norm2[layer][None, :]
        w_t = w_in[layer].T.astype(BF16)
        wf_t = w_four[layer].T.astype(BF16)
        wo = w_out[layer].astype(BF16)
        w_r = jnp.concatenate([w_route_group[layer].T, w_route_expert[layer].T,
                               jnp.zeros((route_pad, d), F32)], axis=0)
        wr_hi = w_r.astype(BF16)
        wr_lo = (w_r - wr_hi.astype(F32)).astype(BF16)
        b_r = jnp.concatenate([b_route_group[layer], b_route_expert[layer], jnp.zeros((route_pad,), F32)])[:, None]
        sink_lat = _sink_rows(swa_sink[layer], SWA_BLOCK)
        sink_ctx = _sink_rows(swa_sink[layer], lc)

        fz, qs, qn, ks, vs, kn, vn = _in_proj(x, sh1, sc1, gn1, w_t, bd_t, cos_t, sin_t,
                                              with_q=True, rope=True, tm=2 * tm)
        if with_ctx_out:
            fz_c, qs_c, qn_c, ks_c, vs_c, kn_c, vn_c = _in_proj(xc, shc1, scc1, gn1, w_t, bd_t, cos_c, sin_c,
                                                                with_q=True, rope=False, tm=lc)
        else:
            ks_c, vs_c, kn_c, vn_c = _in_proj(xc, shc1, scc1, gn1, w_t[KS_LO:], bd_t, cos_c, sin_c,
                                              with_q=False, rope=False, tm=lc)

        yf = _fourier(fz, cn, sn, wf_t, tk=1024)
        ys = _swa(qs, ks, vs, ks_c, vs_c, mask, sink_lat)
        yn = _na(qn, kn, vn, kn_c, vn_c, _na_bias(na_rpb[layer], s))
        x1, h2, route = _out_proj(x, yf, ys, yn, wo, g1, sh2, sc2, gn2, wr_hi, wr_lo, b_r, tm=2 * tm)

        moe_w = (w_exp_gate, w_exp_up, w_exp_down, layer)
        if with_ctx_out:
            yf_c = _fourier(fz_c, cn_c, sn_c, wf_t, tk=lc)
            ys_c, yn_c = _ctx_attn(qs_c, ks_c, vs_c, qn_c, kn_c, vn_c, sink_ctx)
            xc1, hc2, route_c = _out_proj(xc, yf_c, ys_c, yn_c, wo, gc1, shc2, scc2, gn2, wr_hi, wr_lo, b_r, tm=lc)
            n_lat = b * s
            lat_chunks = n_lat // MOE_CHUNK
            plan = _moe_plan(jnp.concatenate([_route_chunks(route), _route_chunks(route_c)], axis=0))
            xs = _dispatch([h2.reshape(n_lat, d), hc2.reshape(b * lc, d)], plan)
            ye = _experts(xs, plan, *moe_w)
            x = _combine(x1, plan, g2, g_final[None, :], ye, c0=0, final_norm=False)
            ctx_chunks = b * lc // MOE_CHUNK
            xc = _combine(xc1.reshape(ctx_chunks, MOE_CHUNK, d), plan, gc2[:ctx_chunks], g_final[None, :], ye,
                          c0=lat_chunks, final_norm=False).reshape(b, lc, d)
        else:
            plan = _moe_plan(_route_chunks(route))
            xs = _dispatch([h2.reshape(b * s, d)], plan)
            ye = _experts(xs, plan, *moe_w)
            x = _combine(x1, plan, g2, g_final[None, :], ye, c0=0, final_norm=True)
    return x
```

```python
import functools

import jax
import jax.numpy as jnp
import numpy as np
from jax import lax
from jax.experimental import pallas as pl
from jax.experimental.pallas import tpu as pltpu

F32 = jnp.float32
BF16 = jnp.bfloat16

D_MODEL = 1024
GRID_W = 64
HEAD_DIM = 64
FOURIER_WIDTH = D_MODEL // 4
FOURIER_GROUPS = 4
FOURIER_GROUP_DIM = FOURIER_WIDTH // FOURIER_GROUPS
SWA_HEADS = (3 * D_MODEL // 8) // HEAD_DIM
SWA_KV_HEADS = 2
SWA_GROUP = SWA_HEADS // SWA_KV_HEADS
SWA_WINDOW = 128
SWA_BLOCK = 128
NA_HEADS = (3 * D_MODEL // 8) // HEAD_DIM
NA_WIN_R = 8
NA_WIN_C = 16
ROPE_BASE = 10000.0
N_GROUPS = 4
EXPERTS_PER_GROUP = 8
N_EXPERTS = N_GROUPS * EXPERTS_PER_GROUP
D_EXPERT = D_MODEL // 2
EPS = 1e-6
NEG_INF = -1e30
LOG2E = 1.4426950408889634

SWA_Q_W = SWA_HEADS * HEAD_DIM
SWA_KV_W = SWA_KV_HEADS * HEAD_DIM
NA_W = NA_HEADS * HEAD_DIM
MIX_WIDTH = FOURIER_WIDTH + SWA_Q_W + NA_W
Q_COLS = MIX_WIDTH
IN_COLS = 2 * MIX_WIDTH

F_LO, F_HI = 0, FOURIER_WIDTH
QS_LO, QS_HI = F_HI, F_HI + SWA_Q_W
QN_LO, QN_HI = QS_HI, QS_HI + NA_W
KS_LO, KS_HI = QN_HI, QN_HI + SWA_KV_W
VS_LO, VS_HI = KS_HI, KS_HI + SWA_KV_W
KN_LO, KN_HI = VS_HI, VS_HI + NA_W
VN_LO, VN_HI = KN_HI, KN_HI + NA_W

LANE = 128
SUBLANE = 8
ROUTE_ROWS = -(-(N_GROUPS + N_EXPERTS) // SUBLANE) * SUBLANE
ROUTE_OUT = SUBLANE
NA_QROWS = 4
NA_KROWS = NA_QROWS + NA_WIN_R
ATTN_KCHUNK = LANE
ATTN_LOOKAHEAD = 2
NA_HEADS_PER_STEP = 2
PROJ_SUBTILE = LANE
MOE_TILE = 512
MOE_DT = F32
ROW_BLOCK = 8
MOE_CHUNK = 512
MOE_LROWS = 2 * MOE_CHUNK + (ROW_BLOCK - 1) * N_EXPERTS
VMEM_LIMIT = 48 * 1024 * 1024


def _cparams(*sem):
    return pltpu.CompilerParams(dimension_semantics=sem, vmem_limit_bytes=VMEM_LIMIT)


def _dot(a, b):
    return jnp.dot(a, b, preferred_element_type=F32)


def _dot_tn(a, b):
    return lax.dot_general(a, b, (((0,), (0,)), ((), ())), preferred_element_type=F32)


def _dot_nt(a, b):
    return lax.dot_general(a, b, (((1,), (1,)), ((), ())), preferred_element_type=F32)


def _split_dot(a, w):
    a_hi = a.astype(BF16)
    a_lo = (a - a_hi.astype(F32)).astype(BF16)
    w_hi = w.astype(BF16)
    w_lo = (w - w_hi.astype(F32)).astype(BF16)
    return _dot(a_hi, w_hi) + (_dot(a_hi, w_lo) + _dot(a_lo, w_hi))


def _mod_kernel(c_ref, w_ref, b_ref, o_ref):
    c = c_ref[...]
    a = c * (1.0 / (1.0 + jnp.exp(-c)))
    o_ref[0] = _split_dot(a, w_ref[0]) + b_ref[0]


def _modulation(c_rows, w_mod, b_mod):
    depth, d, n6 = w_mod.shape
    r = c_rows.shape[0]
    tn = 1536
    return pl.pallas_call(
        _mod_kernel,
        out_shape=jax.ShapeDtypeStruct((depth, r, n6), F32),
        grid=(depth, n6 // tn),
        in_specs=[
            pl.BlockSpec((r, d), lambda l, j: (0, 0)),
            pl.BlockSpec((1, d, tn), lambda l, j: (l, 0, j)),
            pl.BlockSpec((1, 1, tn), lambda l, j: (l, 0, j)),
        ],
        out_specs=pl.BlockSpec((1, r, tn), lambda l, j: (l, 0, j)),
        compiler_params=_cparams("parallel", "parallel"),
        name="modulation",
    )(c_rows, w_mod, b_mod.reshape(depth, 1, n6))


def _rope_rows(t, cos_t, sin_t, n_heads):
    outs = []
    for h in range(n_heads):
        th = t[HEAD_DIM * h:HEAD_DIM * (h + 1)]
        sw = jnp.concatenate([th[16:32], th[0:16], th[48:64], th[32:48]], axis=0)
        outs.append(th * cos_t + sw * sin_t)
    return jnp.concatenate(outs, axis=0)


def _in_proj_kernel(x_ref, sh_ref, sc_ref, g_ref, wt_ref, bdt_ref, cos_ref, sin_ref, *outs, with_q, rope):
    xf = x_ref[0]
    ms = jnp.mean(xf * xf, axis=-1, keepdims=True)
    y = xf * lax.rsqrt(ms + EPS) * g_ref[...]
    h = y * (1.0 + sc_ref[0]) + sh_ref[0]
    pt = _dot_nt(wt_ref[...], h.astype(BF16))
    q_scale = HEAD_DIM ** -0.5 * LOG2E
    if with_q:
        fz_ref, qs_ref, qn_ref, ks_ref, vs_ref, kn_ref, vn_ref = outs
        fz_ref[0] = _dot(bdt_ref[...], pt[F_LO:F_HI].astype(BF16)).astype(BF16)
        qs = pt[QS_LO:QS_HI]
        if rope:
            qs = _rope_rows(qs, cos_ref[...], sin_ref[...], SWA_HEADS)
        qs_ref[0] = (qs * q_scale).astype(BF16)
        qn_ref[0] = (pt[QN_LO:QN_HI] * q_scale).astype(BF16)
        off = 0
    else:
        ks_ref, vs_ref, kn_ref, vn_ref = outs
        off = KS_LO
    ks = pt[KS_LO - off:KS_HI - off]
    if rope:
        ks = _rope_rows(ks, cos_ref[...], sin_ref[...], SWA_KV_HEADS)
    ks_ref[0] = ks.astype(BF16)
    vs_ref[0] = pt[VS_LO - off:VS_HI - off].astype(BF16)
    kn_ref[0] = pt[KN_LO - off:KN_HI - off].astype(BF16)
    vn_ref[0] = pt[VN_LO - off:VN_HI - off].astype(BF16)


def _in_proj(x, shift, scale, gain, w_t, bd_t, cos_t, sin_t, *, with_q, rope, tm):
    b, t, d = x.shape
    nf = w_t.shape[0]
    rows = ([2 * FOURIER_WIDTH, SWA_Q_W, NA_W] if with_q else []) + [SWA_KV_W, SWA_KV_W, NA_W, NA_W]
    return pl.pallas_call(
        functools.partial(_in_proj_kernel, with_q=with_q, rope=rope),
        out_shape=[jax.ShapeDtypeStruct((b, r, t), BF16) for r in rows],
        grid=(b, t // tm),
        in_specs=[
            pl.BlockSpec((1, tm, d), lambda i, j: (i, j, 0)),
            pl.BlockSpec((1, 1, d), lambda i, j: (i, 0, 0)),
            pl.BlockSpec((1, 1, d), lambda i, j: (i, 0, 0)),
            pl.BlockSpec((1, d), lambda i, j: (0, 0)),
            pl.BlockSpec((nf, d), lambda i, j: (0, 0)),
            pl.BlockSpec(bd_t.shape, lambda i, j: (0, 0)),
            pl.BlockSpec((HEAD_DIM, tm), lambda i, j: (0, j)),
            pl.BlockSpec((HEAD_DIM, tm), lambda i, j: (0, j)),
        ],
        out_specs=[pl.BlockSpec((1, r, tm), lambda i, j: (i, 0, j)) for r in rows],
        compiler_params=_cparams("parallel", "parallel"),
        name="in_proj_q" if with_q else "in_proj_kv",
    )(x, shift, scale, gain, w_t, bd_t, cos_t, sin_t)


def _fourier_kernel(fz_ref, cn_ref, sn_ref, wft_ref, o_ref):
    zc = fz_ref[0, 0:FOURIER_WIDTH, :]
    zs = fz_ref[0, FOURIER_WIDTH:2 * FOURIER_WIDTH, :]
    y = _dot(zc, cn_ref[...]) - _dot(zs, sn_ref[...])
    o_ref[0] = _dot(wft_ref[...], y.astype(BF16)).astype(BF16)


def _fourier(fz, cn, sn, wf_t, *, tk):
    b, _, t = fz.shape
    return pl.pallas_call(
        _fourier_kernel,
        out_shape=jax.ShapeDtypeStruct((b, FOURIER_WIDTH, t), BF16),
        grid=(t // tk, b),
        in_specs=[
            pl.BlockSpec((1, 2 * FOURIER_WIDTH, t), lambda k, i: (i, 0, 0)),
            pl.BlockSpec((t, tk), lambda k, i: (0, k)),
            pl.BlockSpec((t, tk), lambda k, i: (0, k)),
            pl.BlockSpec((FOURIER_WIDTH, FOURIER_WIDTH), lambda k, i: (0, 0)),
        ],
        out_specs=pl.BlockSpec((1, FOURIER_WIDTH, tk), lambda k, i: (i, 0, k)),
        compiler_params=_cparams("parallel", "parallel"),
        name="fourier",
    )(fz, cn, sn, wf_t)


def _key_chunks(k, v, bias=None):
    n = k.shape[1] // ATTN_KCHUNK
    cut = lambda a, j, axis: lax.slice_in_dim(a, j * ATTN_KCHUNK, (j + 1) * ATTN_KCHUNK, axis=axis)
    return [(cut(k, j, 1), cut(v, j, 1), None if bias is None else cut(bias, j, 0)) for j in range(n)]


def _logits(q_t, chunks):
    return _dot_tn(jnp.concatenate([k_t for k_t, _, _ in chunks], axis=1), q_t)


def _softmax_pv(s, chunks, sink_row):
    pieces = []
    off = 0
    for k_t, _, bias in chunks:
        piece = s[off:off + k_t.shape[1]]
        pieces.append(piece if bias is None else piece + bias)
        off += k_t.shape[1]
    m = functools.reduce(jnp.maximum, [jnp.max(p, axis=0, keepdims=True) for p in pieces])
    if sink_row is not None:
        m = jnp.maximum(m, sink_row)
    probs = [jnp.exp2(p - m) for p in pieces]
    den = functools.reduce(jnp.add, [jnp.sum(p, axis=0, keepdims=True) for p in probs])
    if sink_row is not None:
        den = den + jnp.exp2(sink_row - m)
    v_all = jnp.concatenate([v_t for _, v_t, _ in chunks], axis=1)
    p_all = jnp.concatenate([p.astype(BF16) for p in probs], axis=0)
    return _dot(v_all, p_all) / den


def _attend(q_t, chunks, sink_row):
    return _softmax_pv(_logits(q_t, chunks), chunks, sink_row)


def _attend_blocks(n_blocks, make_block, sink_row, store):
    blocks, logits = {}, {}
    for j in range(min(ATTN_LOOKAHEAD, n_blocks)):
        blocks[j] = make_block(j)
        logits[j] = _logits(*blocks[j])
    for j in range(n_blocks):
        ahead = j + ATTN_LOOKAHEAD
        if ahead < n_blocks:
            blocks[ahead] = make_block(ahead)
            logits[ahead] = _logits(*blocks[ahead])
        store(j, _softmax_pv(logits.pop(j), blocks.pop(j)[1], sink_row))


def _swa_kernel(q_ref, k_ref, v_ref, kc_ref, vc_ref, mask_ref, sink_ref, o_ref, *, seq):
    nb = seq // SWA_BLOCK
    ctx_chunks = _key_chunks(kc_ref[0], vc_ref[0])

    def make_block(n):
        q0 = n * SWA_BLOCK
        q_t = jnp.concatenate(
            [q_ref[0, HEAD_DIM * h:HEAD_DIM * (h + 1), q0:q0 + SWA_BLOCK] for h in range(SWA_GROUP)], axis=1)
        chunks = [(k_ref[0, :, q0:q0 + SWA_BLOCK], v_ref[0, :, q0:q0 + SWA_BLOCK], None)] + ctx_chunks
        for side, kb in ((0, n - 1), (1, n + 1)):
            if 0 <= kb < nb:
                k0 = kb * SWA_BLOCK
                chunks.append((k_ref[0, :, k0:k0 + SWA_BLOCK], v_ref[0, :, k0:k0 + SWA_BLOCK], mask_ref[side]))
        return q_t, chunks

    def store(n, o):
        q0 = n * SWA_BLOCK
        for h in range(SWA_GROUP):
            o_ref[0, HEAD_DIM * h:HEAD_DIM * (h + 1), q0:q0 + SWA_BLOCK] = (
                o[:, SWA_BLOCK * h:SWA_BLOCK * (h + 1)].astype(BF16))

    _attend_blocks(nb, make_block, sink_ref[0], store)


def _swa(qs, ks, vs, kc, vc, mask, sink_rows):
    b, _, t = qs.shape
    lc = kc.shape[2]
    gw = SWA_GROUP * HEAD_DIM
    return pl.pallas_call(
        functools.partial(_swa_kernel, seq=t),
        out_shape=jax.ShapeDtypeStruct((b, SWA_Q_W, t), BF16),
        grid=(b, SWA_KV_HEADS),
        in_specs=[
            pl.BlockSpec((1, gw, t), lambda i, g: (i, g, 0)),
            pl.BlockSpec((1, HEAD_DIM, t), lambda i, g: (i, g, 0)),
            pl.BlockSpec((1, HEAD_DIM, t), lambda i, g: (i, g, 0)),
            pl.BlockSpec((1, HEAD_DIM, lc), lambda i, g: (i, g, 0)),
            pl.BlockSpec((1, HEAD_DIM, lc), lambda i, g: (i, g, 0)),
            pl.BlockSpec((2, SWA_BLOCK, SWA_GROUP * SWA_BLOCK), lambda i, g: (0, 0, 0)),
            pl.BlockSpec((1, 1, SWA_GROUP * SWA_BLOCK), lambda i, g: (g, 0, 0)),
        ],
        out_specs=pl.BlockSpec((1, gw, t), lambda i, g: (i, g, 0)),
        compiler_params=_cparams("parallel", "parallel"),
        name="swa",
    )(qs, ks, vs, kc, vc, mask, sink_rows)


def _na_bias_kernel(rpb_ref, o_ref, u_ref, *, total_rows):
    hd = pl.program_id(0)
    kc = lax.broadcasted_iota(jnp.int32, (GRID_W, LANE), 0)
    lane = lax.broadcasted_iota(jnp.int32, (GRID_W, LANE), 1)
    qc = lane % GRID_W
    dc = jnp.clip(kc - qc, -(NA_WIN_C - 1), NA_WIN_C - 1) + (NA_WIN_C - 1)
    c0 = jnp.clip(qc - NA_WIN_C // 2, 0, GRID_W - NA_WIN_C)
    valid_c = (kc >= c0) & (kc < c0 + NA_WIN_C)
    n_dr = 2 * NA_WIN_R - 1
    for dr in range(n_dr):
        u = jnp.full((GRID_W, LANE), NEG_INF, F32)
        for d in range(2 * NA_WIN_C - 1):
            u = jnp.where(valid_c & (dc == d), rpb_ref[hd, dr, d] * LOG2E, u)
        u_ref[dr] = u
    n_rows = o_ref.shape[2] // GRID_W
    block_types = [(0, 0), (NA_QROWS, 0), (total_rows - NA_QROWS, total_rows - NA_KROWS)]
    neg = jnp.full((GRID_W, LANE), NEG_INF, F32)
    for t, (r_base, k_base) in enumerate(block_types):
        for kl in range(n_rows):
            kr = k_base + kl
            for lg in range(NA_QROWS // 2):
                halves = []
                for rq in (2 * lg, 2 * lg + 1):
                    r = r_base + rq
                    r0 = min(max(r - NA_WIN_R // 2, 0), total_rows - NA_WIN_R)
                    ok = r0 <= kr < r0 + NA_WIN_R
                    halves.append(u_ref[kr - r + NA_WIN_R - 1] if ok else neg)
                o_ref[0, t, GRID_W * kl:GRID_W * (kl + 1), LANE * lg:LANE * (lg + 1)] = jnp.where(
                    lane < GRID_W, halves[0], halves[1])


def _na_bias(rpb, seq):
    nh = rpb.shape[0]
    return pl.pallas_call(
        functools.partial(_na_bias_kernel, total_rows=seq // GRID_W),
        out_shape=jax.ShapeDtypeStruct((nh, 3, NA_KROWS * GRID_W, NA_QROWS * GRID_W), F32),
        grid=(nh,),
        in_specs=[pl.BlockSpec(memory_space=pltpu.SMEM)],
        out_specs=pl.BlockSpec((1, 3, NA_KROWS * GRID_W, NA_QROWS * GRID_W), lambda h: (h, 0, 0, 0)),
        scratch_shapes=[pltpu.VMEM((2 * NA_WIN_R - 1, GRID_W, LANE), F32)],
        compiler_params=_cparams("parallel"),
        name="na_bias",
    )(rpb)


def _na_kernel(q_ref, k_ref, v_ref, kc_ref, vc_ref, bias_ref, o_ref, *, seq):
    n_rows = seq // GRID_W
    qw = NA_QROWS * GRID_W
    kw = NA_KROWS * GRID_W
    nblk = n_rows // NA_QROWS
    rows_per_chunk = ATTN_KCHUNK // GRID_W
    head_rows = [slice(HEAD_DIM * h, HEAD_DIM * (h + 1)) for h in range(NA_HEADS_PER_STEP)]
    ctx_chunks = [_key_chunks(kc_ref[0, hs, :], vc_ref[0, hs, :]) for hs in head_rows]

    def make_block(jj):
        h, j = divmod(jj, nblk)
        hs = head_rows[h]
        k_row = min(max(j * NA_QROWS - NA_WIN_R // 2, 0), n_rows - NA_KROWS)
        btype = 0 if j == 0 else (2 if j == nblk - 1 else 1)
        q0 = j * qw
        chunks = list(ctx_chunks[h])
        for cj in range(kw // ATTN_KCHUNK):
            first = k_row + cj * rows_per_chunk
            in_window = False
            for r in range(j * NA_QROWS, (j + 1) * NA_QROWS):
                r0 = min(max(r - NA_WIN_R // 2, 0), n_rows - NA_WIN_R)
                in_window = in_window or (first < r0 + NA_WIN_R and first + rows_per_chunk > r0)
            if in_window:
                k0 = first * GRID_W
                chunks.append((k_ref[0, hs, k0:k0 + ATTN_KCHUNK], v_ref[0, hs, k0:k0 + ATTN_KCHUNK],
                               bias_ref[h, btype, cj * ATTN_KCHUNK:(cj + 1) * ATTN_KCHUNK, :]))
        return q_ref[0, hs, q0:q0 + qw], chunks

    def store(jj, o):
        h, j = divmod(jj, nblk)
        o_ref[0, head_rows[h], j * qw:(j + 1) * qw] = o.astype(BF16)

    _attend_blocks(nblk * NA_HEADS_PER_STEP, make_block, None, store)


def _na(qn, kn, vn, kc, vc, bias):
    b, _, t = qn.shape
    lc = kc.shape[2]
    hw = HEAD_DIM * NA_HEADS_PER_STEP
    head = lambda i, h: (i, h, 0)
    return pl.pallas_call(
        functools.partial(_na_kernel, seq=t),
        out_shape=jax.ShapeDtypeStruct((b, NA_W, t), BF16),
        grid=(b, NA_HEADS // NA_HEADS_PER_STEP),
        in_specs=[
            pl.BlockSpec((1, hw, t), head),
            pl.BlockSpec((1, hw, t), head),
            pl.BlockSpec((1, hw, t), head),
            pl.BlockSpec((1, hw, lc), head),
            pl.BlockSpec((1, hw, lc), head),
            pl.BlockSpec((NA_HEADS_PER_STEP,) + bias.shape[1:], lambda i, h: (h, 0, 0, 0)),
        ],
        out_specs=pl.BlockSpec((1, hw, t), head),
        compiler_params=_cparams("parallel", "parallel"),
        name="na",
    )(qn, kn, vn, kc, vc, bias)


def _ctx_attn_kernel(qs_ref, ks_ref, vs_ref, qn_ref, kn_ref, vn_ref, sink_ref, ys_ref, yn_ref):
    lc = qs_ref.shape[2]
    for g in range(SWA_KV_HEADS):
        q_t = jnp.concatenate(
            [qs_ref[0, HEAD_DIM * (SWA_GROUP * g + h):HEAD_DIM * (SWA_GROUP * g + h + 1), :] for h in range(SWA_GROUP)],
            axis=1)
        kv = slice(HEAD_DIM * g, HEAD_DIM * (g + 1))
        o = _attend(q_t, _key_chunks(ks_ref[0, kv, :], vs_ref[0, kv, :]), sink_ref[g])
        for h in range(SWA_GROUP):
            hh = SWA_GROUP * g + h
            ys_ref[0, HEAD_DIM * hh:HEAD_DIM * (hh + 1), :] = o[:, lc * h:lc * (h + 1)].astype(BF16)
    for h in range(NA_HEADS):
        sl = slice(HEAD_DIM * h, HEAD_DIM * (h + 1))
        o = _attend(qn_ref[0, sl, :], _key_chunks(kn_ref[0, sl, :], vn_ref[0, sl, :]), None)
        yn_ref[0, sl, :] = o.astype(BF16)


def _ctx_attn(qs, ks, vs, qn, kn, vn, sink_rows):
    b, _, lc = qs.shape
    full = lambda a: pl.BlockSpec((1,) + a.shape[1:], lambda i: (i, 0, 0))
    return pl.pallas_call(
        _ctx_attn_kernel,
        out_shape=[jax.ShapeDtypeStruct((b, SWA_Q_W, lc), BF16), jax.ShapeDtypeStruct((b, NA_W, lc), BF16)],
        grid=(b,),
        in_specs=[full(qs), full(ks), full(vs), full(qn), full(kn), full(vn),
                  pl.BlockSpec(sink_rows.shape, lambda i: (0, 0, 0))],
        out_specs=[pl.BlockSpec((1, SWA_Q_W, lc), lambda i: (i, 0, 0)),
                   pl.BlockSpec((1, NA_W, lc), lambda i: (i, 0, 0))],
        compiler_params=_cparams("parallel"),
        name="ctx_attn",
    )(qs, ks, vs, qn, kn, vn, sink_rows)


def _route(logits):
    row = lax.broadcasted_iota(jnp.int32, logits.shape, 0)
    big = jnp.int32(logits.shape[0])
    colmax = lambda a: jnp.max(a, axis=0, keepdims=True)
    first = lambda hit: jnp.min(jnp.where(hit, row, big), axis=0, keepdims=True)
    gmask = row < N_GROUPS
    gl = jnp.where(gmask, logits, NEG_INF)
    gmax = colmax(gl)
    g_sel = first(gl == gmax)
    p_g = 1.0 / jnp.sum(jnp.where(gmask, jnp.exp(logits - gmax), 0.0), axis=0, keepdims=True)
    lo = N_GROUPS + EXPERTS_PER_GROUP * g_sel
    el = jnp.where((row >= lo) & (row < lo + EXPERTS_PER_GROUP), logits, NEG_INF)
    v1 = colmax(el)
    i1 = first(el == v1)
    el2 = jnp.where(row == i1, NEG_INF, el)
    v2 = colmax(el2)
    i2 = first(el2 == v2)
    e21 = jnp.exp(v2 - v1)
    w1 = p_g / (1.0 + e21)
    w2 = p_g * e21 / (1.0 + e21)
    out_row = lax.broadcasted_iota(jnp.int32, (ROUTE_OUT, logits.shape[1]), 0)
    out = jnp.where(out_row == 0, (i1 - N_GROUPS).astype(F32), 0.0)
    out = jnp.where(out_row == 1, (i2 - N_GROUPS).astype(F32), out)
    out = jnp.where(out_row == 2, w1, out)
    return jnp.where(out_row == 3, w2, out)


def _out_proj_kernel(x_ref, yf_ref, ys_ref, yn_ref, wo_ref, g1_ref, sh_ref, sc_ref, gn_ref, wrh_ref, wrl_ref,
                     br_ref, x1_ref, h2_ref, rt_ref):
    toks = [slice(i, i + PROJ_SUBTILE) for i in range(0, x_ref.shape[1], PROJ_SUBTILE)]
    ys = [_dot_tn(jnp.concatenate([yf_ref[0, :, tok], ys_ref[0, :, tok], yn_ref[0, :, tok]], axis=0), wo_ref[...])
          for tok in toks]
    for tok, y in zip(toks, ys):
        x1 = x_ref[0, tok, :] + g1_ref[0] * y
        x1_ref[0, tok, :] = x1
        ms = jnp.mean(x1 * x1, axis=-1, keepdims=True)
        h2 = (x1 * lax.rsqrt(ms + EPS) * gn_ref[...]) * (1.0 + sc_ref[0]) + sh_ref[0]
        h_hi = h2.astype(BF16)
        h2_ref[0, tok, :] = h_hi
        h_lo = (h2 - h_hi.astype(F32)).astype(BF16)
        logits = (_dot_nt(wrh_ref[...], h_hi) + (_dot_nt(wrh_ref[...], h_lo) + _dot_nt(wrl_ref[...], h_hi))
                  + br_ref[...])
        rt_ref[0, :, tok] = _route(logits)


def _out_proj(x, yf, ys, yn, w_out, gate, shift, scale, gain, wr_hi, wr_lo, b_route, *, tm):
    b, t, d = x.shape
    vec = pl.BlockSpec((1, 1, d), lambda i, j: (i, 0, 0))
    feat = lambda r: pl.BlockSpec((1, r, tm), lambda i, j: (i, 0, j))
    tok = lambda w: pl.BlockSpec((1, tm, w), lambda i, j: (i, j, 0))
    whole = lambda a: pl.BlockSpec(a.shape, lambda i, j: (0, 0))
    return pl.pallas_call(
        _out_proj_kernel,
        out_shape=[jax.ShapeDtypeStruct((b, t, d), F32), jax.ShapeDtypeStruct((b, t, d), BF16),
                   jax.ShapeDtypeStruct((b, ROUTE_OUT, t), F32)],
        grid=(b, t // tm),
        in_specs=[tok(d), feat(FOURIER_WIDTH), feat(SWA_Q_W), feat(NA_W), whole(w_out),
                  vec, vec, vec, pl.BlockSpec((1, d), lambda i, j: (0, 0)),
                  whole(wr_hi), whole(wr_lo), whole(b_route)],
        out_specs=[tok(d), tok(d), feat(ROUTE_OUT)],
        compiler_params=_cparams("parallel", "parallel"),
        name="out_proj",
    )(x, yf, ys, yn, w_out, gate, shift, scale, gain, wr_hi, wr_lo, b_route)


def _start_or_wait(copy, wait, priority=0):
    if wait:
        copy.wait()
    else:
        copy.start(priority=priority)


def _run_copies(local, remote, sem, runs_ref, c, *, to_remote, wait):
    if wait:
        n = pl.multiple_of(runs_ref[3, c, 0], ROW_BLOCK)
        loc, rem = local.at[pl.ds(0, n)], remote.at[pl.ds(0, n)]
        (pltpu.make_async_copy(loc, rem, sem) if to_remote else pltpu.make_async_copy(rem, loc, sem)).wait()
        return
    for e in range(N_EXPERTS):
        rows = runs_ref[0, c, e]

        @pl.when(rows > 0)
        def _(e=e, rows=rows):
            n = pl.multiple_of(rows, ROW_BLOCK)
            loc = local.at[pl.ds(pl.multiple_of(runs_ref[1, c, e], ROW_BLOCK), n)]
            rem = remote.at[pl.ds(pl.multiple_of(runs_ref[2, c, e], ROW_BLOCK), n)]
            (pltpu.make_async_copy(loc, rem, sem) if to_remote else pltpu.make_async_copy(rem, loc, sem)).start()


def _tail_copies(zero, remote, sem, tstart_ref, tn_ref, *, wait, priority=0):
    def body(e, carry):
        rows = tn_ref[e]

        @pl.when(rows > 0)
        def _():
            n = pl.multiple_of(rows, ROW_BLOCK)
            rem = remote.at[pl.ds(pl.multiple_of(tstart_ref[e], ROW_BLOCK), n)]
            _start_or_wait(pltpu.make_async_copy(zero.at[pl.ds(0, n)], rem, sem), wait, priority)

        return carry

    lax.fori_loop(0, N_EXPERTS, body, 0)


def _unused_tile_copies(zero, remote, sem, nt_ref, total_tiles, *, wait, priority=0):
    def body(t, carry):
        row = pl.multiple_of(t * MOE_TILE, MOE_TILE)
        _start_or_wait(pltpu.make_async_copy(zero, remote.at[pl.ds(row, MOE_TILE)], sem), wait, priority)
        return carry

    lax.fori_loop(nt_ref[0], total_tiles, body, 0)


def _experts_kernel(t0_ref, tcnt_ref, nt_ref, xs_ref, wg_ref, wu_ref, wd_ref, ys_ref,
                    xbuf, ybuf, zero, wg_s, wu_s, wd_s, xsem, ysem, zsem):
    e = pl.program_id(0)
    first_tile = t0_ref[e]
    n = tcnt_ref[e]
    n_tiles = nt_ref[0]
    total_tiles = xs_ref.shape[0] // MOE_TILE

    def tile_rows(ref, a):
        return ref.at[pl.ds(pl.multiple_of(a * MOE_TILE, MOE_TILE), MOE_TILE)]

    x_copy = lambda a: pltpu.make_async_copy(tile_rows(xs_ref, a), xbuf.at[a % 2], xsem.at[a % 2])
    y_copy = lambda a: pltpu.make_async_copy(ybuf.at[a % 2], tile_rows(ys_ref, a), ysem.at[a % 2])

    last = e == N_EXPERTS - 1

    @pl.when(last)
    def _():
        zero[...] = jnp.zeros(zero.shape, MOE_DT)
        _unused_tile_copies(zero, ys_ref, zsem, nt_ref, total_tiles, wait=False)

    @pl.when(n > 0)
    def _():
        @pl.when(first_tile == 0)
        def _():
            x_copy(first_tile).start(priority=1)

        wg_s[...] = wg_ref[0, 0].astype(BF16)
        wu_s[...] = wu_ref[0, 0].astype(BF16)
        wd_s[...] = wd_ref[0, 0].astype(BF16)

        def body(a, carry):
            slot = a % 2
            x_copy(a).wait()

            @pl.when(a + 1 < n_tiles)
            def _():
                x_copy(a + 1).start(priority=1)

            @pl.when(a >= 2)
            def _():
                y_copy(a - 2).wait()

            x = xbuf[slot].astype(BF16)
            g = _dot(x, wg_s[...])
            u = _dot(x, wu_s[...])
            hid = (g * (1.0 / (1.0 + jnp.exp(-g)))) * u
            ybuf[slot] = _dot(hid.astype(BF16), wd_s[...]).astype(MOE_DT)
            y_copy(a).start(priority=1)
            return carry

        lax.fori_loop(first_tile, first_tile + n, body, 0)

    @pl.when(last)
    def _():
        y_copy(n_tiles - 1).wait()

        @pl.when(n_tiles >= 2)
        def _():
            y_copy(n_tiles - 2).wait()

        _unused_tile_copies(zero, ys_ref, zsem, nt_ref, total_tiles, wait=True)


def _experts(xs, plan, w_gate, w_up, w_down, layer):
    r, d = xs.shape
    de = w_gate.shape[3]
    wspec = lambda a, b: pl.BlockSpec((1, 1, a, b), lambda e, *_: (layer, e, 0, 0))
    any_spec = pl.BlockSpec(memory_space=pl.ANY)
    return pl.pallas_call(
        _experts_kernel,
        out_shape=jax.ShapeDtypeStruct((r, d), MOE_DT),
        grid_spec=pltpu.PrefetchScalarGridSpec(
            num_scalar_prefetch=3,
            grid=(N_EXPERTS,),
            in_specs=[any_spec, wspec(d, de), wspec(d, de), wspec(de, d)],
            out_specs=any_spec,
            scratch_shapes=[pltpu.VMEM((2, MOE_TILE, d), MOE_DT), pltpu.VMEM((2, MOE_TILE, d), MOE_DT),
                            pltpu.VMEM((MOE_TILE, d), MOE_DT),
                            pltpu.VMEM((d, de), BF16), pltpu.VMEM((d, de), BF16), pltpu.VMEM((de, d), BF16),
                            pltpu.SemaphoreType.DMA((2,)), pltpu.SemaphoreType.DMA((2,)),
                            pltpu.SemaphoreType.DMA(())],
        ),
        compiler_params=_cparams("arbitrary"),
        name="experts",
    )(plan["tile0"], plan["tcnt"], plan["n_tiles"], xs, w_gate, w_up, w_down)


def _dispatch_kernel(runs_ref, tstart_ref, tn_ref, nt_ref, *rest, part_chunks):
    n_parts = len(part_chunks)
    h_refs = rest[:n_parts]
    pos_ref, xs_ref, buf, zero, sem, zsem = rest[n_parts:]
    n_chunks = sum(part_chunks)
    c = pl.program_id(0)
    slot = c % 2
    total_tiles = xs_ref.shape[0] // MOE_TILE

    @pl.when(c == 0)
    def _():
        zero[...] = jnp.zeros(zero.shape, MOE_DT)
        _tail_copies(zero, xs_ref, zsem, tstart_ref, tn_ref, wait=False, priority=1)
        _unused_tile_copies(zero, xs_ref, zsem, nt_ref, total_tiles, wait=False, priority=1)

    row = lax.broadcasted_iota(jnp.int32, (MOE_LROWS, MOE_CHUNK), 0)
    hit = (row == pos_ref[0, 0:1, :]) | (row == pos_ref[0, 1:2, :])
    sel = jnp.where(hit, 1.0, 0.0).astype(BF16)
    first = 0
    for h_ref, n in zip(h_refs, part_chunks):
        @pl.when((c >= first) & (c < first + n))
        def _(h_ref=h_ref):
            buf[slot] = _dot(sel, h_ref[...]).astype(MOE_DT)
        first += n
    _run_copies(buf.at[slot], xs_ref, sem.at[slot], runs_ref, c, to_remote=True, wait=False)

    @pl.when(c > 0)
    def _():
        _run_copies(buf.at[1 - slot], xs_ref, sem.at[1 - slot], runs_ref, c - 1, to_remote=True, wait=True)

    @pl.when(c == n_chunks - 1)
    def _():
        _run_copies(buf.at[slot], xs_ref, sem.at[slot], runs_ref, c, to_remote=True, wait=True)
        _tail_copies(zero, xs_ref, zsem, tstart_ref, tn_ref, wait=True)
        _unused_tile_copies(zero, xs_ref, zsem, nt_ref, total_tiles, wait=True)


def _dispatch(h_parts, plan):
    d = h_parts[0].shape[1]
    part_chunks = tuple(h.shape[0] // MOE_CHUNK for h in h_parts)
    in_specs = []
    first = 0
    for n in part_chunks:
        in_specs.append(pl.BlockSpec(
            (MOE_CHUNK, d), lambda i, *_, first=first, n=n: (jnp.clip(i - first, 0, n - 1), 0)))
        first += n
    in_specs.append(pl.BlockSpec((1, 8, MOE_CHUNK), lambda i, *_: (i, 0, 0)))
    return pl.pallas_call(
        functools.partial(_dispatch_kernel, part_chunks=part_chunks),
        out_shape=jax.ShapeDtypeStruct((plan["rows"], d), MOE_DT),
        grid_spec=pltpu.PrefetchScalarGridSpec(
            num_scalar_prefetch=4,
            grid=(sum(part_chunks),),
            in_specs=in_specs,
            out_specs=pl.BlockSpec(memory_space=pl.ANY),
            scratch_shapes=[pltpu.VMEM((2, MOE_LROWS, d), MOE_DT), pltpu.VMEM((MOE_TILE, d), MOE_DT),
                            pltpu.SemaphoreType.DMA((2,)), pltpu.SemaphoreType.DMA(())],
        ),
        compiler_params=_cparams("arbitrary"),
        name="dispatch",
    )(plan["runs"], plan["tstart"], plan["tn"], plan["n_tiles"], *h_parts, plan["pos"])


def _combine_kernel(runs_ref, x1_ref, pos_ref, wt_ref, g2_ref, gf_ref, ys_ref, o_ref, buf, sem,
                    *, c0, n_chunks, final_norm):
    i = pl.program_id(0)
    c = i + c0
    slot = i % 2
    gather = functools.partial(_run_copies, remote=ys_ref, runs_ref=runs_ref, to_remote=False)

    @pl.when(i == 0)
    def _():
        buf[...] = jnp.zeros(buf.shape, MOE_DT)
        gather(buf.at[slot], sem=sem.at[slot], c=c, wait=False)

    @pl.when(i + 1 < n_chunks)
    def _():
        gather(buf.at[1 - slot], sem=sem.at[1 - slot], c=c + 1, wait=False)

    gather(buf.at[slot], sem=sem.at[slot], c=c, wait=True)
    row = lax.broadcasted_iota(jnp.int32, (MOE_LROWS, MOE_CHUNK), 0)
    hit1 = row == pos_ref[0, 0:1, :]
    hit2 = row == pos_ref[0, 1:2, :]
    w_row = jnp.sum(jnp.where(hit1, wt_ref[0, 0:1, :], 0.0) + jnp.where(hit2, wt_ref[0, 1:2, :], 0.0),
                    axis=1, keepdims=True)
    sel = jnp.where(hit1 | hit2, 1.0, 0.0).astype(BF16)
    x2 = x1_ref[0] + g2_ref[0] * _dot_tn(sel, (buf[slot] * w_row).astype(BF16))
    if final_norm:
        ms = jnp.mean(x2 * x2, axis=-1, keepdims=True)
        x2 = x2 * lax.rsqrt(ms + EPS) * gf_ref[...]
    o_ref[0] = x2


def _combine(x1, plan, gate, g_final, ys, *, c0, final_norm):
    b, t, d = x1.shape
    per_batch = t // MOE_CHUNK
    n_chunks = b * per_batch
    tok = lambda w: pl.BlockSpec((1, MOE_CHUNK, w), lambda i, *_: (i // per_batch, i % per_batch, 0))
    return pl.pallas_call(
        functools.partial(_combine_kernel, c0=c0, n_chunks=n_chunks, final_norm=final_norm),
        out_shape=jax.ShapeDtypeStruct((b, t, d), F32),
        grid_spec=pltpu.PrefetchScalarGridSpec(
            num_scalar_prefetch=1,
            grid=(n_chunks,),
            in_specs=[tok(d),
                      pl.BlockSpec((1, 8, MOE_CHUNK), lambda i, *_: (i + c0, 0, 0)),
                      pl.BlockSpec((1, 8, MOE_CHUNK), lambda i, *_: (i + c0, 0, 0)),
                      pl.BlockSpec((1, 1, d), lambda i, *_: (i // per_batch, 0, 0)),
                      pl.BlockSpec((1, d), lambda i, *_: (0, 0)),
                      pl.BlockSpec(memory_space=pl.ANY)],
            out_specs=tok(d),
            scratch_shapes=[pltpu.VMEM((2, MOE_LROWS, d), MOE_DT), pltpu.SemaphoreType.DMA((2,))],
        ),
        compiler_params=_cparams("arbitrary"),
        name="combine",
    )(plan["runs"], x1, plan["pos"], plan["wts"], gate, g_final, ys)


def _rope_tables(seq):
    half = HEAD_DIM // 4
    t = np.arange(seq)
    rows, cols = t // GRID_W, t % GRID_W
    freqs = ROPE_BASE ** (-np.arange(half, dtype=np.float64) / half)

    def cs(pos):
        ang = pos[None, :] * freqs[:, None]
        return np.cos(ang), np.sin(ang)

    cr, sr = cs(rows)
    cc, sc = cs(cols)
    return (jnp.asarray(np.concatenate([cr, cr, cc, cc], axis=0), F32),
            jnp.asarray(np.concatenate([-sr, sr, -sc, sc], axis=0), F32))


def _dft_tables(n):
    k = np.arange(n)
    ang = ((k[:, None] * k[None, :]) % n) * (2.0 * np.pi / n)
    return jnp.asarray(np.cos(ang), F32), jnp.asarray(np.sin(ang), F32)


def _channel_dft_t():
    c, s = _dft_tables(FOURIER_GROUP_DIM)
    eye = jnp.eye(FOURIER_GROUPS, dtype=F32)
    scale = FOURIER_GROUP_DIM ** -0.5
    return jnp.concatenate([jnp.kron(eye, c), jnp.kron(eye, s)], axis=0) * scale


def _position_dft(n):
    c, s = _dft_tables(n)
    scale = n ** -0.5
    return (c * scale).astype(BF16), (s * scale).astype(BF16)


def _swa_mask():
    kk = np.arange(SWA_BLOCK)[:, None]
    q = np.arange(SWA_BLOCK)[None, :]
    tiles = []
    for block_offset in (-SWA_BLOCK, SWA_BLOCK):
        ok = np.abs(block_offset + kk - q) <= SWA_WINDOW
        tiles.append(np.tile(np.where(ok, 0.0, NEG_INF), (1, SWA_GROUP)))
    return jnp.asarray(np.stack(tiles), F32)


def _sink_rows(sink, width):
    return jnp.repeat(sink.astype(F32).reshape(SWA_KV_HEADS, SWA_GROUP) * LOG2E, width, axis=1)[:, None, :]


def _route_chunks(route):
    r = jnp.moveaxis(route, 1, 0).reshape(ROUTE_OUT, -1, MOE_CHUNK)
    return jnp.swapaxes(r, 0, 1)


def _moe_plan(route_chunks):
    i32 = jnp.int32
    nc = route_chunks.shape[0]
    n = nc * MOE_CHUNK
    ids = route_chunks[:, 0:2, :].astype(i32).reshape(nc, 2 * MOE_CHUNK)
    onehot = (ids[:, :, None] == jnp.arange(N_EXPERTS, dtype=i32)).astype(i32)
    pair = jnp.arange(2 * MOE_CHUNK, dtype=i32)
    earlier = (pair[None, :] < pair[:, None]).astype(BF16)
    before = jnp.einsum("pq,cqe->cpe", earlier, onehot.astype(BF16), preferred_element_type=F32).astype(i32)
    rank = jnp.sum(before * onehot, axis=-1)
    cnt = jnp.sum(onehot, axis=1)
    run = (cnt + ROW_BLOCK - 1) // ROW_BLOCK * ROW_BLOCK
    lo = jnp.cumsum(run, axis=1) - run
    pos = (jnp.sum(onehot * lo[:, None, :], axis=-1) + rank).reshape(nc, 2, MOE_CHUNK)
    pos = jnp.concatenate([pos, jnp.full((nc, 6, MOE_CHUNK), -1, i32)], axis=1)
    seg = jnp.sum(run, axis=0)
    padded = (seg + MOE_TILE - 1) // MOE_TILE * MOE_TILE
    start = jnp.cumsum(padded) - padded
    off = start[None, :] + jnp.cumsum(run, axis=0) - run
    wts = jnp.concatenate([route_chunks[:, 2:4, :], jnp.zeros((nc, 6, MOE_CHUNK), F32)], axis=1)
    rows = -(-(2 * n + (ROW_BLOCK - 1) * N_EXPERTS * nc + (MOE_TILE - 1) * N_EXPERTS) // MOE_TILE) * MOE_TILE
    n_tiles = jnp.sum(padded) // MOE_TILE
    total = jnp.broadcast_to(jnp.sum(run, axis=1, keepdims=True), run.shape)
    return {"pos": pos, "wts": wts, "runs": jnp.stack([run, lo, off, total]).astype(i32),
            "tstart": (start + seg).astype(i32), "tn": (padded - seg).astype(i32),
            "tile0": (start // MOE_TILE).astype(i32), "tcnt": (padded // MOE_TILE).astype(i32),
            "n_tiles": n_tiles.astype(i32).reshape(1), "rows": rows}


def kernel(x, c, ctx, c_ctx, w_mod, b_mod, g_norm1, g_norm2, w_in, w_four, w_out, swa_sink, na_rpb,
           w_route_group, b_route_group, w_route_expert, b_route_expert, w_exp_gate, w_exp_up,
           w_exp_down, g_final):
    b, s, d = x.shape
    lc = ctx.shape[1]
    depth = w_mod.shape[0]
    tm = 512

    c_rows = jnp.concatenate([c, c_ctx[None, :], jnp.zeros((7, d), F32)], axis=0)
    mod = _modulation(c_rows, w_mod, b_mod)

    cos_t, sin_t = _rope_tables(s)
    cos_c, sin_c = cos_t[:, :lc], sin_t[:, :lc]
    bd_t = _channel_dft_t().astype(BF16)
    cn, sn = _position_dft(s)
    cn_c, sn_c = _position_dft(lc)
    mask = _swa_mask()
    route_pad = ROUTE_ROWS - N_GROUPS - N_EXPERTS

    xc = ctx
    for layer in range(depth):
        with_ctx_out = layer < depth - 1
        lat = [mod[layer, :b, i * d:(i + 1) * d][:, None, :] for i in range(6)]
        cx = [jnp.broadcast_to(mod[layer, b, i * d:(i + 1) * d][None, None, :], (b, 1, d)) for i in range(6)]
        sh1, sc1, g1, sh2, sc2, g2 = lat
        shc1, scc1, gc1, shc2, scc2, gc2 = cx
        gn1 = g_norm1[layer][None, :]
        gn2 = g_norm2[layer][None, :]
        w_t = w_in[layer].T.astype(BF16)
        wf_t = w_four[layer].T.astype(BF16)
        wo = w_out[layer].astype(BF16)
        w_r = jnp.concatenate([w_route_group[layer].T, w_route_expert[layer].T,
                               jnp.zeros((route_pad, d), F32)], axis=0)
        wr_hi = w_r.astype(BF16)
        wr_lo = (w_r - wr_hi.astype(F32)).astype(BF16)
        b_r = jnp.concatenate([b_route_group[layer], b_route_expert[layer], jnp.zeros((route_pad,), F32)])[:, None]
        sink_lat = _sink_rows(swa_sink[layer], SWA_BLOCK)
        sink_ctx = _sink_rows(swa_sink[layer], lc)

        fz, qs, qn, ks, vs, kn, vn = _in_proj(x, sh1, sc1, gn1, w_t, bd_t, cos_t, sin_t,
                                              with_q=True, rope=True, tm=2 * tm)
        if with_ctx_out:
            fz_c, qs_c, qn_c, ks_c, vs_c, kn_c, vn_c = _in_proj(xc, shc1, scc1, gn1, w_t, bd_t, cos_c, sin_c,
                                                                with_q=True, rope=False, tm=lc)
        else:
            ks_c, vs_c, kn_c, vn_c = _in_proj(xc, shc1, scc1, gn1, w_t[KS_LO:], bd_t, cos_c, sin_c,
                                              with_q=False, rope=False, tm=lc)

        yf = _fourier(fz, cn, sn, wf_t, tk=1024)
        ys = _swa(qs, ks, vs, ks_c, vs_c, mask, sink_lat)
        yn = _na(qn, kn, vn, kn_c, vn_c, _na_bias(na_rpb[layer], s))
        x1, h2, route = _out_proj(x, yf, ys, yn, wo, g1, sh2, sc2, gn2, wr_hi, wr_lo, b_r, tm=2 * tm)

        moe_w = (w_exp_gate, w_exp_up, w_exp_down, layer)
        if with_ctx_out:
            yf_c = _fourier(fz_c, cn_c, sn_c, wf_t, tk=lc)
            ys_c, yn_c = _ctx_attn(qs_c, ks_c, vs_c, qn_c, kn_c, vn_c, sink_ctx)
            xc1, hc2, route_c = _out_proj(xc, yf_c, ys_c, yn_c, wo, gc1, shc2, scc2, gn2, wr_hi, wr_lo, b_r, tm=lc)
            n_lat = b * s
            lat_chunks = n_lat // MOE_CHUNK
            plan = _moe_plan(jnp.concatenate([_route_chunks(route), _route_chunks(route_c)], axis=0))
            xs = _dispatch([h2.reshape(n_lat, d), hc2.reshape(b * lc, d)], plan)
            ye = _experts(xs, plan, *moe_w)
            x = _combine(x1, plan, g2, g_final[None, :], ye, c0=0, final_norm=False)
            ctx_chunks = b * lc // MOE_CHUNK
            xc = _combine(xc1.reshape(ctx_chunks, MOE_CHUNK, d), plan, gc2[:ctx_chunks], g_final[None, :], ye,
                          c0=lat_chunks, final_norm=False).reshape(b, lc, d)
        else:
            plan = _moe_plan(_route_chunks(route))
            xs = _dispatch([h2.reshape(b * s, d)], plan)
            ye = _experts(xs, plan, *moe_w)
            x = _combine(x1, plan, g2, g_final[None, :], ye, c0=0, final_norm=True)
    return x
```

```python
import functools

import jax
import jax.numpy as jnp
import numpy as np
from jax import lax
from jax.experimental import pallas as pl
from jax.experimental.pallas import tpu as pltpu

F32 = jnp.float32
BF16 = jnp.bfloat16

D_MODEL = 1024
GRID_W = 64
HEAD_DIM = 64
FOURIER_WIDTH = D_MODEL // 4
FOURIER_GROUPS = 4
FOURIER_GROUP_DIM = FOURIER_WIDTH // FOURIER_GROUPS
SWA_HEADS = (3 * D_MODEL // 8) // HEAD_DIM
SWA_KV_HEADS = 2
SWA_GROUP = SWA_HEADS // SWA_KV_HEADS
SWA_WINDOW = 128
SWA_BLOCK = 128
NA_HEADS = (3 * D_MODEL // 8) // HEAD_DIM
NA_WIN_R = 8
NA_WIN_C = 16
ROPE_BASE = 10000.0
N_GROUPS = 4
EXPERTS_PER_GROUP = 8
N_EXPERTS = N_GROUPS * EXPERTS_PER_GROUP
D_EXPERT = D_MODEL // 2
EPS = 1e-6
NEG_INF = -1e30
LOG2E = 1.4426950408889634

SWA_Q_W = SWA_HEADS * HEAD_DIM
SWA_KV_W = SWA_KV_HEADS * HEAD_DIM
NA_W = NA_HEADS * HEAD_DIM
MIX_WIDTH = FOURIER_WIDTH + SWA_Q_W + NA_W
Q_COLS = MIX_WIDTH
IN_COLS = 2 * MIX_WIDTH

F_LO, F_HI = 0, FOURIER_WIDTH
QS_LO, QS_HI = F_HI, F_HI + SWA_Q_W
QN_LO, QN_HI = QS_HI, QS_HI + NA_W
KS_LO, KS_HI = QN_HI, QN_HI + SWA_KV_W
VS_LO, VS_HI = KS_HI, KS_HI + SWA_KV_W
KN_LO, KN_HI = VS_HI, VS_HI + NA_W
VN_LO, VN_HI = KN_HI, KN_HI + NA_W

LANE = 128
SUBLANE = 8
ROUTE_ROWS = -(-(N_GROUPS + N_EXPERTS) // SUBLANE) * SUBLANE
ROUTE_OUT = SUBLANE
NA_QROWS = 4
NA_KROWS = NA_QROWS + NA_WIN_R
ATTN_KCHUNK = LANE
ATTN_LOOKAHEAD = 2
NA_HEADS_PER_STEP = 2
PROJ_TILE = 1024
PROJ_SUBTILE = LANE
FOURIER_TILE = 1024
MOE_TILE = 512
MOE_DT = F32
ROW_BLOCK = 8
MOE_CHUNK = 512
MOE_LROWS = 2 * MOE_CHUNK + (ROW_BLOCK - 1) * N_EXPERTS
VMEM_LIMIT = 48 * 1024 * 1024


def _cparams(*sem):
    return pltpu.CompilerParams(dimension_semantics=sem, vmem_limit_bytes=VMEM_LIMIT)


def _dot(a, b):
    return jnp.dot(a, b, preferred_element_type=F32)


def _dot_tn(a, b):
    return lax.dot_general(a, b, (((0,), (0,)), ((), ())), preferred_element_type=F32)


def _dot_nt(a, b):
    return lax.dot_general(a, b, (((1,), (1,)), ((), ())), preferred_element_type=F32)


def _split_dot(a, w):
    a_hi = a.astype(BF16)
    a_lo = (a - a_hi.astype(F32)).astype(BF16)
    w_hi = w.astype(BF16)
    w_lo = (w - w_hi.astype(F32)).astype(BF16)
    return _dot(a_hi, w_hi) + (_dot(a_hi, w_lo) + _dot(a_lo, w_hi))


def _mod_kernel(c_ref, w_ref, b_ref, o_ref):
    c = c_ref[...]
    a = c * (1.0 / (1.0 + jnp.exp(-c)))
    o_ref[0] = _split_dot(a, w_ref[0]) + b_ref[0]


def _modulation(c_rows, w_mod, b_mod):
    depth, d, n6 = w_mod.shape
    r = c_rows.shape[0]
    tn = 1536
    return pl.pallas_call(
        _mod_kernel,
        out_shape=jax.ShapeDtypeStruct((depth, r, n6), F32),
        grid=(depth, n6 // tn),
        in_specs=[
            pl.BlockSpec((r, d), lambda l, j: (0, 0)),
            pl.BlockSpec((1, d, tn), lambda l, j: (l, 0, j)),
            pl.BlockSpec((1, 1, tn), lambda l, j: (l, 0, j)),
        ],
        out_specs=pl.BlockSpec((1, r, tn), lambda l, j: (l, 0, j)),
        compiler_params=_cparams("parallel", "parallel"),
        name="modulation",
    )(c_rows, w_mod, b_mod.reshape(depth, 1, n6))


def _rope_rows(t, cos_t, sin_t, n_heads):
    outs = []
    for h in range(n_heads):
        th = t[HEAD_DIM * h:HEAD_DIM * (h + 1)]
        sw = jnp.concatenate([th[16:32], th[0:16], th[48:64], th[32:48]], axis=0)
        outs.append(th * cos_t + sw * sin_t)
    return jnp.concatenate(outs, axis=0)


def _in_proj_kernel(x_ref, sh_ref, sc_ref, g_ref, wt_ref, bdt_ref, cos_ref, sin_ref, *outs, with_q, rope):
    xf = x_ref[0]
    ms = jnp.mean(xf * xf, axis=-1, keepdims=True)
    y = xf * lax.rsqrt(ms + EPS) * g_ref[...]
    h = y * (1.0 + sc_ref[0]) + sh_ref[0]
    pt = _dot_nt(wt_ref[...], h.astype(BF16))
    q_scale = HEAD_DIM ** -0.5 * LOG2E
    if with_q:
        fz_ref, qs_ref, qn_ref, ks_ref, vs_ref, kn_ref, vn_ref = outs
        fz_ref[0] = _dot(bdt_ref[...], pt[F_LO:F_HI].astype(BF16)).astype(BF16)
        qs = pt[QS_LO:QS_HI]
        if rope:
            qs = _rope_rows(qs, cos_ref[...], sin_ref[...], SWA_HEADS)
        qs_ref[0] = (qs * q_scale).astype(BF16)
        qn_ref[0] = (pt[QN_LO:QN_HI] * q_scale).astype(BF16)
        off = 0
    else:
        ks_ref, vs_ref, kn_ref, vn_ref = outs
        off = KS_LO
    ks = pt[KS_LO - off:KS_HI - off]
    if rope:
        ks = _rope_rows(ks, cos_ref[...], sin_ref[...], SWA_KV_HEADS)
    ks_ref[0] = ks.astype(BF16)
    vs_ref[0] = pt[VS_LO - off:VS_HI - off].astype(BF16)
    kn_ref[0] = pt[KN_LO - off:KN_HI - off].astype(BF16)
    vn_ref[0] = pt[VN_LO - off:VN_HI - off].astype(BF16)


def _in_proj(x, shift, scale, gain, w_t, bd_t, cos_t, sin_t, *, with_q, rope, tm):
    b, t, d = x.shape
    nf = w_t.shape[0]
    rows = ([2 * FOURIER_WIDTH, SWA_Q_W, NA_W] if with_q else []) + [SWA_KV_W, SWA_KV_W, NA_W, NA_W]
    return pl.pallas_call(
        functools.partial(_in_proj_kernel, with_q=with_q, rope=rope),
        out_shape=[jax.ShapeDtypeStruct((b, r, t), BF16) for r in rows],
        grid=(b, t // tm),
        in_specs=[
            pl.BlockSpec((1, tm, d), lambda i, j: (i, j, 0)),
            pl.BlockSpec((1, 1, d), lambda i, j: (i, 0, 0)),
            pl.BlockSpec((1, 1, d), lambda i, j: (i, 0, 0)),
            pl.BlockSpec((1, d), lambda i, j: (0, 0)),
            pl.BlockSpec((nf, d), lambda i, j: (0, 0)),
            pl.BlockSpec(bd_t.shape, lambda i, j: (0, 0)),
            pl.BlockSpec((HEAD_DIM, tm), lambda i, j: (0, j)),
            pl.BlockSpec((HEAD_DIM, tm), lambda i, j: (0, j)),
        ],
        out_specs=[pl.BlockSpec((1, r, tm), lambda i, j: (i, 0, j)) for r in rows],
        compiler_params=_cparams("parallel", "parallel"),
        name="in_proj_q" if with_q else "in_proj_kv",
    )(x, shift, scale, gain, w_t, bd_t, cos_t, sin_t)


def _fourier_kernel(fz_ref, cn_ref, sn_ref, wft_ref, o_ref):
    zc = fz_ref[0, 0:FOURIER_WIDTH, :]
    zs = fz_ref[0, FOURIER_WIDTH:2 * FOURIER_WIDTH, :]
    y = _dot(zc, cn_ref[...]) - _dot(zs, sn_ref[...])
    o_ref[0] = _dot(wft_ref[...], y.astype(BF16)).astype(BF16)


def _fourier(fz, cn, sn, wf_t, *, tk):
    b, _, t = fz.shape
    return pl.pallas_call(
        _fourier_kernel,
        out_shape=jax.ShapeDtypeStruct((b, FOURIER_WIDTH, t), BF16),
        grid=(t // tk, b),
        in_specs=[
            pl.BlockSpec((1, 2 * FOURIER_WIDTH, t), lambda k, i: (i, 0, 0)),
            pl.BlockSpec((t, tk), lambda k, i: (0, k)),
            pl.BlockSpec((t, tk), lambda k, i: (0, k)),
            pl.BlockSpec((FOURIER_WIDTH, FOURIER_WIDTH), lambda k, i: (0, 0)),
        ],
        out_specs=pl.BlockSpec((1, FOURIER_WIDTH, tk), lambda k, i: (i, 0, k)),
        compiler_params=_cparams("parallel", "parallel"),
        name="fourier",
    )(fz, cn, sn, wf_t)


def _key_chunks(k, v, bias=None):
    n = k.shape[1] // ATTN_KCHUNK
    cut = lambda a, j, axis: lax.slice_in_dim(a, j * ATTN_KCHUNK, (j + 1) * ATTN_KCHUNK, axis=axis)
    return [(cut(k, j, 1), cut(v, j, 1), None if bias is None else cut(bias, j, 0)) for j in range(n)]


def _logits(q_t, chunks):
    return _dot_tn(jnp.concatenate([k_t for k_t, _, _ in chunks], axis=1), q_t)


def _softmax_pv(s, chunks, sink_row):
    pieces = []
    off = 0
    for k_t, _, bias in chunks:
        piece = s[off:off + k_t.shape[1]]
        pieces.append(piece if bias is None else piece + bias)
        off += k_t.shape[1]
    m = functools.reduce(jnp.maximum, [jnp.max(p, axis=0, keepdims=True) for p in pieces])
    if sink_row is not None:
        m = jnp.maximum(m, sink_row)
    probs = [jnp.exp2(p - m) for p in pieces]
    den = functools.reduce(jnp.add, [jnp.sum(p, axis=0, keepdims=True) for p in probs])
    if sink_row is not None:
        den = den + jnp.exp2(sink_row - m)
    v_all = jnp.concatenate([v_t for _, v_t, _ in chunks], axis=1)
    p_all = jnp.concatenate([p.astype(BF16) for p in probs], axis=0)
    return _dot(v_all, p_all) / den


def _attend(q_t, chunks, sink_row):
    return _softmax_pv(_logits(q_t, chunks), chunks, sink_row)


def _attend_blocks(n_blocks, make_block, sink_row, store):
    blocks, logits = {}, {}
    for j in range(min(ATTN_LOOKAHEAD, n_blocks)):
        blocks[j] = make_block(j)
        logits[j] = _logits(*blocks[j])
    for j in range(n_blocks):
        ahead = j + ATTN_LOOKAHEAD
        if ahead < n_blocks:
            blocks[ahead] = make_block(ahead)
            logits[ahead] = _logits(*blocks[ahead])
        store(j, _softmax_pv(logits.pop(j), blocks.pop(j)[1], sink_row))


def _swa_kernel(q_ref, k_ref, v_ref, kc_ref, vc_ref, mask_ref, sink_ref, o_ref, *, seq):
    nb = seq // SWA_BLOCK
    ctx_chunks = _key_chunks(kc_ref[0], vc_ref[0])

    def make_block(n):
        q0 = n * SWA_BLOCK
        q_t = jnp.concatenate(
            [q_ref[0, HEAD_DIM * h:HEAD_DIM * (h + 1), q0:q0 + SWA_BLOCK] for h in range(SWA_GROUP)], axis=1)
        chunks = [(k_ref[0, :, q0:q0 + SWA_BLOCK], v_ref[0, :, q0:q0 + SWA_BLOCK], None)] + ctx_chunks
        for side, kb in ((0, n - 1), (1, n + 1)):
            if 0 <= kb < nb:
                k0 = kb * SWA_BLOCK
                chunks.append((k_ref[0, :, k0:k0 + SWA_BLOCK], v_ref[0, :, k0:k0 + SWA_BLOCK], mask_ref[side]))
        return q_t, chunks

    def store(n, o):
        q0 = n * SWA_BLOCK
        for h in range(SWA_GROUP):
            o_ref[0, HEAD_DIM * h:HEAD_DIM * (h + 1), q0:q0 + SWA_BLOCK] = (
                o[:, SWA_BLOCK * h:SWA_BLOCK * (h + 1)].astype(BF16))

    _attend_blocks(nb, make_block, sink_ref[0], store)


def _swa(qs, ks, vs, kc, vc, mask, sink_rows):
    b, _, t = qs.shape
    lc = kc.shape[2]
    gw = SWA_GROUP * HEAD_DIM
    return pl.pallas_call(
        functools.partial(_swa_kernel, seq=t),
        out_shape=jax.ShapeDtypeStruct((b, SWA_Q_W, t), BF16),
        grid=(b, SWA_KV_HEADS),
        in_specs=[
            pl.BlockSpec((1, gw, t), lambda i, g: (i, g, 0)),
            pl.BlockSpec((1, HEAD_DIM, t), lambda i, g: (i, g, 0)),
            pl.BlockSpec((1, HEAD_DIM, t), lambda i, g: (i, g, 0)),
            pl.BlockSpec((1, HEAD_DIM, lc), lambda i, g: (i, g, 0)),
            pl.BlockSpec((1, HEAD_DIM, lc), lambda i, g: (i, g, 0)),
            pl.BlockSpec((2, SWA_BLOCK, SWA_GROUP * SWA_BLOCK), lambda i, g: (0, 0, 0)),
            pl.BlockSpec((1, 1, SWA_GROUP * SWA_BLOCK), lambda i, g: (g, 0, 0)),
        ],
        out_specs=pl.BlockSpec((1, gw, t), lambda i, g: (i, g, 0)),
        compiler_params=_cparams("parallel", "parallel"),
        name="swa",
    )(qs, ks, vs, kc, vc, mask, sink_rows)


def _na_bias_kernel(rpb_ref, o_ref, u_ref, *, total_rows):
    hd = pl.program_id(0)
    kc = lax.broadcasted_iota(jnp.int32, (GRID_W, LANE), 0)
    lane = lax.broadcasted_iota(jnp.int32, (GRID_W, LANE), 1)
    qc = lane % GRID_W
    dc = jnp.clip(kc - qc, -(NA_WIN_C - 1), NA_WIN_C - 1) + (NA_WIN_C - 1)
    c0 = jnp.clip(qc - NA_WIN_C // 2, 0, GRID_W - NA_WIN_C)
    valid_c = (kc >= c0) & (kc < c0 + NA_WIN_C)
    n_dr = 2 * NA_WIN_R - 1
    for dr in range(n_dr):
        u = jnp.full((GRID_W, LANE), NEG_INF, F32)
        for d in range(2 * NA_WIN_C - 1):
            u = jnp.where(valid_c & (dc == d), rpb_ref[hd, dr, d] * LOG2E, u)
        u_ref[dr] = u
    n_rows = o_ref.shape[2] // GRID_W
    block_types = [(0, 0), (NA_QROWS, 0), (total_rows - NA_QROWS, total_rows - NA_KROWS)]
    neg = jnp.full((GRID_W, LANE), NEG_INF, F32)
    for t, (r_base, k_base) in enumerate(block_types):
        for kl in range(n_rows):
            kr = k_base + kl
            for lg in range(NA_QROWS // 2):
                halves = []
                for rq in (2 * lg, 2 * lg + 1):
                    r = r_base + rq
                    r0 = min(max(r - NA_WIN_R // 2, 0), total_rows - NA_WIN_R)
                    ok = r0 <= kr < r0 + NA_WIN_R
                    halves.append(u_ref[kr - r + NA_WIN_R - 1] if ok else neg)
                o_ref[0, t, GRID_W * kl:GRID_W * (kl + 1), LANE * lg:LANE * (lg + 1)] = jnp.where(
                    lane < GRID_W, halves[0], halves[1])


def _na_bias(rpb, seq):
    nh = rpb.shape[0]
    return pl.pallas_call(
        functools.partial(_na_bias_kernel, total_rows=seq // GRID_W),
        out_shape=jax.ShapeDtypeStruct((nh, 3, NA_KROWS * GRID_W, NA_QROWS * GRID_W), F32),
        grid=(nh,),
        in_specs=[pl.BlockSpec(memory_space=pltpu.SMEM)],
        out_specs=pl.BlockSpec((1, 3, NA_KROWS * GRID_W, NA_QROWS * GRID_W), lambda h: (h, 0, 0, 0)),
        scratch_shapes=[pltpu.VMEM((2 * NA_WIN_R - 1, GRID_W, LANE), F32)],
        compiler_params=_cparams("parallel"),
        name="na_bias",
    )(rpb)


def _na_kernel(q_ref, k_ref, v_ref, kc_ref, vc_ref, bias_ref, o_ref, *, seq):
    n_rows = seq // GRID_W
    qw = NA_QROWS * GRID_W
    kw = NA_KROWS * GRID_W
    nblk = n_rows // NA_QROWS
    rows_per_chunk = ATTN_KCHUNK // GRID_W
    head_rows = [slice(HEAD_DIM * h, HEAD_DIM * (h + 1)) for h in range(NA_HEADS_PER_STEP)]
    ctx_chunks = [_key_chunks(kc_ref[0, hs, :], vc_ref[0, hs, :]) for hs in head_rows]

    def make_block(jj):
        h, j = divmod(jj, nblk)
        hs = head_rows[h]
        k_row = min(max(j * NA_QROWS - NA_WIN_R // 2, 0), n_rows - NA_KROWS)
        btype = 0 if j == 0 else (2 if j == nblk - 1 else 1)
        q0 = j * qw
        chunks = list(ctx_chunks[h])
        for cj in range(kw // ATTN_KCHUNK):
            first = k_row + cj * rows_per_chunk
            in_window = False
            for r in range(j * NA_QROWS, (j + 1) * NA_QROWS):
                r0 = min(max(r - NA_WIN_R // 2, 0), n_rows - NA_WIN_R)
                in_window = in_window or (first < r0 + NA_WIN_R and first + rows_per_chunk > r0)
            if in_window:
                k0 = first * GRID_W
                chunks.append((k_ref[0, hs, k0:k0 + ATTN_KCHUNK], v_ref[0, hs, k0:k0 + ATTN_KCHUNK],
                               bias_ref[h, btype, cj * ATTN_KCHUNK:(cj + 1) * ATTN_KCHUNK, :]))
        return q_ref[0, hs, q0:q0 + qw], chunks

    def store(jj, o):
        h, j = divmod(jj, nblk)
        o_ref[0, head_rows[h], j * qw:(j + 1) * qw] = o.astype(BF16)

    _attend_blocks(nblk * NA_HEADS_PER_STEP, make_block, None, store)


def _na(qn, kn, vn, kc, vc, bias):
    b, _, t = qn.shape
    lc = kc.shape[2]
    hw = HEAD_DIM * NA_HEADS_PER_STEP
    head = lambda i, h: (i, h, 0)
    return pl.pallas_call(
        functools.partial(_na_kernel, seq=t),
        out_shape=jax.ShapeDtypeStruct((b, NA_W, t), BF16),
        grid=(b, NA_HEADS // NA_HEADS_PER_STEP),
        in_specs=[
            pl.BlockSpec((1, hw, t), head),
            pl.BlockSpec((1, hw, t), head),
            pl.BlockSpec((1, hw, t), head),
            pl.BlockSpec((1, hw, lc), head),
            pl.BlockSpec((1, hw, lc), head),
            pl.BlockSpec((NA_HEADS_PER_STEP,) + bias.shape[1:], lambda i, h: (h, 0, 0, 0)),
        ],
        out_specs=pl.BlockSpec((1, hw, t), head),
        compiler_params=_cparams("parallel", "parallel"),
        name="na",
    )(qn, kn, vn, kc, vc, bias)


def _ctx_attn_kernel(qs_ref, ks_ref, vs_ref, qn_ref, kn_ref, vn_ref, sink_ref, ys_ref, yn_ref):
    lc = qs_ref.shape[2]
    for g in range(SWA_KV_HEADS):
        q_t = jnp.concatenate(
            [qs_ref[0, HEAD_DIM * (SWA_GROUP * g + h):HEAD_DIM * (SWA_GROUP * g + h + 1), :] for h in range(SWA_GROUP)],
            axis=1)
        kv = slice(HEAD_DIM * g, HEAD_DIM * (g + 1))
        o = _attend(q_t, _key_chunks(ks_ref[0, kv, :], vs_ref[0, kv, :]), sink_ref[g])
        for h in range(SWA_GROUP):
            hh = SWA_GROUP * g + h
            ys_ref[0, HEAD_DIM * hh:HEAD_DIM * (hh + 1), :] = o[:, lc * h:lc * (h + 1)].astype(BF16)
    for h in range(NA_HEADS):
        sl = slice(HEAD_DIM * h, HEAD_DIM * (h + 1))
        o = _attend(qn_ref[0, sl, :], _key_chunks(kn_ref[0, sl, :], vn_ref[0, sl, :]), None)
        yn_ref[0, sl, :] = o.astype(BF16)


def _ctx_attn(qs, ks, vs, qn, kn, vn, sink_rows):
    b, _, lc = qs.shape
    full = lambda a: pl.BlockSpec((1,) + a.shape[1:], lambda i: (i, 0, 0))
    return pl.pallas_call(
        _ctx_attn_kernel,
        out_shape=[jax.ShapeDtypeStruct((b, SWA_Q_W, lc), BF16), jax.ShapeDtypeStruct((b, NA_W, lc), BF16)],
        grid=(b,),
        in_specs=[full(qs), full(ks), full(vs), full(qn), full(kn), full(vn),
                  pl.BlockSpec(sink_rows.shape, lambda i: (0, 0, 0))],
        out_specs=[pl.BlockSpec((1, SWA_Q_W, lc), lambda i: (i, 0, 0)),
                   pl.BlockSpec((1, NA_W, lc), lambda i: (i, 0, 0))],
        compiler_params=_cparams("parallel"),
        name="ctx_attn",
    )(qs, ks, vs, qn, kn, vn, sink_rows)


def _route(logits):
    row = lax.broadcasted_iota(jnp.int32, logits.shape, 0)
    big = jnp.int32(logits.shape[0])
    colmax = lambda a: jnp.max(a, axis=0, keepdims=True)
    first = lambda hit: jnp.min(jnp.where(hit, row, big), axis=0, keepdims=True)
    gmask = row < N_GROUPS
    gl = jnp.where(gmask, logits, NEG_INF)
    gmax = colmax(gl)
    g_sel = first(gl == gmax)
    p_g = 1.0 / jnp.sum(jnp.where(gmask, jnp.exp(logits - gmax), 0.0), axis=0, keepdims=True)
    lo = N_GROUPS + EXPERTS_PER_GROUP * g_sel
    el = jnp.where((row >= lo) & (row < lo + EXPERTS_PER_GROUP), logits, NEG_INF)
    v1 = colmax(el)
    i1 = first(el == v1)
    el2 = jnp.where(row == i1, NEG_INF, el)
    v2 = colmax(el2)
    i2 = first(el2 == v2)
    e21 = jnp.exp(v2 - v1)
    w1 = p_g / (1.0 + e21)
    w2 = p_g * e21 / (1.0 + e21)
    out_row = lax.broadcasted_iota(jnp.int32, (ROUTE_OUT, logits.shape[1]), 0)
    out = jnp.where(out_row == 0, (i1 - N_GROUPS).astype(F32), 0.0)
    out = jnp.where(out_row == 1, (i2 - N_GROUPS).astype(F32), out)
    out = jnp.where(out_row == 2, w1, out)
    return jnp.where(out_row == 3, w2, out)


def _out_proj_kernel(x_ref, yf_ref, ys_ref, yn_ref, wo_ref, g1_ref, sh_ref, sc_ref, gn_ref, wrh_ref, wrl_ref,
                     br_ref, x1_ref, h2_ref, rt_ref):
    toks = [slice(i, i + PROJ_SUBTILE) for i in range(0, x_ref.shape[1], PROJ_SUBTILE)]
    ys = [_dot_tn(jnp.concatenate([yf_ref[0, :, tok], ys_ref[0, :, tok], yn_ref[0, :, tok]], axis=0), wo_ref[...])
          for tok in toks]
    for tok, y in zip(toks, ys):
        x1 = x_ref[0, tok, :] + g1_ref[0] * y
        x1_ref[0, tok, :] = x1
        ms = jnp.mean(x1 * x1, axis=-1, keepdims=True)
        h2 = (x1 * lax.rsqrt(ms + EPS) * gn_ref[...]) * (1.0 + sc_ref[0]) + sh_ref[0]
        h_hi = h2.astype(BF16)
        h2_ref[0, tok, :] = h_hi
        h_lo = (h2 - h_hi.astype(F32)).astype(BF16)
        logits = (_dot_nt(wrh_ref[...], h_hi) + (_dot_nt(wrh_ref[...], h_lo) + _dot_nt(wrl_ref[...], h_hi))
                  + br_ref[...])
        rt_ref[0, :, tok] = _route(logits)


def _out_proj(x, yf, ys, yn, w_out, gate, shift, scale, gain, wr_hi, wr_lo, b_route, *, tm):
    b, t, d = x.shape
    vec = pl.BlockSpec((1, 1, d), lambda i, j: (i, 0, 0))
    feat = lambda r: pl.BlockSpec((1, r, tm), lambda i, j: (i, 0, j))
    tok = lambda w: pl.BlockSpec((1, tm, w), lambda i, j: (i, j, 0))
    whole = lambda a: pl.BlockSpec(a.shape, lambda i, j: (0, 0))
    return pl.pallas_call(
        _out_proj_kernel,
        out_shape=[jax.ShapeDtypeStruct((b, t, d), F32), jax.ShapeDtypeStruct((b, t, d), BF16),
                   jax.ShapeDtypeStruct((b, ROUTE_OUT, t), F32)],
        grid=(b, t // tm),
        in_specs=[tok(d), feat(FOURIER_WIDTH), feat(SWA_Q_W), feat(NA_W), whole(w_out),
                  vec, vec, vec, pl.BlockSpec((1, d), lambda i, j: (0, 0)),
                  whole(wr_hi), whole(wr_lo), whole(b_route)],
        out_specs=[tok(d), tok(d), feat(ROUTE_OUT)],
        compiler_params=_cparams("parallel", "parallel"),
        name="out_proj",
    )(x, yf, ys, yn, w_out, gate, shift, scale, gain, wr_hi, wr_lo, b_route)


def _start_or_wait(copy, wait, priority=0):
    if wait:
        copy.wait()
    else:
        copy.start(priority=priority)


def _run_copies(local, remote, sem, runs_ref, c, *, to_remote, wait):
    if wait:
        n = pl.multiple_of(runs_ref[3, c, 0], ROW_BLOCK)
        loc, rem = local.at[pl.ds(0, n)], remote.at[pl.ds(0, n)]
        (pltpu.make_async_copy(loc, rem, sem) if to_remote else pltpu.make_async_copy(rem, loc, sem)).wait()
        return
    for e in range(N_EXPERTS):
        rows = runs_ref[0, c, e]

        @pl.when(rows > 0)
        def _(e=e, rows=rows):
            n = pl.multiple_of(rows, ROW_BLOCK)
            loc = local.at[pl.ds(pl.multiple_of(runs_ref[1, c, e], ROW_BLOCK), n)]
            rem = remote.at[pl.ds(pl.multiple_of(runs_ref[2, c, e], ROW_BLOCK), n)]
            (pltpu.make_async_copy(loc, rem, sem) if to_remote else pltpu.make_async_copy(rem, loc, sem)).start()


def _tail_copies(zero, remote, sem, tstart_ref, tn_ref, *, wait, priority=0):
    def body(e, carry):
        rows = tn_ref[e]

        @pl.when(rows > 0)
        def _():
            n = pl.multiple_of(rows, ROW_BLOCK)
            rem = remote.at[pl.ds(pl.multiple_of(tstart_ref[e], ROW_BLOCK), n)]
            _start_or_wait(pltpu.make_async_copy(zero.at[pl.ds(0, n)], rem, sem), wait, priority)

        return carry

    lax.fori_loop(0, N_EXPERTS, body, 0)


def _unused_tile_copies(zero, remote, sem, nt_ref, total_tiles, *, wait, priority=0):
    def body(t, carry):
        row = pl.multiple_of(t * MOE_TILE, MOE_TILE)
        _start_or_wait(pltpu.make_async_copy(zero, remote.at[pl.ds(row, MOE_TILE)], sem), wait, priority)
        return carry

    lax.fori_loop(nt_ref[0], total_tiles, body, 0)


def _experts_kernel(t0_ref, tcnt_ref, nt_ref, xs_ref, wg_ref, wu_ref, wd_ref, ys_ref,
                    xbuf, ybuf, zero, wg_s, wu_s, wd_s, xsem, ysem, zsem):
    e = pl.program_id(0)
    first_tile = t0_ref[e]
    n = tcnt_ref[e]
    n_tiles = nt_ref[0]
    total_tiles = xs_ref.shape[0] // MOE_TILE

    def tile_rows(ref, a):
        return ref.at[pl.ds(pl.multiple_of(a * MOE_TILE, MOE_TILE), MOE_TILE)]

    x_copy = lambda a: pltpu.make_async_copy(tile_rows(xs_ref, a), xbuf.at[a % 2], xsem.at[a % 2])
    y_copy = lambda a: pltpu.make_async_copy(ybuf.at[a % 2], tile_rows(ys_ref, a), ysem.at[a % 2])

    last = e == N_EXPERTS - 1

    @pl.when(last)
    def _():
        zero[...] = jnp.zeros(zero.shape, MOE_DT)
        _unused_tile_copies(zero, ys_ref, zsem, nt_ref, total_tiles, wait=False)

    @pl.when(n > 0)
    def _():
        @pl.when(first_tile == 0)
        def _():
            x_copy(first_tile).start(priority=1)

        wg_s[...] = wg_ref[0, 0].astype(BF16)
        wu_s[...] = wu_ref[0, 0].astype(BF16)
        wd_s[...] = wd_ref[0, 0].astype(BF16)

        def body(a, carry):
            slot = a % 2
            x_copy(a).wait()

            @pl.when(a + 1 < n_tiles)
            def _():
                x_copy(a + 1).start(priority=1)

            @pl.when(a >= 2)
            def _():
                y_copy(a - 2).wait()

            x = xbuf[slot].astype(BF16)
            g = _dot(x, wg_s[...])
            u = _dot(x, wu_s[...])
            hid = (g * (1.0 / (1.0 + jnp.exp(-g)))) * u
            ybuf[slot] = _dot(hid.astype(BF16), wd_s[...]).astype(MOE_DT)
            y_copy(a).start(priority=1)
            return carry

        lax.fori_loop(first_tile, first_tile + n, body, 0)

    @pl.when(last)
    def _():
        y_copy(n_tiles - 1).wait()

        @pl.when(n_tiles >= 2)
        def _():
            y_copy(n_tiles - 2).wait()

        _unused_tile_copies(zero, ys_ref, zsem, nt_ref, total_tiles, wait=True)


def _experts(xs, plan, w_gate, w_up, w_down, layer):
    r, d = xs.shape
    de = w_gate.shape[3]
    wspec = lambda a, b: pl.BlockSpec((1, 1, a, b), lambda e, *_: (layer, e, 0, 0))
    any_spec = pl.BlockSpec(memory_space=pl.ANY)
    return pl.pallas_call(
        _experts_kernel,
        out_shape=jax.ShapeDtypeStruct((r, d), MOE_DT),
        grid_spec=pltpu.PrefetchScalarGridSpec(
            num_scalar_prefetch=3,
            grid=(N_EXPERTS,),
            in_specs=[any_spec, wspec(d, de), wspec(d, de), wspec(de, d)],
            out_specs=any_spec,
            scratch_shapes=[pltpu.VMEM((2, MOE_TILE, d), MOE_DT), pltpu.VMEM((2, MOE_TILE, d), MOE_DT),
                            pltpu.VMEM((MOE_TILE, d), MOE_DT),
                            pltpu.VMEM((d, de), BF16), pltpu.VMEM((d, de), BF16), pltpu.VMEM((de, d), BF16),
                            pltpu.SemaphoreType.DMA((2,)), pltpu.SemaphoreType.DMA((2,)),
                            pltpu.SemaphoreType.DMA(())],
        ),
        compiler_params=_cparams("arbitrary"),
        name="experts",
    )(plan["tile0"], plan["tcnt"], plan["n_tiles"], xs, w_gate, w_up, w_down)


def _dispatch_kernel(runs_ref, tstart_ref, tn_ref, nt_ref, *rest, part_chunks):
    n_parts = len(part_chunks)
    h_refs = rest[:n_parts]
    pos_ref, xs_ref, buf, zero, sem, zsem = rest[n_parts:]
    n_chunks = sum(part_chunks)
    c = pl.program_id(0)
    slot = c % 2
    total_tiles = xs_ref.shape[0] // MOE_TILE

    @pl.when(c == 0)
    def _():
        zero[...] = jnp.zeros(zero.shape, MOE_DT)
        _tail_copies(zero, xs_ref, zsem, tstart_ref, tn_ref, wait=False, priority=1)
        _unused_tile_copies(zero, xs_ref, zsem, nt_ref, total_tiles, wait=False, priority=1)

    row = lax.broadcasted_iota(jnp.int32, (MOE_LROWS, MOE_CHUNK), 0)
    hit = (row == pos_ref[0, 0:1, :]) | (row == pos_ref[0, 1:2, :])
    sel = jnp.where(hit, 1.0, 0.0).astype(BF16)
    first = 0
    for h_ref, n in zip(h_refs, part_chunks):
        @pl.when((c >= first) & (c < first + n))
        def _(h_ref=h_ref):
            buf[slot] = _dot(sel, h_ref[...]).astype(MOE_DT)
        first += n
    _run_copies(buf.at[slot], xs_ref, sem.at[slot], runs_ref, c, to_remote=True, wait=False)

    @pl.when(c > 0)
    def _():
        _run_copies(buf.at[1 - slot], xs_ref, sem.at[1 - slot], runs_ref, c - 1, to_remote=True, wait=True)

    @pl.when(c == n_chunks - 1)
    def _():
        _run_copies(buf.at[slot], xs_ref, sem.at[slot], runs_ref, c, to_remote=True, wait=True)
        _tail_copies(zero, xs_ref, zsem, tstart_ref, tn_ref, wait=True)
        _unused_tile_copies(zero, xs_ref, zsem, nt_ref, total_tiles, wait=True)


def _dispatch(h_parts, plan):
    d = h_parts[0].shape[1]
    part_chunks = tuple(h.shape[0] // MOE_CHUNK for h in h_parts)
    in_specs = []
    first = 0
    for n in part_chunks:
        in_specs.append(pl.BlockSpec(
            (MOE_CHUNK, d), lambda i, *_, first=first, n=n: (jnp.clip(i - first, 0, n - 1), 0)))
        first += n
    in_specs.append(pl.BlockSpec((1, 8, MOE_CHUNK), lambda i, *_: (i, 0, 0)))
    return pl.pallas_call(
        functools.partial(_dispatch_kernel, part_chunks=part_chunks),
        out_shape=jax.ShapeDtypeStruct((plan["rows"], d), MOE_DT),
        grid_spec=pltpu.PrefetchScalarGridSpec(
            num_scalar_prefetch=4,
            grid=(sum(part_chunks),),
            in_specs=in_specs,
            out_specs=pl.BlockSpec(memory_space=pl.ANY),
            scratch_shapes=[pltpu.VMEM((2, MOE_LROWS, d), MOE_DT), pltpu.VMEM((MOE_TILE, d), MOE_DT),
                            pltpu.SemaphoreType.DMA((2,)), pltpu.SemaphoreType.DMA(())],
        ),
        compiler_params=_cparams("arbitrary"),
        name="dispatch",
    )(plan["runs"], plan["tstart"], plan["tn"], plan["n_tiles"], *h_parts, plan["pos"])


def _combine_kernel(runs_ref, x1_ref, pos_ref, wt_ref, g2_ref, gf_ref, ys_ref, o_ref, buf, sem,
                    *, c0, n_chunks, final_norm):
    i = pl.program_id(0)
    c = i + c0
    slot = i % 2
    gather = functools.partial(_run_copies, remote=ys_ref, runs_ref=runs_ref, to_remote=False)

    @pl.when(i == 0)
    def _():
        buf[...] = jnp.zeros(buf.shape, MOE_DT)
        gather(buf.at[slot], sem=sem.at[slot], c=c, wait=False)

    @pl.when(i + 1 < n_chunks)
    def _():
        gather(buf.at[1 - slot], sem=sem.at[1 - slot], c=c + 1, wait=False)

    gather(buf.at[slot], sem=sem.at[slot], c=c, wait=True)
    row = lax.broadcasted_iota(jnp.int32, (MOE_LROWS, MOE_CHUNK), 0)
    hit1 = row == pos_ref[0, 0:1, :]
    hit2 = row == pos_ref[0, 1:2, :]
    w_row = jnp.sum(jnp.where(hit1, wt_ref[0, 0:1, :], 0.0) + jnp.where(hit2, wt_ref[0, 1:2, :], 0.0),
                    axis=1, keepdims=True)
    sel = jnp.where(hit1 | hit2, 1.0, 0.0).astype(BF16)
    x2 = x1_ref[0] + g2_ref[0] * _dot_tn(sel, (buf[slot] * w_row).astype(BF16))
    if final_norm:
        ms = jnp.mean(x2 * x2, axis=-1, keepdims=True)
        x2 = x2 * lax.rsqrt(ms + EPS) * gf_ref[...]
    o_ref[0] = x2


def _combine(x1, plan, gate, g_final, ys, *, c0, final_norm):
    b, t, d = x1.shape
    per_batch = t // MOE_CHUNK
    n_chunks = b * per_batch
    tok = lambda w: pl.BlockSpec((1, MOE_CHUNK, w), lambda i, *_: (i // per_batch, i % per_batch, 0))
    return pl.pallas_call(
        functools.partial(_combine_kernel, c0=c0, n_chunks=n_chunks, final_norm=final_norm),
        out_shape=jax.ShapeDtypeStruct((b, t, d), F32),
        grid_spec=pltpu.PrefetchScalarGridSpec(
            num_scalar_prefetch=1,
            grid=(n_chunks,),
            in_specs=[tok(d),
                      pl.BlockSpec((1, 8, MOE_CHUNK), lambda i, *_: (i + c0, 0, 0)),
                      pl.BlockSpec((1, 8, MOE_CHUNK), lambda i, *_: (i + c0, 0, 0)),
                      pl.BlockSpec((1, 1, d), lambda i, *_: (i // per_batch, 0, 0)),
                      pl.BlockSpec((1, d), lambda i, *_: (0, 0)),
                      pl.BlockSpec(memory_space=pl.ANY)],
            out_specs=tok(d),
            scratch_shapes=[pltpu.VMEM((2, MOE_LROWS, d), MOE_DT), pltpu.SemaphoreType.DMA((2,))],
        ),
        compiler_params=_cparams("arbitrary"),
        name="combine",
    )(plan["runs"], x1, plan["pos"], plan["wts"], gate, g_final, ys)


def _rope_tables(seq):
    half = HEAD_DIM // 4
    t = np.arange(seq)
    rows, cols = t // GRID_W, t % GRID_W
    freqs = ROPE_BASE ** (-np.arange(half, dtype=np.float64) / half)

    def cs(pos):
        ang = pos[None, :] * freqs[:, None]
        return np.cos(ang), np.sin(ang)

    cr, sr = cs(rows)
    cc, sc = cs(cols)
    return (jnp.asarray(np.concatenate([cr, cr, cc, cc], axis=0), F32),
            jnp.asarray(np.concatenate([-sr, sr, -sc, sc], axis=0), F32))


def _dft_tables(n):
    k = np.arange(n)
    ang = ((k[:, None] * k[None, :]) % n) * (2.0 * np.pi / n)
    return jnp.asarray(np.cos(ang), F32), jnp.asarray(np.sin(ang), F32)


def _channel_dft_t():
    c, s = _dft_tables(FOURIER_GROUP_DIM)
    eye = jnp.eye(FOURIER_GROUPS, dtype=F32)
    scale = FOURIER_GROUP_DIM ** -0.5
    return jnp.concatenate([jnp.kron(eye, c), jnp.kron(eye, s)], axis=0) * scale


def _position_dft(n):
    c, s = _dft_tables(n)
    scale = n ** -0.5
    return (c * scale).astype(BF16), (s * scale).astype(BF16)


def _swa_mask():
    kk = np.arange(SWA_BLOCK)[:, None]
    q = np.arange(SWA_BLOCK)[None, :]
    tiles = []
    for block_offset in (-SWA_BLOCK, SWA_BLOCK):
        ok = np.abs(block_offset + kk - q) <= SWA_WINDOW
        tiles.append(np.tile(np.where(ok, 0.0, NEG_INF), (1, SWA_GROUP)))
    return jnp.asarray(np.stack(tiles), F32)


def _sink_rows(sink, width):
    return jnp.repeat(sink.astype(F32).reshape(SWA_KV_HEADS, SWA_GROUP) * LOG2E, width, axis=1)[:, None, :]


def _route_chunks(route):
    r = jnp.moveaxis(route, 1, 0).reshape(ROUTE_OUT, -1, MOE_CHUNK)
    return jnp.swapaxes(r, 0, 1)


def _moe_plan(route_chunks):
    i32 = jnp.int32
    nc = route_chunks.shape[0]
    n = nc * MOE_CHUNK
    ids = route_chunks[:, 0:2, :].astype(i32).reshape(nc, 2 * MOE_CHUNK)
    onehot = (ids[:, :, None] == jnp.arange(N_EXPERTS, dtype=i32)).astype(i32)
    pair = jnp.arange(2 * MOE_CHUNK, dtype=i32)
    earlier = (pair[None, :] < pair[:, None]).astype(BF16)
    before = jnp.einsum("pq,cqe->cpe", earlier, onehot.astype(BF16), preferred_element_type=F32).astype(i32)
    rank = jnp.sum(before * onehot, axis=-1)
    cnt = jnp.sum(onehot, axis=1)
    run = (cnt + ROW_BLOCK - 1) // ROW_BLOCK * ROW_BLOCK
    lo = jnp.cumsum(run, axis=1) - run
    pos = (jnp.sum(onehot * lo[:, None, :], axis=-1) + rank).reshape(nc, 2, MOE_CHUNK)
    pos = jnp.concatenate([pos, jnp.full((nc, 6, MOE_CHUNK), -1, i32)], axis=1)
    seg = jnp.sum(run, axis=0)
    padded = (seg + MOE_TILE - 1) // MOE_TILE * MOE_TILE
    start = jnp.cumsum(padded) - padded
    off = start[None, :] + jnp.cumsum(run, axis=0) - run
    wts = jnp.concatenate([route_chunks[:, 2:4, :], jnp.zeros((nc, 6, MOE_CHUNK), F32)], axis=1)
    rows = -(-(2 * n + (ROW_BLOCK - 1) * N_EXPERTS * nc + (MOE_TILE - 1) * N_EXPERTS) // MOE_TILE) * MOE_TILE
    n_tiles = jnp.sum(padded) // MOE_TILE
    total = jnp.broadcast_to(jnp.sum(run, axis=1, keepdims=True), run.shape)
    return {"pos": pos, "wts": wts, "runs": jnp.stack([run, lo, off, total]).astype(i32),
            "tstart": (start + seg).astype(i32), "tn": (padded - seg).astype(i32),
            "tile0": (start // MOE_TILE).astype(i32), "tcnt": (padded // MOE_TILE).astype(i32),
            "n_tiles": n_tiles.astype(i32).reshape(1), "rows": rows}


def kernel(x, c, ctx, c_ctx, w_mod, b_mod, g_norm1, g_norm2, w_in, w_four, w_out, swa_sink, na_rpb,
           w_route_group, b_route_group, w_route_expert, b_route_expert, w_exp_gate, w_exp_up,
           w_exp_down, g_final):
    b, s, d = x.shape
    lc = ctx.shape[1]
    depth = w_mod.shape[0]

    c_rows = jnp.concatenate([c, c_ctx[None, :], jnp.zeros((7, d), F32)], axis=0)
    mod = _modulation(c_rows, w_mod, b_mod)

    cos_t, sin_t = _rope_tables(s)
    cos_c, sin_c = cos_t[:, :lc], sin_t[:, :lc]
    bd_t = _channel_dft_t().astype(BF16)
    cn, sn = _position_dft(s)
    cn_c, sn_c = _position_dft(lc)
    mask = _swa_mask()
    route_pad = ROUTE_ROWS - N_GROUPS - N_EXPERTS

    xc = ctx
    for layer in range(depth):
        with_ctx_out = layer < depth - 1
        lat = [mod[layer, :b, i * d:(i + 1) * d][:, None, :] for i in range(6)]
        cx = [jnp.broadcast_to(mod[layer, b, i * d:(i + 1) * d][None, None, :], (b, 1, d)) for i in range(6)]
        sh1, sc1, g1, sh2, sc2, g2 = lat
        shc1, scc1, gc1, shc2, scc2, gc2 = cx
        gn1 = g_norm1[layer][None, :]
        gn2 = g_norm2[layer][None, :]
        w_t = w_in[layer].T.astype(BF16)
        wf_t = w_four[layer].T.astype(BF16)
        wo = w_out[layer].astype(BF16)
        w_r = jnp.concatenate([w_route_group[layer].T, w_route_expert[layer].T,
                               jnp.zeros((route_pad, d), F32)], axis=0)
        wr_hi = w_r.astype(BF16)
        wr_lo = (w_r - wr_hi.astype(F32)).astype(BF16)
        b_r = jnp.concatenate([b_route_group[layer], b_route_expert[layer], jnp.zeros((route_pad,), F32)])[:, None]
        sink_lat = _sink_rows(swa_sink[layer], SWA_BLOCK)
        sink_ctx = _sink_rows(swa_sink[layer], lc)

        fz, qs, qn, ks, vs, kn, vn = _in_proj(x, sh1, sc1, gn1, w_t, bd_t, cos_t, sin_t,
                                              with_q=True, rope=True, tm=PROJ_TILE)
        if with_ctx_out:
            fz_c, qs_c, qn_c, ks_c, vs_c, kn_c, vn_c = _in_proj(xc, shc1, scc1, gn1, w_t, bd_t, cos_c, sin_c,
                                                                with_q=True, rope=False, tm=lc)
        else:
            ks_c, vs_c, kn_c, vn_c = _in_proj(xc, shc1, scc1, gn1, w_t[KS_LO:], bd_t, cos_c, sin_c,
                                              with_q=False, rope=False, tm=lc)

        yf = _fourier(fz, cn, sn, wf_t, tk=FOURIER_TILE)
        ys = _swa(qs, ks, vs, ks_c, vs_c, mask, sink_lat)
        yn = _na(qn, kn, vn, kn_c, vn_c, _na_bias(na_rpb[layer], s))
        x1, h2, route = _out_proj(x, yf, ys, yn, wo, g1, sh2, sc2, gn2, wr_hi, wr_lo, b_r, tm=PROJ_TILE)

        moe_w = (w_exp_gate, w_exp_up, w_exp_down, layer)
        if with_ctx_out:
            yf_c = _fourier(fz_c, cn_c, sn_c, wf_t, tk=lc)
            ys_c, yn_c = _ctx_attn(qs_c, ks_c, vs_c, qn_c, kn_c, vn_c, sink_ctx)
            xc1, hc2, route_c = _out_proj(xc, yf_c, ys_c, yn_c, wo, gc1, shc2, scc2, gn2, wr_hi, wr_lo, b_r, tm=lc)
            n_lat = b * s
            lat_chunks = n_lat // MOE_CHUNK
            plan = _moe_plan(jnp.concatenate([_route_chunks(route), _route_chunks(route_c)], axis=0))
            xs = _dispatch([h2.reshape(n_lat, d), hc2.reshape(b * lc, d)], plan)
            ye = _experts(xs, plan, *moe_w)
            x = _combine(x1, plan, g2, g_final[None, :], ye, c0=0, final_norm=False)
            ctx_chunks = b * lc // MOE_CHUNK
            xc = _combine(xc1.reshape(ctx_chunks, MOE_CHUNK, d), plan, gc2[:ctx_chunks], g_final[None, :], ye,
                          c0=lat_chunks, final_norm=False).reshape(b, lc, d)
        else:
            plan = _moe_plan(_route_chunks(route))
            xs = _dispatch([h2.reshape(b * s, d)], plan)
            ye = _experts(xs, plan, *moe_w)
            x = _combine(x1, plan, g2, g_final[None, :], ye, c0=0, final_norm=True)
    return x
```

```python
import functools

import jax
import jax.numpy as jnp
import numpy as np
from jax import lax
from jax.experimental import pallas as pl
from jax.experimental.pallas import tpu as pltpu

F32 = jnp.float32
BF16 = jnp.bfloat16

D_MODEL = 1024
GRID_W = 64
HEAD_DIM = 64
FOURIER_WIDTH = D_MODEL // 4
FOURIER_GROUPS = 4
FOURIER_GROUP_DIM = FOURIER_WIDTH // FOURIER_GROUPS
SWA_HEADS = (3 * D_MODEL // 8) // HEAD_DIM
SWA_KV_HEADS = 2
SWA_GROUP = SWA_HEADS // SWA_KV_HEADS
SWA_WINDOW = 128
SWA_BLOCK = 128
NA_HEADS = (3 * D_MODEL // 8) // HEAD_DIM
NA_WIN_R = 8
NA_WIN_C = 16
ROPE_BASE = 10000.0
N_GROUPS = 4
EXPERTS_PER_GROUP = 8
N_EXPERTS = N_GROUPS * EXPERTS_PER_GROUP
D_EXPERT = D_MODEL // 2
EPS = 1e-6
NEG_INF = -1e30
LOG2E = 1.4426950408889634

SWA_Q_W = SWA_HEADS * HEAD_DIM
SWA_KV_W = SWA_KV_HEADS * HEAD_DIM
NA_W = NA_HEADS * HEAD_DIM
MIX_WIDTH = FOURIER_WIDTH + SWA_Q_W + NA_W
Q_COLS = MIX_WIDTH
IN_COLS = 2 * MIX_WIDTH

F_LO, F_HI = 0, FOURIER_WIDTH
QS_LO, QS_HI = F_HI, F_HI + SWA_Q_W
QN_LO, QN_HI = QS_HI, QS_HI + NA_W
KS_LO, KS_HI = QN_HI, QN_HI + SWA_KV_W
VS_LO, VS_HI = KS_HI, KS_HI + SWA_KV_W
KN_LO, KN_HI = VS_HI, VS_HI + NA_W
VN_LO, VN_HI = KN_HI, KN_HI + NA_W

LANE = 128
SUBLANE = 8
ROUTE_ROWS = -(-(N_GROUPS + N_EXPERTS) // SUBLANE) * SUBLANE
ROUTE_OUT = SUBLANE
NA_QROWS = 4
NA_KROWS = NA_QROWS + NA_WIN_R
ATTN_KCHUNK = LANE
ATTN_LOOKAHEAD = 2
NA_HEADS_PER_STEP = 2
PROJ_SUBTILE = LANE
MOE_TILE = 512
MOE_DT = F32
ROW_BLOCK = 8
MOE_CHUNK = 512
MOE_LROWS = 2 * MOE_CHUNK + (ROW_BLOCK - 1) * N_EXPERTS
VMEM_LIMIT = 48 * 1024 * 1024


def _cparams(*sem):
    return pltpu.CompilerParams(dimension_semantics=sem, vmem_limit_bytes=VMEM_LIMIT)


def _dot(a, b):
    return jnp.dot(a, b, preferred_element_type=F32)


def _dot_tn(a, b):
    return lax.dot_general(a, b, (((0,), (0,)), ((), ())), preferred_element_type=F32)


def _dot_nt(a, b):
    return lax.dot_general(a, b, (((1,), (1,)), ((), ())), preferred_element_type=F32)


def _split_dot(a, w):
    a_hi = a.astype(BF16)
    a_lo = (a - a_hi.astype(F32)).astype(BF16)
    w_hi = w.astype(BF16)
    w_lo = (w - w_hi.astype(F32)).astype(BF16)
    return _dot(a_hi, w_hi) + (_dot(a_hi, w_lo) + _dot(a_lo, w_hi))


def _mod_kernel(c_ref, w_ref, b_ref, o_ref):
    c = c_ref[...]
    a = c * (1.0 / (1.0 + jnp.exp(-c)))
    o_ref[0] = _split_dot(a, w_ref[0]) + b_ref[0]


def _modulation(c_rows, w_mod, b_mod):
    depth, d, n6 = w_mod.shape
    r = c_rows.shape[0]
    tn = 1536
    return pl.pallas_call(
        _mod_kernel,
        out_shape=jax.ShapeDtypeStruct((depth, r, n6), F32),
        grid=(depth, n6 // tn),
        in_specs=[
            pl.BlockSpec((r, d), lambda l, j: (0, 0)),
            pl.BlockSpec((1, d, tn), lambda l, j: (l, 0, j)),
            pl.BlockSpec((1, 1, tn), lambda l, j: (l, 0, j)),
        ],
        out_specs=pl.BlockSpec((1, r, tn), lambda l, j: (l, 0, j)),
        compiler_params=_cparams("parallel", "parallel"),
        name="modulation",
    )(c_rows, w_mod, b_mod.reshape(depth, 1, n6))


def _rope_rows(t, cos_t, sin_t, n_heads):
    outs = []
    for h in range(n_heads):
        th = t[HEAD_DIM * h:HEAD_DIM * (h + 1)]
        sw = jnp.concatenate([th[16:32], th[0:16], th[48:64], th[32:48]], axis=0)
        outs.append(th * cos_t + sw * sin_t)
    return jnp.concatenate(outs, axis=0)


def _in_proj_kernel(x_ref, sh_ref, sc_ref, g_ref, wt_ref, bdt_ref, cos_ref, sin_ref, *outs, with_q, rope):
    xf = x_ref[0]
    ms = jnp.mean(xf * xf, axis=-1, keepdims=True)
    y = xf * lax.rsqrt(ms + EPS) * g_ref[...]
    h = y * (1.0 + sc_ref[0]) + sh_ref[0]
    pt = _dot_nt(wt_ref[...], h.astype(BF16))
    q_scale = HEAD_DIM ** -0.5 * LOG2E
    if with_q:
        fz_ref, qs_ref, qn_ref, ks_ref, vs_ref, kn_ref, vn_ref = outs
        fz_ref[0] = _dot(bdt_ref[...], pt[F_LO:F_HI].astype(BF16)).astype(BF16)
        qs = pt[QS_LO:QS_HI]
        if rope:
            qs = _rope_rows(qs, cos_ref[...], sin_ref[...], SWA_HEADS)
        qs_ref[0] = (qs * q_scale).astype(BF16)
        qn_ref[0] = (pt[QN_LO:QN_HI] * q_scale).astype(BF16)
        off = 0
    else:
        ks_ref, vs_ref, kn_ref, vn_ref = outs
        off = KS_LO
    ks = pt[KS_LO - off:KS_HI - off]
    if rope:
        ks = _rope_rows(ks, cos_ref[...], sin_ref[...], SWA_KV_HEADS)
    ks_ref[0] = ks.astype(BF16)
    vs_ref[0] = pt[VS_LO - off:VS_HI - off].astype(BF16)
    kn_ref[0] = pt[KN_LO - off:KN_HI - off].astype(BF16)
    vn_ref[0] = pt[VN_LO - off:VN_HI - off].astype(BF16)


def _in_proj(x, shift, scale, gain, w_t, bd_t, cos_t, sin_t, *, with_q, rope, tm):
    b, t, d = x.shape
    nf = w_t.shape[0]
    rows = ([2 * FOURIER_WIDTH, SWA_Q_W, NA_W] if with_q else []) + [SWA_KV_W, SWA_KV_W, NA_W, NA_W]
    return pl.pallas_call(
        functools.partial(_in_proj_kernel, with_q=with_q, rope=rope),
        out_shape=[jax.ShapeDtypeStruct((b, r, t), BF16) for r in rows],
        grid=(b, t // tm),
        in_specs=[
            pl.BlockSpec((1, tm, d), lambda i, j: (i, j, 0)),
            pl.BlockSpec((1, 1, d), lambda i, j: (i, 0, 0)),
            pl.BlockSpec((1, 1, d), lambda i, j: (i, 0, 0)),
            pl.BlockSpec((1, d), lambda i, j: (0, 0)),
            pl.BlockSpec((nf, d), lambda i, j: (0, 0)),
            pl.BlockSpec(bd_t.shape, lambda i, j: (0, 0)),
            pl.BlockSpec((HEAD_DIM, tm), lambda i, j: (0, j)),
            pl.BlockSpec((HEAD_DIM, tm), lambda i, j: (0, j)),
        ],
        out_specs=[pl.BlockSpec((1, r, tm), lambda i, j: (i, 0, j)) for r in rows],
        compiler_params=_cparams("parallel", "parallel"),
        name="in_proj_q" if with_q else "in_proj_kv",
    )(x, shift, scale, gain, w_t, bd_t, cos_t, sin_t)


def _fourier_kernel(fz_ref, cn_ref, sn_ref, wft_ref, o_ref):
    zc = fz_ref[0, 0:FOURIER_WIDTH, :]
    zs = fz_ref[0, FOURIER_WIDTH:2 * FOURIER_WIDTH, :]
    y = _dot(zc, cn_ref[...]) - _dot(zs, sn_ref[...])
    o_ref[0] = _dot(wft_ref[...], y.astype(BF16)).astype(BF16)


def _fourier(fz, cn, sn, wf_t, *, tk):
    b, _, t = fz.shape
    return pl.pallas_call(
        _fourier_kernel,
        out_shape=jax.ShapeDtypeStruct((b, FOURIER_WIDTH, t), BF16),
        grid=(t // tk, b),
        in_specs=[
            pl.BlockSpec((1, 2 * FOURIER_WIDTH, t), lambda k, i: (i, 0, 0)),
            pl.BlockSpec((t, tk), lambda k, i: (0, k)),
            pl.BlockSpec((t, tk), lambda k, i: (0, k)),
            pl.BlockSpec((FOURIER_WIDTH, FOURIER_WIDTH), lambda k, i: (0, 0)),
        ],
        out_specs=pl.BlockSpec((1, FOURIER_WIDTH, tk), lambda k, i: (i, 0, k)),
        compiler_params=_cparams("parallel", "parallel"),
        name="fourier",
    )(fz, cn, sn, wf_t)


def _key_chunks(k, v, bias=None):
    n = k.shape[1] // ATTN_KCHUNK
    cut = lambda a, j, axis: lax.slice_in_dim(a, j * ATTN_KCHUNK, (j + 1) * ATTN_KCHUNK, axis=axis)
    return [(cut(k, j, 1), cut(v, j, 1), None if bias is None else cut(bias, j, 0)) for j in range(n)]


def _logits(q_t, chunks):
    return _dot_tn(jnp.concatenate([k_t for k_t, _, _ in chunks], axis=1), q_t)


def _softmax_pv(s, chunks, sink_row):
    pieces = []
    off = 0
    for k_t, _, bias in chunks:
        piece = s[off:off + k_t.shape[1]]
        pieces.append(piece if bias is None else piece + bias)
        off += k_t.shape[1]
    m = functools.reduce(jnp.maximum, [jnp.max(p, axis=0, keepdims=True) for p in pieces])
    if sink_row is not None:
        m = jnp.maximum(m, sink_row)
    probs = [jnp.exp2(p - m) for p in pieces]
    den = functools.reduce(jnp.add, [jnp.sum(p, axis=0, keepdims=True) for p in probs])
    if sink_row is not None:
        den = den + jnp.exp2(sink_row - m)
    v_all = jnp.concatenate([v_t for _, v_t, _ in chunks], axis=1)
    p_all = jnp.concatenate([p.astype(BF16) for p in probs], axis=0)
    return _dot(v_all, p_all) / den


def _attend(q_t, chunks, sink_row):
    return _softmax_pv(_logits(q_t, chunks), chunks, sink_row)


def _attend_blocks(n_blocks, make_block, sink_row, store):
    blocks, logits = {}, {}
    for j in range(min(ATTN_LOOKAHEAD, n_blocks)):
        blocks[j] = make_block(j)
        logits[j] = _logits(*blocks[j])
    for j in range(n_blocks):
        ahead = j + ATTN_LOOKAHEAD
        if ahead < n_blocks:
            blocks[ahead] = make_block(ahead)
            logits[ahead] = _logits(*blocks[ahead])
        store(j, _softmax_pv(logits.pop(j), blocks.pop(j)[1], sink_row))


def _swa_kernel(q_ref, k_ref, v_ref, kc_ref, vc_ref, mask_ref, sink_ref, o_ref, *, seq):
    nb = seq // SWA_BLOCK
    ctx_chunks = _key_chunks(kc_ref[0], vc_ref[0])

    def make_block(n):
        q0 = n * SWA_BLOCK
        q_t = jnp.concatenate(
            [q_ref[0, HEAD_DIM * h:HEAD_DIM * (h + 1), q0:q0 + SWA_BLOCK] for h in range(SWA_GROUP)], axis=1)
        chunks = [(k_ref[0, :, q0:q0 + SWA_BLOCK], v_ref[0, :, q0:q0 + SWA_BLOCK], None)] + ctx_chunks
        for side, kb in ((0, n - 1), (1, n + 1)):
            if 0 <= kb < nb:
                k0 = kb * SWA_BLOCK
                chunks.append((k_ref[0, :, k0:k0 + SWA_BLOCK], v_ref[0, :, k0:k0 + SWA_BLOCK], mask_ref[side]))
        return q_t, chunks

    def store(n, o):
        q0 = n * SWA_BLOCK
        for h in range(SWA_GROUP):
            o_ref[0, HEAD_DIM * h:HEAD_DIM * (h + 1), q0:q0 + SWA_BLOCK] = (
                o[:, SWA_BLOCK * h:SWA_BLOCK * (h + 1)].astype(BF16))

    _attend_blocks(nb, make_block, sink_ref[0], store)


def _swa(qs, ks, vs, kc, vc, mask, sink_rows):
    b, _, t = qs.shape
    lc = kc.shape[2]
    gw = SWA_GROUP * HEAD_DIM
    return pl.pallas_call(
        functools.partial(_swa_kernel, seq=t),
        out_shape=jax.ShapeDtypeStruct((b, SWA_Q_W, t), BF16),
        grid=(b, SWA_KV_HEADS),
        in_specs=[
            pl.BlockSpec((1, gw, t), lambda i, g: (i, g, 0)),
            pl.BlockSpec((1, HEAD_DIM, t), lambda i, g: (i, g, 0)),
            pl.BlockSpec((1, HEAD_DIM, t), lambda i, g: (i, g, 0)),
            pl.BlockSpec((1, HEAD_DIM, lc), lambda i, g: (i, g, 0)),
            pl.BlockSpec((1, HEAD_DIM, lc), lambda i, g: (i, g, 0)),
            pl.BlockSpec((2, SWA_BLOCK, SWA_GROUP * SWA_BLOCK), lambda i, g: (0, 0, 0)),
            pl.BlockSpec((1, 1, SWA_GROUP * SWA_BLOCK), lambda i, g: (g, 0, 0)),
        ],
        out_specs=pl.BlockSpec((1, gw, t), lambda i, g: (i, g, 0)),
        compiler_params=_cparams("parallel", "parallel"),
        name="swa",
    )(qs, ks, vs, kc, vc, mask, sink_rows)


def _na_bias_kernel(rpb_ref, o_ref, u_ref, *, total_rows):
    hd = pl.program_id(0)
    kc = lax.broadcasted_iota(jnp.int32, (GRID_W, LANE), 0)
    lane = lax.broadcasted_iota(jnp.int32, (GRID_W, LANE), 1)
    qc = lane % GRID_W
    dc = jnp.clip(kc - qc, -(NA_WIN_C - 1), NA_WIN_C - 1) + (NA_WIN_C - 1)
    c0 = jnp.clip(qc - NA_WIN_C // 2, 0, GRID_W - NA_WIN_C)
    valid_c = (kc >= c0) & (kc < c0 + NA_WIN_C)
    n_dr = 2 * NA_WIN_R - 1
    for dr in range(n_dr):
        u = jnp.full((GRID_W, LANE), NEG_INF, F32)
        for d in range(2 * NA_WIN_C - 1):
            u = jnp.where(valid_c & (dc == d), rpb_ref[hd, dr, d] * LOG2E, u)
        u_ref[dr] = u
    n_rows = o_ref.shape[2] // GRID_W
    block_types = [(0, 0), (NA_QROWS, 0), (total_rows - NA_QROWS, total_rows - NA_KROWS)]
    neg = jnp.full((GRID_W, LANE), NEG_INF, F32)
    for t, (r_base, k_base) in enumerate(block_types):
        for kl in range(n_rows):
            kr = k_base + kl
            for lg in range(NA_QROWS // 2):
                halves = []
                for rq in (2 * lg, 2 * lg + 1):
                    r = r_base + rq
                    r0 = min(max(r - NA_WIN_R // 2, 0), total_rows - NA_WIN_R)
                    ok = r0 <= kr < r0 + NA_WIN_R
                    halves.append(u_ref[kr - r + NA_WIN_R - 1] if ok else neg)
                o_ref[0, t, GRID_W * kl:GRID_W * (kl + 1), LANE * lg:LANE * (lg + 1)] = jnp.where(
                    lane < GRID_W, halves[0], halves[1])


def _na_bias(rpb, seq):
    nh = rpb.shape[0]
    return pl.pallas_call(
        functools.partial(_na_bias_kernel, total_rows=seq // GRID_W),
        out_shape=jax.ShapeDtypeStruct((nh, 3, NA_KROWS * GRID_W, NA_QROWS * GRID_W), F32),
        grid=(nh,),
        in_specs=[pl.BlockSpec(memory_space=pltpu.SMEM)],
        out_specs=pl.BlockSpec((1, 3, NA_KROWS * GRID_W, NA_QROWS * GRID_W), lambda h: (h, 0, 0, 0)),
        scratch_shapes=[pltpu.VMEM((2 * NA_WIN_R - 1, GRID_W, LANE), F32)],
        compiler_params=_cparams("parallel"),
        name="na_bias",
    )(rpb)


def _na_kernel(q_ref, k_ref, v_ref, kc_ref, vc_ref, bias_ref, o_ref, *, seq):
    n_rows = seq // GRID_W
    qw = NA_QROWS * GRID_W
    kw = NA_KROWS * GRID_W
    nblk = n_rows // NA_QROWS
    rows_per_chunk = ATTN_KCHUNK // GRID_W
    head_rows = [slice(HEAD_DIM * h, HEAD_DIM * (h + 1)) for h in range(NA_HEADS_PER_STEP)]
    ctx_chunks = [_key_chunks(kc_ref[0, hs, :], vc_ref[0, hs, :]) for hs in head_rows]

    def make_block(jj):
        h, j = divmod(jj, nblk)
        hs = head_rows[h]
        k_row = min(max(j * NA_QROWS - NA_WIN_R // 2, 0), n_rows - NA_KROWS)
        btype = 0 if j == 0 else (2 if j == nblk - 1 else 1)
        q0 = j * qw
        chunks = list(ctx_chunks[h])
        for cj in range(kw // ATTN_KCHUNK):
            first = k_row + cj * rows_per_chunk
            in_window = False
            for r in range(j * NA_QROWS, (j + 1) * NA_QROWS):
                r0 = min(max(r - NA_WIN_R // 2, 0), n_rows - NA_WIN_R)
                in_window = in_window or (first < r0 + NA_WIN_R and first + rows_per_chunk > r0)
            if in_window:
                k0 = first * GRID_W
                chunks.append((k_ref[0, hs, k0:k0 + ATTN_KCHUNK], v_ref[0, hs, k0:k0 + ATTN_KCHUNK],
                               bias_ref[h, btype, cj * ATTN_KCHUNK:(cj + 1) * ATTN_KCHUNK, :]))
        return q_ref[0, hs, q0:q0 + qw], chunks

    def store(jj, o):
        h, j = divmod(jj, nblk)
        o_ref[0, head_rows[h], j * qw:(j + 1) * qw] = o.astype(BF16)

    _attend_blocks(nblk * NA_HEADS_PER_STEP, make_block, None, store)


def _na(qn, kn, vn, kc, vc, bias):
    b, _, t = qn.shape
    lc = kc.shape[2]
    hw = HEAD_DIM * NA_HEADS_PER_STEP
    head = lambda i, h: (i, h, 0)
    return pl.pallas_call(
        functools.partial(_na_kernel, seq=t),
        out_shape=jax.ShapeDtypeStruct((b, NA_W, t), BF16),
        grid=(b, NA_HEADS // NA_HEADS_PER_STEP),
        in_specs=[
            pl.BlockSpec((1, hw, t), head),
            pl.BlockSpec((1, hw, t), head),
            pl.BlockSpec((1, hw, t), head),
            pl.BlockSpec((1, hw, lc), head),
            pl.BlockSpec((1, hw, lc), head),
            pl.BlockSpec((NA_HEADS_PER_STEP,) + bias.shape[1:], lambda i, h: (h, 0, 0, 0)),
        ],
        out_specs=pl.BlockSpec((1, hw, t), head),
        compiler_params=_cparams("parallel", "parallel"),
        name="na",
    )(qn, kn, vn, kc, vc, bias)


def _ctx_attn_kernel(qs_ref, ks_ref, vs_ref, qn_ref, kn_ref, vn_ref, sink_ref, ys_ref, yn_ref):
    lc = qs_ref.shape[2]
    for g in range(SWA_KV_HEADS):
        q_t = jnp.concatenate(
            [qs_ref[0, HEAD_DIM * (SWA_GROUP * g + h):HEAD_DIM * (SWA_GROUP * g + h + 1), :] for h in range(SWA_GROUP)],
            axis=1)
        kv = slice(HEAD_DIM * g, HEAD_DIM * (g + 1))
        o = _attend(q_t, _key_chunks(ks_ref[0, kv, :], vs_ref[0, kv, :]), sink_ref[g])
        for h in range(SWA_GROUP):
            hh = SWA_GROUP * g + h
            ys_ref[0, HEAD_DIM * hh:HEAD_DIM * (hh + 1), :] = o[:, lc * h:lc * (h + 1)].astype(BF16)
    for h in range(NA_HEADS):
        sl = slice(HEAD_DIM * h, HEAD_DIM * (h + 1))
        o = _attend(qn_ref[0, sl, :], _key_chunks(kn_ref[0, sl, :], vn_ref[0, sl, :]), None)
        yn_ref[0, sl, :] = o.astype(BF16)


def _ctx_attn(qs, ks, vs, qn, kn, vn, sink_rows):
    b, _, lc = qs.shape
    full = lambda a: pl.BlockSpec((1,) + a.shape[1:], lambda i: (i, 0, 0))
    return pl.pallas_call(
        _ctx_attn_kernel,
        out_shape=[jax.ShapeDtypeStruct((b, SWA_Q_W, lc), BF16), jax.ShapeDtypeStruct((b, NA_W, lc), BF16)],
        grid=(b,),
        in_specs=[full(qs), full(ks), full(vs), full(qn), full(kn), full(vn),
                  pl.BlockSpec(sink_rows.shape, lambda i: (0, 0, 0))],
        out_specs=[pl.BlockSpec((1, SWA_Q_W, lc), lambda i: (i, 0, 0)),
                   pl.BlockSpec((1, NA_W, lc), lambda i: (i, 0, 0))],
        compiler_params=_cparams("parallel"),
        name="ctx_attn",
    )(qs, ks, vs, qn, kn, vn, sink_rows)


def _route(logits):
    row = lax.broadcasted_iota(jnp.int32, logits.shape, 0)
    big = jnp.int32(logits.shape[0])
    colmax = lambda a: jnp.max(a, axis=0, keepdims=True)
    first = lambda hit: jnp.min(jnp.where(hit, row, big), axis=0, keepdims=True)
    gmask = row < N_GROUPS
    gl = jnp.where(gmask, logits, NEG_INF)
    gmax = colmax(gl)
    g_sel = first(gl == gmax)
    p_g = 1.0 / jnp.sum(jnp.where(gmask, jnp.exp(logits - gmax), 0.0), axis=0, keepdims=True)
    lo = N_GROUPS + EXPERTS_PER_GROUP * g_sel
    el = jnp.where((row >= lo) & (row < lo + EXPERTS_PER_GROUP), logits, NEG_INF)
    v1 = colmax(el)
    i1 = first(el == v1)
    el2 = jnp.where(row == i1, NEG_INF, el)
    v2 = colmax(el2)
    i2 = first(el2 == v2)
    e21 = jnp.exp(v2 - v1)
    w1 = p_g / (1.0 + e21)
    w2 = p_g * e21 / (1.0 + e21)
    out_row = lax.broadcasted_iota(jnp.int32, (ROUTE_OUT, logits.shape[1]), 0)
    out = jnp.where(out_row == 0, (i1 - N_GROUPS).astype(F32), 0.0)
    out = jnp.where(out_row == 1, (i2 - N_GROUPS).astype(F32), out)
    out = jnp.where(out_row == 2, w1, out)
    return jnp.where(out_row == 3, w2, out)


def _out_proj_kernel(x_ref, yf_ref, ys_ref, yn_ref, wo_ref, g1_ref, sh_ref, sc_ref, gn_ref, wrh_ref, wrl_ref,
                     br_ref, x1_ref, h2_ref, rt_ref):
    toks = [slice(i, i + PROJ_SUBTILE) for i in range(0, x_ref.shape[1], PROJ_SUBTILE)]
    ys = [_dot_tn(jnp.concatenate([yf_ref[0, :, tok], ys_ref[0, :, tok], yn_ref[0, :, tok]], axis=0), wo_ref[...])
          for tok in toks]
    for tok, y in zip(toks, ys):
        x1 = x_ref[0, tok, :] + g1_ref[0] * y
        x1_ref[0, tok, :] = x1
        ms = jnp.mean(x1 * x1, axis=-1, keepdims=True)
        h2 = (x1 * lax.rsqrt(ms + EPS) * gn_ref[...]) * (1.0 + sc_ref[0]) + sh_ref[0]
        h_hi = h2.astype(BF16)
        h2_ref[0, tok, :] = h_hi
        h_lo = (h2 - h_hi.astype(F32)).astype(BF16)
        logits = (_dot_nt(wrh_ref[...], h_hi) + (_dot_nt(wrh_ref[...], h_lo) + _dot_nt(wrl_ref[...], h_hi))
                  + br_ref[...])
        rt_ref[0, :, tok] = _route(logits)


def _out_proj(x, yf, ys, yn, w_out, gate, shift, scale, gain, wr_hi, wr_lo, b_route, *, tm):
    b, t, d = x.shape
    vec = pl.BlockSpec((1, 1, d), lambda i, j: (i, 0, 0))
    feat = lambda r: pl.BlockSpec((1, r, tm), lambda i, j: (i, 0, j))
    tok = lambda w: pl.BlockSpec((1, tm, w), lambda i, j: (i, j, 0))
    whole = lambda a: pl.BlockSpec(a.shape, lambda i, j: (0, 0))
    return pl.pallas_call(
        _out_proj_kernel,
        out_shape=[jax.ShapeDtypeStruct((b, t, d), F32), jax.ShapeDtypeStruct((b, t, d), BF16),
                   jax.ShapeDtypeStruct((b, ROUTE_OUT, t), F32)],
        grid=(b, t // tm),
        in_specs=[tok(d), feat(FOURIER_WIDTH), feat(SWA_Q_W), feat(NA_W), whole(w_out),
                  vec, vec, vec, pl.BlockSpec((1, d), lambda i, j: (0, 0)),
                  whole(wr_hi), whole(wr_lo), whole(b_route)],
        out_specs=[tok(d), tok(d), feat(ROUTE_OUT)],
        compiler_params=_cparams("parallel", "parallel"),
        name="out_proj",
    )(x, yf, ys, yn, w_out, gate, shift, scale, gain, wr_hi, wr_lo, b_route)


def _start_or_wait(copy, wait, priority=0):
    if wait:
        copy.wait()
    else:
        copy.start(priority=priority)


def _run_copies(local, remote, sem, runs_ref, c, *, to_remote, wait):
    if wait:
        n = pl.multiple_of(runs_ref[3, c, 0], ROW_BLOCK)
        loc, rem = local.at[pl.ds(0, n)], remote.at[pl.ds(0, n)]
        (pltpu.make_async_copy(loc, rem, sem) if to_remote else pltpu.make_async_copy(rem, loc, sem)).wait()
        return
    for e in range(N_EXPERTS):
        rows = runs_ref[0, c, e]

        @pl.when(rows > 0)
        def _(e=e, rows=rows):
            n = pl.multiple_of(rows, ROW_BLOCK)
            loc = local.at[pl.ds(pl.multiple_of(runs_ref[1, c, e], ROW_BLOCK), n)]
            rem = remote.at[pl.ds(pl.multiple_of(runs_ref[2, c, e], ROW_BLOCK), n)]
            (pltpu.make_async_copy(loc, rem, sem) if to_remote else pltpu.make_async_copy(rem, loc, sem)).start()


def _tail_copies(zero, remote, sem, tstart_ref, tn_ref, *, wait, priority=0):
    def body(e, carry):
        rows = tn_ref[e]

        @pl.when(rows > 0)
        def _():
            n = pl.multiple_of(rows, ROW_BLOCK)
            rem = remote.at[pl.ds(pl.multiple_of(tstart_ref[e], ROW_BLOCK), n)]
            _start_or_wait(pltpu.make_async_copy(zero.at[pl.ds(0, n)], rem, sem), wait, priority)

        return carry

    lax.fori_loop(0, N_EXPERTS, body, 0)


def _unused_tile_copies(zero, remote, sem, nt_ref, total_tiles, *, wait, priority=0):
    def body(t, carry):
        row = pl.multiple_of(t * MOE_TILE, MOE_TILE)
        _start_or_wait(pltpu.make_async_copy(zero, remote.at[pl.ds(row, MOE_TILE)], sem), wait, priority)
        return carry

    lax.fori_loop(nt_ref[0], total_tiles, body, 0)


def _experts_kernel(t0_ref, tcnt_ref, nt_ref, xs_ref, wg_ref, wu_ref, wd_ref, ys_ref,
                    xbuf, ybuf, zero, wg_s, wu_s, wd_s, xsem, ysem, zsem):
    e = pl.program_id(0)
    first_tile = t0_ref[e]
    n = tcnt_ref[e]
    n_tiles = nt_ref[0]
    total_tiles = xs_ref.shape[0] // MOE_TILE

    def tile_rows(ref, a):
        return ref.at[pl.ds(pl.multiple_of(a * MOE_TILE, MOE_TILE), MOE_TILE)]

    x_copy = lambda a: pltpu.make_async_copy(tile_rows(xs_ref, a), xbuf.at[a % 2], xsem.at[a % 2])
    y_copy = lambda a: pltpu.make_async_copy(ybuf.at[a % 2], tile_rows(ys_ref, a), ysem.at[a % 2])

    last = e == N_EXPERTS - 1

    @pl.when(last)
    def _():
        zero[...] = jnp.zeros(zero.shape, MOE_DT)
        _unused_tile_copies(zero, ys_ref, zsem, nt_ref, total_tiles, wait=False)

    @pl.when(n > 0)
    def _():
        @pl.when(first_tile == 0)
        def _():
            x_copy(first_tile).start(priority=1)

        wg_s[...] = wg_ref[0, 0].astype(BF16)
        wu_s[...] = wu_ref[0, 0].astype(BF16)
        wd_s[...] = wd_ref[0, 0].astype(BF16)

        def body(a, carry):
            slot = a % 2
            x_copy(a).wait()

            @pl.when(a + 1 < n_tiles)
            def _():
                x_copy(a + 1).start(priority=1)

            @pl.when(a >= 2)
            def _():
                y_copy(a - 2).wait()

            half = MOE_TILE // 2
            halves = [xbuf[slot, r:r + half, :].astype(BF16) for r in (0, half)]
            gate_up = [(_dot(x, wg_s[...]), _dot(x, wu_s[...])) for x in halves]
            for r, (g, u) in zip((0, half), gate_up):
                hid = (g * (1.0 / (1.0 + jnp.exp(-g)))) * u
                ybuf[slot, r:r + half, :] = _dot(hid.astype(BF16), wd_s[...]).astype(MOE_DT)
            y_copy(a).start(priority=1)
            return carry

        lax.fori_loop(first_tile, first_tile + n, body, 0)

    @pl.when(last)
    def _():
        y_copy(n_tiles - 1).wait()

        @pl.when(n_tiles >= 2)
        def _():
            y_copy(n_tiles - 2).wait()

        _unused_tile_copies(zero, ys_ref, zsem, nt_ref, total_tiles, wait=True)


def _experts(xs, plan, w_gate, w_up, w_down, layer):
    r, d = xs.shape
    de = w_gate.shape[3]
    wspec = lambda a, b: pl.BlockSpec((1, 1, a, b), lambda e, *_: (layer, e, 0, 0))
    any_spec = pl.BlockSpec(memory_space=pl.ANY)
    return pl.pallas_call(
        _experts_kernel,
        out_shape=jax.ShapeDtypeStruct((r, d), MOE_DT),
        grid_spec=pltpu.PrefetchScalarGridSpec(
            num_scalar_prefetch=3,
            grid=(N_EXPERTS,),
            in_specs=[any_spec, wspec(d, de), wspec(d, de), wspec(de, d)],
            out_specs=any_spec,
            scratch_shapes=[pltpu.VMEM((2, MOE_TILE, d), MOE_DT), pltpu.VMEM((2, MOE_TILE, d), MOE_DT),
                            pltpu.VMEM((MOE_TILE, d), MOE_DT),
                            pltpu.VMEM((d, de), BF16), pltpu.VMEM((d, de), BF16), pltpu.VMEM((de, d), BF16),
                            pltpu.SemaphoreType.DMA((2,)), pltpu.SemaphoreType.DMA((2,)),
                            pltpu.SemaphoreType.DMA(())],
        ),
        compiler_params=_cparams("arbitrary"),
        name="experts",
    )(plan["tile0"], plan["tcnt"], plan["n_tiles"], xs, w_gate, w_up, w_down)


def _dispatch_kernel(runs_ref, tstart_ref, tn_ref, nt_ref, *rest, part_chunks):
    n_parts = len(part_chunks)
    h_refs = rest[:n_parts]
    pos_ref, xs_ref, buf, zero, sem, zsem = rest[n_parts:]
    n_chunks = sum(part_chunks)
    c = pl.program_id(0)
    slot = c % 2
    total_tiles = xs_ref.shape[0] // MOE_TILE

    @pl.when(c == 0)
    def _():
        zero[...] = jnp.zeros(zero.shape, MOE_DT)
        _tail_copies(zero, xs_ref, zsem, tstart_ref, tn_ref, wait=False, priority=1)
        _unused_tile_copies(zero, xs_ref, zsem, nt_ref, total_tiles, wait=False, priority=1)

    row = lax.broadcasted_iota(jnp.int32, (MOE_LROWS, MOE_CHUNK), 0)
    hit = (row == pos_ref[0, 0:1, :]) | (row == pos_ref[0, 1:2, :])
    sel = jnp.where(hit, 1.0, 0.0).astype(BF16)
    first = 0
    for h_ref, n in zip(h_refs, part_chunks):
        @pl.when((c >= first) & (c < first + n))
        def _(h_ref=h_ref):
            buf[slot] = _dot(sel, h_ref[...]).astype(MOE_DT)
        first += n
    _run_copies(buf.at[slot], xs_ref, sem.at[slot], runs_ref, c, to_remote=True, wait=False)

    @pl.when(c > 0)
    def _():
        _run_copies(buf.at[1 - slot], xs_ref, sem.at[1 - slot], runs_ref, c - 1, to_remote=True, wait=True)

    @pl.when(c == n_chunks - 1)
    def _():
        _run_copies(buf.at[slot], xs_ref, sem.at[slot], runs_ref, c, to_remote=True, wait=True)
        _tail_copies(zero, xs_ref, zsem, tstart_ref, tn_ref, wait=True)
        _unused_tile_copies(zero, xs_ref, zsem, nt_ref, total_tiles, wait=True)


def _dispatch(h_parts, plan):
    d = h_parts[0].shape[1]
    part_chunks = tuple(h.shape[0] // MOE_CHUNK for h in h_parts)
    in_specs = []
    first = 0
    for n in part_chunks:
        in_specs.append(pl.BlockSpec(
            (MOE_CHUNK, d), lambda i, *_, first=first, n=n: (jnp.clip(i - first, 0, n - 1), 0)))
        first += n
    in_specs.append(pl.BlockSpec((1, 8, MOE_CHUNK), lambda i, *_: (i, 0, 0)))
    return pl.pallas_call(
        functools.partial(_dispatch_kernel, part_chunks=part_chunks),
        out_shape=jax.ShapeDtypeStruct((plan["rows"], d), MOE_DT),
        grid_spec=pltpu.PrefetchScalarGridSpec(
            num_scalar_prefetch=4,
            grid=(sum(part_chunks),),
            in_specs=in_specs,
            out_specs=pl.BlockSpec(memory_space=pl.ANY),
            scratch_shapes=[pltpu.VMEM((2, MOE_LROWS, d), MOE_DT), pltpu.VMEM((MOE_TILE, d), MOE_DT),
                            pltpu.SemaphoreType.DMA((2,)), pltpu.SemaphoreType.DMA(())],
        ),
        compiler_params=_cparams("arbitrary"),
        name="dispatch",
    )(plan["runs"], plan["tstart"], plan["tn"], plan["n_tiles"], *h_parts, plan["pos"])


def _combine_kernel(runs_ref, x1_ref, pos_ref, wt_ref, g2_ref, gf_ref, ys_ref, o_ref, buf, sem,
                    *, c0, n_chunks, final_norm):
    i = pl.program_id(0)
    c = i + c0
    slot = i % 2
    gather = functools.partial(_run_copies, remote=ys_ref, runs_ref=runs_ref, to_remote=False)

    @pl.when(i == 0)
    def _():
        buf[...] = jnp.zeros(buf.shape, MOE_DT)
        gather(buf.at[slot], sem=sem.at[slot], c=c, wait=False)

    @pl.when(i + 1 < n_chunks)
    def _():
        gather(buf.at[1 - slot], sem=sem.at[1 - slot], c=c + 1, wait=False)

    gather(buf.at[slot], sem=sem.at[slot], c=c, wait=True)
    row = lax.broadcasted_iota(jnp.int32, (MOE_LROWS, MOE_CHUNK), 0)
    hit1 = row == pos_ref[0, 0:1, :]
    hit2 = row == pos_ref[0, 1:2, :]
    w_row = jnp.sum(jnp.where(hit1, wt_ref[0, 0:1, :], 0.0) + jnp.where(hit2, wt_ref[0, 1:2, :], 0.0),
                    axis=1, keepdims=True)
    sel = jnp.where(hit1 | hit2, 1.0, 0.0).astype(BF16)
    x2 = x1_ref[0] + g2_ref[0] * _dot_tn(sel, (buf[slot] * w_row).astype(BF16))
    if final_norm:
        ms = jnp.mean(x2 * x2, axis=-1, keepdims=True)
        x2 = x2 * lax.rsqrt(ms + EPS) * gf_ref[...]
    o_ref[0] = x2


def _combine(x1, plan, gate, g_final, ys, *, c0, final_norm):
    b, t, d = x1.shape
    per_batch = t // MOE_CHUNK
    n_chunks = b * per_batch
    tok = lambda w: pl.BlockSpec((1, MOE_CHUNK, w), lambda i, *_: (i // per_batch, i % per_batch, 0))
    return pl.pallas_call(
        functools.partial(_combine_kernel, c0=c0, n_chunks=n_chunks, final_norm=final_norm),
        out_shape=jax.ShapeDtypeStruct((b, t, d), F32),
        grid_spec=pltpu.PrefetchScalarGridSpec(
            num_scalar_prefetch=1,
            grid=(n_chunks,),
            in_specs=[tok(d),
                      pl.BlockSpec((1, 8, MOE_CHUNK), lambda i, *_: (i + c0, 0, 0)),
                      pl.BlockSpec((1, 8, MOE_CHUNK), lambda i, *_: (i + c0, 0, 0)),
                      pl.BlockSpec((1, 1, d), lambda i, *_: (i // per_batch, 0, 0)),
                      pl.BlockSpec((1, d), lambda i, *_: (0, 0)),
                      pl.BlockSpec(memory_space=pl.ANY)],
            out_specs=tok(d),
            scratch_shapes=[pltpu.VMEM((2, MOE_LROWS, d), MOE_DT), pltpu.SemaphoreType.DMA((2,))],
        ),
        compiler_params=_cparams("arbitrary"),
        name="combine",
    )(plan["runs"], x1, plan["pos"], plan["wts"], gate, g_final, ys)


def _rope_tables(seq):
    half = HEAD_DIM // 4
    t = np.arange(seq)
    rows, cols = t // GRID_W, t % GRID_W
    freqs = ROPE_BASE ** (-np.arange(half, dtype=np.float64) / half)

    def cs(pos):
        ang = pos[None, :] * freqs[:, None]
        return np.cos(ang), np.sin(ang)

    cr, sr = cs(rows)
    cc, sc = cs(cols)
    return (jnp.asarray(np.concatenate([cr, cr, cc, cc], axis=0), F32),
            jnp.asarray(np.concatenate([-sr, sr, -sc, sc], axis=0), F32))


def _dft_tables(n):
    k = np.arange(n)
    ang = ((k[:, None] * k[None, :]) % n) * (2.0 * np.pi / n)
    return jnp.asarray(np.cos(ang), F32), jnp.asarray(np.sin(ang), F32)


def _channel_dft_t():
    c, s = _dft_tables(FOURIER_GROUP_DIM)
    eye = jnp.eye(FOURIER_GROUPS, dtype=F32)
    scale = FOURIER_GROUP_DIM ** -0.5
    return jnp.concatenate([jnp.kron(eye, c), jnp.kron(eye, s)], axis=0) * scale


def _position_dft(n):
    c, s = _dft_tables(n)
    scale = n ** -0.5
    return (c * scale).astype(BF16), (s * scale).astype(BF16)


def _swa_mask():
    kk = np.arange(SWA_BLOCK)[:, None]
    q = np.arange(SWA_BLOCK)[None, :]
    tiles = []
    for block_offset in (-SWA_BLOCK, SWA_BLOCK):
        ok = np.abs(block_offset + kk - q) <= SWA_WINDOW
        tiles.append(np.tile(np.where(ok, 0.0, NEG_INF), (1, SWA_GROUP)))
    return jnp.asarray(np.stack(tiles), F32)


def _sink_rows(sink, width):
    return jnp.repeat(sink.astype(F32).reshape(SWA_KV_HEADS, SWA_GROUP) * LOG2E, width, axis=1)[:, None, :]


def _route_chunks(route):
    r = jnp.moveaxis(route, 1, 0).reshape(ROUTE_OUT, -1, MOE_CHUNK)
    return jnp.swapaxes(r, 0, 1)


def _moe_plan(route_chunks):
    i32 = jnp.int32
    nc = route_chunks.shape[0]
    n = nc * MOE_CHUNK
    ids = route_chunks[:, 0:2, :].astype(i32).reshape(nc, 2 * MOE_CHUNK)
    onehot = (ids[:, :, None] == jnp.arange(N_EXPERTS, dtype=i32)).astype(i32)
    pair = jnp.arange(2 * MOE_CHUNK, dtype=i32)
    earlier = (pair[None, :] < pair[:, None]).astype(BF16)
    before = jnp.einsum("pq,cqe->cpe", earlier, onehot.astype(BF16), preferred_element_type=F32).astype(i32)
    rank = jnp.sum(before * onehot, axis=-1)
    cnt = jnp.sum(onehot, axis=1)
    run = (cnt + ROW_BLOCK - 1) // ROW_BLOCK * ROW_BLOCK
    lo = jnp.cumsum(run, axis=1) - run
    pos = (jnp.sum(onehot * lo[:, None, :], axis=-1) + rank).reshape(nc, 2, MOE_CHUNK)
    pos = jnp.concatenate([pos, jnp.full((nc, 6, MOE_CHUNK), -1, i32)], axis=1)
    seg = jnp.sum(run, axis=0)
    padded = (seg + MOE_TILE - 1) // MOE_TILE * MOE_TILE
    start = jnp.cumsum(padded) - padded
    off = start[None, :] + jnp.cumsum(run, axis=0) - run
    wts = jnp.concatenate([route_chunks[:, 2:4, :], jnp.zeros((nc, 6, MOE_CHUNK), F32)], axis=1)
    rows = -(-(2 * n + (ROW_BLOCK - 1) * N_EXPERTS * nc + (MOE_TILE - 1) * N_EXPERTS) // MOE_TILE) * MOE_TILE
    n_tiles = jnp.sum(padded) // MOE_TILE
    total = jnp.broadcast_to(jnp.sum(run, axis=1, keepdims=True), run.shape)
    return {"pos": pos, "wts": wts, "runs": jnp.stack([run, lo, off, total]).astype(i32),
            "tstart": (start + seg).astype(i32), "tn": (padded - seg).astype(i32),
            "tile0": (start // MOE_TILE).astype(i32), "tcnt": (padded // MOE_TILE).astype(i32),
            "n_tiles": n_tiles.astype(i32).reshape(1), "rows": rows}


def kernel(x, c, ctx, c_ctx, w_mod, b_mod, g_norm1, g_norm2, w_in, w_four, w_out, swa_sink, na_rpb,
           w_route_group, b_route_group, w_route_expert, b_route_expert, w_exp_gate, w_exp_up,
           w_exp_down, g_final):
    b, s, d = x.shape
    lc = ctx.shape[1]
    depth = w_mod.shape[0]
    tm = 512

    c_rows = jnp.concatenate([c, c_ctx[None, :], jnp.zeros((7, d), F32)], axis=0)
    mod = _modulation(c_rows, w_mod, b_mod)

    cos_t, sin_t = _rope_tables(s)
    cos_c, sin_c = cos_t[:, :lc], sin_t[:, :lc]
    bd_t = _channel_dft_t().astype(BF16)
    cn, sn = _position_dft(s)
    cn_c, sn_c = _position_dft(lc)
    mask = _swa_mask()
    route_pad = ROUTE_ROWS - N_GROUPS - N_EXPERTS

    xc = ctx
    for layer in range(depth):
        with_ctx_out = layer < depth - 1
        lat = [mod[layer, :b, i * d:(i + 1) * d][:, None, :] for i in range(6)]
        cx = [jnp.broadcast_to(mod[layer, b, i * d:(i + 1) * d][None, None, :], (b, 1, d)) for i in range(6)]
        sh1, sc1, g1, sh2, sc2, g2 = lat
        shc1, scc1, gc1, shc2, scc2, gc2 = cx
        gn1 = g_norm1[layer][None, :]
        gn2 = g_norm2[layer][None, :]
        w_t = w_in[layer].T.astype(BF16)
        wf_t = w_four[layer].T.astype(BF16)
        wo = w_out[layer].astype(BF16)
        w_r = jnp.concatenate([w_route_group[layer].T, w_route_expert[layer].T,
                               jnp.zeros((route_pad, d), F32)], axis=0)
        wr_hi = w_r.astype(BF16)
        wr_lo = (w_r - wr_hi.astype(F32)).astype(BF16)
        b_r = jnp.concatenate([b_route_group[layer], b_route_expert[layer], jnp.zeros((route_pad,), F32)])[:, None]
        sink_lat = _sink_rows(swa_sink[layer], SWA_BLOCK)
        sink_ctx = _sink_rows(swa_sink[layer], lc)

        fz, qs, qn, ks, vs, kn, vn = _in_proj(x, sh1, sc1, gn1, w_t, bd_t, cos_t, sin_t,
                                              with_q=True, rope=True, tm=2 * tm)
        if with_ctx_out:
            fz_c, qs_c, qn_c, ks_c, vs_c, kn_c, vn_c = _in_proj(xc, shc1, scc1, gn1, w_t, bd_t, cos_c, sin_c,
                                                                with_q=True, rope=False, tm=lc)
        else:
            ks_c, vs_c, kn_c, vn_c = _in_proj(xc, shc1, scc1, gn1, w_t[KS_LO:], bd_t, cos_c, sin_c,
                                              with_q=False, rope=False, tm=lc)

        yf = _fourier(fz, cn, sn, wf_t, tk=1024)
        ys = _swa(qs, ks, vs, ks_c, vs_c, mask, sink_lat)
        yn = _na(qn, kn, vn, kn_c, vn_c, _na_bias(na_rpb[layer], s))
        x1, h2, route = _out_proj(x, yf, ys, yn, wo, g1, sh2, sc2, gn2, wr_hi, wr_lo, b_r, tm=2 * tm)

        moe_w = (w_exp_gate, w_exp_up, w_exp_down, layer)
        if with_ctx_out:
            yf_c = _fourier(fz_c, cn_c, sn_c, wf_t, tk=lc)
            ys_c, yn_c = _ctx_attn(qs_c, ks_c, vs_c, qn_c, kn_c, vn_c, sink_ctx)
            xc1, hc2, route_c = _out_proj(xc, yf_c, ys_c, yn_c, wo, gc1, shc2, scc2, gn2, wr_hi, wr_lo, b_r, tm=lc)
            n_lat = b * s
            lat_chunks = n_lat // MOE_CHUNK
            plan = _moe_plan(jnp.concatenate([_route_chunks(route), _route_chunks(route_c)], axis=0))
            xs = _dispatch([h2.reshape(n_lat, d), hc2.reshape(b * lc, d)], plan)
            ye = _experts(xs, plan, *moe_w)
            x = _combine(x1, plan, g2, g_final[None, :], ye, c0=0, final_norm=False)
            ctx_chunks = b * lc // MOE_CHUNK
            xc = _combine(xc1.reshape(ctx_chunks, MOE_CHUNK, d), plan, gc2[:ctx_chunks], g_final[None, :], ye,
                          c0=lat_chunks, final_norm=False).reshape(b, lc, d)
        else:
            plan = _moe_plan(_route_chunks(route))
            xs = _dispatch([h2.reshape(b * s, d)], plan)
            ye = _experts(xs, plan, *moe_w)
            x = _combine(x1, plan, g2, g_final[None, :], ye, c0=0, final_norm=True)
    return x
```
